```python
import jax, jax.numpy as jnp
from jax import lax
import numpy as np

D_MODEL = 2048
BATCH = 8
SEQ = 8192
DEPTH = 4

GRID_W = 64
CTX_LEN = 256
D_CONV = D_MODEL // 2
D_ATTN = D_MODEL - D_CONV
HEAD_DIM = 64
N_HEADS = D_ATTN // HEAD_DIM
N_KV_HEADS = 4
GQA_GROUP = N_HEADS // N_KV_HEADS
WINDOW = 128
BLOCK = 128
CONV_WIDTH = 3
D_FF = 4 * D_MODEL
ROPE_THETA = 10000.0
ROPE_AXIS_DIM = HEAD_DIM // 2
EPS = 1e-6
N_MOD = 6
KV_START = 3 * D_CONV + D_ATTN
D_IN_PROJ = KV_START + 2 * N_KV_HEADS * HEAD_DIM
SCALE = HEAD_DIM ** -0.5
NEG_INF = -1e30

kernel_name = "hybrid_conv_swa_dit_block"


def rmsnorm(x, g):
    xf = x.astype(jnp.float32)
    y = xf * lax.rsqrt(jnp.mean(xf * xf, axis=-1, keepdims=True) + EPS)
    return (y * g.astype(jnp.float32)).astype(x.dtype)


def modulate(h, shift, scale):
    return h * (1 + scale) + shift


def short_conv(u, w, b):
    n = u.shape[1]
    up = jnp.pad(u, ((0, 0), (1, 1), (0, 0)))
    return up[:, 0:n] * w[0] + up[:, 1:n + 1] * w[1] + up[:, 2:n + 2] * w[2] + b


def gated_conv_mixer(p_conv, w, b):
    bg, cg, h = jnp.split(p_conv, 3, axis=-1)
    return bg * short_conv(cg * h, w, b)


def rope_tables(n_tokens, dtype):
    rows = n_tokens // GRID_W
    row_pos = jnp.repeat(jnp.arange(rows, dtype=jnp.float32), GRID_W)
    col_pos = jnp.tile(jnp.arange(GRID_W, dtype=jnp.float32), rows)
    inv = ROPE_THETA ** (-jnp.arange(0, ROPE_AXIS_DIM, 2, dtype=jnp.float32) / ROPE_AXIS_DIM)
    ang_r = row_pos[:, None] * inv[None, :]
    ang_c = col_pos[:, None] * inv[None, :]
    return (jnp.cos(ang_r)[:, None, :].astype(dtype), jnp.sin(ang_r)[:, None, :].astype(dtype),
            jnp.cos(ang_c)[:, None, :].astype(dtype), jnp.sin(ang_c)[:, None, :].astype(dtype))


def rotate(x, cos, sin):
    x1, x2 = jnp.split(x, 2, axis=-1)
    return jnp.concatenate([x1 * cos - x2 * sin, x2 * cos + x1 * sin], axis=-1)


def rope_2d(x, tabs):
    cr, sr, cc, sc = tabs
    xr, xc = jnp.split(x, 2, axis=-1)
    return jnp.concatenate([rotate(xr, cr, sr), rotate(xc, cc, sc)], axis=-1)


def band_mask(nb, n_tokens):
    n = jnp.arange(nb)[:, None, None]
    r = jnp.arange(BLOCK)[None, :, None]
    j = jnp.arange(3 * BLOCK)[None, None, :]
    q_pos = n * BLOCK + r
    k_pos = (n - 1) * BLOCK + j
    return (jnp.abs(k_pos - q_pos) <= WINDOW) & (k_pos >= 0) & (k_pos < n_tokens)


def latent_window_attention(q, k, v, kc, vc, sink):
    bsz, n_tok = q.shape[0], q.shape[1]
    n_ctx = kc.shape[1]
    nb = n_tok // BLOCK
    qb = q.reshape(bsz, nb, BLOCK, N_KV_HEADS, GQA_GROUP, HEAD_DIM)

    def band(t):
        tb = t.reshape(bsz, nb, BLOCK, N_KV_HEADS, HEAD_DIM)
        tb = jnp.pad(tb, ((0, 0), (1, 1), (0, 0), (0, 0), (0, 0)))
        return jnp.concatenate([tb[:, :-2], tb[:, 1:-1], tb[:, 2:]], axis=2)

    kw, vw = band(k), band(v)
    s_loc = jnp.einsum('bnqhgd,bnkhd->bnhgqk', qb, kw).astype(jnp.float32) * SCALE
    s_loc = jnp.where(band_mask(nb, n_tok)[None, :, None, None], s_loc, NEG_INF)
    s_ctx = jnp.einsum('bnqhgd,bchd->bnhgqc', qb, kc).astype(jnp.float32) * SCALE
    snk = jnp.broadcast_to(sink.astype(jnp.float32).reshape(1, 1, N_KV_HEADS, GQA_GROUP, 1, 1),
                           s_loc.shape[:-1] + (1,))
    p = jax.nn.softmax(jnp.concatenate([s_loc, s_ctx, snk], axis=-1), axis=-1).astype(v.dtype)
    nk = 3 * BLOCK
    o = (jnp.einsum('bnhgqk,bnkhd->bnqhgd', p[..., :nk], vw)
         + jnp.einsum('bnhgqc,bchd->bnqhgd', p[..., nk:nk + n_ctx], vc))
    return o.reshape(bsz, n_tok, D_ATTN)


def context_attention(qc, kc, vc, sink):
    bsz, n_ctx = qc.shape[0], qc.shape[1]
    qg = qc.reshape(bsz, n_ctx, N_KV_HEADS, GQA_GROUP, HEAD_DIM)
    s = jnp.einsum('blhgd,bchd->bhglc', qg, kc).astype(jnp.float32) * SCALE
    snk = jnp.broadcast_to(sink.astype(jnp.float32).reshape(1, N_KV_HEADS, GQA_GROUP, 1, 1),
                           s.shape[:-1] + (1,))
    p = jax.nn.softmax(jnp.concatenate([s, snk], axis=-1), axis=-1).astype(vc.dtype)
    o = jnp.einsum('bhglc,bchd->blhgd', p[..., :n_ctx], vc)
    return o.reshape(bsz, n_ctx, D_ATTN)


def mixer_merge(conv_out, attn_out, g_oc, g_oa, w_out):
    return jnp.concatenate([rmsnorm(conv_out, g_oc), rmsnorm(attn_out, g_oa)], axis=-1) @ w_out


def sq_relu_mlp(h, w1, w2):
    return jnp.square(jax.nn.relu(h @ w1)) @ w2


def _fwd_setup_inputs(seed: int = 0) -> dict:
    key = jax.random.key(seed)
    ks = jax.random.split(key, 18)

    def nrm(k, shape, s):
        return jax.random.normal(k, shape, jnp.float32) * s

    return {
        "x": nrm(ks[0], (BATCH, SEQ, D_MODEL), 1.0),
        "c": nrm(ks[1], (BATCH, D_MODEL), 1.0),
        "ctx": nrm(ks[2], (BATCH, CTX_LEN, D_MODEL), 1.0),
        "c_ctx": nrm(ks[3], (D_MODEL,), 1.0),
        "w_ada": nrm(ks[4], (DEPTH, D_MODEL, N_MOD * D_MODEL), 0.5 * D_MODEL ** -0.5),
        "b_ada": nrm(ks[5], (DEPTH, N_MOD * D_MODEL), 0.02),
        "g_norm1": 1.0 + nrm(ks[6], (DEPTH, D_MODEL), 0.02),
        "g_norm2": 1.0 + nrm(ks[7], (DEPTH, D_MODEL), 0.02),
        "w_in": nrm(ks[8], (DEPTH, D_MODEL, D_IN_PROJ), D_MODEL ** -0.5),
        "conv_w": nrm(ks[9], (DEPTH, CONV_WIDTH, D_CONV), CONV_WIDTH ** -0.5),
        "conv_b": nrm(ks[10], (DEPTH, D_CONV), 0.02),
        "sink": nrm(ks[11], (DEPTH, N_HEADS), 0.5),
        "g_out_conv": 1.0 + nrm(ks[12], (DEPTH, D_CONV), 0.02),
        "g_out_attn": 1.0 + nrm(ks[13], (DEPTH, D_ATTN), 0.02),
        "w_out": nrm(ks[14], (DEPTH, D_MODEL, D_MODEL), D_MODEL ** -0.5),
        "w_mlp1": nrm(ks[15], (DEPTH, D_MODEL, D_FF), D_MODEL ** -0.5),
        "w_mlp2": nrm(ks[16], (DEPTH, D_FF, D_MODEL), D_FF ** -0.5),
        "g_final": 1.0 + nrm(ks[17], (D_MODEL,), 0.02),
    }


def _fwd_reference(x, c, ctx, c_ctx, w_ada, b_ada, g_norm1, g_norm2, w_in, conv_w, conv_b, sink,
              g_out_conv, g_out_attn, w_out, w_mlp1, w_mlp2, g_final):
    bsz, n_tok, _ = x.shape
    n_ctx = ctx.shape[1]
    tabs = rope_tables(n_tok, x.dtype)
    sc = jax.nn.silu(c)
    scc = jax.nn.silu(c_ctx)
    for i in range(DEPTH):
        last = i == DEPTH - 1
        m = jnp.split((sc @ w_ada[i] + b_ada[i])[:, None, :], N_MOD, axis=-1)
        mc = jnp.split(scc @ w_ada[i] + b_ada[i], N_MOD, axis=-1)

        h = modulate(rmsnorm(x, g_norm1[i]), m[0], m[1])
        hc = modulate(rmsnorm(ctx, g_norm1[i]), mc[0], mc[1])
        p = h @ w_in[i]
        p_conv = p[..., :3 * D_CONV]
        q = p[..., 3 * D_CONV:KV_START].reshape(bsz, n_tok, N_HEADS, HEAD_DIM)
        k, v = jnp.split(p[..., KV_START:], 2, axis=-1)
        q = rope_2d(q, tabs)
        k = rope_2d(k.reshape(bsz, n_tok, N_KV_HEADS, HEAD_DIM), tabs)
        v = v.reshape(bsz, n_tok, N_KV_HEADS, HEAD_DIM)
        kc, vc = jnp.split(hc @ w_in[i][:, KV_START:], 2, axis=-1)
        kc = kc.reshape(bsz, n_ctx, N_KV_HEADS, HEAD_DIM)
        vc = vc.reshape(bsz, n_ctx, N_KV_HEADS, HEAD_DIM)

        conv_out = gated_conv_mixer(p_conv, conv_w[i], conv_b[i])
        attn_out = latent_window_attention(q, k, v, kc, vc, sink[i])
        x = x + m[2] * mixer_merge(conv_out, attn_out, g_out_conv[i], g_out_attn[i], w_out[i])

        if not last:
            pc = hc @ w_in[i][:, :KV_START]
            ctx_conv = gated_conv_mixer(pc[..., :3 * D_CONV], conv_w[i], conv_b[i])
            qc = pc[..., 3 * D_CONV:].reshape(bsz, n_ctx, N_HEADS, HEAD_DIM)
            ctx_attn = context_attention(qc, kc, vc, sink[i])
            ctx = ctx + mc[2] * mixer_merge(ctx_conv, ctx_attn, g_out_conv[i], g_out_attn[i], w_out[i])

        x = x + m[5] * sq_relu_mlp(modulate(rmsnorm(x, g_norm2[i]), m[3], m[4]), w_mlp1[i], w_mlp2[i])
        if not last:
            ctx = ctx + mc[5] * sq_relu_mlp(modulate(rmsnorm(ctx, g_norm2[i]), mc[3], mc[4]),
                                            w_mlp1[i], w_mlp2[i])
    return rmsnorm(x, g_final)


import jax as _jax
import jax.numpy as _jnp

TWIN_FORMAT = 'train_step'
FWD_PARAMS = ['x', 'c', 'ctx', 'c_ctx', 'w_ada', 'b_ada', 'g_norm1', 'g_norm2', 'w_in', 'conv_w', 'conv_b', 'sink', 'g_out_conv', 'g_out_attn', 'w_out', 'w_mlp1', 'w_mlp2', 'g_final']
TWIN_WEIGHTS = ['c_ctx', 'w_ada', 'b_ada', 'g_norm1', 'g_norm2', 'w_in', 'conv_w', 'conv_b', 'sink', 'g_out_conv', 'g_out_attn', 'w_out', 'w_mlp1', 'w_mlp2', 'g_final']
TWIN_DIFF_INPUT = 'x'
TWIN_INPUTS = ['x', 'c', 'ctx', 'c_ctx', 'w_ada', 'b_ada', 'g_norm1', 'g_norm2', 'w_in', 'conv_w', 'conv_b', 'sink', 'g_out_conv', 'g_out_attn', 'w_out', 'w_mlp1', 'w_mlp2', 'g_final', 'loss_target', 'm_c_ctx', 'm_w_ada', 'm_b_ada', 'm_g_norm1', 'm_g_norm2', 'm_w_in', 'm_conv_w', 'm_conv_b', 'm_sink', 'm_g_out_conv', 'm_g_out_attn', 'm_w_out', 'm_w_mlp1', 'm_w_mlp2', 'm_g_final', 'v_c_ctx', 'v_w_ada', 'v_b_ada', 'v_g_norm1', 'v_g_norm2', 'v_w_in', 'v_conv_w', 'v_conv_b', 'v_sink', 'v_g_out_conv', 'v_g_out_attn', 'v_w_out', 'v_w_mlp1', 'v_w_mlp2', 'v_g_final']
TWIN_OUTPUTS = ['loss', 'grad_x', 'grad_c_ctx', 'grad_w_ada', 'grad_b_ada', 'grad_g_norm1', 'grad_g_norm2', 'grad_w_in', 'grad_conv_w', 'grad_conv_b', 'grad_sink', 'grad_g_out_conv', 'grad_g_out_attn', 'grad_w_out', 'grad_w_mlp1', 'grad_w_mlp2', 'grad_g_final', 'delta_c_ctx', 'delta_w_ada', 'delta_b_ada', 'delta_g_norm1', 'delta_g_norm2', 'delta_w_in', 'delta_conv_w', 'delta_conv_b', 'delta_sink', 'delta_g_out_conv', 'delta_g_out_attn', 'delta_w_out', 'delta_w_mlp1', 'delta_w_mlp2', 'delta_g_final', 'new_m_c_ctx', 'new_m_w_ada', 'new_m_b_ada', 'new_m_g_norm1', 'new_m_g_norm2', 'new_m_w_in', 'new_m_conv_w', 'new_m_conv_b', 'new_m_sink', 'new_m_g_out_conv', 'new_m_g_out_attn', 'new_m_w_out', 'new_m_w_mlp1', 'new_m_w_mlp2', 'new_m_g_final', 'new_v_c_ctx', 'new_v_w_ada', 'new_v_b_ada', 'new_v_g_norm1', 'new_v_g_norm2', 'new_v_w_in', 'new_v_conv_w', 'new_v_conv_b', 'new_v_sink', 'new_v_g_out_conv', 'new_v_g_out_attn', 'new_v_w_out', 'new_v_w_mlp1', 'new_v_w_mlp2', 'new_v_g_final']
TWIN_LEAF_KINDS = {'loss': 'loss', 'grad_x': 'grad_x', 'grad_c_ctx': 'grad_w', 'grad_w_ada': 'grad_w', 'grad_b_ada': 'grad_w', 'grad_g_norm1': 'grad_w', 'grad_g_norm2': 'grad_w', 'grad_w_in': 'grad_w', 'grad_conv_w': 'grad_w', 'grad_conv_b': 'grad_w', 'grad_sink': 'grad_w', 'grad_g_out_conv': 'grad_w', 'grad_g_out_attn': 'grad_w', 'grad_w_out': 'grad_w', 'grad_w_mlp1': 'grad_w', 'grad_w_mlp2': 'grad_w', 'grad_g_final': 'grad_w', 'delta_c_ctx': 'delta_w', 'delta_w_ada': 'delta_w', 'delta_b_ada': 'delta_w', 'delta_g_norm1': 'delta_w', 'delta_g_norm2': 'delta_w', 'delta_w_in': 'delta_w', 'delta_conv_w': 'delta_w', 'delta_conv_b': 'delta_w', 'delta_sink': 'delta_w', 'delta_g_out_conv': 'delta_w', 'delta_g_out_attn': 'delta_w', 'delta_w_out': 'delta_w', 'delta_w_mlp1': 'delta_w', 'delta_w_mlp2': 'delta_w', 'delta_g_final': 'delta_w', 'new_m_c_ctx': 'new_m', 'new_m_w_ada': 'new_m', 'new_m_b_ada': 'new_m', 'new_m_g_norm1': 'new_m', 'new_m_g_norm2': 'new_m', 'new_m_w_in': 'new_m', 'new_m_conv_w': 'new_m', 'new_m_conv_b': 'new_m', 'new_m_sink': 'new_m', 'new_m_g_out_conv': 'new_m', 'new_m_g_out_attn': 'new_m', 'new_m_w_out': 'new_m', 'new_m_w_mlp1': 'new_m', 'new_m_w_mlp2': 'new_m', 'new_m_g_final': 'new_m', 'new_v_c_ctx': 'new_v', 'new_v_w_ada': 'new_v', 'new_v_b_ada': 'new_v', 'new_v_g_norm1': 'new_v', 'new_v_g_norm2': 'new_v', 'new_v_w_in': 'new_v', 'new_v_conv_w': 'new_v', 'new_v_conv_b': 'new_v', 'new_v_sink': 'new_v', 'new_v_g_out_conv': 'new_v', 'new_v_g_out_attn': 'new_v', 'new_v_w_out': 'new_v', 'new_v_w_mlp1': 'new_v', 'new_v_w_mlp2': 'new_v', 'new_v_g_final': 'new_v'}


def _forward(args):
    return _fwd_reference(*[args[k] for k in FWD_PARAMS])


def _output_shape():
    def fwd():
        inp = _fwd_setup_inputs(0)
        return _fwd_reference(*[inp[k] for k in FWD_PARAMS])
    out = _jax.eval_shape(fwd)
    return out.shape, out.dtype

N_MICROBATCH = 1
ADAM_LR = 0.001
ADAM_B1 = 0.9
ADAM_B2 = 0.999
ADAM_EPS = 1e-08
ADAM_WD = 0.01
ADAM_STEP = 10
PER_EXAMPLE_BATCH_AXIS = {'x': 0, 'c': 0, 'ctx': 0, 'loss_target': 0}
SHARED_INPUTS = []
_WEIGHT_DTYPES = {'c_ctx': _jnp.float32, 'w_ada': _jnp.float32, 'b_ada': _jnp.float32, 'g_norm1': _jnp.float32, 'g_norm2': _jnp.float32, 'w_in': _jnp.float32, 'conv_w': _jnp.float32, 'conv_b': _jnp.float32, 'sink': _jnp.float32, 'g_out_conv': _jnp.float32, 'g_out_attn': _jnp.float32, 'w_out': _jnp.float32, 'w_mlp1': _jnp.float32, 'w_mlp2': _jnp.float32, 'g_final': _jnp.float32}
MOMENT_SCALE = {'c_ctx': 4.523967e-02, 'w_ada': 6.347592e-02, 'b_ada': 1.165453e-01, 'g_norm1': 4.452137e-02, 'g_norm2': 5.059192e-02, 'w_in': 3.577617e-02, 'conv_w': 3.444161e-02, 'conv_b': 3.073699e-02, 'sink': 5.495905e-04, 'g_out_conv': 3.333736e-02, 'g_out_attn': 4.602689e-02, 'w_out': 4.061846e-02, 'w_mlp1': 2.633192e-02, 'w_mlp2': 5.043190e-02, 'g_final': 3.231317e+01}


def _to_microbatches(a, axis):
    t = _jnp.moveaxis(a, axis, 0)
    t = t.reshape((N_MICROBATCH, t.shape[0] // N_MICROBATCH) + t.shape[1:])
    return _jnp.moveaxis(t, 1, axis + 1)


def setup_inputs(seed: int = 0) -> dict:
    inp = _fwd_setup_inputs(seed)
    key = _jax.random.fold_in(_jax.random.key(seed), 7919)
    shape, _ = _output_shape()
    out = dict(inp)
    out["loss_target"] = _jax.random.normal(_jax.random.fold_in(key, 0), shape, _jnp.float32)
    for i, name in enumerate(TWIN_WEIGHTS):
        w = inp[name].astype(_jnp.float32)
        if MOMENT_SCALE is None:
            s = _jnp.sqrt(_jnp.mean(_jnp.square(w)) + 1e-30)
        else:
            s = MOMENT_SCALE[name]
        km, kv = _jax.random.split(_jax.random.fold_in(key, i + 1))
        out[name] = w
        out["m_" + name] = s * _jax.random.normal(km, w.shape, _jnp.float32)
        out["v_" + name] = (s * s) * _jax.random.uniform(kv, w.shape, _jnp.float32, 0.5, 1.5)
    if N_MICROBATCH > 1:
        for name, axis in PER_EXAMPLE_BATCH_AXIS.items():
            out[name] = _to_microbatches(out[name], axis)
    return {'x': out['x'], 'c': out['c'], 'ctx': out['ctx'], 'c_ctx': out['c_ctx'], 'w_ada': out['w_ada'], 'b_ada': out['b_ada'], 'g_norm1': out['g_norm1'], 'g_norm2': out['g_norm2'], 'w_in': out['w_in'], 'conv_w': out['conv_w'], 'conv_b': out['conv_b'], 'sink': out['sink'], 'g_out_conv': out['g_out_conv'], 'g_out_attn': out['g_out_attn'], 'w_out': out['w_out'], 'w_mlp1': out['w_mlp1'], 'w_mlp2': out['w_mlp2'], 'g_final': out['g_final'], 'loss_target': out['loss_target'], 'm_c_ctx': out['m_c_ctx'], 'm_w_ada': out['m_w_ada'], 'm_b_ada': out['m_b_ada'], 'm_g_norm1': out['m_g_norm1'], 'm_g_norm2': out['m_g_norm2'], 'm_w_in': out['m_w_in'], 'm_conv_w': out['m_conv_w'], 'm_conv_b': out['m_conv_b'], 'm_sink': out['m_sink'], 'm_g_out_conv': out['m_g_out_conv'], 'm_g_out_attn': out['m_g_out_attn'], 'm_w_out': out['m_w_out'], 'm_w_mlp1': out['m_w_mlp1'], 'm_w_mlp2': out['m_w_mlp2'], 'm_g_final': out['m_g_final'], 'v_c_ctx': out['v_c_ctx'], 'v_w_ada': out['v_w_ada'], 'v_b_ada': out['v_b_ada'], 'v_g_norm1': out['v_g_norm1'], 'v_g_norm2': out['v_g_norm2'], 'v_w_in': out['v_w_in'], 'v_conv_w': out['v_conv_w'], 'v_conv_b': out['v_conv_b'], 'v_sink': out['v_sink'], 'v_g_out_conv': out['v_g_out_conv'], 'v_g_out_attn': out['v_g_out_attn'], 'v_w_out': out['v_w_out'], 'v_w_mlp1': out['v_w_mlp1'], 'v_w_mlp2': out['v_w_mlp2'], 'v_g_final': out['v_g_final']}


def _loss(weights, diff, rest, loss_target):
    with _jax.named_scope("forward"):
        args = {**rest, TWIN_DIFF_INPUT: diff, **{k: w.astype(_WEIGHT_DTYPES[k]) for k, w in weights.items()}}
        y = _forward(args)
    with _jax.named_scope("loss_head"):
        err = _jnp.square(y.astype(_jnp.float32) - loss_target)
        return 0.5 * _jnp.sum(_jnp.mean(err, axis=-1)) if err.ndim else 0.5 * err


def _adamw(w, g, m, v):
    m = ADAM_B1 * m + (1.0 - ADAM_B1) * g
    v = ADAM_B2 * v + (1.0 - ADAM_B2) * _jnp.square(g)
    m_hat = m / (1.0 - ADAM_B1 ** ADAM_STEP)
    v_hat = v / (1.0 - ADAM_B2 ** ADAM_STEP)
    delta = -ADAM_LR * (m_hat / (_jnp.sqrt(v_hat) + ADAM_EPS) + ADAM_WD * w)
    return delta, m, v


def reference(x, c, ctx, c_ctx, w_ada, b_ada, g_norm1, g_norm2, w_in, conv_w, conv_b, sink, g_out_conv, g_out_attn, w_out, w_mlp1, w_mlp2, g_final, loss_target, m_c_ctx, m_w_ada, m_b_ada, m_g_norm1, m_g_norm2, m_w_in, m_conv_w, m_conv_b, m_sink, m_g_out_conv, m_g_out_attn, m_w_out, m_w_mlp1, m_w_mlp2, m_g_final, v_c_ctx, v_w_ada, v_b_ada, v_g_norm1, v_g_norm2, v_w_in, v_conv_w, v_conv_b, v_sink, v_g_out_conv, v_g_out_attn, v_w_out, v_w_mlp1, v_w_mlp2, v_g_final):
    given = dict(x=x, c=c, ctx=ctx, c_ctx=c_ctx, w_ada=w_ada, b_ada=b_ada, g_norm1=g_norm1, g_norm2=g_norm2, w_in=w_in, conv_w=conv_w, conv_b=conv_b, sink=sink, g_out_conv=g_out_conv, g_out_attn=g_out_attn, w_out=w_out, w_mlp1=w_mlp1, w_mlp2=w_mlp2, g_final=g_final, loss_target=loss_target, m_c_ctx=m_c_ctx, m_w_ada=m_w_ada, m_b_ada=m_b_ada, m_g_norm1=m_g_norm1, m_g_norm2=m_g_norm2, m_w_in=m_w_in, m_conv_w=m_conv_w, m_conv_b=m_conv_b, m_sink=m_sink, m_g_out_conv=m_g_out_conv, m_g_out_attn=m_g_out_attn, m_w_out=m_w_out, m_w_mlp1=m_w_mlp1, m_w_mlp2=m_w_mlp2, m_g_final=m_g_final, v_c_ctx=v_c_ctx, v_w_ada=v_w_ada, v_b_ada=v_b_ada, v_g_norm1=v_g_norm1, v_g_norm2=v_g_norm2, v_w_in=v_w_in, v_conv_w=v_conv_w, v_conv_b=v_conv_b, v_sink=v_sink, v_g_out_conv=v_g_out_conv, v_g_out_attn=v_g_out_attn, v_w_out=v_w_out, v_w_mlp1=v_w_mlp1, v_w_mlp2=v_w_mlp2, v_g_final=v_g_final)
    weights = {n: given[n] for n in TWIN_WEIGHTS}
    shared = {n: given[n] for n in SHARED_INPUTS}
    per_example = {n: given[n] for n in ['x', 'c', 'ctx']}
    grad_fn = _jax.value_and_grad(_loss, argnums=(0, 1))

    def one_microbatch(ex, loss_target):
        ex = dict(ex)
        diff = ex.pop(TWIN_DIFF_INPUT)
        return grad_fn(weights, diff, {**shared, **ex}, loss_target)

    if N_MICROBATCH == 1:
        loss, (grad_w, grad_x) = one_microbatch(per_example, given["loss_target"])
    else:
        def body(carry, xs):
            loss_sum, grad_sum = carry
            l_k, (gw_k, gx_k) = one_microbatch(xs[0], xs[1])
            with _jax.named_scope("update"):
                return (loss_sum + l_k, _jax.tree.map(_jnp.add, grad_sum, gw_k)), gx_k

        init = (_jnp.zeros((), _jnp.float32), _jax.tree.map(_jnp.zeros_like, weights))
        (loss, grad_w), grad_x = _jax.lax.scan(body, init, (per_example, given["loss_target"]))
    with _jax.named_scope("update"):
        delta_w, new_m, new_v = {}, {}, {}
        for n in TWIN_WEIGHTS:
            delta_w[n], new_m[n], new_v[n] = _adamw(weights[n], grad_w[n], given["m_" + n], given["v_" + n])
    return (loss, grad_x, *[grad_w[n] for n in TWIN_WEIGHTS], *[delta_w[n] for n in TWIN_WEIGHTS],
            *[new_m[n] for n in TWIN_WEIGHTS], *[new_v[n] for n in TWIN_WEIGHTS])
```

```python
import functools

import jax
import jax.numpy as jnp
from jax import lax
from jax.experimental import pallas as pl
from jax.experimental.pallas import tpu as pltpu

F32 = jnp.float32
I32 = jnp.int32
MXU_DTYPE = jnp.bfloat16
EPS = 1e-6
HEAD_DIM = 64
N_KV = 4
WINDOW = 128
QBLK = 128
LANES = 128
GRID_W = 64
ROPE_THETA = 10000.0
NEG_INF = -1e30
N_MOD = 6
HALO = 8
ADAM_LR, ADAM_B1, ADAM_B2, ADAM_EPS, ADAM_WD, ADAM_STEP = 0.001, 0.9, 0.999, 1e-08, 0.01, 10
V7X_VMEM_BYTES = 64 * 1024 * 1024
VMEM_LIMIT = V7X_VMEM_BYTES * 3 // 4
MESH = pl.DeviceIdType.MESH
ANY = pl.BlockSpec(memory_space=pl.ANY)


def _params():
    return pltpu.CompilerParams(vmem_limit_bytes=VMEM_LIMIT)


def _pick(n, cands):
    for c in cands:
        if n % c == 0:
            return c
    raise ValueError(f"no tile of {cands} divides {n}")


def _mx(v):
    return v.astype(MXU_DTYPE)


def _dot(a, b):
    return jnp.dot(_mx(a), _mx(b), preferred_element_type=F32)


def _dot_nt(a, b):
    return lax.dot_general(_mx(a), _mx(b), (((1,), (1,)), ((), ())), preferred_element_type=F32)


def _dot_tn(a, b):
    return lax.dot_general(_mx(a), _mx(b), (((0,), (0,)), ((), ())), preferred_element_type=F32)


def _silu(v):
    return v / (1.0 + jnp.exp(-v))


def _mm_nn(a, b3, *, out_dtypes, epilogue, name):
    m, k = a.shape
    s, _, ns = b3.shape
    tm = _pick(m, (768, 512, 640, 256, 128))
    tn = _pick(ns, (1024, 1152, 640, 512, 256, 128))
    tk = _pick(k, (2048, 1024, 512))
    nbs, nk = ns // tn, k // tk
    n_out = len(out_dtypes)

    def body(a_ref, b_ref, *rest):
        outs, acc = rest[:n_out], rest[n_out]
        kk = pl.program_id(2)

        @pl.when(kk == 0)
        def _():
            acc[...] = jnp.zeros_like(acc)

        acc[...] += jnp.dot(a_ref[...], b_ref[...], preferred_element_type=F32)

        @pl.when(kk == nk - 1)
        def _():
            for o, v in zip(outs, epilogue(acc[...])):
                o[...] = v.astype(o.dtype)

    return pl.pallas_call(
        body, grid=(m // tm, s * nbs, nk),
        in_specs=[pl.BlockSpec((tm, tk), lambda i, j, kk: (i, kk)),
                  pl.BlockSpec((None, tk, tn), lambda i, j, kk: (j // nbs, kk, j % nbs))],
        out_specs=[pl.BlockSpec((tm, tn), lambda i, j, kk: (i, j))] * n_out,
        out_shape=[jax.ShapeDtypeStruct((m, s * ns), dt) for dt in out_dtypes],
        scratch_shapes=[pltpu.VMEM((tm, tn), F32)], compiler_params=_params(), name=name)(a, b3)


def _mm_nt(a, b3, *, out_dtype, name, extra=None, epilogue=None):
    m = a.shape[0]
    s, ko, ns = b3.shape
    tm = _pick(m, (768, 512, 640, 256, 128))
    tko = _pick(ko, (1024, 512))
    tn = _pick(ns, (2048, 1152, 1024, 640, 512, 256, 128))
    nbs = ns // tn
    nk = s * nbs

    def body(a_ref, b_ref, *rest):
        x_ref = rest[0] if extra is not None else None
        o_ref, acc = rest[-2], rest[-1]
        kk = pl.program_id(2)

        @pl.when(kk == 0)
        def _():
            acc[...] = jnp.zeros_like(acc)

        acc[...] += lax.dot_general(a_ref[...], b_ref[...], (((1,), (1,)), ((), ())), preferred_element_type=F32)

        @pl.when(kk == nk - 1)
        def _():
            v = acc[...]
            if epilogue is not None:
                v = epilogue(v, x_ref[...])
            o_ref[...] = v.astype(o_ref.dtype)

    in_specs = [pl.BlockSpec((tm, tn), lambda i, j, kk: (i, kk)),
                pl.BlockSpec((None, tko, tn), lambda i, j, kk: (kk // nbs, j, kk % nbs))]
    args = [a, b3]
    if extra is not None:
        in_specs.append(pl.BlockSpec((tm, tko), lambda i, j, kk: (i, j)))
        args.append(extra)
    return pl.pallas_call(
        body, grid=(m // tm, ko // tko, nk), in_specs=in_specs,
        out_specs=pl.BlockSpec((tm, tko), lambda i, j, kk: (i, j)),
        out_shape=jax.ShapeDtypeStruct((m, ko), out_dtype),
        scratch_shapes=[pltpu.VMEM((tm, tko), F32)], compiler_params=_params(), name=name)(*args)


def _mm_tn(a, b, *, shards, name):
    t, k = a.shape
    ns = b.shape[1] // shards
    tt = _pick(t, (768, 512, 640, 256, 128))
    tk = _pick(k, (1024, 512))
    tn = _pick(ns, (1024, 1152, 640, 512, 256, 128))
    nbs, nt = ns // tn, t // tt

    def body(a_ref, b_ref, o_ref, acc):
        tt_i = pl.program_id(2)

        @pl.when(tt_i == 0)
        def _():
            acc[...] = jnp.zeros_like(acc)

        acc[...] += lax.dot_general(a_ref[...], b_ref[...], (((0,), (0,)), ((), ())), preferred_element_type=F32)

        @pl.when(tt_i == nt - 1)
        def _():
            o_ref[...] = acc[...]

    return pl.pallas_call(
        body, grid=(k // tk, shards * nbs, nt),
        in_specs=[pl.BlockSpec((tt, tk), lambda i, j, q: (q, i)),
                  pl.BlockSpec((tt, tn), lambda i, j, q: (q, j))],
        out_specs=pl.BlockSpec((None, tk, tn), lambda i, j, q: (j // nbs, i, j % nbs)),
        out_shape=jax.ShapeDtypeStruct((shards, k, ns), F32),
        scratch_shapes=[pltpu.VMEM((tk, tn), F32)], compiler_params=_params(), name=name)(a, b)


def _row_tile(t, ctx_rows):
    return 256 if t % 256 == 0 and ctx_rows % 256 == 0 else 128


def _mod_spec(d, ncb, idx):
    return pl.BlockSpec((None, None, 1, d), lambda i: (jnp.where(i >= ncb, 1, 0), idx, 0, 0))


def _norm_fwd(x, z, gate_mods, mods, g, *, gate_idx, shift_idx, scale_idx, ctx_rows, name):
    t, d = x.shape
    tr = _row_tile(t, ctx_rows)
    ncb = ctx_rows // tr
    row = pl.BlockSpec((tr, d), lambda i: (i, 0))
    vec = pl.BlockSpec((1, d), lambda i: (0, 0))
    resid = z is not None

    def body(*refs):
        if resid:
            x_ref, z_ref, gt_ref, g_ref, sh_ref, sc_ref, xo_ref, h_ref = refs
            xn = x_ref[...] + gt_ref[...] * z_ref[...]
            xo_ref[...] = xn
        else:
            x_ref, g_ref, sh_ref, sc_ref, h_ref = refs
            xn = x_ref[...]
        r = lax.rsqrt(jnp.mean(xn * xn, axis=-1, keepdims=True) + EPS)
        h_ref[...] = ((xn * r * g_ref[...]) * (1.0 + sc_ref[...]) + sh_ref[...]).astype(h_ref.dtype)

    mspecs = [vec, _mod_spec(d, ncb, shift_idx), _mod_spec(d, ncb, scale_idx)]
    if resid:
        in_specs = [row, row, _mod_spec(d, ncb, gate_idx)] + mspecs
        args = (x, z, gate_mods, g, mods, mods)
        out_specs = [row, row]
        out_shape = [jax.ShapeDtypeStruct((t, d), F32), jax.ShapeDtypeStruct((t, d), MXU_DTYPE)]
    else:
        in_specs = [row] + mspecs
        args = (x, g, mods, mods)
        out_specs = row
        out_shape = jax.ShapeDtypeStruct((t, d), MXU_DTYPE)
    return pl.pallas_call(body, grid=(t // tr,), in_specs=in_specs, out_specs=out_specs, out_shape=out_shape,
                          compiler_params=_params(), name=name)(*args)


R_DSHIFT, R_DSCALE, R_DG, R_DGATE = 0, 2, 4, 5


def _norm_bwd(dx, dh, xin, mods, g, *, scale_idx, ctx_rows, name, prev=None):
    t, d = dx.shape
    tr = _row_tile(t, ctx_rows)
    ncb = ctx_rows // tr
    row = pl.BlockSpec((tr, d), lambda i: (i, 0))
    vec = pl.BlockSpec((1, d), lambda i: (0, 0))
    acc_spec = pl.BlockSpec((8, d), lambda i: (0, 0))
    has_prev = prev is not None

    def body(*refs):
        if has_prev:
            dx_ref, dh_ref, x_ref, g_ref, sc_ref, z_ref, gt_ref, dxo_ref, dz_ref, acc = refs
        else:
            dx_ref, dh_ref, x_ref, g_ref, sc_ref, dxo_ref, acc = refs
        i = pl.program_id(0)

        @pl.when(i == 0)
        def _():
            acc[...] = jnp.zeros_like(acc)

        lat = jnp.where(i >= ncb, 1.0, 0.0)
        x = x_ref[...]
        r = lax.rsqrt(jnp.mean(x * x, axis=-1, keepdims=True) + EPS)
        xhat = x * r
        gv = g_ref[...]
        dhv = dh_ref[...]
        dn = dhv * (1.0 + sc_ref[...])
        dxhat = dn * gv
        dxin = dx_ref[...] + r * (dxhat - xhat * jnp.mean(dxhat * xhat, axis=-1, keepdims=True))
        dxo_ref[...] = dxin
        dshift = jnp.sum(dhv, axis=0, keepdims=True)
        dscale = jnp.sum(dhv * (xhat * gv), axis=0, keepdims=True)
        acc[R_DSHIFT:R_DSHIFT + 1, :] += dshift * (1.0 - lat)
        acc[R_DSHIFT + 1:R_DSHIFT + 2, :] += dshift * lat
        acc[R_DSCALE:R_DSCALE + 1, :] += dscale * (1.0 - lat)
        acc[R_DSCALE + 1:R_DSCALE + 2, :] += dscale * lat
        acc[R_DG:R_DG + 1, :] += jnp.sum(dn * xhat, axis=0, keepdims=True)
        if has_prev:
            dz_ref[...] = (dxin * gt_ref[...]).astype(dz_ref.dtype)
            dgate = jnp.sum(dxin * z_ref[...], axis=0, keepdims=True)
            acc[R_DGATE:R_DGATE + 1, :] += dgate * (1.0 - lat)
            acc[R_DGATE + 1:R_DGATE + 2, :] += dgate * lat

    in_specs = [row, row, row, vec, _mod_spec(d, ncb, scale_idx)]
    args = [dx, dh, xin, g, mods]
    out_specs = [row]
    out_shape = [jax.ShapeDtypeStruct((t, d), F32)]
    if has_prev:
        z, gate_mods, gate_idx = prev
        in_specs += [row, _mod_spec(d, ncb, gate_idx)]
        args += [z, gate_mods]
        out_specs.append(row)
        out_shape.append(jax.ShapeDtypeStruct((t, d), MXU_DTYPE))
    out_specs.append(acc_spec)
    out_shape.append(jax.ShapeDtypeStruct((8, d), F32))
    return pl.pallas_call(body, grid=(t // tr,), in_specs=in_specs, out_specs=out_specs, out_shape=out_shape,
                          compiler_params=_params(), name=name)(*args)


R_FGATE, R_FG, R_FLOSS = 0, 2, 3


def _final(x1, o, gate_mods, g_final, tgt, *, ctx_rows, name):
    t, d = x1.shape
    tr = _row_tile(t, ctx_rows)
    ncb, nb = ctx_rows // tr, t // tr
    row = pl.BlockSpec((tr, d), lambda i: (i, 0))
    vec = pl.BlockSpec((1, d), lambda i: (0, 0))

    def body(x_ref, o_ref, gt_ref, g_ref, t_ref, dx_ref, do_ref, acc, lsum):
        i = pl.program_id(0)

        @pl.when(i == 0)
        def _():
            acc[...] = jnp.zeros_like(acc)
            lsum[...] = jnp.zeros_like(lsum)

        lat = jnp.where(i >= ncb, 1.0, 0.0)
        gt = gt_ref[...]
        ov = o_ref[...]
        x = x_ref[...] + gt * ov
        r = lax.rsqrt(jnp.mean(x * x, axis=-1, keepdims=True) + EPS)
        xhat = x * r
        gv = g_ref[...]
        err = (xhat * gv - t_ref[...]) * lat
        dy = err / d
        dxhat = dy * gv
        dxv = r * (dxhat - xhat * jnp.mean(dxhat * xhat, axis=-1, keepdims=True))
        dx_ref[...] = dxv
        do_ref[...] = (dxv * gt).astype(do_ref.dtype)
        dgate = jnp.sum(dxv * ov, axis=0, keepdims=True)
        acc[R_FGATE:R_FGATE + 1, :] += dgate * (1.0 - lat)
        acc[R_FGATE + 1:R_FGATE + 2, :] += dgate * lat
        acc[R_FG:R_FG + 1, :] += jnp.sum(dy * xhat, axis=0, keepdims=True)
        lsum[...] += jnp.sum(err * err, axis=0, keepdims=True)

        @pl.when(i == nb - 1)
        def _():
            total = (0.5 / d) * jnp.sum(lsum[...], axis=-1, keepdims=True)
            acc[R_FLOSS:R_FLOSS + 1, :] = jnp.broadcast_to(total, (1, d))

    return pl.pallas_call(
        body, grid=(nb,),
        in_specs=[row, row, _mod_spec(d, ncb, 5), vec, pl.BlockSpec((tr, d), lambda i: (jnp.maximum(i - ncb, 0), 0))],
        out_specs=[row, row, pl.BlockSpec((8, d), lambda i: (0, 0))],
        out_shape=[jax.ShapeDtypeStruct((t, d), F32), jax.ShapeDtypeStruct((t, d), MXU_DTYPE),
                   jax.ShapeDtypeStruct((8, d), F32)],
        scratch_shapes=[pltpu.VMEM((1, d), F32)], compiler_params=_params(), name=name)(x1, o, gate_mods, g_final, tgt)


def _rope_tables(s, ctx_rows):
    rows = s // GRID_W
    row_pos = jnp.repeat(jnp.arange(rows, dtype=F32), GRID_W)
    col_pos = jnp.tile(jnp.arange(GRID_W, dtype=F32), rows)
    quarter = HEAD_DIM // 4
    inv = ROPE_THETA ** (-jnp.arange(0, 2 * quarter, 2, dtype=F32) / (2 * quarter))
    ang_r, ang_c = row_pos[:, None] * inv[None, :], col_pos[:, None] * inv[None, :]
    cr, sr, cc, sc = jnp.cos(ang_r), jnp.sin(ang_r), jnp.cos(ang_c), jnp.sin(ang_c)
    zero = jnp.zeros_like(sr)
    cos = jnp.concatenate([cr, cr, cc, cc], axis=1)
    sa = jnp.concatenate([zero, sr, zero, sc], axis=1)
    sb = jnp.concatenate([-sr, zero, -sc, zero], axis=1)

    def full(tab, fill):
        tab = jnp.tile(tab, (1, LANES // HEAD_DIM))
        return jnp.concatenate([jnp.full((ctx_rows, LANES), fill, F32), tab], axis=0)

    return full(cos, 1.0), full(sa, 0.0), full(sb, 0.0)


def _rope_apply(x, cos, sa, sb, transpose):
    n = x.shape[1] // LANES
    cos, sa, sb = (jnp.tile(v, (1, n)) for v in (cos, sa, sb))
    quarter = HEAD_DIM // 4
    width = x.shape[1]
    if transpose:
        return x * cos + pltpu.roll(x * sa, width - quarter, 1) + pltpu.roll(x * sb, quarter, 1)
    return x * cos + pltpu.roll(x, quarter, 1) * sa + pltpu.roll(x, width - quarter, 1) * sb


def _rope_fwd(p, tabs, *, da, kw, kv_start, name):
    t = p.shape[0]
    tr = _pick(t, (256, 128))

    def body(q_ref, k_ref, c_ref, a_ref, b_ref, o_ref):
        cos, sa, sb = c_ref[...], a_ref[...], b_ref[...]
        o_ref[:, :da] = _rope_apply(q_ref[...], cos, sa, sb, False).astype(o_ref.dtype)
        o_ref[:, da:] = _rope_apply(k_ref[...], cos, sa, sb, False).astype(o_ref.dtype)

    tab = pl.BlockSpec((tr, LANES), lambda i: (i, 0))
    return pl.pallas_call(
        body, grid=(t // tr,),
        in_specs=[pl.BlockSpec((tr, da), lambda i: (i, (kv_start - da) // da)),
                  pl.BlockSpec((tr, kw), lambda i: (i, kv_start // kw)), tab, tab, tab],
        out_specs=pl.BlockSpec((tr, da + kw), lambda i: (i, 0)),
        out_shape=jax.ShapeDtypeStruct((t, da + kw), MXU_DTYPE), compiler_params=_params(), name=name)(p, p, *tabs)


def _rope_bwd(dq, dk, dv, tabs, *, name):
    t, da = dq.shape
    kw = dk.shape[1]
    tr = _pick(t, (256, 128))

    def body(q_ref, k_ref, v_ref, c_ref, a_ref, b_ref, o_ref):
        cos, sa, sb = c_ref[...], a_ref[...], b_ref[...]
        o_ref[:, :da] = _rope_apply(q_ref[...], cos, sa, sb, True).astype(o_ref.dtype)
        o_ref[:, da:da + kw] = _rope_apply(k_ref[...], cos, sa, sb, True).astype(o_ref.dtype)
        o_ref[:, da + kw:] = v_ref[...].astype(o_ref.dtype)

    tab = pl.BlockSpec((tr, LANES), lambda i: (i, 0))
    return pl.pallas_call(
        body, grid=(t // tr,),
        in_specs=[pl.BlockSpec((tr, da), lambda i: (i, 0)), pl.BlockSpec((tr, kw), lambda i: (i, 0)),
                  pl.BlockSpec((tr, kw), lambda i: (i, 0)), tab, tab, tab],
        out_specs=pl.BlockSpec((tr, da + 2 * kw), lambda i: (i, 0)),
        out_shape=jax.ShapeDtypeStruct((t, da + 2 * kw), MXU_DTYPE), compiler_params=_params(), name=name)(dq, dk, dv, *tabs)


def _attn_specs(t, ctx_rows, da, kv_start):
    nblk = t // QBLK
    kcb = da // LANES
    vcb = (kv_start + N_KV * HEAD_DIM) // LANES

    def clip(i):
        return jnp.clip(i, 0, nblk - 1)

    def kv(base):
        return [pl.BlockSpec((QBLK, LANES), lambda b, i: (clip(i - 1), base + b)),
                pl.BlockSpec((QBLK, LANES), lambda b, i: (i, base + b)),
                pl.BlockSpec((QBLK, LANES), lambda b, i: (clip(i + 1), base + b)),
                pl.BlockSpec((ctx_rows, LANES), lambda b, i: (0, base + b))]

    return kv(kcb), kv(vcb)


def _attn_mask(i, t, ctx_rows):
    nk = 3 * QBLK + ctx_rows
    rr = lax.broadcasted_iota(I32, (QBLK, nk), 0)
    cc = lax.broadcasted_iota(I32, (QBLK, nk), 1)
    keyrow = (i - 1) * QBLK + cc
    first_key = jnp.where(i * QBLK >= ctx_rows, ctx_rows, t)
    local = (cc < 3 * QBLK) & (keyrow >= first_key) & (keyrow < t) & (jnp.abs(cc - QBLK - rr) <= WINDOW)
    return local | (cc >= 3 * QBLK)


def _head_copies(refs, half):
    x = jnp.concatenate([r[...].astype(F32) for r in refs], axis=0)
    lane = lax.broadcasted_iota(I32, x.shape, 1)
    low = lane < HEAD_DIM
    xs = jnp.where(low if half == 0 else ~low, x, 0.0)
    both = xs + pltpu.roll(xs, HEAD_DIM, 1)
    return _mx(jnp.where(low, both, 0.0)), _mx(jnp.where(low, 0.0, both)), _mx(both)


def _softmax(qblk, kx, valid, snk):
    s = _dot_nt(qblk, kx) * (HEAD_DIM ** -0.5)
    s = jnp.where(valid, s, NEG_INF)
    m = jnp.maximum(jnp.max(s, axis=-1, keepdims=True), snk)
    ex = jnp.exp(s - m)
    es = jnp.exp(snk - m)
    den = jnp.sum(ex, axis=-1, keepdims=True) + es
    return ex / den, es / den


def _attn_fwd(qk, p, sink, *, ctx_rows, da, kv_start, name):
    t = qk.shape[0]
    group = da // HEAD_DIM // N_KV
    qw = 2 * group * HEAD_DIM
    kspecs, vspecs = _attn_specs(t, ctx_rows, da, kv_start)

    def body(sink_ref, q_ref, *rest):
        k_refs, v_refs, o_ref = rest[0:4], rest[4:8], rest[8]
        b, i = pl.program_id(0), pl.program_id(1)
        valid = _attn_mask(i, t, ctx_rows)
        for half in range(2):
            k_lo, k_hi, _ = _head_copies(k_refs, half)
            v_lo, v_hi, _ = _head_copies(v_refs, half)
            for qb in range(group // 2):
                c0 = (half * (group // 2) + qb) * LANES
                qblk = q_ref[:, c0:c0 + LANES]
                out = jnp.zeros((QBLK, LANES), F32)
                for e, (kx, vx) in enumerate(((k_lo, v_lo), (k_hi, v_hi))):
                    pr, _ = _softmax(qblk, kx, valid, sink_ref[(2 * b + half) * group + 2 * qb + e])
                    out = out + _dot(pr, vx)
                o_ref[:, c0:c0 + LANES] = out

    return pl.pallas_call(
        body, grid=(N_KV // 2, t // QBLK),
        in_specs=[pl.BlockSpec(memory_space=pltpu.SMEM), pl.BlockSpec((QBLK, qw), lambda b, i: (i, b))] + kspecs + vspecs,
        out_specs=pl.BlockSpec((QBLK, qw), lambda b, i: (i, b)),
        out_shape=jax.ShapeDtypeStruct((t, da), F32), compiler_params=_params(), name=name)(
            sink, qk, qk, qk, qk, qk, p, p, p, p)


def _attn_bwd(qk, p, dao, sink, *, ctx_rows, da, kv_start, name):
    t = qk.shape[0]
    nblk = t // QBLK
    group = da // HEAD_DIM // N_KV
    qw = 2 * group * HEAD_DIM
    kw = N_KV * HEAD_DIM
    nloc = 3 * QBLK
    kspecs, vspecs = _attn_specs(t, ctx_rows, da, kv_start)

    def body(sink_ref, q_ref, do_ref, *rest):
        k_refs, v_refs = rest[0:4], rest[4:8]
        dq_ref, dkl_ref, dvl_ref, dkc_ref, dvc_ref, ds_ref = rest[8:14]
        b, i = pl.program_id(0), pl.program_id(1)

        @pl.when((b == 0) & (i == 0))
        def _():
            ds_ref[...] = jnp.zeros_like(ds_ref)

        @pl.when(i == 0)
        def _():
            dkc_ref[...] = jnp.zeros_like(dkc_ref)
            dvc_ref[...] = jnp.zeros_like(dvc_ref)

        valid = _attn_mask(i, t, ctx_rows)
        lane_q = lax.broadcasted_iota(I32, (QBLK, LANES), 1)
        lane_k = lax.broadcasted_iota(I32, (nloc + ctx_rows, LANES), 1)
        srow = lax.broadcasted_iota(I32, ds_ref.shape, 0)
        slane = lax.broadcasted_iota(I32, ds_ref.shape, 1)
        dk_acc = jnp.zeros((nloc + ctx_rows, LANES), F32)
        dv_acc = jnp.zeros((nloc + ctx_rows, LANES), F32)
        for half in range(2):
            k_lo, k_hi, _ = _head_copies(k_refs, half)
            _, _, v_both = _head_copies(v_refs, half)
            xk = jnp.zeros((nloc + ctx_rows, LANES), F32)
            xv = jnp.zeros((nloc + ctx_rows, LANES), F32)
            for qb in range(group // 2):
                c0 = (half * (group // 2) + qb) * LANES
                qblk = q_ref[:, c0:c0 + LANES]
                doblk = do_ref[:, c0:c0 + LANES]
                dq = jnp.zeros((QBLK, LANES), F32)
                for e, kx in enumerate((k_lo, k_hi)):
                    head = (2 * b + half) * group + 2 * qb + e
                    mine = (lane_q < HEAD_DIM) if e == 0 else (lane_q >= HEAD_DIM)
                    pr, psink = _softmax(qblk, kx, valid, sink_ref[head])
                    do_e = jnp.where(mine, doblk, 0.0)
                    dp = _dot_nt(do_e, v_both)
                    dsum = jnp.sum(pr * dp, axis=-1, keepdims=True)
                    dsc = pr * (dp - dsum) * (HEAD_DIM ** -0.5)
                    dq = dq + _dot(dsc, kx)
                    xk = xk + _dot_tn(dsc, jnp.where(mine, qblk.astype(F32), 0.0))
                    xv = xv + _dot_tn(pr, do_e)
                    ds_ref[...] += jnp.where((srow == 0) & (slane == head), -jnp.sum(psink * dsum), 0.0)
                dq_ref[:, c0:c0 + LANES] = dq
            here = (lane_k < HEAD_DIM) if half == 0 else (lane_k >= HEAD_DIM)
            dk_acc = dk_acc + jnp.where(here, xk + pltpu.roll(xk, HEAD_DIM, 1), 0.0)
            dv_acc = dv_acc + jnp.where(here, xv + pltpu.roll(xv, HEAD_DIM, 1), 0.0)
        dkl_ref[...] = dk_acc[:nloc]
        dvl_ref[...] = dv_acc[:nloc]
        dkc_ref[...] += dk_acc[nloc:]
        dvc_ref[...] += dv_acc[nloc:]

    qspec = pl.BlockSpec((QBLK, qw), lambda b, i: (i, b))
    loc = pl.BlockSpec((None, nloc, LANES), lambda b, i: (i, 0, b))
    cspec = pl.BlockSpec((ctx_rows, LANES), lambda b, i: (0, b))
    return pl.pallas_call(
        body, grid=(N_KV // 2, nblk),
        in_specs=[pl.BlockSpec(memory_space=pltpu.SMEM), qspec, qspec] + kspecs + vspecs,
        out_specs=[qspec, loc, loc, cspec, cspec, pl.BlockSpec((8, LANES), lambda b, i: (0, 0))],
        out_shape=[jax.ShapeDtypeStruct((t, da), F32), jax.ShapeDtypeStruct((nblk, nloc, kw), F32),
                   jax.ShapeDtypeStruct((nblk, nloc, kw), F32), jax.ShapeDtypeStruct((ctx_rows, kw), F32),
                   jax.ShapeDtypeStruct((ctx_rows, kw), F32), jax.ShapeDtypeStruct((8, LANES), F32)],
        compiler_params=_params(), name=name)(sink, qk, dao, qk, qk, qk, qk, p, p, p, p)


def _kv_reduce(dkl, dvl, dkc, dvc, *, ctx_rows, name):
    nblk, _, kw = dkl.shape
    t = nblk * QBLK
    ncb = ctx_rows // QBLK

    def clip(i):
        return jnp.clip(i, 0, nblk - 1)

    def body(ka, kb, kc, kx, va, vb, vc, vx, dk_ref, dv_ref):
        m = pl.program_id(0)
        lat = m >= ncb
        wa = jnp.where(lat & (m + 1 <= nblk - 1), 1.0, 0.0)
        wc = jnp.where(lat & (m - 1 >= ncb), 1.0, 0.0)
        wl = jnp.where(lat, 1.0, 0.0)
        dk_ref[...] = wl * (kb[...] + wa * ka[...] + wc * kc[...]) + (1.0 - wl) * kx[...]
        dv_ref[...] = wl * (vb[...] + wa * va[...] + wc * vc[...]) + (1.0 - wl) * vx[...]

    slots = [pl.BlockSpec((None, QBLK, kw), lambda m: (clip(m + 1), 0, 0)),
             pl.BlockSpec((None, QBLK, kw), lambda m: (m, 1, 0)),
             pl.BlockSpec((None, QBLK, kw), lambda m: (clip(m - 1), 2, 0))]
    cspec = pl.BlockSpec((QBLK, kw), lambda m: (jnp.clip(m, 0, ncb - 1), 0))
    out = pl.BlockSpec((QBLK, kw), lambda m: (m, 0))
    return pl.pallas_call(
        body, grid=(nblk,), in_specs=slots + [cspec] + slots + [cspec], out_specs=[out, out],
        out_shape=[jax.ShapeDtypeStruct((t, kw), F32)] * 2, compiler_params=_params(), name=name)(
            dkl, dkl, dkl, dkc, dvl, dvl, dvl, dvc)


MERGE_ROWS = 128


def _halo_specs(t, c, col):
    hb = MERGE_ROWS // HALO
    return [pl.BlockSpec((HALO, c), lambda i: (jnp.maximum(i * hb - 1, 0), col)),
            pl.BlockSpec((MERGE_ROWS, c), lambda i: (i, col)),
            pl.BlockSpec((HALO, c), lambda i: (jnp.minimum((i + 1) * hb, t // HALO - 1), col))]


def _ext(refs):
    return jnp.concatenate([r[...] for r in refs], axis=0)


def _conv_ext(cg, hh, w_ref, b_ref, i, t, ctx_rows):
    n = cg.shape[0]
    u = cg * hh
    row = i * MERGE_ROWS - HALO + lax.broadcasted_iota(I32, u.shape, 0)
    first = (row == 0) | (row == ctx_rows)
    last = (row == ctx_rows - 1) | (row == t - 1)
    u_dn = jnp.where(first, 0.0, pltpu.roll(u, 1, 0))
    u_up = jnp.where(last, 0.0, pltpu.roll(u, n - 1, 0))
    cv = w_ref[0:1, :] * u_dn + w_ref[1:2, :] * u + w_ref[2:3, :] * u_up + b_ref[...]
    return u, u_dn, u_up, cv, first, last


def _merge_fwd(p, ao, conv_w, conv_b, g_oc, g_oa, *, ctx_rows, name):
    t = p.shape[0]
    c = ao.shape[1]
    main = slice(HALO, HALO + MERGE_ROWS)

    def body(bg_ref, cgp, cgm, cgn, hhp, hhm, hhn, ao_ref, w_ref, b_ref, gc_ref, ga_ref, o_ref):
        i = pl.program_id(0)
        _, _, _, cv, _, _ = _conv_ext(_ext((cgp, cgm, cgn)), _ext((hhp, hhm, hhn)), w_ref, b_ref, i, t, ctx_rows)
        co = bg_ref[...] * cv[main]
        rc = lax.rsqrt(jnp.mean(co * co, axis=-1, keepdims=True) + EPS)
        o_ref[:, :c] = (co * rc * gc_ref[...]).astype(o_ref.dtype)
        av = ao_ref[...]
        ra = lax.rsqrt(jnp.mean(av * av, axis=-1, keepdims=True) + EPS)
        o_ref[:, c:] = (av * ra * ga_ref[...]).astype(o_ref.dtype)

    vec = pl.BlockSpec((1, c), lambda i: (0, 0))
    return pl.pallas_call(
        body, grid=(t // MERGE_ROWS,),
        in_specs=[pl.BlockSpec((MERGE_ROWS, c), lambda i: (i, 0))] + _halo_specs(t, c, 1) + _halo_specs(t, c, 2)
        + [pl.BlockSpec((MERGE_ROWS, c), lambda i: (i, 0)), pl.BlockSpec((3, c), lambda i: (0, 0)), vec, vec, vec],
        out_specs=pl.BlockSpec((MERGE_ROWS, 2 * c), lambda i: (i, 0)),
        out_shape=jax.ShapeDtypeStruct((t, 2 * c), MXU_DTYPE), compiler_params=_params(), name=name)(
            p, p, p, p, p, p, p, ao, conv_w, conv_b, g_oc, g_oa)


R_DGOC, R_DGOA, R_DCB, R_DCW = 0, 1, 2, 3


def _merge_bwd(dmg, p, ao, conv_w, conv_b, g_oc, g_oa, *, ctx_rows, name):
    t = p.shape[0]
    c = ao.shape[1]
    main = slice(HALO, HALO + MERGE_ROWS)

    def body(dyp, dym, dyn, dya_ref, bgp, bgm, bgn, cgp, cgm, cgn, hhp, hhm, hhn, ao_ref, w_ref, b_ref, gc_ref, ga_ref,
             dp_ref, dao_ref, acc):
        i = pl.program_id(0)

        @pl.when(i == 0)
        def _():
            acc[...] = jnp.zeros_like(acc)

        bg, cg, hh = _ext((bgp, bgm, bgn)), _ext((cgp, cgm, cgn)), _ext((hhp, hhm, hhn))
        n = bg.shape[0]
        u, u_dn, u_up, cv, first, last = _conv_ext(cg, hh, w_ref, b_ref, i, t, ctx_rows)
        co = bg * cv
        rc = lax.rsqrt(jnp.mean(co * co, axis=-1, keepdims=True) + EPS)
        cohat = co * rc
        dyc = _ext((dyp, dym, dyn))
        t1 = dyc * gc_ref[...]
        dco = rc * (t1 - cohat * jnp.mean(t1 * cohat, axis=-1, keepdims=True))
        dcv = dco * bg
        dcv_next = jnp.where(last, 0.0, pltpu.roll(dcv, n - 1, 0))
        dcv_prev = jnp.where(first, 0.0, pltpu.roll(dcv, 1, 0))
        du = w_ref[1:2, :] * dcv + w_ref[0:1, :] * dcv_next + w_ref[2:3, :] * dcv_prev
        dp_ref[:, :c] = (dco * cv)[main].astype(dp_ref.dtype)
        dp_ref[:, c:2 * c] = (du * hh)[main].astype(dp_ref.dtype)
        dp_ref[:, 2 * c:] = (du * cg)[main].astype(dp_ref.dtype)
        dcv_m = dcv[main]
        acc[R_DGOC:R_DGOC + 1, :] += jnp.sum((dyc * cohat)[main], axis=0, keepdims=True)
        acc[R_DCB:R_DCB + 1, :] += jnp.sum(dcv_m, axis=0, keepdims=True)
        acc[R_DCW:R_DCW + 1, :] += jnp.sum(dcv_m * u_dn[main], axis=0, keepdims=True)
        acc[R_DCW + 1:R_DCW + 2, :] += jnp.sum(dcv_m * u[main], axis=0, keepdims=True)
        acc[R_DCW + 2:R_DCW + 3, :] += jnp.sum(dcv_m * u_up[main], axis=0, keepdims=True)
        av = ao_ref[...]
        ra = lax.rsqrt(jnp.mean(av * av, axis=-1, keepdims=True) + EPS)
        ahat = av * ra
        dya = dya_ref[...]
        t2 = dya * ga_ref[...]
        dao_ref[...] = ra * (t2 - ahat * jnp.mean(t2 * ahat, axis=-1, keepdims=True))
        acc[R_DGOA:R_DGOA + 1, :] += jnp.sum(dya * ahat, axis=0, keepdims=True)

    vec = pl.BlockSpec((1, c), lambda i: (0, 0))
    tile = pl.BlockSpec((MERGE_ROWS, c), lambda i: (i, 0))
    return pl.pallas_call(
        body, grid=(t // MERGE_ROWS,),
        in_specs=_halo_specs(t, c, 0) + [pl.BlockSpec((MERGE_ROWS, c), lambda i: (i, 1))]
        + _halo_specs(t, c, 0) + _halo_specs(t, c, 1) + _halo_specs(t, c, 2)
        + [tile, pl.BlockSpec((3, c), lambda i: (0, 0)), vec, vec, vec],
        out_specs=[pl.BlockSpec((MERGE_ROWS, 3 * c), lambda i: (i, 0)), tile, pl.BlockSpec((8, c), lambda i: (0, 0))],
        out_shape=[jax.ShapeDtypeStruct((t, 3 * c), MXU_DTYPE), jax.ShapeDtypeStruct((t, c), F32),
                   jax.ShapeDtypeStruct((8, c), F32)],
        compiler_params=_params(), name=name)(dmg, dmg, dmg, dmg, p, p, p, p, p, p, p, p, p, ao, conv_w, conv_b, g_oc, g_oa)


def _local_step(x0, tgt, mods, wts, small, *, ctx_rows):
    t, d = x0.shape
    depth = mods.shape[0]
    c = d // 2
    da = d - c
    kw = N_KV * HEAD_DIM
    kv_start = 3 * c + da
    shards = wts[0][0].shape[0]
    tabs = _rope_tables(t - ctx_rows, ctx_rows)
    kwargs = dict(ctx_rows=ctx_rows)
    akw = dict(ctx_rows=ctx_rows, da=da, kv_start=kv_start)

    saved = []
    xs, z_prev = x0, None
    for l in range(depth):
        w_in, w_out, w1, w2 = wts[l]
        if z_prev is None:
            x_in = xs
            h1 = _norm_fwd(xs, None, None, mods[l], small["g_norm1"][l], gate_idx=None, shift_idx=0, scale_idx=1,
                           name=f"norm1_fwd_{l}", **kwargs)
        else:
            x_in, h1 = _norm_fwd(xs, z_prev, mods[l - 1], mods[l], small["g_norm1"][l], gate_idx=5, shift_idx=0,
                                 scale_idx=1, name=f"norm1_fwd_{l}", **kwargs)
        (p,) = _mm_nn(h1, w_in, out_dtypes=(F32,), epilogue=lambda v: (v,), name=f"in_proj_{l}")
        qk = _rope_fwd(p, tabs, da=da, kw=kw, kv_start=kv_start, name=f"rope_fwd_{l}")
        ao = _attn_fwd(qk, p, small["sink"][l], name=f"attn_fwd_{l}", **akw)
        mg = _merge_fwd(p, ao, small["conv_w"][l], small["conv_b"][l], small["g_out_conv"][l], small["g_out_attn"][l],
                        name=f"merge_fwd_{l}", **kwargs)
        (z,) = _mm_nn(mg, w_out, out_dtypes=(F32,), epilogue=lambda v: (v,), name=f"out_proj_{l}")
        x_mid, h2 = _norm_fwd(x_in, z, mods[l], mods[l], small["g_norm2"][l], gate_idx=2, shift_idx=3, scale_idx=4,
                              name=f"norm2_fwd_{l}", **kwargs)
        a_act, s_act = _mm_nn(h2, w1, out_dtypes=(MXU_DTYPE, MXU_DTYPE),
                              epilogue=lambda v: (v, jnp.square(jnp.maximum(v, 0.0))), name=f"mlp1_{l}")
        (o,) = _mm_nn(s_act, w2, out_dtypes=(F32,), epilogue=lambda v: (v,), name=f"mlp2_{l}")
        saved.append(dict(x_in=x_in, h1=h1, p=p, qk=qk, ao=ao, mg=mg, z=z, x_mid=x_mid, h2=h2, a=a_act, s=s_act, o=o))
        xs, z_prev = x_mid, o

    dx, do, fin = _final(xs, z_prev, mods[depth - 1], small["g_final"], tgt, name="final", **kwargs)

    grads = [None] * depth
    dmods = [[None] * N_MOD for _ in range(depth)]
    sg = {k: [None] * depth for k in ("g_norm1", "g_norm2", "conv_w", "conv_b", "sink", "g_out_conv", "g_out_attn")}
    dmods[depth - 1][5] = fin[R_FGATE:R_FGATE + 2]
    for l in reversed(range(depth)):
        w_in, w_out, w1, w2 = wts[l]
        sv = saved[l]
        da_act = _mm_nt(do, w2, out_dtype=MXU_DTYPE, extra=sv["a"],
                        epilogue=lambda v, a: v * (2.0 * jnp.maximum(a.astype(F32), 0.0)), name=f"mlp2_dx_{l}")
        g_w2 = _mm_tn(sv["s"], do, shards=1, name=f"mlp2_dw_{l}")
        dh2 = _mm_nt(da_act, w1, out_dtype=F32, name=f"mlp1_dx_{l}")
        g_w1 = _mm_tn(sv["h2"], da_act, shards=shards, name=f"mlp1_dw_{l}")
        dx, dz, sums2 = _norm_bwd(dx, dh2, sv["x_mid"], mods[l], small["g_norm2"][l], scale_idx=4,
                                  prev=(sv["z"], mods[l], 2), name=f"norm2_bwd_{l}", **kwargs)
        dmg = _mm_nt(dz, w_out, out_dtype=F32, name=f"out_proj_dx_{l}")
        g_wo = _mm_tn(sv["mg"], dz, shards=1, name=f"out_proj_dw_{l}")
        dpc, dao, msum = _merge_bwd(dmg, sv["p"], sv["ao"], small["conv_w"][l], small["conv_b"][l],
                                    small["g_out_conv"][l], small["g_out_attn"][l], name=f"merge_bwd_{l}", **kwargs)
        dq, dkl, dvl, dkc, dvc, dsink = _attn_bwd(sv["qk"], sv["p"], dao, small["sink"][l], name=f"attn_bwd_{l}", **akw)
        dk, dv = _kv_reduce(dkl, dvl, dkc, dvc, ctx_rows=ctx_rows, name=f"kv_reduce_{l}")
        dqkv = _rope_bwd(dq, dk, dv, tabs, name=f"rope_bwd_{l}")
        dp = jnp.concatenate([dpc, dqkv], axis=1)
        dh1 = _mm_nt(dp, w_in, out_dtype=F32, name=f"in_proj_dx_{l}")
        g_wi = _mm_tn(sv["h1"], dp, shards=shards, name=f"in_proj_dw_{l}")
        if l > 0:
            dx, do, sums1 = _norm_bwd(dx, dh1, sv["x_in"], mods[l], small["g_norm1"][l], scale_idx=1,
                                      prev=(saved[l - 1]["o"], mods[l - 1], 5), name=f"norm1_bwd_{l}", **kwargs)
            dmods[l - 1][5] = sums1[R_DGATE:R_DGATE + 2]
        else:
            dx, sums1 = _norm_bwd(dx, dh1, sv["x_in"], mods[l], small["g_norm1"][l], scale_idx=1,
                                  name=f"norm1_bwd_{l}", **kwargs)
        grads[l] = (g_wi, g_wo, g_w1, g_w2)
        dmods[l][0] = sums1[R_DSHIFT:R_DSHIFT + 2]
        dmods[l][1] = sums1[R_DSCALE:R_DSCALE + 2]
        dmods[l][2] = sums2[R_DGATE:R_DGATE + 2]
        dmods[l][3] = sums2[R_DSHIFT:R_DSHIFT + 2]
        dmods[l][4] = sums2[R_DSCALE:R_DSCALE + 2]
        sg["g_norm1"][l] = sums1[R_DG]
        sg["g_norm2"][l] = sums2[R_DG]
        sg["g_out_conv"][l] = msum[R_DGOC]
        sg["g_out_attn"][l] = msum[R_DGOA]
        sg["conv_b"][l] = msum[R_DCB]
        sg["conv_w"][l] = msum[R_DCW:R_DCW + 3]
        sg["sink"][l] = dsink[0, :da // HEAD_DIM]
    dmods = jnp.stack([jnp.stack(row, axis=1) for row in dmods])
    sg = {k: jnp.stack(v) for k, v in sg.items()}
    sg["g_final"] = fin[R_FG]
    return fin[R_FLOSS, 0], dx, grads, dmods, sg


N_DEV = 8
N_CHIP = 4


def _place():
    mx, my, mc = lax.axis_index("x"), lax.axis_index("y"), lax.axis_index("c")
    others = [(1 - mx, my), (mx, 1 - my), (1 - mx, 1 - my)]
    return mx, my, mc, others


def _remote(src, dst, send_sems, recv_sems, k, dev):
    return pltpu.make_async_remote_copy(src_ref=src, dst_ref=dst, send_sem=send_sems.at[k], recv_sem=recv_sems.at[k],
                                        device_id=dev, device_id_type=MESH)


def _allgather8(x, name):
    r, ccols = x.shape

    def body(x_ref, out_ref, send_sems, recv_sems, local_sem):
        mx, my, mc, _ = _place()
        me = 4 * mx + 2 * my + mc
        mine = pltpu.make_async_copy(x_ref, out_ref.at[me], local_sem)
        mine.start()
        sent = []
        for k in range(1, N_DEV):
            fx, fy, fc = (k >> 2) & 1, (k >> 1) & 1, k & 1
            px, py, pc = (1 - mx if fx else mx), (1 - my if fy else my), (1 - mc if fc else mc)
            cp = _remote(x_ref, out_ref.at[me], send_sems, recv_sems, k - 1, (px, py, pc))
            cp.start()
            sent.append((cp, 4 * px + 2 * py + pc, (px, py, pc)))
        for k, (cp, peer, dev) in enumerate(sent):
            _remote(x_ref, out_ref.at[peer], send_sems, recv_sems, k, dev).wait_recv()
        for cp, _, _ in sent:
            cp.wait_send()
        mine.wait()

    vm = pl.BlockSpec(memory_space=pltpu.VMEM)
    return pl.pallas_call(
        body, in_specs=[vm], out_specs=vm, out_shape=jax.ShapeDtypeStruct((N_DEV, r, ccols), x.dtype),
        scratch_shapes=[pltpu.SemaphoreType.DMA((N_DEV - 1,)), pltpu.SemaphoreType.DMA((N_DEV - 1,)),
                        pltpu.SemaphoreType.DMA], name=name)(x)


def _gather_weights(halves, name):
    n = len(halves)

    def body(*refs):
        ins, outs = refs[:n], refs[n:2 * n]
        send_sems, recv_sems, local_sems = refs[2 * n:]
        mx, my, mc, others = _place()
        chip = 2 * mx + my
        sib = (mx, my, 1 - mc)
        def src(w):
            return ins[w].at[pl.ds(mc, 1)]

        def slot(w, ch, core):
            return outs[w].at[ch, pl.ds(core, 1)]

        locs, sends = [], []
        for w in range(n):
            loc = pltpu.make_async_copy(src(w), slot(w, chip, mc), local_sems.at[w])
            loc.start()
            locs.append(loc)
            first = [_remote(src(w), slot(w, chip, mc), send_sems, recv_sems, 7 * w, sib)]
            first += [_remote(src(w), slot(w, chip, mc), send_sems, recv_sems, 7 * w + 1 + j, (ox, oy, mc))
                      for j, (ox, oy) in enumerate(others)]
            for cp in first:
                cp.start()
            sends += first
        for w in range(n):
            for j, (ox, oy) in enumerate(others):
                och = 2 * ox + oy
                _remote(src(w), slot(w, och, mc), send_sems, recv_sems, 7 * w + 1 + j, (ox, oy, mc)).wait_recv()
                fw = _remote(slot(w, och, mc), slot(w, och, mc), send_sems, recv_sems, 7 * w + 4 + j, sib)
                fw.start()
                sends.append(fw)
        for w in range(n):
            _remote(src(w), slot(w, chip, 1 - mc), send_sems, recv_sems, 7 * w, sib).wait_recv()
            for j, (ox, oy) in enumerate(others):
                _remote(src(w), slot(w, 2 * ox + oy, 1 - mc), send_sems, recv_sems, 7 * w + 4 + j, sib).wait_recv()
        for cp in sends:
            cp.wait_send()
        for loc in locs:
            loc.wait()

    return pl.pallas_call(
        body, in_specs=[ANY] * n, out_specs=[ANY] * n,
        out_shape=[jax.ShapeDtypeStruct((N_CHIP,) + h.shape, h.dtype) for h in halves],
        scratch_shapes=[pltpu.SemaphoreType.DMA((7 * n,)), pltpu.SemaphoreType.DMA((7 * n,)),
                        pltpu.SemaphoreType.DMA((n,))], name=name)(*halves)


def _pair_exchange(gs, name):
    n = len(gs)

    def body(*refs):
        ins, outs = refs[:n], refs[n:2 * n]
        send_sems, recv_sems = refs[2 * n:]
        mx, my, mc, _ = _place()
        cps = [_remote(ins[w].at[:, pl.ds(1 - mc, 1)], outs[w], send_sems, recv_sems, w, (mx, my, 1 - mc))
               for w in range(n)]
        for cp in cps:
            cp.start()
        for cp in cps:
            cp.wait()

    return pl.pallas_call(
        body, in_specs=[ANY] * n, out_specs=[ANY] * n,
        out_shape=[jax.ShapeDtypeStruct((g.shape[0], 1) + g.shape[2:], g.dtype) for g in gs],
        scratch_shapes=[pltpu.SemaphoreType.DMA((n,)), pltpu.SemaphoreType.DMA((n,))], name=name)(*gs)


def _pair_add(g, got, core, name):
    s, _, rh, ccols = g.shape
    tr = _pick(rh, (256, 128))

    def body(core_ref, g_ref, r_ref, o_ref):
        o_ref[...] = (g_ref[...] + r_ref[...]).astype(o_ref.dtype)

    spec = pltpu.PrefetchScalarGridSpec(
        num_scalar_prefetch=1, grid=(s, rh // tr),
        in_specs=[pl.BlockSpec((None, None, tr, ccols), lambda a, i, cr: (a, cr[0], i, 0)),
                  pl.BlockSpec((None, None, tr, ccols), lambda a, i, cr: (a, 0, i, 0))],
        out_specs=pl.BlockSpec((None, tr, ccols), lambda a, i, cr: (a, i, 0)))
    return pl.pallas_call(body, grid_spec=spec, out_shape=jax.ShapeDtypeStruct((s, rh, ccols), MXU_DTYPE),
                          compiler_params=_params(), name=name)(core, g, got)


def _chip_scatter(ps, name):
    n = len(ps)

    def body(*refs):
        ins, outs = refs[:n], refs[n:2 * n]
        send_sems, recv_sems, local_sems = refs[2 * n:]
        mx, my, mc, others = _place()
        chip = 2 * mx + my
        locs, sends, waits = [], [], []
        for w in range(n):
            loc = pltpu.make_async_copy(ins[w].at[chip], outs[w].at[chip], local_sems.at[w])
            loc.start()
            locs.append(loc)
            for j, (ox, oy) in enumerate(others):
                och = 2 * ox + oy
                cp = _remote(ins[w].at[och], outs[w].at[chip], send_sems, recv_sems, 3 * w + j, (ox, oy, mc))
                cp.start()
                sends.append(cp)
                waits.append(_remote(ins[w].at[och], outs[w].at[och], send_sems, recv_sems, 3 * w + j, (ox, oy, mc)))
        for cp in waits:
            cp.wait_recv()
        for cp in sends:
            cp.wait_send()
        for loc in locs:
            loc.wait()

    return pl.pallas_call(
        body, in_specs=[ANY] * n, out_specs=[ANY] * n,
        out_shape=[jax.ShapeDtypeStruct(p.shape, p.dtype) for p in ps],
        scratch_shapes=[pltpu.SemaphoreType.DMA((3 * n,)), pltpu.SemaphoreType.DMA((3 * n,)),
                        pltpu.SemaphoreType.DMA((n,))], name=name)(*ps)


def _chip_sum(rb, name):
    s, rh, ccols = rb.shape
    tr = _pick(rh, (256, 128))

    def body(r_ref, o_ref):
        tot = r_ref[0].astype(F32)
        for k in range(1, s):
            tot = tot + r_ref[k].astype(F32)
        o_ref[...] = tot

    return pl.pallas_call(
        body, grid=(rh // tr,), in_specs=[pl.BlockSpec((s, tr, ccols), lambda i: (0, i, 0))],
        out_specs=pl.BlockSpec((tr, ccols), lambda i: (i, 0)), out_shape=jax.ShapeDtypeStruct((rh, ccols), F32),
        compiler_params=_params(), name=name)(rb)


def _pair_share(ths, name):
    n = len(ths)

    def body(*refs):
        ins, outs = refs[:n], refs[n:2 * n]
        send_sems, recv_sems, local_sems = refs[2 * n:]
        mx, my, mc, _ = _place()
        locs, cps, waits = [], [], []
        for w in range(n):
            loc = pltpu.make_async_copy(ins[w], outs[w].at[mc], local_sems.at[w])
            loc.start()
            locs.append(loc)
            cp = _remote(ins[w], outs[w].at[mc], send_sems, recv_sems, w, (mx, my, 1 - mc))
            cp.start()
            cps.append(cp)
            waits.append(_remote(ins[w], outs[w].at[1 - mc], send_sems, recv_sems, w, (mx, my, 1 - mc)))
        for cp in waits:
            cp.wait_recv()
        for cp in cps:
            cp.wait_send()
        for loc in locs:
            loc.wait()

    return pl.pallas_call(
        body, in_specs=[ANY] * n, out_specs=[ANY] * n,
        out_shape=[jax.ShapeDtypeStruct((2,) + t.shape, t.dtype) for t in ths],
        scratch_shapes=[pltpu.SemaphoreType.DMA((n,)), pltpu.SemaphoreType.DMA((n,)), pltpu.SemaphoreType.DMA((n,))],
        name=name)(*ths)


def _reduce_scatter(grads, core, tag):
    gs = [g.reshape(g.shape[0], 2, g.shape[1] // 2, g.shape[2]) for g in grads]
    got = _pair_exchange(gs, f"rs_pair_{tag}")
    ps = [_pair_add(g, r, core, f"rs_add_{tag}_{w}") for w, (g, r) in enumerate(zip(gs, got))]
    rb = _chip_scatter(ps, f"rs_chips_{tag}")
    th = [_chip_sum(r, f"rs_sum_{tag}_{w}") for w, r in enumerate(rb)]
    full = _pair_share(th, f"rs_share_{tag}")
    return [f.reshape(2 * f.shape[1], f.shape[2]) for f in full]


def _cast(w, name):
    r, ccols = w.shape
    tr = _pick(r, (256, 128))

    def body(w_ref, o_ref):
        o_ref[...] = w_ref[...].astype(o_ref.dtype)

    spec = pl.BlockSpec((tr, ccols), lambda i: (i, 0))
    return pl.pallas_call(body, grid=(r // tr,), in_specs=[spec], out_specs=spec,
                          out_shape=jax.ShapeDtypeStruct((r, ccols), MXU_DTYPE), compiler_params=_params(), name=name)(w)


def _adam_math(g, w, m, v):
    m = ADAM_B1 * m + (1.0 - ADAM_B1) * g
    v = ADAM_B2 * v + (1.0 - ADAM_B2) * jnp.square(g)
    m_hat = m / (1.0 - ADAM_B1 ** ADAM_STEP)
    v_hat = v / (1.0 - ADAM_B2 ** ADAM_STEP)
    return -ADAM_LR * (m_hat / (jnp.sqrt(v_hat) + ADAM_EPS) + ADAM_WD * w), m, v


def _adamw_layer(l, g, w, m, v, bufs, name):
    depth, r, ccols = w.shape
    tr = _pick(r, (128,))

    def body(g_ref, w_ref, m_ref, v_ref, b0, b1, b2, b3, go_ref, d_ref, mo_ref, vo_ref):
        gv = g_ref[...]
        d, m2, v2 = _adam_math(gv, w_ref[...], m_ref[...], v_ref[...])
        go_ref[...] = gv
        d_ref[...] = d
        mo_ref[...] = m2
        vo_ref[...] = v2

    lay = pl.BlockSpec((None, tr, ccols), lambda i: (l, i, 0))
    return pl.pallas_call(
        body, grid=(r // tr,), in_specs=[pl.BlockSpec((tr, ccols), lambda i: (i, 0)), lay, lay, lay] + [ANY] * 4,
        out_specs=[lay] * 4, out_shape=[jax.ShapeDtypeStruct((depth, r, ccols), F32)] * 4,
        input_output_aliases={4: 0, 5: 1, 6: 2, 7: 3}, compiler_params=_params(), name=name)(g, w, m, v, *bufs)


def _adamw_small(g, g2, w, m, v, name):
    two = g2 is not None

    def body(*refs):
        if two:
            g_ref, g2_ref, w_ref, m_ref, v_ref, go_ref, d_ref, mo_ref, vo_ref = refs
            gv = g_ref[...] + g2_ref[...]
        else:
            g_ref, w_ref, m_ref, v_ref, go_ref, d_ref, mo_ref, vo_ref = refs
            gv = g_ref[...]
        d, m2, v2 = _adam_math(gv, w_ref[...], m_ref[...], v_ref[...])
        go_ref[...] = gv
        d_ref[...] = d
        mo_ref[...] = m2
        vo_ref[...] = v2

    args = [g] + ([g2] if two else []) + [w, m, v]
    vm = pl.BlockSpec(memory_space=pltpu.VMEM)
    return pl.pallas_call(body, in_specs=[vm] * len(args), out_specs=[vm] * 4,
                          out_shape=[jax.ShapeDtypeStruct(w.shape, F32)] * 4, name=name)(*args)


def _sum8(g, name):
    def body(g_ref, o_ref):
        tot = g_ref[0]
        for k in range(1, N_DEV):
            tot = tot + g_ref[k]
        o_ref[...] = tot

    vm = pl.BlockSpec(memory_space=pltpu.VMEM)
    return pl.pallas_call(body, in_specs=[vm], out_specs=vm, out_shape=jax.ShapeDtypeStruct(g.shape[1:], F32),
                          compiler_params=_params(), name=name)(g)


def _pack(arrs, width):
    flat = jnp.concatenate([a.reshape(-1).astype(F32) for a in arrs])
    rows = -(-flat.size // (8 * width)) * 8
    return jnp.pad(flat, (0, rows * width - flat.size)).reshape(rows, width)


def _unpack(flat, shapes):
    out, off = [], 0
    for shp in shapes:
        size = 1
        for v in shp:
            size *= v
        out.append(flat[..., off:off + size].reshape(flat.shape[:-1] + tuple(shp)))
        off += size
    return out


COND_ROWS = 16


def _ada_fwd(cond, w_ada, b_cols, name):
    depth, d, ns = w_ada.shape
    tn = _pick(ns, (512, 384, 256, 128))

    def body(c_ref, w_ref, b_ref, o_ref):
        o_ref[...] = _dot(_silu(c_ref[...]), w_ref[...]) + b_ref[...]

    return pl.pallas_call(
        body, grid=(depth, ns // tn),
        in_specs=[pl.BlockSpec((COND_ROWS, d), lambda l, j: (0, 0)), pl.BlockSpec((None, d, tn), lambda l, j: (l, 0, j)),
                  pl.BlockSpec((None, 1, tn), lambda l, j: (l, 0, j))],
        out_specs=pl.BlockSpec((None, COND_ROWS, tn), lambda l, j: (l, 0, j)),
        out_shape=jax.ShapeDtypeStruct((depth, COND_ROWS, ns), F32), compiler_params=_params(), name=name)(cond, w_ada, b_cols)


def _ada_bwd(cond, dmod, w_ada, name):
    depth, d, ns = w_ada.shape
    tn = _pick(ns, (512, 384, 256, 128))

    def body(c_ref, dm_ref, w_ref, gw_ref, dc_ref):
        @pl.when((pl.program_id(0) == 0) & (pl.program_id(1) == 0))
        def _():
            dc_ref[...] = jnp.zeros_like(dc_ref)

        dm = dm_ref[...]
        gw_ref[...] = _dot_tn(_silu(c_ref[...]), dm)
        dc_ref[...] += _dot_nt(dm, w_ref[...])

    return pl.pallas_call(
        body, grid=(depth, ns // tn),
        in_specs=[pl.BlockSpec((COND_ROWS, d), lambda l, j: (0, 0)),
                  pl.BlockSpec((None, COND_ROWS, tn), lambda l, j: (l, 0, j)),
                  pl.BlockSpec((None, d, tn), lambda l, j: (l, 0, j))],
        out_specs=[pl.BlockSpec((None, d, tn), lambda l, j: (l, 0, j)), pl.BlockSpec((COND_ROWS, d), lambda l, j: (0, 0))],
        out_shape=[jax.ShapeDtypeStruct((depth, d, ns), F32), jax.ShapeDtypeStruct((COND_ROWS, d), F32)],
        compiler_params=_params(), name=name)(cond, dmod, w_ada)


def _cctx_grad(parts, c_ctx, name):
    def body(p_ref, c_ref, o_ref):
        tot = p_ref[0, 0:1, :]
        for k in range(1, N_CHIP):
            tot = tot + p_ref[2 * k, 0:1, :]
        z = c_ref[...]
        sg = 1.0 / (1.0 + jnp.exp(-z))
        o_ref[...] = tot * (sg + z * sg * (1.0 - sg))

    vm = pl.BlockSpec(memory_space=pltpu.VMEM)
    return pl.pallas_call(body, in_specs=[vm, vm], out_specs=vm, out_shape=jax.ShapeDtypeStruct(c_ctx.shape, F32),
                          name=name)(parts, c_ctx)


def kernel(x, c, ctx, c_ctx, w_ada, b_ada, g_norm1, g_norm2, w_in, conv_w, conv_b, sink, g_out_conv, g_out_attn, w_out, w_mlp1, w_mlp2, g_final, loss_target, m_c_ctx, m_w_ada, m_b_ada, m_g_norm1, m_g_norm2, m_w_in, m_conv_w, m_conv_b, m_sink, m_g_out_conv, m_g_out_attn, m_w_out, m_w_mlp1, m_w_mlp2, m_g_final, v_c_ctx, v_w_ada, v_b_ada, v_g_norm1, v_g_norm2, v_w_in, v_conv_w, v_conv_b, v_sink, v_g_out_conv, v_g_out_attn, v_w_out, v_w_mlp1, v_w_mlp2, v_g_final):
    mx, my, mc = lax.axis_index("x"), lax.axis_index("y"), lax.axis_index("c")
    chip, rank = 2 * mx + my, 4 * mx + 2 * my + mc
    core = jnp.reshape(mc, (1,)).astype(I32)
    depth, d = g_norm1.shape
    s_len, ctx_rows = x.shape[1], ctx.shape[1]
    cw_cols = conv_w.shape[2]
    c_conv = cw_cols * N_CHIP
    n_heads = sink.shape[1]
    ns_ada = w_ada.shape[2]

    got = _allgather8(_pack([c, conv_w], d), "gather_cond")
    flat = got.reshape(N_DEV, -1)
    conv_w_full = jnp.transpose(flat[::2, d:d + conv_w.size].reshape(N_CHIP, depth, 3, cw_cols), (1, 2, 0, 3))
    conv_w_full = conv_w_full.reshape(depth, 3, c_conv)
    cond = jnp.zeros((COND_ROWS, d), F32).at[:N_DEV].set(flat[:, :d]).at[N_DEV].set(c_ctx)

    b_cols = lax.dynamic_slice_in_dim(b_ada, chip * ns_ada, ns_ada, axis=1)[:, None, :]
    mod_cols = _ada_fwd(cond, w_ada, b_cols, "ada_fwd")
    got = _allgather8(mod_cols.reshape(depth * COND_ROWS, ns_ada), "gather_mod")
    mod_all = jnp.transpose(got[::2].reshape(N_CHIP, depth, COND_ROWS, ns_ada), (1, 2, 0, 3))
    mod_all = mod_all.reshape(depth, COND_ROWS, N_CHIP * ns_ada)
    mod_me = lax.dynamic_index_in_dim(mod_all, rank, axis=1, keepdims=False)
    mods = jnp.stack([mod_all[:, N_DEV], mod_me], axis=1).reshape(depth, 2, N_MOD, 1, d)

    wts = []
    for l in range(depth):
        halves = [_cast(w[l], f"cast_{nm}_{l}").reshape(2, w.shape[1] // 2, w.shape[2])
                  for nm, w in (("w_in", w_in), ("w_out", w_out), ("w_mlp1", w_mlp1), ("w_mlp2", w_mlp2))]
        gi, go, g1, g2 = _gather_weights(halves, f"gather_weights_{l}")
        wts.append((gi.reshape(N_CHIP, w_in.shape[1], w_in.shape[2]),
                    go.reshape(1, N_CHIP * w_out.shape[1], w_out.shape[2]),
                    g1.reshape(N_CHIP, w_mlp1.shape[1], w_mlp1.shape[2]),
                    g2.reshape(1, N_CHIP * w_mlp2.shape[1], w_mlp2.shape[2])))

    small = dict(g_norm1=g_norm1[:, None], g_norm2=g_norm2[:, None], conv_w=conv_w_full, conv_b=conv_b[:, None], sink=sink,
                 g_out_conv=g_out_conv[:, None], g_out_attn=g_out_attn[:, None], g_final=g_final[None])
    x0 = jnp.concatenate([ctx[0], x[0]], axis=0)
    loss_part, dx0, grads, dmods, sg = _local_step(x0, loss_target[0], mods, wts, small, ctx_rows=ctx_rows)
    loss = lax.psum(loss_part, ("x", "y", "c"))
    grad_x = dx0[ctx_rows:][None]

    stacked = {nm: [lax.empty(w.shape, F32) for _ in range(4)]
               for nm, w in (("w_in", w_in), ("w_out", w_out), ("w_mlp1", w_mlp1), ("w_mlp2", w_mlp2))}
    for l in reversed(range(depth)):
        g_wi, g_wo, g_w1, g_w2 = grads[l]
        g_wo = g_wo.reshape(N_CHIP, w_out.shape[1], w_out.shape[2])
        g_w2 = g_w2.reshape(N_CHIP, w_mlp2.shape[1], w_mlp2.shape[2])
        sums = _reduce_scatter([g_wi, g_wo, g_w1, g_w2], core, str(l))
        for nm, gsum, (w, m, v) in zip(("w_in", "w_out", "w_mlp1", "w_mlp2"), sums,
                                       ((w_in, m_w_in, v_w_in), (w_out, m_w_out, v_w_out),
                                        (w_mlp1, m_w_mlp1, v_w_mlp1), (w_mlp2, m_w_mlp2, v_w_mlp2))):
            stacked[nm] = _adamw_layer(l, gsum, w, m, v, stacked[nm], f"adamw_{nm}_{l}")

    names = ("g_norm1", "g_norm2", "conv_w", "conv_b", "sink", "g_out_conv", "g_out_attn", "g_final")
    shapes = [(depth, 2, N_MOD * d)] + [sg[k].shape for k in names]
    got = _allgather8(_pack([dmods] + [sg[k] for k in names], d), "gather_small")
    tot = _unpack(_sum8(got, "sum_small").reshape(-1), shapes)
    dmod_tot, small_tot = tot[0], dict(zip(names, tot[1:]))
    dmod_lat = _unpack(got.reshape(N_DEV, -1), shapes[:1])[0][:, :, 1]
    dm_rows = jnp.zeros((depth, COND_ROWS, N_MOD * d), F32)
    dm_rows = dm_rows.at[:, :N_DEV].set(jnp.transpose(dmod_lat, (1, 0, 2))).at[:, N_DEV].set(dmod_tot[:, 0])
    dm_cols = lax.dynamic_slice_in_dim(dm_rows, chip * ns_ada, ns_ada, axis=2)
    g_w_ada, dcond = _ada_bwd(cond, dm_cols, w_ada, "ada_bwd")
    got = _allgather8(dcond[N_DEV:N_DEV + 8], "gather_dcond")
    g_c_ctx = _cctx_grad(got, c_ctx[None], "c_ctx_grad")

    res = {}
    ada_bufs = [lax.empty((1,) + (depth * d, ns_ada), F32) for _ in range(4)]
    res["w_ada"] = [r.reshape(w_ada.shape) for r in _adamw_layer(
        0, g_w_ada.reshape(depth * d, ns_ada), w_ada.reshape(1, depth * d, ns_ada),
        m_w_ada.reshape(1, depth * d, ns_ada), v_w_ada.reshape(1, depth * d, ns_ada), ada_bufs, "adamw_w_ada")]
    for nm in ("w_in", "w_out", "w_mlp1", "w_mlp2"):
        res[nm] = stacked[nm]
    res["c_ctx"] = [r[0] for r in _adamw_small(g_c_ctx, None, c_ctx[None], m_c_ctx[None], v_c_ctx[None], "adamw_c_ctx")]
    res["b_ada"] = _adamw_small(dmod_tot[:, 0], dmod_tot[:, 1], b_ada, m_b_ada, v_b_ada, "adamw_b_ada")
    cw_grad = lax.dynamic_slice_in_dim(small_tot["conv_w"], chip * cw_cols, cw_cols, axis=2)
    res["conv_w"] = [r.reshape(conv_w.shape) for r in _adamw_small(
        cw_grad.reshape(depth * 3, cw_cols), None, conv_w.reshape(depth * 3, cw_cols),
        m_conv_w.reshape(depth * 3, cw_cols), v_conv_w.reshape(depth * 3, cw_cols), "adamw_conv_w")]
    for nm, w, m, v in (("g_norm1", g_norm1, m_g_norm1, v_g_norm1), ("g_norm2", g_norm2, m_g_norm2, v_g_norm2),
                        ("conv_b", conv_b, m_conv_b, v_conv_b), ("sink", sink, m_sink, v_sink),
                        ("g_out_conv", g_out_conv, m_g_out_conv, v_g_out_conv),
                        ("g_out_attn", g_out_attn, m_g_out_attn, v_g_out_attn)):
        res[nm] = _adamw_small(small_tot[nm], None, w, m, v, f"adamw_{nm}")
    res["g_final"] = [r[0] for r in _adamw_small(small_tot["g_final"][None], None, g_final[None], m_g_final[None],
                                                 v_g_final[None], "adamw_g_final")]
    order = ("c_ctx", "w_ada", "b_ada", "g_norm1", "g_norm2", "w_in", "conv_w", "conv_b", "sink", "g_out_conv",
             "g_out_attn", "w_out", "w_mlp1", "w_mlp2", "g_final")
    return (loss, grad_x, *[res[n][0] for n in order], *[res[n][1] for n in order], *[res[n][2] for n in order],
            *[res[n][3] for n in order])
```

```python
import functools

import jax
import jax.numpy as jnp
from jax import lax
from jax.experimental import pallas as pl
from jax.experimental.pallas import tpu as pltpu

F32 = jnp.float32
I32 = jnp.int32
MXU_DTYPE = jnp.bfloat16
EPS = 1e-6
HEAD_DIM = 64
N_KV = 4
WINDOW = 128
QBLK = 128
LANES = 128
GRID_W = 64
ROPE_THETA = 10000.0
NEG_INF = -1e30
N_MOD = 6
HALO = 8
ADAM_LR, ADAM_B1, ADAM_B2, ADAM_EPS, ADAM_WD, ADAM_STEP = 0.001, 0.9, 0.999, 1e-08, 0.01, 10
V7X_VMEM_BYTES = 64 * 1024 * 1024
VMEM_LIMIT = V7X_VMEM_BYTES * 3 // 4
MESH = pl.DeviceIdType.MESH
ANY = pl.BlockSpec(memory_space=pl.ANY)


def _params():
    return pltpu.CompilerParams(vmem_limit_bytes=VMEM_LIMIT)


def _pick(n, cands):
    for c in cands:
        if n % c == 0:
            return c
    raise ValueError(f"no tile of {cands} divides {n}")


def _mx(v):
    return v.astype(MXU_DTYPE)


def _dot(a, b):
    return jnp.dot(_mx(a), _mx(b), preferred_element_type=F32)


def _dot_nt(a, b):
    return lax.dot_general(_mx(a), _mx(b), (((1,), (1,)), ((), ())), preferred_element_type=F32)


def _dot_tn(a, b):
    return lax.dot_general(_mx(a), _mx(b), (((0,), (0,)), ((), ())), preferred_element_type=F32)


def _silu(v):
    return v / (1.0 + jnp.exp(-v))


class _Side:
    def __init__(self, inputs, out_shapes, scratch, start, finish, aliases=()):
        self.inputs, self.out_shapes, self.scratch = list(inputs), list(out_shapes), list(scratch)
        self.start, self.finish, self.aliases = start, finish, tuple(aliases)


def _call(body, *, grid, in_specs, out_specs, out_shape, scratch, args, name, side=None):
    if side is None:
        res = pl.pallas_call(body, grid=grid, in_specs=in_specs, out_specs=out_specs, out_shape=out_shape,
                             scratch_shapes=scratch, compiler_params=_params(), name=name)(*args)
        return list(res), []
    ni, no, ns = len(in_specs), len(out_specs), len(scratch)
    si, so = len(side.inputs), len(side.out_shapes)

    def full(*refs):
        ins, sins = refs[:ni], refs[ni:ni + si]
        outs, souts = refs[ni + si:ni + si + no], refs[ni + si + no:ni + si + no + so]
        scr, sems = refs[ni + si + no + so:ni + si + no + so + ns], refs[ni + si + no + so + ns:]
        ids = [pl.program_id(k) for k in range(len(grid))]
        first, last = ids[0] == 0, ids[0] == grid[0] - 1
        for k in range(1, len(grid)):
            first, last = first & (ids[k] == 0), last & (ids[k] == grid[k] - 1)

        @pl.when(first)
        def _():
            side.start(sins, souts, sems)

        body(*ins, *outs, *scr)

        @pl.when(last)
        def _():
            side.finish(sins, souts, sems)

    res = pl.pallas_call(
        full, grid=grid, in_specs=list(in_specs) + [ANY] * si, out_specs=list(out_specs) + [ANY] * so,
        out_shape=list(out_shape) + side.out_shapes, scratch_shapes=list(scratch) + side.scratch,
        input_output_aliases={ni + a: no + b for a, b in side.aliases}, compiler_params=_params(), name=name)(
            *args, *side.inputs)
    return list(res[:no]), list(res[no:])


def _run_side(side, name):
    si, so = len(side.inputs), len(side.out_shapes)

    def body(*refs):
        side.start(refs[:si], refs[si:si + so], refs[si + so:])
        side.finish(refs[:si], refs[si:si + so], refs[si + so:])

    res = pl.pallas_call(body, in_specs=[ANY] * si, out_specs=[ANY] * so, out_shape=side.out_shapes,
                         scratch_shapes=side.scratch, input_output_aliases=dict(side.aliases), name=name)(*side.inputs)
    return list(res)


def _mm_nn(a, b3, *, out_dtypes, epilogue, name, side=None):
    m, k = a.shape
    s, _, ns = b3.shape
    tm = _pick(m, (768, 512, 640, 256, 128))
    tn = _pick(ns, (1024, 1152, 640, 512, 256, 128))
    tk = _pick(k, (2048, 1024, 512))
    nbs, nk = ns // tn, k // tk
    n_out = len(out_dtypes)

    def body(a_ref, b_ref, *rest):
        outs = rest[:n_out]

        def write(v):
            for o, r in zip(outs, epilogue(v)):
                o[...] = r.astype(o.dtype)

        part = jnp.dot(a_ref[...], b_ref[...], preferred_element_type=F32)
        if nk == 1:
            write(part)
            return
        acc = rest[n_out]
        kk = pl.program_id(2)

        @pl.when(kk == 0)
        def _():
            acc[...] = part

        @pl.when(kk > 0)
        def _():
            acc[...] += part

        @pl.when(kk == nk - 1)
        def _():
            write(acc[...])

    return _call(
        body, grid=(m // tm, s * nbs, nk),
        in_specs=[pl.BlockSpec((tm, tk), lambda i, j, kk: (i, kk)),
                  pl.BlockSpec((None, tk, tn), lambda i, j, kk: (j // nbs, kk, j % nbs))],
        out_specs=[pl.BlockSpec((tm, tn), lambda i, j, kk: (i, j))] * n_out,
        out_shape=[jax.ShapeDtypeStruct((m, s * ns), dt) for dt in out_dtypes],
        scratch=[pltpu.VMEM((tm, tn), F32)] if nk > 1 else [], args=(a, b3), name=name, side=side)


def _mm_nt(a, b3, *, out_dtype, name, extra=None, epilogue=None, side=None):
    m = a.shape[0]
    s, ko, ns = b3.shape
    tm = _pick(m, (768, 512, 640, 256, 128))
    tko = _pick(ko, (1024, 512))
    tn = _pick(ns, (2048, 1152, 1024, 640, 512, 256, 128))
    nbs = ns // tn
    nk = s * nbs
    n_in = 3 if extra is not None else 2

    def body(a_ref, b_ref, *rest):
        x_ref = rest[0] if extra is not None else None
        o_ref = rest[n_in - 2]

        def write(v):
            if epilogue is not None:
                v = epilogue(v, x_ref[...])
            o_ref[...] = v.astype(o_ref.dtype)

        part = lax.dot_general(a_ref[...], b_ref[...], (((1,), (1,)), ((), ())), preferred_element_type=F32)
        if nk == 1:
            write(part)
            return
        acc = rest[n_in - 1]
        kk = pl.program_id(2)

        @pl.when(kk == 0)
        def _():
            acc[...] = part

        @pl.when(kk > 0)
        def _():
            acc[...] += part

        @pl.when(kk == nk - 1)
        def _():
            write(acc[...])

    in_specs = [pl.BlockSpec((tm, tn), lambda i, j, kk: (i, kk)),
                pl.BlockSpec((None, tko, tn), lambda i, j, kk: (kk // nbs, j, kk % nbs))]
    args = [a, b3]
    if extra is not None:
        in_specs.append(pl.BlockSpec((tm, tko), lambda i, j, kk: (i, j)))
        args.append(extra)
    return _call(
        body, grid=(m // tm, ko // tko, nk), in_specs=in_specs,
        out_specs=[pl.BlockSpec((tm, tko), lambda i, j, kk: (i, j))],
        out_shape=[jax.ShapeDtypeStruct((m, ko), out_dtype)],
        scratch=[pltpu.VMEM((tm, tko), F32)] if nk > 1 else [], args=args, name=name, side=side)


def _mm_tn(a, b, *, shards, name, side=None):
    t, k = a.shape
    ns = b.shape[1] // shards
    tt = _pick(t, (768, 512, 640, 256, 128))
    tk = _pick(k, (1024, 512))
    tn = _pick(ns, (1024, 1152, 640, 512, 256, 128))
    nbs, nt = ns // tn, t // tt

    def body(a_ref, b_ref, o_ref):
        tt_i = pl.program_id(2)
        part = lax.dot_general(a_ref[...], b_ref[...], (((0,), (0,)), ((), ())), preferred_element_type=F32)

        @pl.when(tt_i == 0)
        def _():
            o_ref[...] = part

        @pl.when(tt_i > 0)
        def _():
            o_ref[...] += part

    return _call(
        body, grid=(k // tk, shards * nbs, nt),
        in_specs=[pl.BlockSpec((tt, tk), lambda i, j, q: (q, i)),
                  pl.BlockSpec((tt, tn), lambda i, j, q: (q, j))],
        out_specs=[pl.BlockSpec((None, tk, tn), lambda i, j, q: (j // nbs, i, j % nbs))],
        out_shape=[jax.ShapeDtypeStruct((shards, k, ns), F32)], scratch=[], args=(a, b), name=name, side=side)


def _row_tile(t, ctx_rows):
    return 256 if t % 256 == 0 and ctx_rows % 256 == 0 else 128


def _mod_spec(d, ncb, idx):
    return pl.BlockSpec((None, None, 1, d), lambda i: (jnp.where(i >= ncb, 1, 0), idx, 0, 0))


def _norm_fwd(x, z, gate_mods, mods, g, *, gate_idx, shift_idx, scale_idx, ctx_rows, name):
    t, d = x.shape
    tr = _row_tile(t, ctx_rows)
    ncb = ctx_rows // tr
    row = pl.BlockSpec((tr, d), lambda i: (i, 0))
    vec = pl.BlockSpec((1, d), lambda i: (0, 0))
    resid = z is not None

    def body(*refs):
        if resid:
            x_ref, z_ref, gt_ref, g_ref, sh_ref, sc_ref, xo_ref, h_ref = refs
            xn = x_ref[...] + gt_ref[...] * z_ref[...]
            xo_ref[...] = xn
        else:
            x_ref, g_ref, sh_ref, sc_ref, h_ref = refs
            xn = x_ref[...]
        r = lax.rsqrt(jnp.mean(xn * xn, axis=-1, keepdims=True) + EPS)
        h_ref[...] = ((xn * r * g_ref[...]) * (1.0 + sc_ref[...]) + sh_ref[...]).astype(h_ref.dtype)

    mspecs = [vec, _mod_spec(d, ncb, shift_idx), _mod_spec(d, ncb, scale_idx)]
    if resid:
        in_specs = [row, row, _mod_spec(d, ncb, gate_idx)] + mspecs
        args = (x, z, gate_mods, g, mods, mods)
        out_specs = [row, row]
        out_shape = [jax.ShapeDtypeStruct((t, d), F32), jax.ShapeDtypeStruct((t, d), MXU_DTYPE)]
    else:
        in_specs = [row] + mspecs
        args = (x, g, mods, mods)
        out_specs = row
        out_shape = jax.ShapeDtypeStruct((t, d), MXU_DTYPE)
    return pl.pallas_call(body, grid=(t // tr,), in_specs=in_specs, out_specs=out_specs, out_shape=out_shape,
                          compiler_params=_params(), name=name)(*args)


R_DSHIFT, R_DSCALE, R_DG, R_DGATE = 0, 2, 4, 5


def _norm_bwd(dx, dh, xin, mods, g, *, scale_idx, ctx_rows, name, prev=None):
    t, d = dx.shape
    tr = _row_tile(t, ctx_rows)
    ncb = ctx_rows // tr
    row = pl.BlockSpec((tr, d), lambda i: (i, 0))
    vec = pl.BlockSpec((1, d), lambda i: (0, 0))
    acc_spec = pl.BlockSpec((8, d), lambda i: (0, 0))
    has_prev = prev is not None

    def body(*refs):
        if has_prev:
            dx_ref, dh_ref, x_ref, g_ref, sc_ref, z_ref, gt_ref, dxo_ref, dz_ref, acc = refs
        else:
            dx_ref, dh_ref, x_ref, g_ref, sc_ref, dxo_ref, acc = refs
        i = pl.program_id(0)

        @pl.when(i == 0)
        def _():
            acc[...] = jnp.zeros_like(acc)

        lat = jnp.where(i >= ncb, 1.0, 0.0)
        x = x_ref[...]
        r = lax.rsqrt(jnp.mean(x * x, axis=-1, keepdims=True) + EPS)
        xhat = x * r
        gv = g_ref[...]
        dhv = dh_ref[...]
        dn = dhv * (1.0 + sc_ref[...])
        dxhat = dn * gv
        dxin = dx_ref[...] + r * (dxhat - xhat * jnp.mean(dxhat * xhat, axis=-1, keepdims=True))
        dxo_ref[...] = dxin
        dshift = jnp.sum(dhv, axis=0, keepdims=True)
        dscale = jnp.sum(dhv * (xhat * gv), axis=0, keepdims=True)
        acc[R_DSHIFT:R_DSHIFT + 1, :] += dshift * (1.0 - lat)
        acc[R_DSHIFT + 1:R_DSHIFT + 2, :] += dshift * lat
        acc[R_DSCALE:R_DSCALE + 1, :] += dscale * (1.0 - lat)
        acc[R_DSCALE + 1:R_DSCALE + 2, :] += dscale * lat
        acc[R_DG:R_DG + 1, :] += jnp.sum(dn * xhat, axis=0, keepdims=True)
        if has_prev:
            dz_ref[...] = (dxin * gt_ref[...]).astype(dz_ref.dtype)
            dgate = jnp.sum(dxin * z_ref[...], axis=0, keepdims=True)
            acc[R_DGATE:R_DGATE + 1, :] += dgate * (1.0 - lat)
            acc[R_DGATE + 1:R_DGATE + 2, :] += dgate * lat

    in_specs = [row, row, row, vec, _mod_spec(d, ncb, scale_idx)]
    args = [dx, dh, xin, g, mods]
    out_specs = [row]
    out_shape = [jax.ShapeDtypeStruct((t, d), F32)]
    if has_prev:
        z, gate_mods, gate_idx = prev
        in_specs += [row, _mod_spec(d, ncb, gate_idx)]
        args += [z, gate_mods]
        out_specs.append(row)
        out_shape.append(jax.ShapeDtypeStruct((t, d), MXU_DTYPE))
    out_specs.append(acc_spec)
    out_shape.append(jax.ShapeDtypeStruct((8, d), F32))
    return pl.pallas_call(body, grid=(t // tr,), in_specs=in_specs, out_specs=out_specs, out_shape=out_shape,
                          compiler_params=_params(), name=name)(*args)


R_FGATE, R_FG, R_FLOSS = 0, 2, 3


def _final(x1, o, gate_mods, g_final, tgt, *, ctx_rows, name):
    t, d = x1.shape
    tr = _row_tile(t, ctx_rows)
    ncb, nb = ctx_rows // tr, t // tr
    row = pl.BlockSpec((tr, d), lambda i: (i, 0))
    vec = pl.BlockSpec((1, d), lambda i: (0, 0))

    def body(x_ref, o_ref, gt_ref, g_ref, t_ref, dx_ref, do_ref, acc, lsum):
        i = pl.program_id(0)

        @pl.when(i == 0)
        def _():
            acc[...] = jnp.zeros_like(acc)
            lsum[...] = jnp.zeros_like(lsum)

        lat = jnp.where(i >= ncb, 1.0, 0.0)
        gt = gt_ref[...]
        ov = o_ref[...]
        x = x_ref[...] + gt * ov
        r = lax.rsqrt(jnp.mean(x * x, axis=-1, keepdims=True) + EPS)
        xhat = x * r
        gv = g_ref[...]
        err = (xhat * gv - t_ref[...]) * lat
        dy = err / d
        dxhat = dy * gv
        dxv = r * (dxhat - xhat * jnp.mean(dxhat * xhat, axis=-1, keepdims=True))
        dx_ref[...] = dxv
        do_ref[...] = (dxv * gt).astype(do_ref.dtype)
        dgate = jnp.sum(dxv * ov, axis=0, keepdims=True)
        acc[R_FGATE:R_FGATE + 1, :] += dgate * (1.0 - lat)
        acc[R_FGATE + 1:R_FGATE + 2, :] += dgate * lat
        acc[R_FG:R_FG + 1, :] += jnp.sum(dy * xhat, axis=0, keepdims=True)
        lsum[...] += jnp.sum(err * err, axis=0, keepdims=True)

        @pl.when(i == nb - 1)
        def _():
            total = (0.5 / d) * jnp.sum(lsum[...], axis=-1, keepdims=True)
            acc[R_FLOSS:R_FLOSS + 1, :] = jnp.broadcast_to(total, (1, d))

    return pl.pallas_call(
        body, grid=(nb,),
        in_specs=[row, row, _mod_spec(d, ncb, 5), vec, pl.BlockSpec((tr, d), lambda i: (jnp.maximum(i - ncb, 0), 0))],
        out_specs=[row, row, pl.BlockSpec((8, d), lambda i: (0, 0))],
        out_shape=[jax.ShapeDtypeStruct((t, d), F32), jax.ShapeDtypeStruct((t, d), MXU_DTYPE),
                   jax.ShapeDtypeStruct((8, d), F32)],
        scratch_shapes=[pltpu.VMEM((1, d), F32)], compiler_params=_params(), name=name)(x1, o, gate_mods, g_final, tgt)


def _rope_tables(s, ctx_rows):
    rows = s // GRID_W
    row_pos = jnp.repeat(jnp.arange(rows, dtype=F32), GRID_W)
    col_pos = jnp.tile(jnp.arange(GRID_W, dtype=F32), rows)
    quarter = HEAD_DIM // 4
    inv = ROPE_THETA ** (-jnp.arange(0, 2 * quarter, 2, dtype=F32) / (2 * quarter))
    ang_r, ang_c = row_pos[:, None] * inv[None, :], col_pos[:, None] * inv[None, :]
    cr, sr, cc, sc = jnp.cos(ang_r), jnp.sin(ang_r), jnp.cos(ang_c), jnp.sin(ang_c)
    zero = jnp.zeros_like(sr)
    cos = jnp.concatenate([cr, cr, cc, cc], axis=1)
    sa = jnp.concatenate([zero, sr, zero, sc], axis=1)
    sb = jnp.concatenate([-sr, zero, -sc, zero], axis=1)

    def full(tab, fill):
        tab = jnp.tile(tab, (1, LANES // HEAD_DIM))
        return jnp.concatenate([jnp.full((ctx_rows, LANES), fill, F32), tab], axis=0)

    return full(cos, 1.0), full(sa, 0.0), full(sb, 0.0)


def _rope_apply(x, cos, sa, sb, transpose):
    n = x.shape[1] // LANES
    cos, sa, sb = (jnp.tile(v, (1, n)) for v in (cos, sa, sb))
    quarter = HEAD_DIM // 4
    width = x.shape[1]
    if transpose:
        return x * cos + pltpu.roll(x * sa, width - quarter, 1) + pltpu.roll(x * sb, quarter, 1)
    return x * cos + pltpu.roll(x, quarter, 1) * sa + pltpu.roll(x, width - quarter, 1) * sb


def _rope_fwd(p, tabs, *, da, kw, kv_start, name):
    t = p.shape[0]
    tr = _pick(t, (256, 128))

    def body(q_ref, k_ref, c_ref, a_ref, b_ref, o_ref):
        cos, sa, sb = c_ref[...], a_ref[...], b_ref[...]
        o_ref[:, :da] = _rope_apply(q_ref[...], cos, sa, sb, False).astype(o_ref.dtype)
        o_ref[:, da:] = _rope_apply(k_ref[...], cos, sa, sb, False).astype(o_ref.dtype)

    tab = pl.BlockSpec((tr, LANES), lambda i: (i, 0))
    return pl.pallas_call(
        body, grid=(t // tr,),
        in_specs=[pl.BlockSpec((tr, da), lambda i: (i, (kv_start - da) // da)),
                  pl.BlockSpec((tr, kw), lambda i: (i, kv_start // kw)), tab, tab, tab],
        out_specs=pl.BlockSpec((tr, da + kw), lambda i: (i, 0)),
        out_shape=jax.ShapeDtypeStruct((t, da + kw), MXU_DTYPE), compiler_params=_params(), name=name)(p, p, *tabs)


def _rope_bwd(dq, dk, dv, tabs, *, name):
    t, da = dq.shape
    kw = dk.shape[1]
    tr = _pick(t, (256, 128))

    def body(q_ref, k_ref, v_ref, c_ref, a_ref, b_ref, o_ref):
        cos, sa, sb = c_ref[...], a_ref[...], b_ref[...]
        o_ref[:, :da] = _rope_apply(q_ref[...], cos, sa, sb, True).astype(o_ref.dtype)
        o_ref[:, da:da + kw] = _rope_apply(k_ref[...], cos, sa, sb, True).astype(o_ref.dtype)
        o_ref[:, da + kw:] = v_ref[...].astype(o_ref.dtype)

    tab = pl.BlockSpec((tr, LANES), lambda i: (i, 0))
    return pl.pallas_call(
        body, grid=(t // tr,),
        in_specs=[pl.BlockSpec((tr, da), lambda i: (i, 0)), pl.BlockSpec((tr, kw), lambda i: (i, 0)),
                  pl.BlockSpec((tr, kw), lambda i: (i, 0)), tab, tab, tab],
        out_specs=pl.BlockSpec((tr, da + 2 * kw), lambda i: (i, 0)),
        out_shape=jax.ShapeDtypeStruct((t, da + 2 * kw), MXU_DTYPE), compiler_params=_params(), name=name)(dq, dk, dv, *tabs)


def _attn_specs(t, ctx_rows, da, kv_start):
    nblk = t // QBLK
    kcb = da // LANES
    vcb = (kv_start + N_KV * HEAD_DIM) // LANES

    def clip(i):
        return jnp.clip(i, 0, nblk - 1)

    def kv(base):
        return [pl.BlockSpec((QBLK, LANES), lambda b, i: (clip(i - 1), base + b)),
                pl.BlockSpec((QBLK, LANES), lambda b, i: (i, base + b)),
                pl.BlockSpec((QBLK, LANES), lambda b, i: (clip(i + 1), base + b)),
                pl.BlockSpec((ctx_rows, LANES), lambda b, i: (0, base + b))]

    return kv(kcb), kv(vcb)


def _attn_mask(i, t, ctx_rows):
    nk = 3 * QBLK + ctx_rows
    rr = lax.broadcasted_iota(I32, (QBLK, nk), 0)
    cc = lax.broadcasted_iota(I32, (QBLK, nk), 1)
    keyrow = (i - 1) * QBLK + cc
    first_key = jnp.where(i * QBLK >= ctx_rows, ctx_rows, t)
    local = (cc < 3 * QBLK) & (keyrow >= first_key) & (keyrow < t) & (jnp.abs(cc - QBLK - rr) <= WINDOW)
    return local | (cc >= 3 * QBLK)


def _head_copies(refs, half):
    x = jnp.concatenate([r[...].astype(F32) for r in refs], axis=0)
    lane = lax.broadcasted_iota(I32, x.shape, 1)
    low = lane < HEAD_DIM
    xs = jnp.where(low if half == 0 else ~low, x, 0.0)
    both = xs + pltpu.roll(xs, HEAD_DIM, 1)
    return _mx(jnp.where(low, both, 0.0)), _mx(jnp.where(low, 0.0, both)), _mx(both)


def _softmax(qblk, kx, valid, snk):
    s = _dot_nt(qblk, kx) * (HEAD_DIM ** -0.5)
    s = jnp.where(valid, s, NEG_INF)
    m = jnp.maximum(jnp.max(s, axis=-1, keepdims=True), snk)
    ex = jnp.exp(s - m)
    es = jnp.exp(snk - m)
    den = jnp.sum(ex, axis=-1, keepdims=True) + es
    return ex / den, es / den


def _attn_fwd(qk, p, sink, *, ctx_rows, da, kv_start, name):
    t = qk.shape[0]
    group = da // HEAD_DIM // N_KV
    qw = 2 * group * HEAD_DIM
    kspecs, vspecs = _attn_specs(t, ctx_rows, da, kv_start)

    def body(sink_ref, q_ref, *rest):
        k_refs, v_refs, o_ref = rest[0:4], rest[4:8], rest[8]
        b, i = pl.program_id(0), pl.program_id(1)
        valid = _attn_mask(i, t, ctx_rows)
        for half in range(2):
            k_lo, k_hi, _ = _head_copies(k_refs, half)
            v_lo, v_hi, _ = _head_copies(v_refs, half)
            for qb in range(group // 2):
                c0 = (half * (group // 2) + qb) * LANES
                qblk = q_ref[:, c0:c0 + LANES]
                out = jnp.zeros((QBLK, LANES), F32)
                for e, (kx, vx) in enumerate(((k_lo, v_lo), (k_hi, v_hi))):
                    pr, _ = _softmax(qblk, kx, valid, sink_ref[(2 * b + half) * group + 2 * qb + e])
                    out = out + _dot(pr, vx)
                o_ref[:, c0:c0 + LANES] = out

    return pl.pallas_call(
        body, grid=(N_KV // 2, t // QBLK),
        in_specs=[pl.BlockSpec(memory_space=pltpu.SMEM), pl.BlockSpec((QBLK, qw), lambda b, i: (i, b))] + kspecs + vspecs,
        out_specs=pl.BlockSpec((QBLK, qw), lambda b, i: (i, b)),
        out_shape=jax.ShapeDtypeStruct((t, da), F32), compiler_params=_params(), name=name)(
            sink, qk, qk, qk, qk, qk, p, p, p, p)


def _attn_bwd(qk, p, dao, sink, *, ctx_rows, da, kv_start, name):
    t = qk.shape[0]
    nblk = t // QBLK
    group = da // HEAD_DIM // N_KV
    qw = 2 * group * HEAD_DIM
    kw = N_KV * HEAD_DIM
    nloc = 3 * QBLK
    kspecs, vspecs = _attn_specs(t, ctx_rows, da, kv_start)

    def body(sink_ref, q_ref, do_ref, *rest):
        k_refs, v_refs = rest[0:4], rest[4:8]
        dq_ref, dkl_ref, dvl_ref, dkc_ref, dvc_ref, ds_ref = rest[8:14]
        b, i = pl.program_id(0), pl.program_id(1)

        @pl.when((b == 0) & (i == 0))
        def _():
            ds_ref[...] = jnp.zeros_like(ds_ref)

        @pl.when(i == 0)
        def _():
            dkc_ref[...] = jnp.zeros_like(dkc_ref)
            dvc_ref[...] = jnp.zeros_like(dvc_ref)

        valid = _attn_mask(i, t, ctx_rows)
        lane_q = lax.broadcasted_iota(I32, (QBLK, LANES), 1)
        lane_k = lax.broadcasted_iota(I32, (nloc + ctx_rows, LANES), 1)
        srow = lax.broadcasted_iota(I32, ds_ref.shape, 0)
        slane = lax.broadcasted_iota(I32, ds_ref.shape, 1)
        dk_acc = jnp.zeros((nloc + ctx_rows, LANES), F32)
        dv_acc = jnp.zeros((nloc + ctx_rows, LANES), F32)
        for half in range(2):
            k_lo, k_hi, _ = _head_copies(k_refs, half)
            _, _, v_both = _head_copies(v_refs, half)
            xk = jnp.zeros((nloc + ctx_rows, LANES), F32)
            xv = jnp.zeros((nloc + ctx_rows, LANES), F32)
            for qb in range(group // 2):
                c0 = (half * (group // 2) + qb) * LANES
                qblk = q_ref[:, c0:c0 + LANES]
                doblk = do_ref[:, c0:c0 + LANES]
                dq = jnp.zeros((QBLK, LANES), F32)
                for e, kx in enumerate((k_lo, k_hi)):
                    head = (2 * b + half) * group + 2 * qb + e
                    mine = (lane_q < HEAD_DIM) if e == 0 else (lane_q >= HEAD_DIM)
                    pr, psink = _softmax(qblk, kx, valid, sink_ref[head])
                    do_e = jnp.where(mine, doblk, 0.0)
                    dp = _dot_nt(do_e, v_both)
                    dsum = jnp.sum(pr * dp, axis=-1, keepdims=True)
                    dsc = pr * (dp - dsum) * (HEAD_DIM ** -0.5)
                    dq = dq + _dot(dsc, kx)
                    xk = xk + _dot_tn(dsc, jnp.where(mine, qblk.astype(F32), 0.0))
                    xv = xv + _dot_tn(pr, do_e)
                    ds_ref[...] += jnp.where((srow == 0) & (slane == head), -jnp.sum(psink * dsum), 0.0)
                dq_ref[:, c0:c0 + LANES] = dq
            here = (lane_k < HEAD_DIM) if half == 0 else (lane_k >= HEAD_DIM)
            dk_acc = dk_acc + jnp.where(here, xk + pltpu.roll(xk, HEAD_DIM, 1), 0.0)
            dv_acc = dv_acc + jnp.where(here, xv + pltpu.roll(xv, HEAD_DIM, 1), 0.0)
        dkl_ref[...] = dk_acc[:nloc]
        dvl_ref[...] = dv_acc[:nloc]
        dkc_ref[...] += dk_acc[nloc:]
        dvc_ref[...] += dv_acc[nloc:]

    qspec = pl.BlockSpec((QBLK, qw), lambda b, i: (i, b))
    loc = pl.BlockSpec((None, nloc, LANES), lambda b, i: (i, 0, b))
    cspec = pl.BlockSpec((ctx_rows, LANES), lambda b, i: (0, b))
    return pl.pallas_call(
        body, grid=(N_KV // 2, nblk),
        in_specs=[pl.BlockSpec(memory_space=pltpu.SMEM), qspec, qspec] + kspecs + vspecs,
        out_specs=[qspec, loc, loc, cspec, cspec, pl.BlockSpec((8, LANES), lambda b, i: (0, 0))],
        out_shape=[jax.ShapeDtypeStruct((t, da), F32), jax.ShapeDtypeStruct((nblk, nloc, kw), F32),
                   jax.ShapeDtypeStruct((nblk, nloc, kw), F32), jax.ShapeDtypeStruct((ctx_rows, kw), F32),
                   jax.ShapeDtypeStruct((ctx_rows, kw), F32), jax.ShapeDtypeStruct((8, LANES), F32)],
        compiler_params=_params(), name=name)(sink, qk, dao, qk, qk, qk, qk, p, p, p, p)


def _kv_reduce(dkl, dvl, dkc, dvc, *, ctx_rows, name):
    nblk, _, kw = dkl.shape
    t = nblk * QBLK
    ncb = ctx_rows // QBLK

    def clip(i):
        return jnp.clip(i, 0, nblk - 1)

    def body(ka, kb, kc, kx, va, vb, vc, vx, dk_ref, dv_ref):
        m = pl.program_id(0)
        lat = m >= ncb
        wa = jnp.where(lat & (m + 1 <= nblk - 1), 1.0, 0.0)
        wc = jnp.where(lat & (m - 1 >= ncb), 1.0, 0.0)
        wl = jnp.where(lat, 1.0, 0.0)
        dk_ref[...] = wl * (kb[...] + wa * ka[...] + wc * kc[...]) + (1.0 - wl) * kx[...]
        dv_ref[...] = wl * (vb[...] + wa * va[...] + wc * vc[...]) + (1.0 - wl) * vx[...]

    slots = [pl.BlockSpec((None, QBLK, kw), lambda m: (clip(m + 1), 0, 0)),
             pl.BlockSpec((None, QBLK, kw), lambda m: (m, 1, 0)),
             pl.BlockSpec((None, QBLK, kw), lambda m: (clip(m - 1), 2, 0))]
    cspec = pl.BlockSpec((QBLK, kw), lambda m: (jnp.clip(m, 0, ncb - 1), 0))
    out = pl.BlockSpec((QBLK, kw), lambda m: (m, 0))
    return pl.pallas_call(
        body, grid=(nblk,), in_specs=slots + [cspec] + slots + [cspec], out_specs=[out, out],
        out_shape=[jax.ShapeDtypeStruct((t, kw), F32)] * 2, compiler_params=_params(), name=name)(
            dkl, dkl, dkl, dkc, dvl, dvl, dvl, dvc)


MERGE_ROWS = 128


def _halo_specs(t, c, col):
    hb = MERGE_ROWS // HALO
    return [pl.BlockSpec((HALO, c), lambda i: (jnp.maximum(i * hb - 1, 0), col)),
            pl.BlockSpec((MERGE_ROWS, c), lambda i: (i, col)),
            pl.BlockSpec((HALO, c), lambda i: (jnp.minimum((i + 1) * hb, t // HALO - 1), col))]


def _ext(refs):
    return jnp.concatenate([r[...] for r in refs], axis=0)


def _conv_ext(cg, hh, w_ref, b_ref, i, t, ctx_rows):
    n = cg.shape[0]
    u = cg * hh
    row = i * MERGE_ROWS - HALO + lax.broadcasted_iota(I32, u.shape, 0)
    first = (row == 0) | (row == ctx_rows)
    last = (row == ctx_rows - 1) | (row == t - 1)
    u_dn = jnp.where(first, 0.0, pltpu.roll(u, 1, 0))
    u_up = jnp.where(last, 0.0, pltpu.roll(u, n - 1, 0))
    cv = w_ref[0:1, :] * u_dn + w_ref[1:2, :] * u + w_ref[2:3, :] * u_up + b_ref[...]
    return u, u_dn, u_up, cv, first, last


def _merge_fwd(p, ao, conv_w, conv_b, g_oc, g_oa, *, ctx_rows, name):
    t = p.shape[0]
    c = ao.shape[1]
    main = slice(HALO, HALO + MERGE_ROWS)

    def body(bg_ref, cgp, cgm, cgn, hhp, hhm, hhn, ao_ref, w_ref, b_ref, gc_ref, ga_ref, o_ref):
        i = pl.program_id(0)
        _, _, _, cv, _, _ = _conv_ext(_ext((cgp, cgm, cgn)), _ext((hhp, hhm, hhn)), w_ref, b_ref, i, t, ctx_rows)
        co = bg_ref[...] * cv[main]
        rc = lax.rsqrt(jnp.mean(co * co, axis=-1, keepdims=True) + EPS)
        o_ref[:, :c] = (co * rc * gc_ref[...]).astype(o_ref.dtype)
        av = ao_ref[...]
        ra = lax.rsqrt(jnp.mean(av * av, axis=-1, keepdims=True) + EPS)
        o_ref[:, c:] = (av * ra * ga_ref[...]).astype(o_ref.dtype)

    vec = pl.BlockSpec((1, c), lambda i: (0, 0))
    return pl.pallas_call(
        body, grid=(t // MERGE_ROWS,),
        in_specs=[pl.BlockSpec((MERGE_ROWS, c), lambda i: (i, 0))] + _halo_specs(t, c, 1) + _halo_specs(t, c, 2)
        + [pl.BlockSpec((MERGE_ROWS, c), lambda i: (i, 0)), pl.BlockSpec((3, c), lambda i: (0, 0)), vec, vec, vec],
        out_specs=pl.BlockSpec((MERGE_ROWS, 2 * c), lambda i: (i, 0)),
        out_shape=jax.ShapeDtypeStruct((t, 2 * c), MXU_DTYPE), compiler_params=_params(), name=name)(
            p, p, p, p, p, p, p, ao, conv_w, conv_b, g_oc, g_oa)


R_DGOC, R_DGOA, R_DCB, R_DCW = 0, 1, 2, 3


def _merge_bwd(dmg, p, ao, conv_w, conv_b, g_oc, g_oa, *, ctx_rows, name):
    t = p.shape[0]
    c = ao.shape[1]
    main = slice(HALO, HALO + MERGE_ROWS)

    def body(dyp, dym, dyn, dya_ref, bgp, bgm, bgn, cgp, cgm, cgn, hhp, hhm, hhn, ao_ref, w_ref, b_ref, gc_ref, ga_ref,
             dp_ref, dao_ref, acc):
        i = pl.program_id(0)

        @pl.when(i == 0)
        def _():
            acc[...] = jnp.zeros_like(acc)

        bg, cg, hh = _ext((bgp, bgm, bgn)), _ext((cgp, cgm, cgn)), _ext((hhp, hhm, hhn))
        n = bg.shape[0]
        u, u_dn, u_up, cv, first, last = _conv_ext(cg, hh, w_ref, b_ref, i, t, ctx_rows)
        co = bg * cv
        rc = lax.rsqrt(jnp.mean(co * co, axis=-1, keepdims=True) + EPS)
        cohat = co * rc
        dyc = _ext((dyp, dym, dyn))
        t1 = dyc * gc_ref[...]
        dco = rc * (t1 - cohat * jnp.mean(t1 * cohat, axis=-1, keepdims=True))
        dcv = dco * bg
        dcv_next = jnp.where(last, 0.0, pltpu.roll(dcv, n - 1, 0))
        dcv_prev = jnp.where(first, 0.0, pltpu.roll(dcv, 1, 0))
        du = w_ref[1:2, :] * dcv + w_ref[0:1, :] * dcv_next + w_ref[2:3, :] * dcv_prev
        dp_ref[:, :c] = (dco * cv)[main].astype(dp_ref.dtype)
        dp_ref[:, c:2 * c] = (du * hh)[main].astype(dp_ref.dtype)
        dp_ref[:, 2 * c:] = (du * cg)[main].astype(dp_ref.dtype)
        dcv_m = dcv[main]
        acc[R_DGOC:R_DGOC + 1, :] += jnp.sum((dyc * cohat)[main], axis=0, keepdims=True)
        acc[R_DCB:R_DCB + 1, :] += jnp.sum(dcv_m, axis=0, keepdims=True)
        acc[R_DCW:R_DCW + 1, :] += jnp.sum(dcv_m * u_dn[main], axis=0, keepdims=True)
        acc[R_DCW + 1:R_DCW + 2, :] += jnp.sum(dcv_m * u[main], axis=0, keepdims=True)
        acc[R_DCW + 2:R_DCW + 3, :] += jnp.sum(dcv_m * u_up[main], axis=0, keepdims=True)
        av = ao_ref[...]
        ra = lax.rsqrt(jnp.mean(av * av, axis=-1, keepdims=True) + EPS)
        ahat = av * ra
        dya = dya_ref[...]
        t2 = dya * ga_ref[...]
        dao_ref[...] = ra * (t2 - ahat * jnp.mean(t2 * ahat, axis=-1, keepdims=True))
        acc[R_DGOA:R_DGOA + 1, :] += jnp.sum(dya * ahat, axis=0, keepdims=True)

    vec = pl.BlockSpec((1, c), lambda i: (0, 0))
    tile = pl.BlockSpec((MERGE_ROWS, c), lambda i: (i, 0))
    return pl.pallas_call(
        body, grid=(t // MERGE_ROWS,),
        in_specs=_halo_specs(t, c, 0) + [pl.BlockSpec((MERGE_ROWS, c), lambda i: (i, 1))]
        + _halo_specs(t, c, 0) + _halo_specs(t, c, 1) + _halo_specs(t, c, 2)
        + [tile, pl.BlockSpec((3, c), lambda i: (0, 0)), vec, vec, vec],
        out_specs=[pl.BlockSpec((MERGE_ROWS, 3 * c), lambda i: (i, 0)), tile, pl.BlockSpec((8, c), lambda i: (0, 0))],
        out_shape=[jax.ShapeDtypeStruct((t, 3 * c), MXU_DTYPE), jax.ShapeDtypeStruct((t, c), F32),
                   jax.ShapeDtypeStruct((8, c), F32)],
        compiler_params=_params(), name=name)(dmg, dmg, dmg, dmg, p, p, p, p, p, p, p, p, p, ao, conv_w, conv_b, g_oc, g_oa)


def _local_step(x0, tgt, mods, wts, small, *, ctx_rows, comm=None):
    t, d = x0.shape
    depth = mods.shape[0]
    c = d // 2
    da = d - c
    kw = N_KV * HEAD_DIM
    kv_start = 3 * c + da
    shards = N_CHIP
    tabs = _rope_tables(t - ctx_rows, ctx_rows)
    kwargs = dict(ctx_rows=ctx_rows)
    akw = dict(ctx_rows=ctx_rows, da=da, kv_start=kv_start)

    saved, all_w = [], []
    xs, z_prev = x0, None
    cur = wts[0] if comm is None else comm.weights(_run_side(_gather_side(comm.halves[0]), "gather_weights_0"))
    for l in range(depth):
        w_in, w_out, w1, w2 = cur
        all_w.append(cur)

        def nxt(k, l=l):
            return _gather_side([comm.halves[l + 1][k]]) if comm is not None and l + 1 < depth else None

        if z_prev is None:
            x_in = xs
            h1 = _norm_fwd(xs, None, None, mods[l], small["g_norm1"][l], gate_idx=None, shift_idx=0, scale_idx=1,
                           name=f"norm1_fwd_{l}", **kwargs)
        else:
            x_in, h1 = _norm_fwd(xs, z_prev, mods[l - 1], mods[l], small["g_norm1"][l], gate_idx=5, shift_idx=0,
                                 scale_idx=1, name=f"norm1_fwd_{l}", **kwargs)
        (p,), got0 = _mm_nn(h1, w_in, out_dtypes=(F32,), epilogue=lambda v: (v,), name=f"in_proj_{l}", side=nxt(0))
        qk = _rope_fwd(p, tabs, da=da, kw=kw, kv_start=kv_start, name=f"rope_fwd_{l}")
        ao = _attn_fwd(qk, p, small["sink"][l], name=f"attn_fwd_{l}", **akw)
        mg = _merge_fwd(p, ao, small["conv_w"][l], small["conv_b"][l], small["g_out_conv"][l], small["g_out_attn"][l],
                        name=f"merge_fwd_{l}", **kwargs)
        (z,), got1 = _mm_nn(mg, w_out, out_dtypes=(F32,), epilogue=lambda v: (v,), name=f"out_proj_{l}", side=nxt(1))
        x_mid, h2 = _norm_fwd(x_in, z, mods[l], mods[l], small["g_norm2"][l], gate_idx=2, shift_idx=3, scale_idx=4,
                              name=f"norm2_fwd_{l}", **kwargs)
        (a_act, s_act), got2 = _mm_nn(h2, w1, out_dtypes=(MXU_DTYPE, MXU_DTYPE),
                                      epilogue=lambda v: (v, jnp.square(jnp.maximum(v, 0.0))), name=f"mlp1_{l}", side=nxt(2))
        (o,), got3 = _mm_nn(s_act, w2, out_dtypes=(F32,), epilogue=lambda v: (v,), name=f"mlp2_{l}", side=nxt(3))
        saved.append(dict(x_in=x_in, h1=h1, p=p, qk=qk, ao=ao, mg=mg, z=z, x_mid=x_mid, h2=h2, a=a_act, s=s_act, o=o))
        xs, z_prev = x_mid, o
        if l + 1 < depth:
            cur = wts[l + 1] if comm is None else comm.weights(got0 + got1 + got2 + got3)

    dx, do, fin = _final(xs, z_prev, mods[depth - 1], small["g_final"], tgt, name="final", **kwargs)

    grads = [None] * depth
    dmods = [[None] * N_MOD for _ in range(depth)]
    sg = {k: [None] * depth for k in ("g_norm1", "g_norm2", "conv_w", "conv_b", "sink", "g_out_conv", "g_out_attn")}
    dmods[depth - 1][5] = fin[R_FGATE:R_FGATE + 2]
    sync = None
    for l in reversed(range(depth)):
        w_in, w_out, w1, w2 = all_w[l]
        sv = saved[l]
        (da_act,), got = _mm_nt(do, w2, out_dtype=MXU_DTYPE, extra=sv["a"],
                                epilogue=lambda v, a: v * (2.0 * jnp.maximum(a.astype(F32), 0.0)), name=f"mlp2_dx_{l}",
                                side=sync and sync.pair_side())
        sync and sync.add(got)
        (g_w2,), got = _mm_tn(sv["s"], do, shards=1, name=f"mlp2_dw_{l}", side=sync and sync.chips_side((2,)))
        sync and sync.land((2,), got)
        (dh2,), got = _mm_nt(da_act, w1, out_dtype=F32, name=f"mlp1_dx_{l}", side=sync and sync.chips_side((3,)))
        sync and sync.land((3,), got)
        (g_w1,), got = _mm_tn(sv["h2"], da_act, shards=shards, name=f"mlp1_dw_{l}", side=sync and sync.chips_side((0, 1)))
        sync and sync.land((0, 1), got)
        sync and sync.sum()
        dx, dz, sums2 = _norm_bwd(dx, dh2, sv["x_mid"], mods[l], small["g_norm2"][l], scale_idx=4,
                                  prev=(sv["z"], mods[l], 2), name=f"norm2_bwd_{l}", **kwargs)
        (dmg,), _ = _mm_nt(dz, w_out, out_dtype=F32, name=f"out_proj_dx_{l}")
        (g_wo,), _ = _mm_tn(sv["mg"], dz, shards=1, name=f"out_proj_dw_{l}")
        dpc, dao, msum = _merge_bwd(dmg, sv["p"], sv["ao"], small["conv_w"][l], small["conv_b"][l],
                                    small["g_out_conv"][l], small["g_out_attn"][l], name=f"merge_bwd_{l}", **kwargs)
        dq, dkl, dvl, dkc, dvc, dsink = _attn_bwd(sv["qk"], sv["p"], dao, small["sink"][l], name=f"attn_bwd_{l}", **akw)
        dk, dv = _kv_reduce(dkl, dvl, dkc, dvc, ctx_rows=ctx_rows, name=f"kv_reduce_{l}")
        dqkv = _rope_bwd(dq, dk, dv, tabs, name=f"rope_bwd_{l}")
        dp = jnp.concatenate([dpc, dqkv], axis=1)
        (dh1,), got = _mm_nt(dp, w_in, out_dtype=F32, name=f"in_proj_dx_{l}", side=sync and sync.share_side())
        sync and sync.adam(got)
        (g_wi,), _ = _mm_tn(sv["h1"], dp, shards=shards, name=f"in_proj_dw_{l}")
        if comm is not None:
            sync = _GradSync(comm, l, (g_wi, g_wo, g_w1, g_w2))
        if l > 0:
            dx, do, sums1 = _norm_bwd(dx, dh1, sv["x_in"], mods[l], small["g_norm1"][l], scale_idx=1,
                                      prev=(saved[l - 1]["o"], mods[l - 1], 5), name=f"norm1_bwd_{l}", **kwargs)
            dmods[l - 1][5] = sums1[R_DGATE:R_DGATE + 2]
        else:
            dx, sums1 = _norm_bwd(dx, dh1, sv["x_in"], mods[l], small["g_norm1"][l], scale_idx=1,
                                  name=f"norm1_bwd_{l}", **kwargs)
        grads[l] = (g_wi, g_wo, g_w1, g_w2)
        dmods[l][0] = sums1[R_DSHIFT:R_DSHIFT + 2]
        dmods[l][1] = sums1[R_DSCALE:R_DSCALE + 2]
        dmods[l][2] = sums2[R_DGATE:R_DGATE + 2]
        dmods[l][3] = sums2[R_DSHIFT:R_DSHIFT + 2]
        dmods[l][4] = sums2[R_DSCALE:R_DSCALE + 2]
        sg["g_norm1"][l] = sums1[R_DG]
        sg["g_norm2"][l] = sums2[R_DG]
        sg["g_out_conv"][l] = msum[R_DGOC]
        sg["g_out_attn"][l] = msum[R_DGOA]
        sg["conv_b"][l] = msum[R_DCB]
        sg["conv_w"][l] = msum[R_DCW:R_DCW + 3]
        sg["sink"][l] = dsink[0, :da // HEAD_DIM]
    if sync is not None:
        sync.run_alone()
    dmods = jnp.stack([jnp.stack(row, axis=1) for row in dmods])
    sg = {k: jnp.stack(v) for k, v in sg.items()}
    sg["g_final"] = fin[R_FG]
    return fin[R_FLOSS, 0], dx, grads, dmods, sg


N_DEV = 8
N_CHIP = 4


def _place():
    mx, my, mc = lax.axis_index("x"), lax.axis_index("y"), lax.axis_index("c")
    others = [(1 - mx, my), (mx, 1 - my), (1 - mx, 1 - my)]
    return mx, my, mc, others


def _remote(src, dst, send_sems, recv_sems, k, dev):
    return pltpu.make_async_remote_copy(src_ref=src, dst_ref=dst, send_sem=send_sems.at[k], recv_sem=recv_sems.at[k],
                                        device_id=dev, device_id_type=MESH)


def _allgather8(x, name):
    r, ccols = x.shape

    def body(x_ref, out_ref, send_sems, recv_sems, local_sem):
        mx, my, mc, _ = _place()
        me = 4 * mx + 2 * my + mc
        mine = pltpu.make_async_copy(x_ref, out_ref.at[me], local_sem)
        mine.start()
        sent = []
        for k in range(1, N_DEV):
            fx, fy, fc = (k >> 2) & 1, (k >> 1) & 1, k & 1
            px, py, pc = (1 - mx if fx else mx), (1 - my if fy else my), (1 - mc if fc else mc)
            cp = _remote(x_ref, out_ref.at[me], send_sems, recv_sems, k - 1, (px, py, pc))
            cp.start()
            sent.append((cp, 4 * px + 2 * py + pc, (px, py, pc)))
        for k, (cp, peer, dev) in enumerate(sent):
            _remote(x_ref, out_ref.at[peer], send_sems, recv_sems, k, dev).wait_recv()
        for cp, _, _ in sent:
            cp.wait_send()
        mine.wait()

    vm = pl.BlockSpec(memory_space=pltpu.VMEM)
    return pl.pallas_call(
        body, in_specs=[vm], out_specs=vm, out_shape=jax.ShapeDtypeStruct((N_DEV, r, ccols), x.dtype),
        scratch_shapes=[pltpu.SemaphoreType.DMA((N_DEV - 1,)), pltpu.SemaphoreType.DMA((N_DEV - 1,)),
                        pltpu.SemaphoreType.DMA], name=name)(x)


def _gather_side(halves):
    n = len(halves)

    def copies(ins, outs, sems):
        send_sems, recv_sems, local_sems = sems
        mx, my, mc, others = _place()
        chip = 2 * mx + my
        sib = (mx, my, 1 - mc)

        def src(w):
            return ins[w].at[pl.ds(mc, 1)]

        def slot(w, ch, core):
            return outs[w].at[ch, pl.ds(core, 1)]

        locs = [pltpu.make_async_copy(src(w), slot(w, chip, mc), local_sems.at[w]) for w in range(n)]
        first, landed, passed, from_sib = [], [], [], []
        for w in range(n):
            first.append(_remote(src(w), slot(w, chip, mc), send_sems, recv_sems, 7 * w, sib))
            from_sib.append(_remote(src(w), slot(w, chip, 1 - mc), send_sems, recv_sems, 7 * w, sib))
            for j, (ox, oy) in enumerate(others):
                och = 2 * ox + oy
                first.append(_remote(src(w), slot(w, chip, mc), send_sems, recv_sems, 7 * w + 1 + j, (ox, oy, mc)))
                landed.append(_remote(src(w), slot(w, och, mc), send_sems, recv_sems, 7 * w + 1 + j, (ox, oy, mc)))
                passed.append(_remote(slot(w, och, mc), slot(w, och, mc), send_sems, recv_sems, 7 * w + 4 + j, sib))
                from_sib.append(_remote(src(w), slot(w, och, 1 - mc), send_sems, recv_sems, 7 * w + 4 + j, sib))
        return locs, first, landed, passed, from_sib

    def start(ins, outs, sems):
        locs, first, _, _, _ = copies(ins, outs, sems)
        for cp in locs + first:
            cp.start()

    def finish(ins, outs, sems):
        locs, first, landed, passed, from_sib = copies(ins, outs, sems)
        for cp, fw in zip(landed, passed):
            cp.wait_recv()
            fw.start()
        for cp in from_sib:
            cp.wait_recv()
        for cp in first + passed:
            cp.wait_send()
        for cp in locs:
            cp.wait()

    return _Side(halves, [jax.ShapeDtypeStruct((N_CHIP,) + h.shape, h.dtype) for h in halves],
                 [pltpu.SemaphoreType.DMA((7 * n,)), pltpu.SemaphoreType.DMA((7 * n,)), pltpu.SemaphoreType.DMA((n,))],
                 start, finish)


def _pair_side(gs):
    n = len(gs)

    def copies(ins, outs, sems):
        mx, my, mc, _ = _place()
        return [_remote(ins[w].at[:, pl.ds(1 - mc, 1)], outs[w], sems[0], sems[1], w, (mx, my, 1 - mc)) for w in range(n)]

    def start(ins, outs, sems):
        for cp in copies(ins, outs, sems):
            cp.start()

    def finish(ins, outs, sems):
        for cp in copies(ins, outs, sems):
            cp.wait()

    return _Side(gs, [jax.ShapeDtypeStruct((g.shape[0], 1) + g.shape[2:], g.dtype) for g in gs],
                 [pltpu.SemaphoreType.DMA((n,)), pltpu.SemaphoreType.DMA((n,))], start, finish)


def _pair_add(g, got, core, name):
    s, _, rh, ccols = g.shape
    tr = _pick(rh, (256, 128))

    def body(core_ref, g_ref, r_ref, o_ref):
        o_ref[...] = (g_ref[...] + r_ref[...]).astype(o_ref.dtype)

    spec = pltpu.PrefetchScalarGridSpec(
        num_scalar_prefetch=1, grid=(s, rh // tr),
        in_specs=[pl.BlockSpec((None, None, tr, ccols), lambda a, i, cr: (a, cr[0], i, 0)),
                  pl.BlockSpec((None, None, tr, ccols), lambda a, i, cr: (a, 0, i, 0))],
        out_specs=pl.BlockSpec((None, tr, ccols), lambda a, i, cr: (a, i, 0)))
    return pl.pallas_call(body, grid_spec=spec, out_shape=jax.ShapeDtypeStruct((s, rh, ccols), MXU_DTYPE),
                          compiler_params=_params(), name=name)(core, g, got)


def _chips_side(ps):
    n = len(ps)

    def copies(ins, outs, sems):
        send_sems, recv_sems, local_sems = sems
        mx, my, mc, others = _place()
        chip = 2 * mx + my
        locs = [pltpu.make_async_copy(ins[w].at[chip], outs[w].at[chip], local_sems.at[w]) for w in range(n)]
        sends, lands = [], []
        for w in range(n):
            for j, (ox, oy) in enumerate(others):
                och = 2 * ox + oy
                sends.append(_remote(ins[w].at[och], outs[w].at[chip], send_sems, recv_sems, 3 * w + j, (ox, oy, mc)))
                lands.append(_remote(ins[w].at[och], outs[w].at[och], send_sems, recv_sems, 3 * w + j, (ox, oy, mc)))
        return locs, sends, lands

    def start(ins, outs, sems):
        locs, sends, _ = copies(ins, outs, sems)
        for cp in locs + sends:
            cp.start()

    def finish(ins, outs, sems):
        locs, sends, lands = copies(ins, outs, sems)
        for cp in lands:
            cp.wait_recv()
        for cp in sends:
            cp.wait_send()
        for cp in locs:
            cp.wait()

    return _Side(ps, [jax.ShapeDtypeStruct(p.shape, p.dtype) for p in ps],
                 [pltpu.SemaphoreType.DMA((3 * n,)), pltpu.SemaphoreType.DMA((3 * n,)), pltpu.SemaphoreType.DMA((n,))],
                 start, finish)


def _chip_sum(rb, core, name):
    s, rh, ccols = rb.shape
    tr = _pick(rh, (256, 128))

    def body(core_ref, r_ref, o_ref):
        tot = r_ref[0].astype(F32)
        for k in range(1, s):
            tot = tot + r_ref[k].astype(F32)
        o_ref[...] = tot

    spec = pltpu.PrefetchScalarGridSpec(
        num_scalar_prefetch=1, grid=(rh // tr,),
        in_specs=[pl.BlockSpec((s, tr, ccols), lambda i, cr: (0, i, 0))],
        out_specs=pl.BlockSpec((None, tr, ccols), lambda i, cr: (cr[0], i, 0)))
    return pl.pallas_call(body, grid_spec=spec, out_shape=jax.ShapeDtypeStruct((2, rh, ccols), F32),
                          compiler_params=_params(), name=name)(core, rb)


def _share_side(fulls):
    n = len(fulls)

    def copies(ins, outs, sems):
        mx, my, mc, _ = _place()
        sib = (mx, my, 1 - mc)
        sends = [_remote(ins[w].at[mc], outs[w].at[mc], sems[0], sems[1], w, sib) for w in range(n)]
        lands = [_remote(ins[w].at[mc], outs[w].at[1 - mc], sems[0], sems[1], w, sib) for w in range(n)]
        return sends, lands

    def start(ins, outs, sems):
        for cp in copies(ins, outs, sems)[0]:
            cp.start()

    def finish(ins, outs, sems):
        sends, lands = copies(ins, outs, sems)
        for cp in lands:
            cp.wait_recv()
        for cp in sends:
            cp.wait_send()

    return _Side(fulls, [jax.ShapeDtypeStruct(f.shape, f.dtype) for f in fulls],
                 [pltpu.SemaphoreType.DMA((n,)), pltpu.SemaphoreType.DMA((n,))], start, finish,
                 aliases=[(w, w) for w in range(n)])


def _cast(w, name):
    r, ccols = w.shape
    tr = _pick(r, (256, 128))

    def body(w_ref, o_ref):
        o_ref[...] = w_ref[...].astype(o_ref.dtype)

    spec = pl.BlockSpec((tr, ccols), lambda i: (i, 0))
    return pl.pallas_call(body, grid=(r // tr,), in_specs=[spec], out_specs=spec,
                          out_shape=jax.ShapeDtypeStruct((r, ccols), MXU_DTYPE), compiler_params=_params(), name=name)(w)


def _adam_math(g, w, m, v):
    m = ADAM_B1 * m + (1.0 - ADAM_B1) * g
    v = ADAM_B2 * v + (1.0 - ADAM_B2) * jnp.square(g)
    m_hat = m / (1.0 - ADAM_B1 ** ADAM_STEP)
    v_hat = v / (1.0 - ADAM_B2 ** ADAM_STEP)
    return -ADAM_LR * (m_hat / (jnp.sqrt(v_hat) + ADAM_EPS) + ADAM_WD * w), m, v


def _adamw_layer(l, g, w, m, v, bufs, name):
    depth, r, ccols = w.shape
    tr = _pick(r, (128,))

    def body(g_ref, w_ref, m_ref, v_ref, b0, b1, b2, b3, go_ref, d_ref, mo_ref, vo_ref):
        gv = g_ref[...]
        d, m2, v2 = _adam_math(gv, w_ref[...], m_ref[...], v_ref[...])
        go_ref[...] = gv
        d_ref[...] = d
        mo_ref[...] = m2
        vo_ref[...] = v2

    lay = pl.BlockSpec((None, tr, ccols), lambda i: (l, i, 0))
    return pl.pallas_call(
        body, grid=(r // tr,), in_specs=[pl.BlockSpec((tr, ccols), lambda i: (i, 0)), lay, lay, lay] + [ANY] * 4,
        out_specs=[lay] * 4, out_shape=[jax.ShapeDtypeStruct((depth, r, ccols), F32)] * 4,
        input_output_aliases={4: 0, 5: 1, 6: 2, 7: 3}, compiler_params=_params(), name=name)(g, w, m, v, *bufs)


def _adamw_small(g, g2, w, m, v, name):
    two = g2 is not None

    def body(*refs):
        if two:
            g_ref, g2_ref, w_ref, m_ref, v_ref, go_ref, d_ref, mo_ref, vo_ref = refs
            gv = g_ref[...] + g2_ref[...]
        else:
            g_ref, w_ref, m_ref, v_ref, go_ref, d_ref, mo_ref, vo_ref = refs
            gv = g_ref[...]
        d, m2, v2 = _adam_math(gv, w_ref[...], m_ref[...], v_ref[...])
        go_ref[...] = gv
        d_ref[...] = d
        mo_ref[...] = m2
        vo_ref[...] = v2

    args = [g] + ([g2] if two else []) + [w, m, v]
    vm = pl.BlockSpec(memory_space=pltpu.VMEM)
    return pl.pallas_call(body, in_specs=[vm] * len(args), out_specs=[vm] * 4,
                          out_shape=[jax.ShapeDtypeStruct(w.shape, F32)] * 4, name=name)(*args)


def _sum8(g, name):
    def body(g_ref, o_ref):
        tot = g_ref[0]
        for k in range(1, N_DEV):
            tot = tot + g_ref[k]
        o_ref[...] = tot

    vm = pl.BlockSpec(memory_space=pltpu.VMEM)
    return pl.pallas_call(body, in_specs=[vm], out_specs=vm, out_shape=jax.ShapeDtypeStruct(g.shape[1:], F32),
                          compiler_params=_params(), name=name)(g)


def _pack(arrs, width):
    flat = jnp.concatenate([a.reshape(-1).astype(F32) for a in arrs])
    rows = -(-flat.size // (8 * width)) * 8
    return jnp.pad(flat, (0, rows * width - flat.size)).reshape(rows, width)


def _unpack(flat, shapes):
    out, off = [], 0
    for shp in shapes:
        size = 1
        for v in shp:
            size *= v
        out.append(flat[..., off:off + size].reshape(flat.shape[:-1] + tuple(shp)))
        off += size
    return out


class _Comm:
    def __init__(self, core, params):
        self.core, self.params = core, params
        depth = params[0][1].shape[0]
        self.halves = [[_cast(w[l], f"cast_{nm}_{l}").reshape(2, w.shape[1] // 2, w.shape[2]) for nm, w, _, _ in params]
                       for l in range(depth)]
        self.stacked = [[lax.empty(w.shape, F32) for _ in range(4)] for _, w, _, _ in params]

    def weights(self, gathered):
        out = []
        for k, g in enumerate(gathered):
            rows, cols = 2 * g.shape[2], g.shape[3]
            out.append(g.reshape(N_CHIP, rows, cols) if k % 2 == 0 else g.reshape(1, N_CHIP * rows, cols))
        return tuple(out)


class _GradSync:
    def __init__(self, comm, l, grads):
        self.comm, self.l = comm, l
        self.gs = [g.reshape(N_CHIP, 2, g.shape[0] * g.shape[1] // (2 * N_CHIP), g.shape[2]) for g in grads]
        self.ps, self.rb, self.full = None, [None] * len(grads), None

    def pair_side(self):
        return _pair_side(self.gs)

    def add(self, got):
        self.ps = [_pair_add(g, r, self.comm.core, f"rs_add_{self.l}_{k}") for k, (g, r) in enumerate(zip(self.gs, got))]

    def chips_side(self, which):
        return _chips_side([self.ps[k] for k in which])

    def land(self, which, got):
        for k, r in zip(which, got):
            self.rb[k] = r

    def sum(self):
        self.full = [_chip_sum(r, self.comm.core, f"rs_sum_{self.l}_{k}") for k, r in enumerate(self.rb)]

    def share_side(self):
        return _share_side(self.full)

    def adam(self, got):
        for k, (full, (nm, w, m, v)) in enumerate(zip(got, self.comm.params)):
            gsum = full.reshape(2 * full.shape[1], full.shape[2])
            self.comm.stacked[k] = _adamw_layer(self.l, gsum, w, m, v, self.comm.stacked[k], f"adamw_{nm}_{self.l}")

    def run_alone(self):
        which = tuple(range(len(self.gs)))
        self.add(_run_side(self.pair_side(), f"rs_pair_{self.l}"))
        self.land(which, _run_side(self.chips_side(which), f"rs_chips_{self.l}"))
        self.sum()
        self.adam(_run_side(self.share_side(), f"rs_share_{self.l}"))


COND_ROWS = 16


def _ada_fwd(cond, w_ada, b_cols, name):
    depth, d, ns = w_ada.shape
    tn = _pick(ns, (512, 384, 256, 128))

    def body(c_ref, w_ref, b_ref, o_ref):
        o_ref[...] = _dot(_silu(c_ref[...]), w_ref[...]) + b_ref[...]

    return pl.pallas_call(
        body, grid=(depth, ns // tn),
        in_specs=[pl.BlockSpec((COND_ROWS, d), lambda l, j: (0, 0)), pl.BlockSpec((None, d, tn), lambda l, j: (l, 0, j)),
                  pl.BlockSpec((None, 1, tn), lambda l, j: (l, 0, j))],
        out_specs=pl.BlockSpec((None, COND_ROWS, tn), lambda l, j: (l, 0, j)),
        out_shape=jax.ShapeDtypeStruct((depth, COND_ROWS, ns), F32), compiler_params=_params(), name=name)(cond, w_ada, b_cols)


def _ada_bwd(cond, dmod, w_ada, name):
    depth, d, ns = w_ada.shape
    tn = _pick(ns, (512, 384, 256, 128))

    def body(c_ref, dm_ref, w_ref, gw_ref, dc_ref):
        @pl.when((pl.program_id(0) == 0) & (pl.program_id(1) == 0))
        def _():
            dc_ref[...] = jnp.zeros_like(dc_ref)

        dm = dm_ref[...]
        gw_ref[...] = _dot_tn(_silu(c_ref[...]), dm)
        dc_ref[...] += _dot_nt(dm, w_ref[...])

    return pl.pallas_call(
        body, grid=(depth, ns // tn),
        in_specs=[pl.BlockSpec((COND_ROWS, d), lambda l, j: (0, 0)),
                  pl.BlockSpec((None, COND_ROWS, tn), lambda l, j: (l, 0, j)),
                  pl.BlockSpec((None, d, tn), lambda l, j: (l, 0, j))],
        out_specs=[pl.BlockSpec((None, d, tn), lambda l, j: (l, 0, j)), pl.BlockSpec((COND_ROWS, d), lambda l, j: (0, 0))],
        out_shape=[jax.ShapeDtypeStruct((depth, d, ns), F32), jax.ShapeDtypeStruct((COND_ROWS, d), F32)],
        compiler_params=_params(), name=name)(cond, dmod, w_ada)


def _cctx_grad(parts, c_ctx, name):
    def body(p_ref, c_ref, o_ref):
        tot = p_ref[0, 0:1, :]
        for k in range(1, N_CHIP):
            tot = tot + p_ref[2 * k, 0:1, :]
        z = c_ref[...]
        sg = 1.0 / (1.0 + jnp.exp(-z))
        o_ref[...] = tot * (sg + z * sg * (1.0 - sg))

    vm = pl.BlockSpec(memory_space=pltpu.VMEM)
    return pl.pallas_call(body, in_specs=[vm, vm], out_specs=vm, out_shape=jax.ShapeDtypeStruct(c_ctx.shape, F32),
                          name=name)(parts, c_ctx)


def kernel(x, c, ctx, c_ctx, w_ada, b_ada, g_norm1, g_norm2, w_in, conv_w, conv_b, sink, g_out_conv, g_out_attn, w_out, w_mlp1, w_mlp2, g_final, loss_target, m_c_ctx, m_w_ada, m_b_ada, m_g_norm1, m_g_norm2, m_w_in, m_conv_w, m_conv_b, m_sink, m_g_out_conv, m_g_out_attn, m_w_out, m_w_mlp1, m_w_mlp2, m_g_final, v_c_ctx, v_w_ada, v_b_ada, v_g_norm1, v_g_norm2, v_w_in, v_conv_w, v_conv_b, v_sink, v_g_out_conv, v_g_out_attn, v_w_out, v_w_mlp1, v_w_mlp2, v_g_final):
    mx, my, mc = lax.axis_index("x"), lax.axis_index("y"), lax.axis_index("c")
    chip, rank = 2 * mx + my, 4 * mx + 2 * my + mc
    core = jnp.reshape(mc, (1,)).astype(I32)
    depth, d = g_norm1.shape
    s_len, ctx_rows = x.shape[1], ctx.shape[1]
    cw_cols = conv_w.shape[2]
    c_conv = cw_cols * N_CHIP
    n_heads = sink.shape[1]
    ns_ada = w_ada.shape[2]

    got = _allgather8(_pack([c, conv_w], d), "gather_cond")
    flat = got.reshape(N_DEV, -1)
    conv_w_full = jnp.transpose(flat[::2, d:d + conv_w.size].reshape(N_CHIP, depth, 3, cw_cols), (1, 2, 0, 3))
    conv_w_full = conv_w_full.reshape(depth, 3, c_conv)
    cond = jnp.zeros((COND_ROWS, d), F32).at[:N_DEV].set(flat[:, :d]).at[N_DEV].set(c_ctx)

    b_cols = lax.dynamic_slice_in_dim(b_ada, chip * ns_ada, ns_ada, axis=1)[:, None, :]
    mod_cols = _ada_fwd(cond, w_ada, b_cols, "ada_fwd")
    got = _allgather8(mod_cols.reshape(depth * COND_ROWS, ns_ada), "gather_mod")
    mod_all = jnp.transpose(got[::2].reshape(N_CHIP, depth, COND_ROWS, ns_ada), (1, 2, 0, 3))
    mod_all = mod_all.reshape(depth, COND_ROWS, N_CHIP * ns_ada)
    mod_me = lax.dynamic_index_in_dim(mod_all, rank, axis=1, keepdims=False)
    mods = jnp.stack([mod_all[:, N_DEV], mod_me], axis=1).reshape(depth, 2, N_MOD, 1, d)

    comm = _Comm(core, (("w_in", w_in, m_w_in, v_w_in), ("w_out", w_out, m_w_out, v_w_out),
                        ("w_mlp1", w_mlp1, m_w_mlp1, v_w_mlp1), ("w_mlp2", w_mlp2, m_w_mlp2, v_w_mlp2)))
    small = dict(g_norm1=g_norm1[:, None], g_norm2=g_norm2[:, None], conv_w=conv_w_full, conv_b=conv_b[:, None], sink=sink,
                 g_out_conv=g_out_conv[:, None], g_out_attn=g_out_attn[:, None], g_final=g_final[None])
    x0 = jnp.concatenate([ctx[0], x[0]], axis=0)
    loss_part, dx0, _, dmods, sg = _local_step(x0, loss_target[0], mods, None, small, ctx_rows=ctx_rows, comm=comm)
    loss = lax.psum(loss_part, ("x", "y", "c"))
    grad_x = dx0[ctx_rows:][None]
    stacked = dict(zip(("w_in", "w_out", "w_mlp1", "w_mlp2"), comm.stacked))

    names = ("g_norm1", "g_norm2", "conv_w", "conv_b", "sink", "g_out_conv", "g_out_attn", "g_final")
    shapes = [(depth, 2, N_MOD * d)] + [sg[k].shape for k in names]
    got = _allgather8(_pack([dmods] + [sg[k] for k in names], d), "gather_small")
    tot = _unpack(_sum8(got, "sum_small").reshape(-1), shapes)
    dmod_tot, small_tot = tot[0], dict(zip(names, tot[1:]))
    dmod_lat = _unpack(got.reshape(N_DEV, -1), shapes[:1])[0][:, :, 1]
    dm_rows = jnp.zeros((depth, COND_ROWS, N_MOD * d), F32)
    dm_rows = dm_rows.at[:, :N_DEV].set(jnp.transpose(dmod_lat, (1, 0, 2))).at[:, N_DEV].set(dmod_tot[:, 0])
    dm_cols = lax.dynamic_slice_in_dim(dm_rows, chip * ns_ada, ns_ada, axis=2)
    g_w_ada, dcond = _ada_bwd(cond, dm_cols, w_ada, "ada_bwd")
    got = _allgather8(dcond[N_DEV:N_DEV + 8], "gather_dcond")
    g_c_ctx = _cctx_grad(got, c_ctx[None], "c_ctx_grad")

    res = {}
    ada_bufs = [lax.empty((1,) + (depth * d, ns_ada), F32) for _ in range(4)]
    res["w_ada"] = [r.reshape(w_ada.shape) for r in _adamw_layer(
        0, g_w_ada.reshape(depth * d, ns_ada), w_ada.reshape(1, depth * d, ns_ada),
        m_w_ada.reshape(1, depth * d, ns_ada), v_w_ada.reshape(1, depth * d, ns_ada), ada_bufs, "adamw_w_ada")]
    for nm in ("w_in", "w_out", "w_mlp1", "w_mlp2"):
        res[nm] = stacked[nm]
    res["c_ctx"] = [r[0] for r in _adamw_small(g_c_ctx, None, c_ctx[None], m_c_ctx[None], v_c_ctx[None], "adamw_c_ctx")]
    res["b_ada"] = _adamw_small(dmod_tot[:, 0], dmod_tot[:, 1], b_ada, m_b_ada, v_b_ada, "adamw_b_ada")
    cw_grad = lax.dynamic_slice_in_dim(small_tot["conv_w"], chip * cw_cols, cw_cols, axis=2)
    res["conv_w"] = [r.reshape(conv_w.shape) for r in _adamw_small(
        cw_grad.reshape(depth * 3, cw_cols), None, conv_w.reshape(depth * 3, cw_cols),
        m_conv_w.reshape(depth * 3, cw_cols), v_conv_w.reshape(depth * 3, cw_cols), "adamw_conv_w")]
    for nm, w, m, v in (("g_norm1", g_norm1, m_g_norm1, v_g_norm1), ("g_norm2", g_norm2, m_g_norm2, v_g_norm2),
                        ("conv_b", conv_b, m_conv_b, v_conv_b), ("sink", sink, m_sink, v_sink),
                        ("g_out_conv", g_out_conv, m_g_out_conv, v_g_out_conv),
                        ("g_out_attn", g_out_attn, m_g_out_attn, v_g_out_attn)):
        res[nm] = _adamw_small(small_tot[nm], None, w, m, v, f"adamw_{nm}")
    res["g_final"] = [r[0] for r in _adamw_small(small_tot["g_final"][None], None, g_final[None], m_g_final[None],
                                                 v_g_final[None], "adamw_g_final")]
    order = ("c_ctx", "w_ada", "b_ada", "g_norm1", "g_norm2", "w_in", "conv_w", "conv_b", "sink", "g_out_conv",
             "g_out_attn", "w_out", "w_mlp1", "w_mlp2", "g_final")
    return (loss, grad_x, *[res[n][0] for n in order], *[res[n][1] for n in order], *[res[n][2] for n in order],
            *[res[n][3] for n in order])
```

```python
import functools

import jax
import jax.numpy as jnp
from jax import lax
from jax.experimental import pallas as pl
from jax.experimental.pallas import tpu as pltpu

F32 = jnp.float32
I32 = jnp.int32
MXU_DTYPE = jnp.bfloat16
EPS = 1e-6
HEAD_DIM = 64
N_KV = 4
WINDOW = 128
QBLK = 128
LANES = 128
GRID_W = 64
ROPE_THETA = 10000.0
NEG_INF = -1e30
N_MOD = 6
HALO = 8
ADAM_LR, ADAM_B1, ADAM_B2, ADAM_EPS, ADAM_WD, ADAM_STEP = 0.001, 0.9, 0.999, 1e-08, 0.01, 10
V7X_VMEM_BYTES = 64 * 1024 * 1024
VMEM_LIMIT = V7X_VMEM_BYTES * 3 // 4
ROW_TILES = (1408, 768, 512, 640, 256, 128)
MESH = pl.DeviceIdType.MESH
ANY = pl.BlockSpec(memory_space=pl.ANY)


def _params():
    return pltpu.CompilerParams(vmem_limit_bytes=VMEM_LIMIT)


def _pick(n, cands):
    for c in cands:
        if n % c == 0:
            return c
    raise ValueError(f"no tile of {cands} divides {n}")


def _mx(v):
    return v.astype(MXU_DTYPE)


def _dot(a, b):
    return jnp.dot(_mx(a), _mx(b), preferred_element_type=F32)


def _dot_nt(a, b):
    return lax.dot_general(_mx(a), _mx(b), (((1,), (1,)), ((), ())), preferred_element_type=F32)


def _dot_tn(a, b):
    return lax.dot_general(_mx(a), _mx(b), (((0,), (0,)), ((), ())), preferred_element_type=F32)


def _silu(v):
    return v / (1.0 + jnp.exp(-v))


class _Side:
    def __init__(self, inputs, out_shapes, scratch, start, finish, aliases=()):
        self.inputs, self.out_shapes, self.scratch = list(inputs), list(out_shapes), list(scratch)
        self.start, self.finish, self.aliases = start, finish, tuple(aliases)


def _call(body, *, grid, in_specs, out_specs, out_shape, scratch, args, name, side=None):
    if side is None:
        res = pl.pallas_call(body, grid=grid, in_specs=in_specs, out_specs=out_specs, out_shape=out_shape,
                             scratch_shapes=scratch, compiler_params=_params(), name=name)(*args)
        return list(res), []
    ni, no, ns = len(in_specs), len(out_specs), len(scratch)
    si, so = len(side.inputs), len(side.out_shapes)

    def full(*refs):
        ins, sins = refs[:ni], refs[ni:ni + si]
        outs, souts = refs[ni + si:ni + si + no], refs[ni + si + no:ni + si + no + so]
        scr, sems = refs[ni + si + no + so:ni + si + no + so + ns], refs[ni + si + no + so + ns:]
        ids = [pl.program_id(k) for k in range(len(grid))]
        first, last = ids[0] == 0, ids[0] == grid[0] - 1
        for k in range(1, len(grid)):
            first, last = first & (ids[k] == 0), last & (ids[k] == grid[k] - 1)

        @pl.when(first)
        def _():
            side.start(sins, souts, sems)

        body(*ins, *outs, *scr)

        @pl.when(last)
        def _():
            side.finish(sins, souts, sems)

    res = pl.pallas_call(
        full, grid=grid, in_specs=list(in_specs) + [ANY] * si, out_specs=list(out_specs) + [ANY] * so,
        out_shape=list(out_shape) + side.out_shapes, scratch_shapes=list(scratch) + side.scratch,
        input_output_aliases={ni + a: no + b for a, b in side.aliases}, compiler_params=_params(), name=name)(
            *args, *side.inputs)
    return list(res[:no]), list(res[no:])


def _run_side(side, name):
    si, so = len(side.inputs), len(side.out_shapes)

    def body(*refs):
        side.start(refs[:si], refs[si:si + so], refs[si + so:])
        side.finish(refs[:si], refs[si:si + so], refs[si + so:])

    res = pl.pallas_call(body, in_specs=[ANY] * si, out_specs=[ANY] * so, out_shape=side.out_shapes,
                         scratch_shapes=side.scratch, input_output_aliases=dict(side.aliases), name=name)(*side.inputs)
    return list(res)


def _mm_nn(a, b3, *, out_dtypes, epilogue, name, side=None):
    m, k = a.shape
    s, _, ns = b3.shape
    tm = _pick(m, ROW_TILES)
    tn = _pick(ns, (1024, 1152, 640, 512, 256, 128))
    tk = _pick(k, (2048, 1024, 512))
    nbs, nk = ns // tn, k // tk
    n_out = len(out_dtypes)

    def body(a_ref, b_ref, *rest):
        outs = rest[:n_out]

        def write(v):
            for o, r in zip(outs, epilogue(v)):
                o[...] = r.astype(o.dtype)

        if nk == 1:
            write(jnp.dot(a_ref[...], b_ref[...], preferred_element_type=F32))
            return
        acc = rest[n_out]
        kk = pl.program_id(2)

        @pl.when(kk == 0)
        def _():
            acc[...] = jnp.zeros_like(acc)

        acc[...] += jnp.dot(a_ref[...], b_ref[...], preferred_element_type=F32)

        @pl.when(kk == nk - 1)
        def _():
            write(acc[...])

    return _call(
        body, grid=(m // tm, s * nbs, nk),
        in_specs=[pl.BlockSpec((tm, tk), lambda i, j, kk: (i, kk)),
                  pl.BlockSpec((None, tk, tn), lambda i, j, kk: (j // nbs, kk, j % nbs))],
        out_specs=[pl.BlockSpec((tm, tn), lambda i, j, kk: (i, j))] * n_out,
        out_shape=[jax.ShapeDtypeStruct((m, s * ns), dt) for dt in out_dtypes],
        scratch=[pltpu.VMEM((tm, tn), F32)] if nk > 1 else [], args=(a, b3), name=name, side=side)


def _mm_nt(a, b3, *, out_dtype, name, extra=None, epilogue=None, side=None):
    m = a.shape[0]
    s, ko, ns = b3.shape
    tm = _pick(m, ROW_TILES)
    tko = _pick(ko, (1024, 512))
    tn = _pick(ns, (2048, 1152, 1024, 640, 512, 256, 128))
    nbs = ns // tn
    nk = s * nbs
    n_in = 3 if extra is not None else 2

    def body(a_ref, b_ref, *rest):
        x_ref = rest[0] if extra is not None else None
        o_ref = rest[n_in - 2]

        def write(v):
            if epilogue is not None:
                v = epilogue(v, x_ref[...])
            o_ref[...] = v.astype(o_ref.dtype)

        if nk == 1:
            write(lax.dot_general(a_ref[...], b_ref[...], (((1,), (1,)), ((), ())), preferred_element_type=F32))
            return
        acc = rest[n_in - 1]
        kk = pl.program_id(2)

        @pl.when(kk == 0)
        def _():
            acc[...] = jnp.zeros_like(acc)

        acc[...] += lax.dot_general(a_ref[...], b_ref[...], (((1,), (1,)), ((), ())), preferred_element_type=F32)

        @pl.when(kk == nk - 1)
        def _():
            write(acc[...])

    in_specs = [pl.BlockSpec((tm, tn), lambda i, j, kk: (i, kk)),
                pl.BlockSpec((None, tko, tn), lambda i, j, kk: (kk // nbs, j, kk % nbs))]
    args = [a, b3]
    if extra is not None:
        in_specs.append(pl.BlockSpec((tm, tko), lambda i, j, kk: (i, j)))
        args.append(extra)
    return _call(
        body, grid=(m // tm, ko // tko, nk), in_specs=in_specs,
        out_specs=[pl.BlockSpec((tm, tko), lambda i, j, kk: (i, j))],
        out_shape=[jax.ShapeDtypeStruct((m, ko), out_dtype)],
        scratch=[pltpu.VMEM((tm, tko), F32)] if nk > 1 else [], args=args, name=name, side=side)


def _mm_tn(a, b, *, shards, name, side=None):
    t, k = a.shape
    ns = b.shape[1] // shards
    tt = _pick(t, (2 * ROW_TILES[0],) + ROW_TILES)
    tk = _pick(k, (1024, 512))
    tn = _pick(ns, (1024, 1152, 640, 512, 256, 128))
    nbs, nt = ns // tn, t // tt

    def body(a_ref, b_ref, o_ref, acc):
        tt_i = pl.program_id(2)

        @pl.when(tt_i == 0)
        def _():
            acc[...] = jnp.zeros_like(acc)

        acc[...] += lax.dot_general(a_ref[...], b_ref[...], (((0,), (0,)), ((), ())), preferred_element_type=F32)

        @pl.when(tt_i == nt - 1)
        def _():
            o_ref[...] = acc[...]

    return _call(
        body, grid=(k // tk, shards * nbs, nt),
        in_specs=[pl.BlockSpec((tt, tk), lambda i, j, q: (q, i)),
                  pl.BlockSpec((tt, tn), lambda i, j, q: (q, j))],
        out_specs=[pl.BlockSpec((None, tk, tn), lambda i, j, q: (j // nbs, i, j % nbs))],
        out_shape=[jax.ShapeDtypeStruct((shards, k, ns), F32)], scratch=[pltpu.VMEM((tk, tn), F32)], args=(a, b),
        name=name, side=side)


def _row_tile(t, ctx_rows):
    return 256 if t % 256 == 0 and ctx_rows % 256 == 0 else 128


def _mod_spec(d, ncb, idx):
    return pl.BlockSpec((None, None, 1, d), lambda i: (jnp.where(i >= ncb, 1, 0), idx, 0, 0))


def _norm_fwd(x, z, gate_mods, mods, g, *, gate_idx, shift_idx, scale_idx, ctx_rows, name):
    t, d = x.shape
    tr = _row_tile(t, ctx_rows)
    ncb = ctx_rows // tr
    row = pl.BlockSpec((tr, d), lambda i: (i, 0))
    vec = pl.BlockSpec((1, d), lambda i: (0, 0))
    resid = z is not None

    def body(*refs):
        if resid:
            x_ref, z_ref, gt_ref, g_ref, sh_ref, sc_ref, xo_ref, h_ref = refs
            xn = x_ref[...] + gt_ref[...] * z_ref[...]
            xo_ref[...] = xn
        else:
            x_ref, g_ref, sh_ref, sc_ref, h_ref = refs
            xn = x_ref[...]
        r = lax.rsqrt(jnp.mean(xn * xn, axis=-1, keepdims=True) + EPS)
        h_ref[...] = ((xn * r * g_ref[...]) * (1.0 + sc_ref[...]) + sh_ref[...]).astype(h_ref.dtype)

    mspecs = [vec, _mod_spec(d, ncb, shift_idx), _mod_spec(d, ncb, scale_idx)]
    if resid:
        in_specs = [row, row, _mod_spec(d, ncb, gate_idx)] + mspecs
        args = (x, z, gate_mods, g, mods, mods)
        out_specs = [row, row]
        out_shape = [jax.ShapeDtypeStruct((t, d), F32), jax.ShapeDtypeStruct((t, d), MXU_DTYPE)]
    else:
        in_specs = [row] + mspecs
        args = (x, g, mods, mods)
        out_specs = row
        out_shape = jax.ShapeDtypeStruct((t, d), MXU_DTYPE)
    return pl.pallas_call(body, grid=(t // tr,), in_specs=in_specs, out_specs=out_specs, out_shape=out_shape,
                          compiler_params=_params(), name=name)(*args)


R_DSHIFT, R_DSCALE, R_DG, R_DGATE = 0, 2, 4, 5


def _norm_bwd(dx, dh, xin, mods, g, *, scale_idx, ctx_rows, name, prev=None):
    t, d = dx.shape
    tr = _row_tile(t, ctx_rows)
    ncb = ctx_rows // tr
    row = pl.BlockSpec((tr, d), lambda i: (i, 0))
    vec = pl.BlockSpec((1, d), lambda i: (0, 0))
    acc_spec = pl.BlockSpec((8, d), lambda i: (0, 0))
    has_prev = prev is not None

    def body(*refs):
        if has_prev:
            dx_ref, dh_ref, x_ref, g_ref, sc_ref, z_ref, gt_ref, dxo_ref, dz_ref, acc = refs
        else:
            dx_ref, dh_ref, x_ref, g_ref, sc_ref, dxo_ref, acc = refs
        i = pl.program_id(0)

        @pl.when(i == 0)
        def _():
            acc[...] = jnp.zeros_like(acc)

        lat = jnp.where(i >= ncb, 1.0, 0.0)
        x = x_ref[...]
        r = lax.rsqrt(jnp.mean(x * x, axis=-1, keepdims=True) + EPS)
        xhat = x * r
        gv = g_ref[...]
        dhv = dh_ref[...]
        dn = dhv * (1.0 + sc_ref[...])
        dxhat = dn * gv
        dxin = dx_ref[...] + r * (dxhat - xhat * jnp.mean(dxhat * xhat, axis=-1, keepdims=True))
        dxo_ref[...] = dxin
        dshift = jnp.sum(dhv, axis=0, keepdims=True)
        dscale = jnp.sum(dhv * (xhat * gv), axis=0, keepdims=True)
        acc[R_DSHIFT:R_DSHIFT + 1, :] += dshift * (1.0 - lat)
        acc[R_DSHIFT + 1:R_DSHIFT + 2, :] += dshift * lat
        acc[R_DSCALE:R_DSCALE + 1, :] += dscale * (1.0 - lat)
        acc[R_DSCALE + 1:R_DSCALE + 2, :] += dscale * lat
        acc[R_DG:R_DG + 1, :] += jnp.sum(dn * xhat, axis=0, keepdims=True)
        if has_prev:
            dz_ref[...] = (dxin * gt_ref[...]).astype(dz_ref.dtype)
            dgate = jnp.sum(dxin * z_ref[...], axis=0, keepdims=True)
            acc[R_DGATE:R_DGATE + 1, :] += dgate * (1.0 - lat)
            acc[R_DGATE + 1:R_DGATE + 2, :] += dgate * lat

    in_specs = [row, row, row, vec, _mod_spec(d, ncb, scale_idx)]
    args = [dx, dh, xin, g, mods]
    out_specs = [row]
    out_shape = [jax.ShapeDtypeStruct((t, d), F32)]
    if has_prev:
        z, gate_mods, gate_idx = prev
        in_specs += [row, _mod_spec(d, ncb, gate_idx)]
        args += [z, gate_mods]
        out_specs.append(row)
        out_shape.append(jax.ShapeDtypeStruct((t, d), MXU_DTYPE))
    out_specs.append(acc_spec)
    out_shape.append(jax.ShapeDtypeStruct((8, d), F32))
    return pl.pallas_call(body, grid=(t // tr,), in_specs=in_specs, out_specs=out_specs, out_shape=out_shape,
                          compiler_params=_params(), name=name)(*args)


R_FGATE, R_FG, R_FLOSS = 0, 2, 3


def _final(x1, o, gate_mods, g_final, tgt, *, ctx_rows, name):
    t, d = x1.shape
    tr = _row_tile(t, ctx_rows)
    ncb, nb = ctx_rows // tr, t // tr
    row = pl.BlockSpec((tr, d), lambda i: (i, 0))
    vec = pl.BlockSpec((1, d), lambda i: (0, 0))

    def body(x_ref, o_ref, gt_ref, g_ref, t_ref, dx_ref, do_ref, acc, lsum):
        i = pl.program_id(0)

        @pl.when(i == 0)
        def _():
            acc[...] = jnp.zeros_like(acc)
            lsum[...] = jnp.zeros_like(lsum)

        lat = jnp.where(i >= ncb, 1.0, 0.0)
        gt = gt_ref[...]
        ov = o_ref[...]
        x = x_ref[...] + gt * ov
        r = lax.rsqrt(jnp.mean(x * x, axis=-1, keepdims=True) + EPS)
        xhat = x * r
        gv = g_ref[...]
        err = (xhat * gv - t_ref[...]) * lat
        dy = err / d
        dxhat = dy * gv
        dxv = r * (dxhat - xhat * jnp.mean(dxhat * xhat, axis=-1, keepdims=True))
        dx_ref[...] = dxv
        do_ref[...] = (dxv * gt).astype(do_ref.dtype)
        dgate = jnp.sum(dxv * ov, axis=0, keepdims=True)
        acc[R_FGATE:R_FGATE + 1, :] += dgate * (1.0 - lat)
        acc[R_FGATE + 1:R_FGATE + 2, :] += dgate * lat
        acc[R_FG:R_FG + 1, :] += jnp.sum(dy * xhat, axis=0, keepdims=True)
        lsum[...] += jnp.sum(err * err, axis=0, keepdims=True)

        @pl.when(i == nb - 1)
        def _():
            total = (0.5 / d) * jnp.sum(lsum[...], axis=-1, keepdims=True)
            acc[R_FLOSS:R_FLOSS + 1, :] = jnp.broadcast_to(total, (1, d))

    return pl.pallas_call(
        body, grid=(nb,),
        in_specs=[row, row, _mod_spec(d, ncb, 5), vec, pl.BlockSpec((tr, d), lambda i: (jnp.maximum(i - ncb, 0), 0))],
        out_specs=[row, row, pl.BlockSpec((8, d), lambda i: (0, 0))],
        out_shape=[jax.ShapeDtypeStruct((t, d), F32), jax.ShapeDtypeStruct((t, d), MXU_DTYPE),
                   jax.ShapeDtypeStruct((8, d), F32)],
        scratch_shapes=[pltpu.VMEM((1, d), F32)], compiler_params=_params(), name=name)(x1, o, gate_mods, g_final, tgt)


def _rope_tables(s, ctx_rows):
    rows = s // GRID_W
    row_pos = jnp.repeat(jnp.arange(rows, dtype=F32), GRID_W)
    col_pos = jnp.tile(jnp.arange(GRID_W, dtype=F32), rows)
    quarter = HEAD_DIM // 4
    inv = ROPE_THETA ** (-jnp.arange(0, 2 * quarter, 2, dtype=F32) / (2 * quarter))
    ang_r, ang_c = row_pos[:, None] * inv[None, :], col_pos[:, None] * inv[None, :]
    cr, sr, cc, sc = jnp.cos(ang_r), jnp.sin(ang_r), jnp.cos(ang_c), jnp.sin(ang_c)
    zero = jnp.zeros_like(sr)
    cos = jnp.concatenate([cr, cr, cc, cc], axis=1)
    sa = jnp.concatenate([zero, sr, zero, sc], axis=1)
    sb = jnp.concatenate([-sr, zero, -sc, zero], axis=1)

    def full(tab, fill):
        tab = jnp.tile(tab, (1, LANES // HEAD_DIM))
        return jnp.concatenate([jnp.full((ctx_rows, LANES), fill, F32), tab], axis=0)

    return full(cos, 1.0), full(sa, 0.0), full(sb, 0.0)


def _rope_apply(x, cos, sa, sb, transpose):
    n = x.shape[1] // LANES
    cos, sa, sb = (jnp.tile(v, (1, n)) for v in (cos, sa, sb))
    quarter = HEAD_DIM // 4
    width = x.shape[1]
    if transpose:
        return x * cos + pltpu.roll(x * sa, width - quarter, 1) + pltpu.roll(x * sb, quarter, 1)
    return x * cos + pltpu.roll(x, quarter, 1) * sa + pltpu.roll(x, width - quarter, 1) * sb


def _twice(x):
    lane = lax.broadcasted_iota(I32, (x.shape[0], LANES), 1)
    out = []
    for j in range(N_KV):
        blk = x[:, (j // 2) * LANES:(j // 2 + 1) * LANES]
        own = jnp.where((lane < HEAD_DIM) if j % 2 == 0 else (lane >= HEAD_DIM), blk, 0.0)
        out.append(own + pltpu.roll(own, HEAD_DIM, 1))
    return jnp.concatenate(out, axis=1)


def _rope_fwd(p, tabs, *, da, kw, kv_start, name):
    t = p.shape[0]
    tr = _pick(t, (256, 128))

    def body(q_ref, k_ref, v_ref, c_ref, a_ref, b_ref, qo_ref, ko_ref, vo_ref):
        cos, sa, sb = c_ref[...], a_ref[...], b_ref[...]
        qo_ref[...] = (_rope_apply(q_ref[...], cos, sa, sb, False) * (HEAD_DIM ** -0.5)).astype(qo_ref.dtype)
        ko_ref[...] = _twice(_rope_apply(k_ref[...], cos, sa, sb, False)).astype(ko_ref.dtype)
        vo_ref[...] = _twice(v_ref[...]).astype(vo_ref.dtype)

    tab = pl.BlockSpec((tr, LANES), lambda i: (i, 0))
    two = pl.BlockSpec((tr, N_KV * LANES), lambda i: (i, 0))
    return pl.pallas_call(
        body, grid=(t // tr,),
        in_specs=[pl.BlockSpec((tr, da), lambda i: (i, (kv_start - da) // da)),
                  pl.BlockSpec((tr, kw), lambda i: (i, kv_start // kw)),
                  pl.BlockSpec((tr, kw), lambda i: (i, kv_start // kw + 1)), tab, tab, tab],
        out_specs=[pl.BlockSpec((tr, da), lambda i: (i, 0)), two, two],
        out_shape=[jax.ShapeDtypeStruct((t, da), MXU_DTYPE), jax.ShapeDtypeStruct((t, N_KV * LANES), MXU_DTYPE),
                   jax.ShapeDtypeStruct((t, N_KV * LANES), MXU_DTYPE)],
        compiler_params=_params(), name=name)(p, p, p, *tabs)


def _rope_bwd(dq, dk, dv, tabs, dp, *, name):
    t, da = dq.shape
    kw = dk.shape[1]
    width = da + 2 * kw
    tr = _pick(t, (256, 128))
    col = (dp.shape[1] - width) // width
    assert col * width == dp.shape[1] - width

    def body(q_ref, k_ref, v_ref, c_ref, a_ref, b_ref, dp_ref, o_ref):
        cos, sa, sb = c_ref[...], a_ref[...], b_ref[...]
        o_ref[:, :da] = _rope_apply(q_ref[...], cos, sa, sb, True).astype(o_ref.dtype)
        o_ref[:, da:da + kw] = _rope_apply(k_ref[...], cos, sa, sb, True).astype(o_ref.dtype)
        o_ref[:, da + kw:] = v_ref[...].astype(o_ref.dtype)

    tab = pl.BlockSpec((tr, LANES), lambda i: (i, 0))
    return pl.pallas_call(
        body, grid=(t // tr,),
        in_specs=[pl.BlockSpec((tr, da), lambda i: (i, 0)), pl.BlockSpec((tr, kw), lambda i: (i, 0)),
                  pl.BlockSpec((tr, kw), lambda i: (i, 0)), tab, tab, tab, ANY],
        out_specs=pl.BlockSpec((tr, width), lambda i: (i, col)),
        out_shape=jax.ShapeDtypeStruct(dp.shape, dp.dtype), input_output_aliases={6: 0},
        compiler_params=_params(), name=name)(dq, dk, dv, *tabs, dp)


def _attn_specs(t, ctx_rows):
    nblk = t // QBLK

    def clip(i):
        return jnp.clip(i, 0, nblk - 1)

    return [pl.BlockSpec((QBLK, 2 * LANES), lambda b, i: (clip(i - 1), b)),
            pl.BlockSpec((QBLK, 2 * LANES), lambda b, i: (i, b)),
            pl.BlockSpec((QBLK, 2 * LANES), lambda b, i: (clip(i + 1), b)),
            pl.BlockSpec((ctx_rows, 2 * LANES), lambda b, i: (0, b))]


def _local_mask(i, rows, t, ctx_rows):
    rr = lax.broadcasted_iota(I32, (rows, 3 * QBLK), 0) & (QBLK - 1)
    cc = lax.broadcasted_iota(I32, (rows, 3 * QBLK), 1)
    keyrow = (i - 1) * QBLK + cc
    first_key = jnp.where(i * QBLK >= ctx_rows, ctx_rows, t)
    return (keyrow >= first_key) & (keyrow < t) & (jnp.abs(cc - QBLK - rr) <= WINDOW)


def _stack_heads(ref, c0, nblocks):
    lane = lax.broadcasted_iota(I32, (QBLK, LANES), 1)
    rows = []
    for qb in range(nblocks):
        blk = ref[:, c0 + qb * LANES:c0 + (qb + 1) * LANES].astype(F32)
        rows += [jnp.where(lane < HEAD_DIM, blk, 0.0), jnp.where(lane < HEAD_DIM, 0.0, blk)]
    return jnp.concatenate(rows, axis=0)


def _unstack_heads(x, nblocks):
    lane = lax.broadcasted_iota(I32, (QBLK, LANES), 1)
    return [jnp.where(lane < HEAD_DIM, x[2 * qb * QBLK:(2 * qb + 1) * QBLK], x[(2 * qb + 1) * QBLK:(2 * qb + 2) * QBLK])
            for qb in range(nblocks)]


def _sink_column(sink_ref, head0, group):
    row = lax.broadcasted_iota(I32, (group * QBLK, 1), 0)
    col = jnp.zeros((group * QBLK, 1), F32)
    for g in range(group):
        col = jnp.where((row >= g * QBLK) & (row < (g + 1) * QBLK), sink_ref[head0 + g], col)
    return col


def _probs(qs, k_loc, k_ctx, valid, snk):
    s_loc = jnp.where(valid, _dot_nt(qs, k_loc), NEG_INF)
    s_ctx = _dot_nt(qs, k_ctx)
    m = jnp.maximum(jnp.maximum(jnp.max(s_loc, axis=-1, keepdims=True), jnp.max(s_ctx, axis=-1, keepdims=True)), snk)
    e_loc, e_ctx, e_snk = jnp.exp(s_loc - m), jnp.exp(s_ctx - m), jnp.exp(snk - m)
    inv = 1.0 / (jnp.sum(e_loc, axis=-1, keepdims=True) + jnp.sum(e_ctx, axis=-1, keepdims=True) + e_snk)
    return e_loc * inv, e_ctx * inv, e_snk * inv


def _attn_fwd(qs, k2, v2, sink, *, ctx_rows, name):
    t, da = qs.shape
    group = da // HEAD_DIM // N_KV
    nqb = group // 2
    kvspecs = _attn_specs(t, ctx_rows)

    def body(sink_ref, q_ref, kp, kc, kn, kx, vp, vc, vn, vx, o_ref):
        b, i = pl.program_id(0), pl.program_id(1)
        valid = _local_mask(i, group * QBLK, t, ctx_rows)
        for half in range(2):
            lanes = slice(half * LANES, (half + 1) * LANES)
            qst = _mx(_stack_heads(q_ref, half * nqb * LANES, nqb))
            k_loc = jnp.concatenate([kp[:, lanes], kc[:, lanes], kn[:, lanes]], axis=0)
            v_loc = jnp.concatenate([vp[:, lanes], vc[:, lanes], vn[:, lanes]], axis=0)
            snk = _sink_column(sink_ref, (2 * b + half) * group, group)
            p_loc, p_ctx, _ = _probs(qst, k_loc, kx[:, lanes], valid, snk)
            out = _dot(p_loc, v_loc) + _dot(p_ctx, vx[:, lanes])
            for qb, blk in enumerate(_unstack_heads(out, nqb)):
                c0 = (half * nqb + qb) * LANES
                o_ref[:, c0:c0 + LANES] = blk

    qspec = pl.BlockSpec((QBLK, 2 * group * HEAD_DIM), lambda b, i: (i, b))
    return pl.pallas_call(
        body, grid=(N_KV // 2, t // QBLK),
        in_specs=[pl.BlockSpec(memory_space=pltpu.SMEM), qspec] + kvspecs + kvspecs,
        out_specs=qspec, out_shape=jax.ShapeDtypeStruct((t, da), F32), compiler_params=_params(), name=name)(
            sink, qs, k2, k2, k2, k2, v2, v2, v2, v2)


def _attn_bwd(qs, k2, v2, dao, sink, *, ctx_rows, name):
    t, da = qs.shape
    nblk = t // QBLK
    group = da // HEAD_DIM // N_KV
    nqb = group // 2
    kw = N_KV * HEAD_DIM
    nloc = 3 * QBLK
    kvspecs = _attn_specs(t, ctx_rows)

    def body(sink_ref, q_ref, do_ref, kp, kc, kn, kx, vp, vc, vn, vx, dq_ref, dkl_ref, dvl_ref, dkc_ref, dvc_ref, ds_ref):
        b, i = pl.program_id(0), pl.program_id(1)

        @pl.when((b == 0) & (i == 0))
        def _():
            ds_ref[...] = jnp.zeros_like(ds_ref)

        @pl.when(i == 0)
        def _():
            dkc_ref[...] = jnp.zeros_like(dkc_ref)
            dvc_ref[...] = jnp.zeros_like(dvc_ref)

        valid = _local_mask(i, group * QBLK, t, ctx_rows)
        srow = lax.broadcasted_iota(I32, ds_ref.shape, 0)
        slane = lax.broadcasted_iota(I32, ds_ref.shape, 1)

        def both(x):
            return x + pltpu.roll(x, HEAD_DIM, 1)

        folded = []
        for half in range(2):
            lanes = slice(half * LANES, (half + 1) * LANES)
            head0 = (2 * b + half) * group
            qst = _mx(_stack_heads(q_ref, half * nqb * LANES, nqb))
            dost = _mx(_stack_heads(do_ref, half * nqb * LANES, nqb))
            k_loc = jnp.concatenate([kp[:, lanes], kc[:, lanes], kn[:, lanes]], axis=0)
            v_loc = jnp.concatenate([vp[:, lanes], vc[:, lanes], vn[:, lanes]], axis=0)
            k_ctx, v_ctx = kx[:, lanes], vx[:, lanes]
            p_loc, p_ctx, p_snk = _probs(qst, k_loc, k_ctx, valid, _sink_column(sink_ref, head0, group))
            dp_loc, dp_ctx = _dot_nt(dost, v_loc), _dot_nt(dost, v_ctx)
            dsum = jnp.sum(p_loc * dp_loc, axis=-1, keepdims=True) + jnp.sum(p_ctx * dp_ctx, axis=-1, keepdims=True)
            ds_loc, ds_ctx = _mx(p_loc * (dp_loc - dsum)), _mx(p_ctx * (dp_ctx - dsum))
            dq = (_dot(ds_loc, k_loc) + _dot(ds_ctx, k_ctx)) * (HEAD_DIM ** -0.5)
            for qb, blk in enumerate(_unstack_heads(dq, nqb)):
                c0 = (half * nqb + qb) * LANES
                dq_ref[:, c0:c0 + LANES] = blk
            p_loc, p_ctx = _mx(p_loc), _mx(p_ctx)
            folded.append((both(_dot_tn(ds_loc, qst)), both(_dot_tn(p_loc, dost)),
                           both(_dot_tn(ds_ctx, qst)), both(_dot_tn(p_ctx, dost))))
            dsnk = p_snk * dsum
            for g in range(group):
                part = -jnp.sum(dsnk[g * QBLK:(g + 1) * QBLK])
                ds_ref[...] += jnp.where((srow == 0) & (slane == head0 + g), part, 0.0)
        lane_l = lax.broadcasted_iota(I32, (nloc, LANES), 1)
        lane_c = lax.broadcasted_iota(I32, (ctx_rows, LANES), 1)
        dkl_ref[...] = jnp.where(lane_l < HEAD_DIM, folded[0][0], folded[1][0])
        dvl_ref[...] = jnp.where(lane_l < HEAD_DIM, folded[0][1], folded[1][1])
        dkc_ref[...] += jnp.where(lane_c < HEAD_DIM, folded[0][2], folded[1][2])
        dvc_ref[...] += jnp.where(lane_c < HEAD_DIM, folded[0][3], folded[1][3])

    qspec = pl.BlockSpec((QBLK, 2 * group * HEAD_DIM), lambda b, i: (i, b))
    loc = pl.BlockSpec((None, nloc, LANES), lambda b, i: (i, 0, b))
    cspec = pl.BlockSpec((ctx_rows, LANES), lambda b, i: (0, b))
    return pl.pallas_call(
        body, grid=(N_KV // 2, nblk),
        in_specs=[pl.BlockSpec(memory_space=pltpu.SMEM), qspec, qspec] + kvspecs + kvspecs,
        out_specs=[qspec, loc, loc, cspec, cspec, pl.BlockSpec((8, LANES), lambda b, i: (0, 0))],
        out_shape=[jax.ShapeDtypeStruct((t, da), F32), jax.ShapeDtypeStruct((nblk, nloc, kw), F32),
                   jax.ShapeDtypeStruct((nblk, nloc, kw), F32), jax.ShapeDtypeStruct((ctx_rows, kw), F32),
                   jax.ShapeDtypeStruct((ctx_rows, kw), F32), jax.ShapeDtypeStruct((8, LANES), F32)],
        compiler_params=_params(), name=name)(sink, qs, dao, k2, k2, k2, k2, v2, v2, v2, v2)


def _kv_reduce(dkl, dvl, dkc, dvc, *, ctx_rows, name):
    nblk, _, kw = dkl.shape
    t = nblk * QBLK
    ncb = ctx_rows // QBLK

    def clip(i):
        return jnp.clip(i, 0, nblk - 1)

    def body(ka, kb, kc, kx, va, vb, vc, vx, dk_ref, dv_ref):
        m = pl.program_id(0)
        lat = m >= ncb
        wa = jnp.where(lat & (m + 1 <= nblk - 1), 1.0, 0.0)
        wc = jnp.where(lat & (m - 1 >= ncb), 1.0, 0.0)
        wl = jnp.where(lat, 1.0, 0.0)
        dk_ref[...] = wl * (kb[...] + wa * ka[...] + wc * kc[...]) + (1.0 - wl) * kx[...]
        dv_ref[...] = wl * (vb[...] + wa * va[...] + wc * vc[...]) + (1.0 - wl) * vx[...]

    slots = [pl.BlockSpec((None, QBLK, kw), lambda m: (clip(m + 1), 0, 0)),
             pl.BlockSpec((None, QBLK, kw), lambda m: (m, 1, 0)),
             pl.BlockSpec((None, QBLK, kw), lambda m: (clip(m - 1), 2, 0))]
    cspec = pl.BlockSpec((QBLK, kw), lambda m: (jnp.clip(m, 0, ncb - 1), 0))
    out = pl.BlockSpec((QBLK, kw), lambda m: (m, 0))
    return pl.pallas_call(
        body, grid=(nblk,), in_specs=slots + [cspec] + slots + [cspec], out_specs=[out, out],
        out_shape=[jax.ShapeDtypeStruct((t, kw), F32)] * 2, compiler_params=_params(), name=name)(
            dkl, dkl, dkl, dkc, dvl, dvl, dvl, dvc)


MERGE_ROWS = 128


def _halo_specs(t, c, col):
    hb = MERGE_ROWS // HALO
    return [pl.BlockSpec((HALO, c), lambda i: (jnp.maximum(i * hb - 1, 0), col)),
            pl.BlockSpec((MERGE_ROWS, c), lambda i: (i, col)),
            pl.BlockSpec((HALO, c), lambda i: (jnp.minimum((i + 1) * hb, t // HALO - 1), col))]


def _ext(refs):
    return jnp.concatenate([r[...] for r in refs], axis=0)


def _conv_ext(cg, hh, w_ref, b_ref, i, t, ctx_rows):
    n = cg.shape[0]
    u = cg * hh
    row = i * MERGE_ROWS - HALO + lax.broadcasted_iota(I32, u.shape, 0)
    first = (row == 0) | (row == ctx_rows)
    last = (row == ctx_rows - 1) | (row == t - 1)
    u_dn = jnp.where(first, 0.0, pltpu.roll(u, 1, 0))
    u_up = jnp.where(last, 0.0, pltpu.roll(u, n - 1, 0))
    cv = w_ref[0:1, :] * u_dn + w_ref[1:2, :] * u + w_ref[2:3, :] * u_up + b_ref[...]
    return u, u_dn, u_up, cv, first, last


def _merge_fwd(p, ao, conv_w, conv_b, g_oc, g_oa, *, ctx_rows, name):
    t = p.shape[0]
    c = ao.shape[1]
    main = slice(HALO, HALO + MERGE_ROWS)

    def body(bg_ref, cgp, cgm, cgn, hhp, hhm, hhn, ao_ref, w_ref, b_ref, gc_ref, ga_ref, o_ref):
        i = pl.program_id(0)
        _, _, _, cv, _, _ = _conv_ext(_ext((cgp, cgm, cgn)), _ext((hhp, hhm, hhn)), w_ref, b_ref, i, t, ctx_rows)
        co = bg_ref[...] * cv[main]
        rc = lax.rsqrt(jnp.mean(co * co, axis=-1, keepdims=True) + EPS)
        o_ref[:, :c] = (co * rc * gc_ref[...]).astype(o_ref.dtype)
        av = ao_ref[...]
        ra = lax.rsqrt(jnp.mean(av * av, axis=-1, keepdims=True) + EPS)
        o_ref[:, c:] = (av * ra * ga_ref[...]).astype(o_ref.dtype)

    vec = pl.BlockSpec((1, c), lambda i: (0, 0))
    return pl.pallas_call(
        body, grid=(t // MERGE_ROWS,),
        in_specs=[pl.BlockSpec((MERGE_ROWS, c), lambda i: (i, 0))] + _halo_specs(t, c, 1) + _halo_specs(t, c, 2)
        + [pl.BlockSpec((MERGE_ROWS, c), lambda i: (i, 0)), pl.BlockSpec((3, c), lambda i: (0, 0)), vec, vec, vec],
        out_specs=pl.BlockSpec((MERGE_ROWS, 2 * c), lambda i: (i, 0)),
        out_shape=jax.ShapeDtypeStruct((t, 2 * c), MXU_DTYPE), compiler_params=_params(), name=name)(
            p, p, p, p, p, p, p, ao, conv_w, conv_b, g_oc, g_oa)


R_DGOC, R_DGOA, R_DCB, R_DCW = 0, 1, 2, 3


def _merge_bwd(dmg, p, ao, conv_w, conv_b, g_oc, g_oa, *, width, ctx_rows, name):
    t = p.shape[0]
    c = ao.shape[1]
    main = slice(HALO, HALO + MERGE_ROWS)

    def body(dyp, dym, dyn, dya_ref, bgp, bgm, bgn, cgp, cgm, cgn, hhp, hhm, hhn, ao_ref, w_ref, b_ref, gc_ref, ga_ref,
             dp_ref, dao_ref, acc):
        i = pl.program_id(0)

        @pl.when(i == 0)
        def _():
            acc[...] = jnp.zeros_like(acc)

        bg, cg, hh = _ext((bgp, bgm, bgn)), _ext((cgp, cgm, cgn)), _ext((hhp, hhm, hhn))
        n = bg.shape[0]
        u, u_dn, u_up, cv, first, last = _conv_ext(cg, hh, w_ref, b_ref, i, t, ctx_rows)
        co = bg * cv
        rc = lax.rsqrt(jnp.mean(co * co, axis=-1, keepdims=True) + EPS)
        cohat = co * rc
        dyc = _ext((dyp, dym, dyn))
        t1 = dyc * gc_ref[...]
        dco = rc * (t1 - cohat * jnp.mean(t1 * cohat, axis=-1, keepdims=True))
        dcv = dco * bg
        dcv_next = jnp.where(last, 0.0, pltpu.roll(dcv, n - 1, 0))
        dcv_prev = jnp.where(first, 0.0, pltpu.roll(dcv, 1, 0))
        du = w_ref[1:2, :] * dcv + w_ref[0:1, :] * dcv_next + w_ref[2:3, :] * dcv_prev
        dp_ref[:, :c] = (dco * cv)[main].astype(dp_ref.dtype)
        dp_ref[:, c:2 * c] = (du * hh)[main].astype(dp_ref.dtype)
        dp_ref[:, 2 * c:] = (du * cg)[main].astype(dp_ref.dtype)
        dcv_m = dcv[main]
        acc[R_DGOC:R_DGOC + 1, :] += jnp.sum((dyc * cohat)[main], axis=0, keepdims=True)
        acc[R_DCB:R_DCB + 1, :] += jnp.sum(dcv_m, axis=0, keepdims=True)
        acc[R_DCW:R_DCW + 1, :] += jnp.sum(dcv_m * u_dn[main], axis=0, keepdims=True)
        acc[R_DCW + 1:R_DCW + 2, :] += jnp.sum(dcv_m * u[main], axis=0, keepdims=True)
        acc[R_DCW + 2:R_DCW + 3, :] += jnp.sum(dcv_m * u_up[main], axis=0, keepdims=True)
        av = ao_ref[...]
        ra = lax.rsqrt(jnp.mean(av * av, axis=-1, keepdims=True) + EPS)
        ahat = av * ra
        dya = dya_ref[...]
        t2 = dya * ga_ref[...]
        dao_ref[...] = ra * (t2 - ahat * jnp.mean(t2 * ahat, axis=-1, keepdims=True))
        acc[R_DGOA:R_DGOA + 1, :] += jnp.sum(dya * ahat, axis=0, keepdims=True)

    vec = pl.BlockSpec((1, c), lambda i: (0, 0))
    tile = pl.BlockSpec((MERGE_ROWS, c), lambda i: (i, 0))
    return pl.pallas_call(
        body, grid=(t // MERGE_ROWS,),
        in_specs=_halo_specs(t, c, 0) + [pl.BlockSpec((MERGE_ROWS, c), lambda i: (i, 1))]
        + _halo_specs(t, c, 0) + _halo_specs(t, c, 1) + _halo_specs(t, c, 2)
        + [tile, pl.BlockSpec((3, c), lambda i: (0, 0)), vec, vec, vec],
        out_specs=[pl.BlockSpec((MERGE_ROWS, 3 * c), lambda i: (i, 0)), tile, pl.BlockSpec((8, c), lambda i: (0, 0))],
        out_shape=[jax.ShapeDtypeStruct((t, width), MXU_DTYPE), jax.ShapeDtypeStruct((t, c), F32),
                   jax.ShapeDtypeStruct((8, c), F32)],
        compiler_params=_params(), name=name)(dmg, dmg, dmg, dmg, p, p, p, p, p, p, p, p, p, ao, conv_w, conv_b, g_oc, g_oa)


def _local_step(x0, tgt, mods, wts, small, *, ctx_rows, comm=None):
    t, d = x0.shape
    depth = mods.shape[0]
    c = d // 2
    da = d - c
    kw = N_KV * HEAD_DIM
    kv_start = 3 * c + da
    shards = N_CHIP
    tabs = _rope_tables(t - ctx_rows, ctx_rows)
    kwargs = dict(ctx_rows=ctx_rows)

    saved, all_w = [], []
    xs, z_prev = x0, None
    cur = wts[0] if comm is None else comm.weights(_run_side(_gather_side(comm.halves[0]), "gather_weights_0"))
    for l in range(depth):
        w_in, w_out, w1, w2 = cur
        all_w.append(cur)

        def nxt(k, l=l):
            return _gather_side([comm.halves[l + 1][k]]) if comm is not None and l + 1 < depth else None

        if z_prev is None:
            x_in = xs
            h1 = _norm_fwd(xs, None, None, mods[l], small["g_norm1"][l], gate_idx=None, shift_idx=0, scale_idx=1,
                           name=f"norm1_fwd_{l}", **kwargs)
        else:
            x_in, h1 = _norm_fwd(xs, z_prev, mods[l - 1], mods[l], small["g_norm1"][l], gate_idx=5, shift_idx=0,
                                 scale_idx=1, name=f"norm1_fwd_{l}", **kwargs)
        (p,), got0 = _mm_nn(h1, w_in, out_dtypes=(F32,), epilogue=lambda v: (v,), name=f"in_proj_{l}", side=nxt(0))
        qs, k2, v2 = _rope_fwd(p, tabs, da=da, kw=kw, kv_start=kv_start, name=f"rope_fwd_{l}")
        ao = _attn_fwd(qs, k2, v2, small["sink"][l], name=f"attn_fwd_{l}", **kwargs)
        mg = _merge_fwd(p, ao, small["conv_w"][l], small["conv_b"][l], small["g_out_conv"][l], small["g_out_attn"][l],
                        name=f"merge_fwd_{l}", **kwargs)
        (z,), got1 = _mm_nn(mg, w_out, out_dtypes=(F32,), epilogue=lambda v: (v,), name=f"out_proj_{l}", side=nxt(1))
        x_mid, h2 = _norm_fwd(x_in, z, mods[l], mods[l], small["g_norm2"][l], gate_idx=2, shift_idx=3, scale_idx=4,
                              name=f"norm2_fwd_{l}", **kwargs)
        (a_act, s_act), got2 = _mm_nn(h2, w1, out_dtypes=(MXU_DTYPE, MXU_DTYPE),
                                      epilogue=lambda v: (v, jnp.square(jnp.maximum(v, 0.0))), name=f"mlp1_{l}", side=nxt(2))
        (o,), got3 = _mm_nn(s_act, w2, out_dtypes=(F32,), epilogue=lambda v: (v,), name=f"mlp2_{l}", side=nxt(3))
        saved.append(dict(x_in=x_in, h1=h1, p=p, qs=qs, k2=k2, v2=v2, ao=ao, mg=mg, z=z, x_mid=x_mid, h2=h2, a=a_act, s=s_act, o=o))
        xs, z_prev = x_mid, o
        if l + 1 < depth:
            cur = wts[l + 1] if comm is None else comm.weights(got0 + got1 + got2 + got3)

    dx, do, fin = _final(xs, z_prev, mods[depth - 1], small["g_final"], tgt, name="final", **kwargs)

    grads = [None] * depth
    dmods = [[None] * N_MOD for _ in range(depth)]
    sg = {k: [None] * depth for k in ("g_norm1", "g_norm2", "conv_w", "conv_b", "sink", "g_out_conv", "g_out_attn")}
    dmods[depth - 1][5] = fin[R_FGATE:R_FGATE + 2]
    sync = None
    for l in reversed(range(depth)):
        w_in, w_out, w1, w2 = all_w[l]
        sv = saved[l]
        (da_act,), got = _mm_nt(do, w2, out_dtype=MXU_DTYPE, extra=sv["a"],
                                epilogue=lambda v, a: v * (2.0 * jnp.maximum(a.astype(F32), 0.0)), name=f"mlp2_dx_{l}",
                                side=sync and sync.pair_side())
        sync and sync.add(got)
        (g_w2,), got = _mm_tn(sv["s"], do, shards=1, name=f"mlp2_dw_{l}", side=sync and sync.chips_side((2,)))
        sync and sync.land((2,), got)
        (dh2,), got = _mm_nt(da_act, w1, out_dtype=F32, name=f"mlp1_dx_{l}", side=sync and sync.chips_side((3,)))
        sync and sync.land((3,), got)
        (g_w1,), got = _mm_tn(sv["h2"], da_act, shards=shards, name=f"mlp1_dw_{l}", side=sync and sync.chips_side((0, 1)))
        sync and sync.land((0, 1), got)
        sync and sync.sum()
        dx, dz, sums2 = _norm_bwd(dx, dh2, sv["x_mid"], mods[l], small["g_norm2"][l], scale_idx=4,
                                  prev=(sv["z"], mods[l], 2), name=f"norm2_bwd_{l}", **kwargs)
        (dmg,), _ = _mm_nt(dz, w_out, out_dtype=F32, name=f"out_proj_dx_{l}")
        (g_wo,), _ = _mm_tn(sv["mg"], dz, shards=1, name=f"out_proj_dw_{l}")
        qkv_w = da + 2 * kw
        in_place = (3 * c) % qkv_w == 0
        dpc, dao, msum = _merge_bwd(dmg, sv["p"], sv["ao"], small["conv_w"][l], small["conv_b"][l],
                                    small["g_out_conv"][l], small["g_out_attn"][l],
                                    width=3 * c + qkv_w if in_place else 3 * c, name=f"merge_bwd_{l}", **kwargs)
        dq, dkl, dvl, dkc, dvc, dsink = _attn_bwd(sv["qs"], sv["k2"], sv["v2"], dao, small["sink"][l],
                                                  name=f"attn_bwd_{l}", **kwargs)
        dk, dv = _kv_reduce(dkl, dvl, dkc, dvc, ctx_rows=ctx_rows, name=f"kv_reduce_{l}")
        if in_place:
            dp = _rope_bwd(dq, dk, dv, tabs, dpc, name=f"rope_bwd_{l}")
        else:
            dqkv = _rope_bwd(dq, dk, dv, tabs, lax.empty((t, qkv_w), MXU_DTYPE), name=f"rope_bwd_{l}")
            dp = jnp.concatenate([dpc, dqkv], axis=1)
        (dh1,), got = _mm_nt(dp, w_in, out_dtype=F32, name=f"in_proj_dx_{l}", side=sync and sync.share_side())
        sync and sync.adam(got)
        (g_wi,), _ = _mm_tn(sv["h1"], dp, shards=shards, name=f"in_proj_dw_{l}")
        if comm is not None:
            sync = _GradSync(comm, l, (g_wi, g_wo, g_w1, g_w2))
        if l > 0:
            dx, do, sums1 = _norm_bwd(dx, dh1, sv["x_in"], mods[l], small["g_norm1"][l], scale_idx=1,
                                      prev=(saved[l - 1]["o"], mods[l - 1], 5), name=f"norm1_bwd_{l}", **kwargs)
            dmods[l - 1][5] = sums1[R_DGATE:R_DGATE + 2]
        else:
            dx, sums1 = _norm_bwd(dx, dh1, sv["x_in"], mods[l], small["g_norm1"][l], scale_idx=1,
                                  name=f"norm1_bwd_{l}", **kwargs)
        grads[l] = (g_wi, g_wo, g_w1, g_w2)
        dmods[l][0] = sums1[R_DSHIFT:R_DSHIFT + 2]
        dmods[l][1] = sums1[R_DSCALE:R_DSCALE + 2]
        dmods[l][2] = sums2[R_DGATE:R_DGATE + 2]
        dmods[l][3] = sums2[R_DSHIFT:R_DSHIFT + 2]
        dmods[l][4] = sums2[R_DSCALE:R_DSCALE + 2]
        sg["g_norm1"][l] = sums1[R_DG]
        sg["g_norm2"][l] = sums2[R_DG]
        sg["g_out_conv"][l] = msum[R_DGOC]
        sg["g_out_attn"][l] = msum[R_DGOA]
        sg["conv_b"][l] = msum[R_DCB]
        sg["conv_w"][l] = msum[R_DCW:R_DCW + 3]
        sg["sink"][l] = dsink[0, :da // HEAD_DIM]
    if sync is not None:
        sync.run_alone()
    dmods = jnp.stack([jnp.stack(row, axis=1) for row in dmods])
    sg = {k: jnp.stack(v) for k, v in sg.items()}
    sg["g_final"] = fin[R_FG]
    return fin[R_FLOSS, 0], dx, grads, dmods, sg


N_DEV = 8
N_CHIP = 4


def _place():
    mx, my, mc = lax.axis_index("x"), lax.axis_index("y"), lax.axis_index("c")
    others = [(1 - mx, my), (mx, 1 - my), (1 - mx, 1 - my)]
    return mx, my, mc, others


def _remote(src, dst, send_sems, recv_sems, k, dev):
    return pltpu.make_async_remote_copy(src_ref=src, dst_ref=dst, send_sem=send_sems.at[k], recv_sem=recv_sems.at[k],
                                        device_id=dev, device_id_type=MESH)


def _allgather8(x, name):
    r, ccols = x.shape

    def body(x_ref, out_ref, send_sems, recv_sems, local_sem):
        mx, my, mc, _ = _place()
        me = 4 * mx + 2 * my + mc
        mine = pltpu.make_async_copy(x_ref, out_ref.at[me], local_sem)
        mine.start()
        sent = []
        for k in range(1, N_DEV):
            fx, fy, fc = (k >> 2) & 1, (k >> 1) & 1, k & 1
            px, py, pc = (1 - mx if fx else mx), (1 - my if fy else my), (1 - mc if fc else mc)
            cp = _remote(x_ref, out_ref.at[me], send_sems, recv_sems, k - 1, (px, py, pc))
            cp.start()
            sent.append((cp, 4 * px + 2 * py + pc, (px, py, pc)))
        for k, (cp, peer, dev) in enumerate(sent):
            _remote(x_ref, out_ref.at[peer], send_sems, recv_sems, k, dev).wait_recv()
        for cp, _, _ in sent:
            cp.wait_send()
        mine.wait()

    vm = pl.BlockSpec(memory_space=pltpu.VMEM)
    return pl.pallas_call(
        body, in_specs=[vm], out_specs=vm, out_shape=jax.ShapeDtypeStruct((N_DEV, r, ccols), x.dtype),
        scratch_shapes=[pltpu.SemaphoreType.DMA((N_DEV - 1,)), pltpu.SemaphoreType.DMA((N_DEV - 1,)),
                        pltpu.SemaphoreType.DMA], name=name)(x)


def _gather_side(halves):
    n = len(halves)

    def copies(ins, outs, sems):
        send_sems, recv_sems, local_sems = sems
        mx, my, mc, others = _place()
        chip = 2 * mx + my
        sib = (mx, my, 1 - mc)

        def src(w):
            return ins[w].at[pl.ds(mc, 1)]

        def slot(w, ch, core):
            return outs[w].at[ch, pl.ds(core, 1)]

        locs = [pltpu.make_async_copy(src(w), slot(w, chip, mc), local_sems.at[w]) for w in range(n)]
        first, landed, passed, from_sib = [], [], [], []
        for w in range(n):
            first.append(_remote(src(w), slot(w, chip, mc), send_sems, recv_sems, 7 * w, sib))
            from_sib.append(_remote(src(w), slot(w, chip, 1 - mc), send_sems, recv_sems, 7 * w, sib))
            for j, (ox, oy) in enumerate(others):
                och = 2 * ox + oy
                first.append(_remote(src(w), slot(w, chip, mc), send_sems, recv_sems, 7 * w + 1 + j, (ox, oy, mc)))
                landed.append(_remote(src(w), slot(w, och, mc), send_sems, recv_sems, 7 * w + 1 + j, (ox, oy, mc)))
                passed.append(_remote(slot(w, och, mc), slot(w, och, mc), send_sems, recv_sems, 7 * w + 4 + j, sib))
                from_sib.append(_remote(src(w), slot(w, och, 1 - mc), send_sems, recv_sems, 7 * w + 4 + j, sib))
        return locs, first, landed, passed, from_sib

    def start(ins, outs, sems):
        locs, first, _, _, _ = copies(ins, outs, sems)
        for cp in locs + first:
            cp.start()

    def finish(ins, outs, sems):
        locs, first, landed, passed, from_sib = copies(ins, outs, sems)
        for cp, fw in zip(landed, passed):
            cp.wait_recv()
            fw.start()
        for cp in from_sib:
            cp.wait_recv()
        for cp in first + passed:
            cp.wait_send()
        for cp in locs:
            cp.wait()

    return _Side(halves, [jax.ShapeDtypeStruct((N_CHIP,) + h.shape, h.dtype) for h in halves],
                 [pltpu.SemaphoreType.DMA((7 * n,)), pltpu.SemaphoreType.DMA((7 * n,)), pltpu.SemaphoreType.DMA((n,))],
                 start, finish)


def _pair_side(gs):
    n = len(gs)

    def copies(ins, outs, sems):
        mx, my, mc, _ = _place()
        return [_remote(ins[w].at[:, pl.ds(1 - mc, 1)], outs[w], sems[0], sems[1], w, (mx, my, 1 - mc)) for w in range(n)]

    def start(ins, outs, sems):
        for cp in copies(ins, outs, sems):
            cp.start()

    def finish(ins, outs, sems):
        for cp in copies(ins, outs, sems):
            cp.wait()

    return _Side(gs, [jax.ShapeDtypeStruct((g.shape[0], 1) + g.shape[2:], g.dtype) for g in gs],
                 [pltpu.SemaphoreType.DMA((n,)), pltpu.SemaphoreType.DMA((n,))], start, finish)


def _pair_add(g, got, core, name):
    s, _, rh, ccols = g.shape
    tr = _pick(rh, (256, 128))

    def body(core_ref, g_ref, r_ref, o_ref):
        o_ref[...] = (g_ref[...] + r_ref[...]).astype(o_ref.dtype)

    spec = pltpu.PrefetchScalarGridSpec(
        num_scalar_prefetch=1, grid=(s, rh // tr),
        in_specs=[pl.BlockSpec((None, None, tr, ccols), lambda a, i, cr: (a, cr[0], i, 0)),
                  pl.BlockSpec((None, None, tr, ccols), lambda a, i, cr: (a, 0, i, 0))],
        out_specs=pl.BlockSpec((None, tr, ccols), lambda a, i, cr: (a, i, 0)))
    return pl.pallas_call(body, grid_spec=spec, out_shape=jax.ShapeDtypeStruct((s, rh, ccols), MXU_DTYPE),
                          compiler_params=_params(), name=name)(core, g, got)


def _chips_side(ps):
    n = len(ps)

    def copies(ins, outs, sems):
        send_sems, recv_sems, local_sems = sems
        mx, my, mc, others = _place()
        chip = 2 * mx + my
        locs = [pltpu.make_async_copy(ins[w].at[chip], outs[w].at[chip], local_sems.at[w]) for w in range(n)]
        sends, lands = [], []
        for w in range(n):
            for j, (ox, oy) in enumerate(others):
                och = 2 * ox + oy
                sends.append(_remote(ins[w].at[och], outs[w].at[chip], send_sems, recv_sems, 3 * w + j, (ox, oy, mc)))
                lands.append(_remote(ins[w].at[och], outs[w].at[och], send_sems, recv_sems, 3 * w + j, (ox, oy, mc)))
        return locs, sends, lands

    def start(ins, outs, sems):
        locs, sends, _ = copies(ins, outs, sems)
        for cp in locs + sends:
            cp.start()

    def finish(ins, outs, sems):
        locs, sends, lands = copies(ins, outs, sems)
        for cp in lands:
            cp.wait_recv()
        for cp in sends:
            cp.wait_send()
        for cp in locs:
            cp.wait()

    return _Side(ps, [jax.ShapeDtypeStruct(p.shape, p.dtype) for p in ps],
                 [pltpu.SemaphoreType.DMA((3 * n,)), pltpu.SemaphoreType.DMA((3 * n,)), pltpu.SemaphoreType.DMA((n,))],
                 start, finish)


def _chip_sum(rb, core, name):
    s, rh, ccols = rb.shape
    tr = _pick(rh, (256, 128))

    def body(core_ref, r_ref, o_ref):
        tot = r_ref[0].astype(F32)
        for k in range(1, s):
            tot = tot + r_ref[k].astype(F32)
        o_ref[...] = tot

    spec = pltpu.PrefetchScalarGridSpec(
        num_scalar_prefetch=1, grid=(rh // tr,),
        in_specs=[pl.BlockSpec((s, tr, ccols), lambda i, cr: (0, i, 0))],
        out_specs=pl.BlockSpec((None, tr, ccols), lambda i, cr: (cr[0], i, 0)))
    return pl.pallas_call(body, grid_spec=spec, out_shape=jax.ShapeDtypeStruct((2, rh, ccols), F32),
                          compiler_params=_params(), name=name)(core, rb)


def _share_side(fulls):
    n = len(fulls)

    def copies(ins, outs, sems):
        mx, my, mc, _ = _place()
        sib = (mx, my, 1 - mc)
        sends = [_remote(ins[w].at[mc], outs[w].at[mc], sems[0], sems[1], w, sib) for w in range(n)]
        lands = [_remote(ins[w].at[mc], outs[w].at[1 - mc], sems[0], sems[1], w, sib) for w in range(n)]
        return sends, lands

    def start(ins, outs, sems):
        for cp in copies(ins, outs, sems)[0]:
            cp.start()

    def finish(ins, outs, sems):
        sends, lands = copies(ins, outs, sems)
        for cp in lands:
            cp.wait_recv()
        for cp in sends:
            cp.wait_send()

    return _Side(fulls, [jax.ShapeDtypeStruct(f.shape, f.dtype) for f in fulls],
                 [pltpu.SemaphoreType.DMA((n,)), pltpu.SemaphoreType.DMA((n,))], start, finish,
                 aliases=[(w, w) for w in range(n)])


def _cast(w, name):
    r, ccols = w.shape
    tr = _pick(r, (256, 128))

    def body(w_ref, o_ref):
        o_ref[...] = w_ref[...].astype(o_ref.dtype)

    spec = pl.BlockSpec((tr, ccols), lambda i: (i, 0))
    return pl.pallas_call(body, grid=(r // tr,), in_specs=[spec], out_specs=spec,
                          out_shape=jax.ShapeDtypeStruct((r, ccols), MXU_DTYPE), compiler_params=_params(), name=name)(w)


def _adam_math(g, w, m, v):
    m = ADAM_B1 * m + (1.0 - ADAM_B1) * g
    v = ADAM_B2 * v + (1.0 - ADAM_B2) * jnp.square(g)
    m_hat = m / (1.0 - ADAM_B1 ** ADAM_STEP)
    v_hat = v / (1.0 - ADAM_B2 ** ADAM_STEP)
    return -ADAM_LR * (m_hat / (jnp.sqrt(v_hat) + ADAM_EPS) + ADAM_WD * w), m, v


def _adamw_layer(l, g, w, m, v, bufs, name):
    depth, r, ccols = w.shape
    tr = _pick(r, (128,))

    def body(g_ref, w_ref, m_ref, v_ref, b0, b1, b2, b3, go_ref, d_ref, mo_ref, vo_ref):
        gv = g_ref[...]
        d, m2, v2 = _adam_math(gv, w_ref[...], m_ref[...], v_ref[...])
        go_ref[...] = gv
        d_ref[...] = d
        mo_ref[...] = m2
        vo_ref[...] = v2

    lay = pl.BlockSpec((None, tr, ccols), lambda i: (l, i, 0))
    return pl.pallas_call(
        body, grid=(r // tr,), in_specs=[pl.BlockSpec((tr, ccols), lambda i: (i, 0)), lay, lay, lay] + [ANY] * 4,
        out_specs=[lay] * 4, out_shape=[jax.ShapeDtypeStruct((depth, r, ccols), F32)] * 4,
        input_output_aliases={4: 0, 5: 1, 6: 2, 7: 3}, compiler_params=_params(), name=name)(g, w, m, v, *bufs)


def _adamw_small(g, g2, w, m, v, name):
    two = g2 is not None

    def body(*refs):
        if two:
            g_ref, g2_ref, w_ref, m_ref, v_ref, go_ref, d_ref, mo_ref, vo_ref = refs
            gv = g_ref[...] + g2_ref[...]
        else:
            g_ref, w_ref, m_ref, v_ref, go_ref, d_ref, mo_ref, vo_ref = refs
            gv = g_ref[...]
        d, m2, v2 = _adam_math(gv, w_ref[...], m_ref[...], v_ref[...])
        go_ref[...] = gv
        d_ref[...] = d
        mo_ref[...] = m2
        vo_ref[...] = v2

    args = [g] + ([g2] if two else []) + [w, m, v]
    vm = pl.BlockSpec(memory_space=pltpu.VMEM)
    return pl.pallas_call(body, in_specs=[vm] * len(args), out_specs=[vm] * 4,
                          out_shape=[jax.ShapeDtypeStruct(w.shape, F32)] * 4, name=name)(*args)


def _sum8(g, name):
    def body(g_ref, o_ref):
        tot = g_ref[0]
        for k in range(1, N_DEV):
            tot = tot + g_ref[k]
        o_ref[...] = tot

    vm = pl.BlockSpec(memory_space=pltpu.VMEM)
    return pl.pallas_call(body, in_specs=[vm], out_specs=vm, out_shape=jax.ShapeDtypeStruct(g.shape[1:], F32),
                          compiler_params=_params(), name=name)(g)


def _pack(arrs, width):
    flat = jnp.concatenate([a.reshape(-1).astype(F32) for a in arrs])
    rows = -(-flat.size // (8 * width)) * 8
    return jnp.pad(flat, (0, rows * width - flat.size)).reshape(rows, width)


def _unpack(flat, shapes):
    out, off = [], 0
    for shp in shapes:
        size = 1
        for v in shp:
            size *= v
        out.append(flat[..., off:off + size].reshape(flat.shape[:-1] + tuple(shp)))
        off += size
    return out


class _Comm:
    def __init__(self, core, params):
        self.core, self.params = core, params
        depth = params[0][1].shape[0]
        self.halves = [[_cast(w[l], f"cast_{nm}_{l}").reshape(2, w.shape[1] // 2, w.shape[2]) for nm, w, _, _ in params]
                       for l in range(depth)]
        self.stacked = [[lax.empty(w.shape, F32) for _ in range(4)] for _, w, _, _ in params]

    def weights(self, gathered):
        out = []
        for k, g in enumerate(gathered):
            rows, cols = 2 * g.shape[2], g.shape[3]
            out.append(g.reshape(N_CHIP, rows, cols) if k % 2 == 0 else g.reshape(1, N_CHIP * rows, cols))
        return tuple(out)


class _GradSync:
    def __init__(self, comm, l, grads):
        self.comm, self.l = comm, l
        self.gs = [g.reshape(N_CHIP, 2, g.shape[0] * g.shape[1] // (2 * N_CHIP), g.shape[2]) for g in grads]
        self.ps, self.rb, self.full = None, [None] * len(grads), None

    def pair_side(self):
        return _pair_side(self.gs)

    def add(self, got):
        self.ps = [_pair_add(g, r, self.comm.core, f"rs_add_{self.l}_{k}") for k, (g, r) in enumerate(zip(self.gs, got))]

    def chips_side(self, which):
        return _chips_side([self.ps[k] for k in which])

    def land(self, which, got):
        for k, r in zip(which, got):
            self.rb[k] = r

    def sum(self):
        self.full = [_chip_sum(r, self.comm.core, f"rs_sum_{self.l}_{k}") for k, r in enumerate(self.rb)]

    def share_side(self):
        return _share_side(self.full)

    def adam(self, got):
        for k, (full, (nm, w, m, v)) in enumerate(zip(got, self.comm.params)):
            gsum = full.reshape(2 * full.shape[1], full.shape[2])
            self.comm.stacked[k] = _adamw_layer(self.l, gsum, w, m, v, self.comm.stacked[k], f"adamw_{nm}_{self.l}")

    def run_alone(self):
        which = tuple(range(len(self.gs)))
        self.add(_run_side(self.pair_side(), f"rs_pair_{self.l}"))
        self.land(which, _run_side(self.chips_side(which), f"rs_chips_{self.l}"))
        self.sum()
        self.adam(_run_side(self.share_side(), f"rs_share_{self.l}"))


COND_ROWS = 16


def _ada_fwd(cond, w_ada, b_cols, name):
    depth, d, ns = w_ada.shape
    tn = _pick(ns, (512, 384, 256, 128))

    def body(c_ref, w_ref, b_ref, o_ref):
        o_ref[...] = _dot(_silu(c_ref[...]), w_ref[...]) + b_ref[...]

    return pl.pallas_call(
        body, grid=(depth, ns // tn),
        in_specs=[pl.BlockSpec((COND_ROWS, d), lambda l, j: (0, 0)), pl.BlockSpec((None, d, tn), lambda l, j: (l, 0, j)),
                  pl.BlockSpec((None, 1, tn), lambda l, j: (l, 0, j))],
        out_specs=pl.BlockSpec((None, COND_ROWS, tn), lambda l, j: (l, 0, j)),
        out_shape=jax.ShapeDtypeStruct((depth, COND_ROWS, ns), F32), compiler_params=_params(), name=name)(cond, w_ada, b_cols)


def _ada_bwd(cond, dmod, w_ada, name):
    depth, d, ns = w_ada.shape
    tn = _pick(ns, (512, 384, 256, 128))

    def body(c_ref, dm_ref, w_ref, gw_ref, dc_ref):
        @pl.when((pl.program_id(0) == 0) & (pl.program_id(1) == 0))
        def _():
            dc_ref[...] = jnp.zeros_like(dc_ref)

        dm = dm_ref[...]
        gw_ref[...] = _dot_tn(_silu(c_ref[...]), dm)
        dc_ref[...] += _dot_nt(dm, w_ref[...])

    return pl.pallas_call(
        body, grid=(depth, ns // tn),
        in_specs=[pl.BlockSpec((COND_ROWS, d), lambda l, j: (0, 0)),
                  pl.BlockSpec((None, COND_ROWS, tn), lambda l, j: (l, 0, j)),
                  pl.BlockSpec((None, d, tn), lambda l, j: (l, 0, j))],
        out_specs=[pl.BlockSpec((None, d, tn), lambda l, j: (l, 0, j)), pl.BlockSpec((COND_ROWS, d), lambda l, j: (0, 0))],
        out_shape=[jax.ShapeDtypeStruct((depth, d, ns), F32), jax.ShapeDtypeStruct((COND_ROWS, d), F32)],
        compiler_params=_params(), name=name)(cond, dmod, w_ada)


def _cctx_grad(parts, c_ctx, name):
    def body(p_ref, c_ref, o_ref):
        tot = p_ref[0, 0:1, :]
        for k in range(1, N_CHIP):
            tot = tot + p_ref[2 * k, 0:1, :]
        z = c_ref[...]
        sg = 1.0 / (1.0 + jnp.exp(-z))
        o_ref[...] = tot * (sg + z * sg * (1.0 - sg))

    vm = pl.BlockSpec(memory_space=pltpu.VMEM)
    return pl.pallas_call(body, in_specs=[vm, vm], out_specs=vm, out_shape=jax.ShapeDtypeStruct(c_ctx.shape, F32),
                          name=name)(parts, c_ctx)


def kernel(x, c, ctx, c_ctx, w_ada, b_ada, g_norm1, g_norm2, w_in, conv_w, conv_b, sink, g_out_conv, g_out_attn, w_out, w_mlp1, w_mlp2, g_final, loss_target, m_c_ctx, m_w_ada, m_b_ada, m_g_norm1, m_g_norm2, m_w_in, m_conv_w, m_conv_b, m_sink, m_g_out_conv, m_g_out_attn, m_w_out, m_w_mlp1, m_w_mlp2, m_g_final, v_c_ctx, v_w_ada, v_b_ada, v_g_norm1, v_g_norm2, v_w_in, v_conv_w, v_conv_b, v_sink, v_g_out_conv, v_g_out_attn, v_w_out, v_w_mlp1, v_w_mlp2, v_g_final):
    mx, my, mc = lax.axis_index("x"), lax.axis_index("y"), lax.axis_index("c")
    chip, rank = 2 * mx + my, 4 * mx + 2 * my + mc
    core = jnp.reshape(mc, (1,)).astype(I32)
    depth, d = g_norm1.shape
    s_len, ctx_rows = x.shape[1], ctx.shape[1]
    cw_cols = conv_w.shape[2]
    c_conv = cw_cols * N_CHIP
    n_heads = sink.shape[1]
    ns_ada = w_ada.shape[2]

    got = _allgather8(_pack([c, conv_w], d), "gather_cond")
    flat = got.reshape(N_DEV, -1)
    conv_w_full = jnp.transpose(flat[::2, d:d + conv_w.size].reshape(N_CHIP, depth, 3, cw_cols), (1, 2, 0, 3))
    conv_w_full = conv_w_full.reshape(depth, 3, c_conv)
    cond = jnp.zeros((COND_ROWS, d), F32).at[:N_DEV].set(flat[:, :d]).at[N_DEV].set(c_ctx)

    b_cols = lax.dynamic_slice_in_dim(b_ada, chip * ns_ada, ns_ada, axis=1)[:, None, :]
    mod_cols = _ada_fwd(cond, w_ada, b_cols, "ada_fwd")
    got = _allgather8(mod_cols.reshape(depth * COND_ROWS, ns_ada), "gather_mod")
    mod_all = jnp.transpose(got[::2].reshape(N_CHIP, depth, COND_ROWS, ns_ada), (1, 2, 0, 3))
    mod_all = mod_all.reshape(depth, COND_ROWS, N_CHIP * ns_ada)
    mod_me = lax.dynamic_index_in_dim(mod_all, rank, axis=1, keepdims=False)
    mods = jnp.stack([mod_all[:, N_DEV], mod_me], axis=1).reshape(depth, 2, N_MOD, 1, d)

    comm = _Comm(core, (("w_in", w_in, m_w_in, v_w_in), ("w_out", w_out, m_w_out, v_w_out),
                        ("w_mlp1", w_mlp1, m_w_mlp1, v_w_mlp1), ("w_mlp2", w_mlp2, m_w_mlp2, v_w_mlp2)))
    small = dict(g_norm1=g_norm1[:, None], g_norm2=g_norm2[:, None], conv_w=conv_w_full, conv_b=conv_b[:, None], sink=sink,
                 g_out_conv=g_out_conv[:, None], g_out_attn=g_out_attn[:, None], g_final=g_final[None])
    x0 = jnp.concatenate([ctx[0], x[0]], axis=0)
    loss_part, dx0, _, dmods, sg = _local_step(x0, loss_target[0], mods, None, small, ctx_rows=ctx_rows, comm=comm)
    loss = lax.psum(loss_part, ("x", "y", "c"))
    grad_x = dx0[ctx_rows:][None]
    stacked = dict(zip(("w_in", "w_out", "w_mlp1", "w_mlp2"), comm.stacked))

    names = ("g_norm1", "g_norm2", "conv_w", "conv_b", "sink", "g_out_conv", "g_out_attn", "g_final")
    shapes = [(depth, 2, N_MOD * d)] + [sg[k].shape for k in names]
    got = _allgather8(_pack([dmods] + [sg[k] for k in names], d), "gather_small")
    tot = _unpack(_sum8(got, "sum_small").reshape(-1), shapes)
    dmod_tot, small_tot = tot[0], dict(zip(names, tot[1:]))
    dmod_lat = _unpack(got.reshape(N_DEV, -1), shapes[:1])[0][:, :, 1]
    dm_rows = jnp.zeros((depth, COND_ROWS, N_MOD * d), F32)
    dm_rows = dm_rows.at[:, :N_DEV].set(jnp.transpose(dmod_lat, (1, 0, 2))).at[:, N_DEV].set(dmod_tot[:, 0])
    dm_cols = lax.dynamic_slice_in_dim(dm_rows, chip * ns_ada, ns_ada, axis=2)
    g_w_ada, dcond = _ada_bwd(cond, dm_cols, w_ada, "ada_bwd")
    got = _allgather8(dcond[N_DEV:N_DEV + 8], "gather_dcond")
    g_c_ctx = _cctx_grad(got, c_ctx[None], "c_ctx_grad")

    res = {}
    ada_bufs = [lax.empty((1,) + (depth * d, ns_ada), F32) for _ in range(4)]
    res["w_ada"] = [r.reshape(w_ada.shape) for r in _adamw_layer(
        0, g_w_ada.reshape(depth * d, ns_ada), w_ada.reshape(1, depth * d, ns_ada),
        m_w_ada.reshape(1, depth * d, ns_ada), v_w_ada.reshape(1, depth * d, ns_ada), ada_bufs, "adamw_w_ada")]
    for nm in ("w_in", "w_out", "w_mlp1", "w_mlp2"):
        res[nm] = stacked[nm]
    res["c_ctx"] = [r[0] for r in _adamw_small(g_c_ctx, None, c_ctx[None], m_c_ctx[None], v_c_ctx[None], "adamw_c_ctx")]
    res["b_ada"] = _adamw_small(dmod_tot[:, 0], dmod_tot[:, 1], b_ada, m_b_ada, v_b_ada, "adamw_b_ada")
    cw_grad = lax.dynamic_slice_in_dim(small_tot["conv_w"], chip * cw_cols, cw_cols, axis=2)
    res["conv_w"] = [r.reshape(conv_w.shape) for r in _adamw_small(
        cw_grad.reshape(depth * 3, cw_cols), None, conv_w.reshape(depth * 3, cw_cols),
        m_conv_w.reshape(depth * 3, cw_cols), v_conv_w.reshape(depth * 3, cw_cols), "adamw_conv_w")]
    for nm, w, m, v in (("g_norm1", g_norm1, m_g_norm1, v_g_norm1), ("g_norm2", g_norm2, m_g_norm2, v_g_norm2),
                        ("conv_b", conv_b, m_conv_b, v_conv_b), ("sink", sink, m_sink, v_sink),
                        ("g_out_conv", g_out_conv, m_g_out_conv, v_g_out_conv),
                        ("g_out_attn", g_out_attn, m_g_out_attn, v_g_out_attn)):
        res[nm] = _adamw_small(small_tot[nm], None, w, m, v, f"adamw_{nm}")
    res["g_final"] = [r[0] for r in _adamw_small(small_tot["g_final"][None], None, g_final[None], m_g_final[None],
                                                 v_g_final[None], "adamw_g_final")]
    order = ("c_ctx", "w_ada", "b_ada", "g_norm1", "g_norm2", "w_in", "conv_w", "conv_b", "sink", "g_out_conv",
             "g_out_attn", "w_out", "w_mlp1", "w_mlp2", "g_final")
    return (loss, grad_x, *[res[n][0] for n in order], *[res[n][1] for n in order], *[res[n][2] for n in order],
            *[res[n][3] for n in order])
```

```python
import functools

import jax
import jax.numpy as jnp
from jax import lax
from jax.experimental import pallas as pl
from jax.experimental.pallas import tpu as pltpu

F32 = jnp.float32
I32 = jnp.int32
MXU_DTYPE = jnp.bfloat16
EPS = 1e-6
HEAD_DIM = 64
N_KV = 4
WINDOW = 128
QBLK = 128
LANES = 128
GRID_W = 64
ROPE_THETA = 10000.0
NEG_INF = -1e30
N_MOD = 6
HALO = 8
ADAM_LR, ADAM_B1, ADAM_B2, ADAM_EPS, ADAM_WD, ADAM_STEP = 0.001, 0.9, 0.999, 1e-08, 0.01, 10
V7X_VMEM_BYTES = 64 * 1024 * 1024
VMEM_LIMIT = V7X_VMEM_BYTES * 3 // 4
ROW_TILES = (1408, 768, 512, 640, 256, 128)
MESH = pl.DeviceIdType.MESH
ANY = pl.BlockSpec(memory_space=pl.ANY)


def _params():
    return pltpu.CompilerParams(vmem_limit_bytes=VMEM_LIMIT)


def _pick(n, cands):
    for c in cands:
        if n % c == 0:
            return c
    raise ValueError(f"no tile of {cands} divides {n}")


def _mx(v):
    return v.astype(MXU_DTYPE)


def _dot(a, b):
    return jnp.dot(_mx(a), _mx(b), preferred_element_type=F32)


def _dot_nt(a, b):
    return lax.dot_general(_mx(a), _mx(b), (((1,), (1,)), ((), ())), preferred_element_type=F32)


def _dot_tn(a, b):
    return lax.dot_general(_mx(a), _mx(b), (((0,), (0,)), ((), ())), preferred_element_type=F32)


def _silu(v):
    return v / (1.0 + jnp.exp(-v))


class _Side:
    def __init__(self, inputs, out_shapes, scratch, start, finish, aliases=(), middle=None):
        self.inputs, self.out_shapes, self.scratch = list(inputs), list(out_shapes), list(scratch)
        self.start, self.finish, self.aliases = start, finish, tuple(aliases)
        self.middle = middle


def _call(body, *, grid, in_specs, out_specs, out_shape, scratch, args, name, side=None):
    if side is None:
        res = pl.pallas_call(body, grid=grid, in_specs=in_specs, out_specs=out_specs, out_shape=out_shape,
                             scratch_shapes=scratch, compiler_params=_params(), name=name)(*args)
        return list(res), []
    ni, no, ns = len(in_specs), len(out_specs), len(scratch)
    si, so = len(side.inputs), len(side.out_shapes)

    def full(*refs):
        ins, sins = refs[:ni], refs[ni:ni + si]
        outs, souts = refs[ni + si:ni + si + no], refs[ni + si + no:ni + si + no + so]
        scr, sems = refs[ni + si + no + so:ni + si + no + so + ns], refs[ni + si + no + so + ns:]
        ids = [pl.program_id(k) for k in range(len(grid))]
        first, last = ids[0] == 0, ids[0] == grid[0] - 1
        for k in range(1, len(grid)):
            first, last = first & (ids[k] == 0), last & (ids[k] == grid[k] - 1)

        @pl.when(first)
        def _():
            side.start(sins, souts, sems)

        body(*ins, *outs, *scr)

        if side.middle is not None:
            lin, total = ids[0], grid[0]
            for k in range(1, len(grid)):
                lin, total = lin * grid[k] + ids[k], total * grid[k]

            @pl.when(lin == (3 * total) // 4)
            def _():
                side.middle(sins, souts, sems)

        @pl.when(last)
        def _():
            side.finish(sins, souts, sems)

    res = pl.pallas_call(
        full, grid=grid, in_specs=list(in_specs) + [ANY] * si, out_specs=list(out_specs) + [ANY] * so,
        out_shape=list(out_shape) + side.out_shapes, scratch_shapes=list(scratch) + side.scratch,
        input_output_aliases={ni + a: no + b for a, b in side.aliases}, compiler_params=_params(), name=name)(
            *args, *side.inputs)
    return list(res[:no]), list(res[no:])


def _run_side(side, name):
    si, so = len(side.inputs), len(side.out_shapes)

    def body(*refs):
        side.start(refs[:si], refs[si:si + so], refs[si + so:])
        if side.middle is not None:
            side.middle(refs[:si], refs[si:si + so], refs[si + so:])
        side.finish(refs[:si], refs[si:si + so], refs[si + so:])

    res = pl.pallas_call(body, in_specs=[ANY] * si, out_specs=[ANY] * so, out_shape=side.out_shapes,
                         scratch_shapes=side.scratch, input_output_aliases=dict(side.aliases), name=name)(*side.inputs)
    return list(res)


def _mm_nn(a, b3, *, out_dtypes, epilogue, name, side=None):
    m, k = a.shape
    s, _, ns = b3.shape
    tm = _pick(m, ROW_TILES)
    tn = _pick(ns, (1024, 1152, 640, 512, 256, 128))
    tk = _pick(k, (2048, 1024, 512))
    nbs, nk = ns // tn, k // tk
    n_out = len(out_dtypes)

    def body(a_ref, b_ref, *rest):
        outs = rest[:n_out]

        def write(v):
            for o, r in zip(outs, epilogue(v)):
                o[...] = r.astype(o.dtype)

        if nk == 1:
            write(jnp.dot(a_ref[...], b_ref[...], preferred_element_type=F32))
            return
        acc = rest[n_out]
        kk = pl.program_id(2)

        @pl.when(kk == 0)
        def _():
            acc[...] = jnp.zeros_like(acc)

        acc[...] += jnp.dot(a_ref[...], b_ref[...], preferred_element_type=F32)

        @pl.when(kk == nk - 1)
        def _():
            write(acc[...])

    return _call(
        body, grid=(m // tm, s * nbs, nk),
        in_specs=[pl.BlockSpec((tm, tk), lambda i, j, kk: (i, kk)),
                  pl.BlockSpec((None, tk, tn), lambda i, j, kk: (j // nbs, kk, j % nbs))],
        out_specs=[pl.BlockSpec((tm, tn), lambda i, j, kk: (i, j))] * n_out,
        out_shape=[jax.ShapeDtypeStruct((m, s * ns), dt) for dt in out_dtypes],
        scratch=[pltpu.VMEM((tm, tn), F32)] if nk > 1 else [], args=(a, b3), name=name, side=side)


def _mm_nt(a, b3, *, out_dtype, name, extra=None, epilogue=None, side=None):
    m = a.shape[0]
    s, ko, ns = b3.shape
    tm = _pick(m, ROW_TILES)
    tko = _pick(ko, (1024, 512))
    tn = _pick(ns, (2048, 1152, 1024, 640, 512, 256, 128))
    nbs = ns // tn
    nk = s * nbs
    n_in = 3 if extra is not None else 2

    def body(a_ref, b_ref, *rest):
        x_ref = rest[0] if extra is not None else None
        o_ref = rest[n_in - 2]

        def write(v):
            if epilogue is not None:
                v = epilogue(v, x_ref[...])
            o_ref[...] = v.astype(o_ref.dtype)

        if nk == 1:
            write(lax.dot_general(a_ref[...], b_ref[...], (((1,), (1,)), ((), ())), preferred_element_type=F32))
            return
        acc = rest[n_in - 1]
        kk = pl.program_id(2)

        @pl.when(kk == 0)
        def _():
            acc[...] = jnp.zeros_like(acc)

        acc[...] += lax.dot_general(a_ref[...], b_ref[...], (((1,), (1,)), ((), ())), preferred_element_type=F32)

        @pl.when(kk == nk - 1)
        def _():
            write(acc[...])

    in_specs = [pl.BlockSpec((tm, tn), lambda i, j, kk: (i, kk)),
                pl.BlockSpec((None, tko, tn), lambda i, j, kk: (kk // nbs, j, kk % nbs))]
    args = [a, b3]
    if extra is not None:
        in_specs.append(pl.BlockSpec((tm, tko), lambda i, j, kk: (i, j)))
        args.append(extra)
    return _call(
        body, grid=(m // tm, ko // tko, nk), in_specs=in_specs,
        out_specs=[pl.BlockSpec((tm, tko), lambda i, j, kk: (i, j))],
        out_shape=[jax.ShapeDtypeStruct((m, ko), out_dtype)],
        scratch=[pltpu.VMEM((tm, tko), F32)] if nk > 1 else [], args=args, name=name, side=side)


def _mm_tn(a, b, *, shards, name, side=None):
    t, k = a.shape
    ns = b.shape[1] // shards
    tt = _pick(t, (2 * ROW_TILES[0],) + ROW_TILES)
    tk = _pick(k, (1024, 512))
    tn = _pick(ns, (1024, 1152, 640, 512, 256, 128))
    nbs, nt = ns // tn, t // tt

    def body(a_ref, b_ref, o_ref, acc):
        tt_i = pl.program_id(2)

        @pl.when(tt_i == 0)
        def _():
            acc[...] = jnp.zeros_like(acc)

        acc[...] += lax.dot_general(a_ref[...], b_ref[...], (((0,), (0,)), ((), ())), preferred_element_type=F32)

        @pl.when(tt_i == nt - 1)
        def _():
            o_ref[...] = acc[...]

    return _call(
        body, grid=(k // tk, shards * nbs, nt),
        in_specs=[pl.BlockSpec((tt, tk), lambda i, j, q: (q, i)),
                  pl.BlockSpec((tt, tn), lambda i, j, q: (q, j))],
        out_specs=[pl.BlockSpec((None, tk, tn), lambda i, j, q: (j // nbs, i, j % nbs))],
        out_shape=[jax.ShapeDtypeStruct((shards, k, ns), F32)], scratch=[pltpu.VMEM((tk, tn), F32)], args=(a, b),
        name=name, side=side)


def _row_tile(t, ctx_rows):
    return 256 if t % 256 == 0 and ctx_rows % 256 == 0 else 128


def _mod_spec(d, ncb, idx):
    return pl.BlockSpec((None, None, 1, d), lambda i: (jnp.where(i >= ncb, 1, 0), idx, 0, 0))


def _norm_fwd(x, z, gate_mods, mods, g, *, gate_idx, shift_idx, scale_idx, ctx_rows, name):
    t, d = x.shape
    tr = _row_tile(t, ctx_rows)
    ncb = ctx_rows // tr
    row = pl.BlockSpec((tr, d), lambda i: (i, 0))
    vec = pl.BlockSpec((1, d), lambda i: (0, 0))
    resid = z is not None

    def body(*refs):
        if resid:
            x_ref, z_ref, gt_ref, g_ref, sh_ref, sc_ref, xo_ref, h_ref = refs
            xn = x_ref[...] + gt_ref[...] * z_ref[...].astype(F32)
            xo_ref[...] = xn
        else:
            x_ref, g_ref, sh_ref, sc_ref, h_ref = refs
            xn = x_ref[...]
        r = lax.rsqrt(jnp.mean(xn * xn, axis=-1, keepdims=True) + EPS)
        h_ref[...] = ((xn * r * g_ref[...]) * (1.0 + sc_ref[...]) + sh_ref[...]).astype(h_ref.dtype)

    mspecs = [vec, _mod_spec(d, ncb, shift_idx), _mod_spec(d, ncb, scale_idx)]
    if resid:
        in_specs = [row, row, _mod_spec(d, ncb, gate_idx)] + mspecs
        args = (x, z, gate_mods, g, mods, mods)
        out_specs = [row, row]
        out_shape = [jax.ShapeDtypeStruct((t, d), F32), jax.ShapeDtypeStruct((t, d), MXU_DTYPE)]
    else:
        in_specs = [row] + mspecs
        args = (x, g, mods, mods)
        out_specs = row
        out_shape = jax.ShapeDtypeStruct((t, d), MXU_DTYPE)
    return pl.pallas_call(body, grid=(t // tr,), in_specs=in_specs, out_specs=out_specs, out_shape=out_shape,
                          compiler_params=_params(), name=name)(*args)


R_DSHIFT, R_DSCALE, R_DG, R_DGATE = 0, 2, 4, 5


def _norm_bwd(dx, dh, xin, mods, g, *, scale_idx, ctx_rows, name, prev=None):
    t, d = dx.shape
    tr = _row_tile(t, ctx_rows)
    ncb = ctx_rows // tr
    row = pl.BlockSpec((tr, d), lambda i: (i, 0))
    vec = pl.BlockSpec((1, d), lambda i: (0, 0))
    acc_spec = pl.BlockSpec((8, d), lambda i: (0, 0))
    has_prev = prev is not None

    def body(*refs):
        if has_prev:
            dx_ref, dh_ref, x_ref, g_ref, sc_ref, z_ref, gt_ref, dxo_ref, dz_ref, acc = refs
        else:
            dx_ref, dh_ref, x_ref, g_ref, sc_ref, dxo_ref, acc = refs
        i = pl.program_id(0)

        @pl.when(i == 0)
        def _():
            acc[...] = jnp.zeros_like(acc)

        lat = jnp.where(i >= ncb, 1.0, 0.0)
        x = x_ref[...]
        r = lax.rsqrt(jnp.mean(x * x, axis=-1, keepdims=True) + EPS)
        xhat = x * r
        gv = g_ref[...]
        dhv = dh_ref[...].astype(F32)
        dn = dhv * (1.0 + sc_ref[...])
        dxhat = dn * gv
        dxin = dx_ref[...] + r * (dxhat - xhat * jnp.mean(dxhat * xhat, axis=-1, keepdims=True))
        dxo_ref[...] = dxin
        dshift = jnp.sum(dhv, axis=0, keepdims=True)
        dscale = jnp.sum(dhv * (xhat * gv), axis=0, keepdims=True)
        acc[R_DSHIFT:R_DSHIFT + 1, :] += dshift * (1.0 - lat)
        acc[R_DSHIFT + 1:R_DSHIFT + 2, :] += dshift * lat
        acc[R_DSCALE:R_DSCALE + 1, :] += dscale * (1.0 - lat)
        acc[R_DSCALE + 1:R_DSCALE + 2, :] += dscale * lat
        acc[R_DG:R_DG + 1, :] += jnp.sum(dn * xhat, axis=0, keepdims=True)
        if has_prev:
            dz_ref[...] = (dxin * gt_ref[...]).astype(dz_ref.dtype)
            dgate = jnp.sum(dxin * z_ref[...].astype(F32), axis=0, keepdims=True)
            acc[R_DGATE:R_DGATE + 1, :] += dgate * (1.0 - lat)
            acc[R_DGATE + 1:R_DGATE + 2, :] += dgate * lat

    in_specs = [row, row, row, vec, _mod_spec(d, ncb, scale_idx)]
    args = [dx, dh, xin, g, mods]
    out_specs = [row]
    out_shape = [jax.ShapeDtypeStruct((t, d), F32)]
    if has_prev:
        z, gate_mods, gate_idx = prev
        in_specs += [row, _mod_spec(d, ncb, gate_idx)]
        args += [z, gate_mods]
        out_specs.append(row)
        out_shape.append(jax.ShapeDtypeStruct((t, d), MXU_DTYPE))
    out_specs.append(acc_spec)
    out_shape.append(jax.ShapeDtypeStruct((8, d), F32))
    return pl.pallas_call(body, grid=(t // tr,), in_specs=in_specs, out_specs=out_specs, out_shape=out_shape,
                          compiler_params=_params(), name=name)(*args)


R_FGATE, R_FG, R_FLOSS = 0, 2, 3


def _final(x1, o, gate_mods, g_final, tgt, *, ctx_rows, name):
    t, d = x1.shape
    tr = _row_tile(t, ctx_rows)
    ncb, nb = ctx_rows // tr, t // tr
    row = pl.BlockSpec((tr, d), lambda i: (i, 0))
    vec = pl.BlockSpec((1, d), lambda i: (0, 0))

    def body(x_ref, o_ref, gt_ref, g_ref, t_ref, dx_ref, do_ref, acc, lsum):
        i = pl.program_id(0)

        @pl.when(i == 0)
        def _():
            acc[...] = jnp.zeros_like(acc)
            lsum[...] = jnp.zeros_like(lsum)

        lat = jnp.where(i >= ncb, 1.0, 0.0)
        gt = gt_ref[...]
        ov = o_ref[...].astype(F32)
        x = x_ref[...] + gt * ov
        r = lax.rsqrt(jnp.mean(x * x, axis=-1, keepdims=True) + EPS)
        xhat = x * r
        gv = g_ref[...]
        err = (xhat * gv - t_ref[...]) * lat
        dy = err / d
        dxhat = dy * gv
        dxv = r * (dxhat - xhat * jnp.mean(dxhat * xhat, axis=-1, keepdims=True))
        dx_ref[...] = dxv
        do_ref[...] = (dxv * gt).astype(do_ref.dtype)
        dgate = jnp.sum(dxv * ov, axis=0, keepdims=True)
        acc[R_FGATE:R_FGATE + 1, :] += dgate * (1.0 - lat)
        acc[R_FGATE + 1:R_FGATE + 2, :] += dgate * lat
        acc[R_FG:R_FG + 1, :] += jnp.sum(dy * xhat, axis=0, keepdims=True)
        lsum[...] += jnp.sum(err * err, axis=0, keepdims=True)

        @pl.when(i == nb - 1)
        def _():
            total = (0.5 / d) * jnp.sum(lsum[...], axis=-1, keepdims=True)
            acc[R_FLOSS:R_FLOSS + 1, :] = jnp.broadcast_to(total, (1, d))

    return pl.pallas_call(
        body, grid=(nb,),
        in_specs=[row, row, _mod_spec(d, ncb, 5), vec, pl.BlockSpec((tr, d), lambda i: (jnp.maximum(i - ncb, 0), 0))],
        out_specs=[row, row, pl.BlockSpec((8, d), lambda i: (0, 0))],
        out_shape=[jax.ShapeDtypeStruct((t, d), F32), jax.ShapeDtypeStruct((t, d), MXU_DTYPE),
                   jax.ShapeDtypeStruct((8, d), F32)],
        scratch_shapes=[pltpu.VMEM((1, d), F32)], compiler_params=_params(), name=name)(x1, o, gate_mods, g_final, tgt)


def _rope_tables(s, ctx_rows):
    rows = s // GRID_W
    row_pos = jnp.repeat(jnp.arange(rows, dtype=F32), GRID_W)
    col_pos = jnp.tile(jnp.arange(GRID_W, dtype=F32), rows)
    quarter = HEAD_DIM // 4
    inv = ROPE_THETA ** (-jnp.arange(0, 2 * quarter, 2, dtype=F32) / (2 * quarter))
    ang_r, ang_c = row_pos[:, None] * inv[None, :], col_pos[:, None] * inv[None, :]
    cr, sr, cc, sc = jnp.cos(ang_r), jnp.sin(ang_r), jnp.cos(ang_c), jnp.sin(ang_c)
    zero = jnp.zeros_like(sr)
    cos = jnp.concatenate([cr, cr, cc, cc], axis=1)
    sa = jnp.concatenate([zero, sr, zero, sc], axis=1)
    sb = jnp.concatenate([-sr, zero, -sc, zero], axis=1)

    def full(tab, fill):
        tab = jnp.tile(tab, (1, LANES // HEAD_DIM))
        return jnp.concatenate([jnp.full((ctx_rows, LANES), fill, F32), tab], axis=0)

    return full(cos, 1.0), full(sa, 0.0), full(sb, 0.0)


def _rope_apply(x, cos, sa, sb, transpose):
    n = x.shape[1] // LANES
    cos, sa, sb = (jnp.tile(v, (1, n)) for v in (cos, sa, sb))
    quarter = HEAD_DIM // 4
    width = x.shape[1]
    if transpose:
        return x * cos + pltpu.roll(x * sa, width - quarter, 1) + pltpu.roll(x * sb, quarter, 1)
    return x * cos + pltpu.roll(x, quarter, 1) * sa + pltpu.roll(x, width - quarter, 1) * sb


def _twice(x):
    lane = lax.broadcasted_iota(I32, (x.shape[0], LANES), 1)
    out = []
    for j in range(N_KV):
        blk = x[:, (j // 2) * LANES:(j // 2 + 1) * LANES]
        own = jnp.where((lane < HEAD_DIM) if j % 2 == 0 else (lane >= HEAD_DIM), blk, 0.0)
        out.append(own + pltpu.roll(own, HEAD_DIM, 1))
    return jnp.concatenate(out, axis=1)


def _rope_fwd(p, tabs, *, da, kw, kv_start, name):
    t = p.shape[0]
    tr = _pick(t, (256, 128))

    def body(q_ref, k_ref, v_ref, c_ref, a_ref, b_ref, qo_ref, ko_ref, vo_ref):
        cos, sa, sb = c_ref[...], a_ref[...], b_ref[...]
        qo_ref[...] = (_rope_apply(q_ref[...], cos, sa, sb, False) * (HEAD_DIM ** -0.5)).astype(qo_ref.dtype)
        ko_ref[...] = _twice(_rope_apply(k_ref[...], cos, sa, sb, False)).astype(ko_ref.dtype)
        vo_ref[...] = _twice(v_ref[...]).astype(vo_ref.dtype)

    tab = pl.BlockSpec((tr, LANES), lambda i: (i, 0))
    two = pl.BlockSpec((tr, N_KV * LANES), lambda i: (i, 0))
    return pl.pallas_call(
        body, grid=(t // tr,),
        in_specs=[pl.BlockSpec((tr, da), lambda i: (i, (kv_start - da) // da)),
                  pl.BlockSpec((tr, kw), lambda i: (i, kv_start // kw)),
                  pl.BlockSpec((tr, kw), lambda i: (i, kv_start // kw + 1)), tab, tab, tab],
        out_specs=[pl.BlockSpec((tr, da), lambda i: (i, 0)), two, two],
        out_shape=[jax.ShapeDtypeStruct((t, da), MXU_DTYPE), jax.ShapeDtypeStruct((t, N_KV * LANES), MXU_DTYPE),
                   jax.ShapeDtypeStruct((t, N_KV * LANES), MXU_DTYPE)],
        compiler_params=_params(), name=name)(p, p, p, *tabs)


def _rope_bwd(dq, dk, dv, tabs, dp, *, name):
    t, da = dq.shape
    kw = dk.shape[1]
    width = da + 2 * kw
    tr = _pick(t, (256, 128))
    col = (dp.shape[1] - width) // width
    assert col * width == dp.shape[1] - width

    def body(q_ref, k_ref, v_ref, c_ref, a_ref, b_ref, dp_ref, o_ref):
        cos, sa, sb = c_ref[...], a_ref[...], b_ref[...]
        o_ref[:, :da] = _rope_apply(q_ref[...], cos, sa, sb, True).astype(o_ref.dtype)
        o_ref[:, da:da + kw] = _rope_apply(k_ref[...], cos, sa, sb, True).astype(o_ref.dtype)
        o_ref[:, da + kw:] = v_ref[...].astype(o_ref.dtype)

    tab = pl.BlockSpec((tr, LANES), lambda i: (i, 0))
    return pl.pallas_call(
        body, grid=(t // tr,),
        in_specs=[pl.BlockSpec((tr, da), lambda i: (i, 0)), pl.BlockSpec((tr, kw), lambda i: (i, 0)),
                  pl.BlockSpec((tr, kw), lambda i: (i, 0)), tab, tab, tab, ANY],
        out_specs=pl.BlockSpec((tr, width), lambda i: (i, col)),
        out_shape=jax.ShapeDtypeStruct(dp.shape, dp.dtype), input_output_aliases={6: 0},
        compiler_params=_params(), name=name)(dq, dk, dv, *tabs, dp)


def _attn_specs(t, ctx_rows):
    nblk = t // QBLK

    def clip(i):
        return jnp.clip(i, 0, nblk - 1)

    return [pl.BlockSpec((QBLK, 2 * LANES), lambda b, i: (clip(i - 1), b)),
            pl.BlockSpec((QBLK, 2 * LANES), lambda b, i: (i, b)),
            pl.BlockSpec((QBLK, 2 * LANES), lambda b, i: (clip(i + 1), b)),
            pl.BlockSpec((ctx_rows, 2 * LANES), lambda b, i: (0, b))]


def _band_bias(rows):
    rr = lax.broadcasted_iota(I32, (rows, 3 * QBLK), 0) & (QBLK - 1)
    cc = lax.broadcasted_iota(I32, (rows, 3 * QBLK), 1)
    return jnp.where(jnp.abs(cc - QBLK - rr) <= WINDOW, 0.0, NEG_INF).astype(F32)


def _local_bias(band_ref, i, t, ctx_rows):
    cc = lax.broadcasted_iota(I32, (1, 3 * QBLK), 1)
    keyrow = (i - 1) * QBLK + cc
    first_key = jnp.where(i * QBLK >= ctx_rows, ctx_rows, t)
    return band_ref[...] + jnp.where((keyrow >= first_key) & (keyrow < t), 0.0, NEG_INF)


def _stack_heads(ref, c0, nblocks):
    lane = lax.broadcasted_iota(I32, (QBLK, LANES), 1)
    rows = []
    for qb in range(nblocks):
        blk = ref[:, c0 + qb * LANES:c0 + (qb + 1) * LANES].astype(F32)
        rows += [jnp.where(lane < HEAD_DIM, blk, 0.0), jnp.where(lane < HEAD_DIM, 0.0, blk)]
    return jnp.concatenate(rows, axis=0)


def _unstack_heads(x, nblocks):
    lane = lax.broadcasted_iota(I32, (QBLK, LANES), 1)
    return [jnp.where(lane < HEAD_DIM, x[2 * qb * QBLK:(2 * qb + 1) * QBLK], x[(2 * qb + 1) * QBLK:(2 * qb + 2) * QBLK])
            for qb in range(nblocks)]


def _sink_column(sink_ref, head0, group):
    row = lax.broadcasted_iota(I32, (group * QBLK, 1), 0)
    col = jnp.zeros((group * QBLK, 1), F32)
    for g in range(group):
        col = jnp.where((row >= g * QBLK) & (row < (g + 1) * QBLK), sink_ref[head0 + g], col)
    return col


def _probs(qs, k_loc, k_ctx, bias, snk):
    s_loc = _dot_nt(qs, k_loc) + bias
    s_ctx = _dot_nt(qs, k_ctx)
    m = jnp.maximum(jnp.maximum(jnp.max(s_loc, axis=-1, keepdims=True), jnp.max(s_ctx, axis=-1, keepdims=True)), snk)
    e_loc, e_ctx, e_snk = jnp.exp(s_loc - m), jnp.exp(s_ctx - m), jnp.exp(snk - m)
    inv = 1.0 / (jnp.sum(e_loc, axis=-1, keepdims=True) + jnp.sum(e_ctx, axis=-1, keepdims=True) + e_snk)
    return e_loc * inv, e_ctx * inv, e_snk * inv


def _attn_fwd(qs, k2, v2, sink, *, ctx_rows, name, side=None):
    t, da = qs.shape
    group = da // HEAD_DIM // N_KV
    nqb = group // 2
    kvspecs = _attn_specs(t, ctx_rows)

    def body(sink_ref, band_ref, q_ref, kp, kc, kn, kx, vp, vc, vn, vx, o_ref):
        b, i = pl.program_id(0), pl.program_id(1)
        valid = _local_bias(band_ref, i, t, ctx_rows)
        for half in range(2):
            lanes = slice(half * LANES, (half + 1) * LANES)
            qst = _mx(_stack_heads(q_ref, half * nqb * LANES, nqb))
            k_loc = jnp.concatenate([kp[:, lanes], kc[:, lanes], kn[:, lanes]], axis=0)
            v_loc = jnp.concatenate([vp[:, lanes], vc[:, lanes], vn[:, lanes]], axis=0)
            snk = _sink_column(sink_ref, (2 * b + half) * group, group)
            p_loc, p_ctx, _ = _probs(qst, k_loc, kx[:, lanes], valid, snk)
            out = _dot(p_loc, v_loc) + _dot(p_ctx, vx[:, lanes])
            for qb, blk in enumerate(_unstack_heads(out, nqb)):
                c0 = (half * nqb + qb) * LANES
                o_ref[:, c0:c0 + LANES] = blk

    qspec = pl.BlockSpec((QBLK, 2 * group * HEAD_DIM), lambda b, i: (i, b))
    band = pl.BlockSpec((group * QBLK, 3 * QBLK), lambda b, i: (0, 0))
    (ao,), got = _call(
        body, grid=(N_KV // 2, t // QBLK),
        in_specs=[pl.BlockSpec(memory_space=pltpu.SMEM), band, qspec] + kvspecs + kvspecs,
        out_specs=[qspec], out_shape=[jax.ShapeDtypeStruct((t, da), F32)], scratch=[],
        args=(sink, _band_bias(group * QBLK), qs, k2, k2, k2, k2, v2, v2, v2, v2), name=name, side=side)
    return ao, got


def _attn_bwd(qs, k2, v2, dao, sink, *, ctx_rows, name, side=None):
    t, da = qs.shape
    nblk = t // QBLK
    group = da // HEAD_DIM // N_KV
    nqb = group // 2
    kw = N_KV * HEAD_DIM
    nloc = 3 * QBLK
    kvspecs = _attn_specs(t, ctx_rows)

    def body(sink_ref, band_ref, q_ref, do_ref, kp, kc, kn, kx, vp, vc, vn, vx,
             dq_ref, dkl_ref, dvl_ref, dkc_ref, dvc_ref, ds_ref):
        b, i = pl.program_id(0), pl.program_id(1)

        @pl.when((b == 0) & (i == 0))
        def _():
            ds_ref[...] = jnp.zeros_like(ds_ref)

        @pl.when(i == 0)
        def _():
            dkc_ref[...] = jnp.zeros_like(dkc_ref)
            dvc_ref[...] = jnp.zeros_like(dvc_ref)

        valid = _local_bias(band_ref, i, t, ctx_rows)
        srow = lax.broadcasted_iota(I32, ds_ref.shape, 0)
        slane = lax.broadcasted_iota(I32, ds_ref.shape, 1)

        def both(x):
            return x + pltpu.roll(x, HEAD_DIM, 1)

        folded = []
        for half in range(2):
            lanes = slice(half * LANES, (half + 1) * LANES)
            head0 = (2 * b + half) * group
            qst = _mx(_stack_heads(q_ref, half * nqb * LANES, nqb))
            dost = _mx(_stack_heads(do_ref, half * nqb * LANES, nqb))
            k_loc = jnp.concatenate([kp[:, lanes], kc[:, lanes], kn[:, lanes]], axis=0)
            v_loc = jnp.concatenate([vp[:, lanes], vc[:, lanes], vn[:, lanes]], axis=0)
            k_ctx, v_ctx = kx[:, lanes], vx[:, lanes]
            p_loc, p_ctx, p_snk = _probs(qst, k_loc, k_ctx, valid, _sink_column(sink_ref, head0, group))
            dp_loc, dp_ctx = _dot_nt(dost, v_loc), _dot_nt(dost, v_ctx)
            dsum = jnp.sum(p_loc * dp_loc, axis=-1, keepdims=True) + jnp.sum(p_ctx * dp_ctx, axis=-1, keepdims=True)
            ds_loc, ds_ctx = _mx(p_loc * (dp_loc - dsum)), _mx(p_ctx * (dp_ctx - dsum))
            dq = (_dot(ds_loc, k_loc) + _dot(ds_ctx, k_ctx)) * (HEAD_DIM ** -0.5)
            for qb, blk in enumerate(_unstack_heads(dq, nqb)):
                c0 = (half * nqb + qb) * LANES
                dq_ref[:, c0:c0 + LANES] = blk
            p_loc, p_ctx = _mx(p_loc), _mx(p_ctx)
            folded.append((both(_dot_tn(ds_loc, qst)), both(_dot_tn(p_loc, dost)),
                           both(_dot_tn(ds_ctx, qst)), both(_dot_tn(p_ctx, dost))))
            dsnk = p_snk * dsum
            for g in range(group):
                part = -jnp.sum(dsnk[g * QBLK:(g + 1) * QBLK])
                ds_ref[...] += jnp.where((srow == 0) & (slane == head0 + g), part, 0.0)
        lane_l = lax.broadcasted_iota(I32, (nloc, LANES), 1)
        lane_c = lax.broadcasted_iota(I32, (ctx_rows, LANES), 1)
        dkl_ref[...] = jnp.where(lane_l < HEAD_DIM, folded[0][0], folded[1][0])
        dvl_ref[...] = jnp.where(lane_l < HEAD_DIM, folded[0][1], folded[1][1])
        dkc_ref[...] += jnp.where(lane_c < HEAD_DIM, folded[0][2], folded[1][2])
        dvc_ref[...] += jnp.where(lane_c < HEAD_DIM, folded[0][3], folded[1][3])

    qspec = pl.BlockSpec((QBLK, 2 * group * HEAD_DIM), lambda b, i: (i, b))
    loc = pl.BlockSpec((None, nloc, LANES), lambda b, i: (i, 0, b))
    cspec = pl.BlockSpec((ctx_rows, LANES), lambda b, i: (0, b))
    band = pl.BlockSpec((group * QBLK, nloc), lambda b, i: (0, 0))
    return _call(
        body, grid=(N_KV // 2, nblk),
        in_specs=[pl.BlockSpec(memory_space=pltpu.SMEM), band, qspec, qspec] + kvspecs + kvspecs,
        out_specs=[qspec, loc, loc, cspec, cspec, pl.BlockSpec((8, LANES), lambda b, i: (0, 0))],
        out_shape=[jax.ShapeDtypeStruct((t, da), F32), jax.ShapeDtypeStruct((nblk, nloc, kw), F32),
                   jax.ShapeDtypeStruct((nblk, nloc, kw), F32), jax.ShapeDtypeStruct((ctx_rows, kw), F32),
                   jax.ShapeDtypeStruct((ctx_rows, kw), F32), jax.ShapeDtypeStruct((8, LANES), F32)],
        scratch=[], args=(sink, _band_bias(group * QBLK), qs, dao, k2, k2, k2, k2, v2, v2, v2, v2), name=name, side=side)


def _kv_reduce(dkl, dvl, dkc, dvc, *, ctx_rows, name):
    nblk, _, kw = dkl.shape
    t = nblk * QBLK
    ncb = ctx_rows // QBLK

    def clip(i):
        return jnp.clip(i, 0, nblk - 1)

    def body(ka, kb, kc, kx, va, vb, vc, vx, dk_ref, dv_ref):
        m = pl.program_id(0)
        lat = m >= ncb
        wa = jnp.where(lat & (m + 1 <= nblk - 1), 1.0, 0.0)
        wc = jnp.where(lat & (m - 1 >= ncb), 1.0, 0.0)
        wl = jnp.where(lat, 1.0, 0.0)
        dk_ref[...] = wl * (kb[...] + wa * ka[...] + wc * kc[...]) + (1.0 - wl) * kx[...]
        dv_ref[...] = wl * (vb[...] + wa * va[...] + wc * vc[...]) + (1.0 - wl) * vx[...]

    slots = [pl.BlockSpec((None, QBLK, kw), lambda m: (clip(m + 1), 0, 0)),
             pl.BlockSpec((None, QBLK, kw), lambda m: (m, 1, 0)),
             pl.BlockSpec((None, QBLK, kw), lambda m: (clip(m - 1), 2, 0))]
    cspec = pl.BlockSpec((QBLK, kw), lambda m: (jnp.clip(m, 0, ncb - 1), 0))
    out = pl.BlockSpec((QBLK, kw), lambda m: (m, 0))
    return pl.pallas_call(
        body, grid=(nblk,), in_specs=slots + [cspec] + slots + [cspec], out_specs=[out, out],
        out_shape=[jax.ShapeDtypeStruct((t, kw), F32)] * 2, compiler_params=_params(), name=name)(
            dkl, dkl, dkl, dkc, dvl, dvl, dvl, dvc)


MERGE_ROWS = 128


def _halo_specs(t, c, col):
    hb = MERGE_ROWS // HALO
    return [pl.BlockSpec((HALO, c), lambda i: (jnp.maximum(i * hb - 1, 0), col)),
            pl.BlockSpec((MERGE_ROWS, c), lambda i: (i, col)),
            pl.BlockSpec((HALO, c), lambda i: (jnp.minimum((i + 1) * hb, t // HALO - 1), col))]


def _ext(refs):
    return jnp.concatenate([r[...] for r in refs], axis=0)


def _conv_ext(cg, hh, w_ref, b_ref, i, t, ctx_rows):
    n = cg.shape[0]
    u = cg * hh
    row = i * MERGE_ROWS - HALO + lax.broadcasted_iota(I32, u.shape, 0)
    first = (row == 0) | (row == ctx_rows)
    last = (row == ctx_rows - 1) | (row == t - 1)
    u_dn = jnp.where(first, 0.0, pltpu.roll(u, 1, 0))
    u_up = jnp.where(last, 0.0, pltpu.roll(u, n - 1, 0))
    cv = w_ref[0:1, :] * u_dn + w_ref[1:2, :] * u + w_ref[2:3, :] * u_up + b_ref[...]
    return u, u_dn, u_up, cv, first, last


def _merge_fwd(p, ao, conv_w, conv_b, g_oc, g_oa, *, ctx_rows, name):
    t = p.shape[0]
    c = ao.shape[1]
    main = slice(HALO, HALO + MERGE_ROWS)

    def body(bg_ref, cgp, cgm, cgn, hhp, hhm, hhn, ao_ref, w_ref, b_ref, gc_ref, ga_ref, o_ref):
        i = pl.program_id(0)
        _, _, _, cv, _, _ = _conv_ext(_ext((cgp, cgm, cgn)), _ext((hhp, hhm, hhn)), w_ref, b_ref, i, t, ctx_rows)
        co = bg_ref[...] * cv[main]
        rc = lax.rsqrt(jnp.mean(co * co, axis=-1, keepdims=True) + EPS)
        o_ref[:, :c] = (co * rc * gc_ref[...]).astype(o_ref.dtype)
        av = ao_ref[...]
        ra = lax.rsqrt(jnp.mean(av * av, axis=-1, keepdims=True) + EPS)
        o_ref[:, c:] = (av * ra * ga_ref[...]).astype(o_ref.dtype)

    vec = pl.BlockSpec((1, c), lambda i: (0, 0))
    return pl.pallas_call(
        body, grid=(t // MERGE_ROWS,),
        in_specs=[pl.BlockSpec((MERGE_ROWS, c), lambda i: (i, 0))] + _halo_specs(t, c, 1) + _halo_specs(t, c, 2)
        + [pl.BlockSpec((MERGE_ROWS, c), lambda i: (i, 0)), pl.BlockSpec((3, c), lambda i: (0, 0)), vec, vec, vec],
        out_specs=pl.BlockSpec((MERGE_ROWS, 2 * c), lambda i: (i, 0)),
        out_shape=jax.ShapeDtypeStruct((t, 2 * c), MXU_DTYPE), compiler_params=_params(), name=name)(
            p, p, p, p, p, p, p, ao, conv_w, conv_b, g_oc, g_oa)


R_DGOC, R_DGOA, R_DCB, R_DCW = 0, 1, 2, 3


def _merge_bwd(dmg, p, ao, conv_w, conv_b, g_oc, g_oa, *, width, ctx_rows, name):
    t = p.shape[0]
    c = ao.shape[1]
    main = slice(HALO, HALO + MERGE_ROWS)

    def body(dyp, dym, dyn, dya_ref, bgp, bgm, bgn, cgp, cgm, cgn, hhp, hhm, hhn, ao_ref, w_ref, b_ref, gc_ref, ga_ref,
             dp_ref, dao_ref, acc):
        i = pl.program_id(0)

        @pl.when(i == 0)
        def _():
            acc[...] = jnp.zeros_like(acc)

        bg, cg, hh = _ext((bgp, bgm, bgn)), _ext((cgp, cgm, cgn)), _ext((hhp, hhm, hhn))
        n = bg.shape[0]
        u, u_dn, u_up, cv, first, last = _conv_ext(cg, hh, w_ref, b_ref, i, t, ctx_rows)
        co = bg * cv
        rc = lax.rsqrt(jnp.mean(co * co, axis=-1, keepdims=True) + EPS)
        cohat = co * rc
        dyc = _ext((dyp, dym, dyn))
        t1 = dyc * gc_ref[...]
        dco = rc * (t1 - cohat * jnp.mean(t1 * cohat, axis=-1, keepdims=True))
        dcv = dco * bg
        dcv_next = jnp.where(last, 0.0, pltpu.roll(dcv, n - 1, 0))
        dcv_prev = jnp.where(first, 0.0, pltpu.roll(dcv, 1, 0))
        du = w_ref[1:2, :] * dcv + w_ref[0:1, :] * dcv_next + w_ref[2:3, :] * dcv_prev
        dp_ref[:, :c] = (dco * cv)[main].astype(dp_ref.dtype)
        dp_ref[:, c:2 * c] = (du * hh)[main].astype(dp_ref.dtype)
        dp_ref[:, 2 * c:] = (du * cg)[main].astype(dp_ref.dtype)
        dcv_m = dcv[main]
        acc[R_DGOC:R_DGOC + 1, :] += jnp.sum((dyc * cohat)[main], axis=0, keepdims=True)
        acc[R_DCB:R_DCB + 1, :] += jnp.sum(dcv_m, axis=0, keepdims=True)
        acc[R_DCW:R_DCW + 1, :] += jnp.sum(dcv_m * u_dn[main], axis=0, keepdims=True)
        acc[R_DCW + 1:R_DCW + 2, :] += jnp.sum(dcv_m * u[main], axis=0, keepdims=True)
        acc[R_DCW + 2:R_DCW + 3, :] += jnp.sum(dcv_m * u_up[main], axis=0, keepdims=True)
        av = ao_ref[...]
        ra = lax.rsqrt(jnp.mean(av * av, axis=-1, keepdims=True) + EPS)
        ahat = av * ra
        dya = dya_ref[...]
        t2 = dya * ga_ref[...]
        dao_ref[...] = ra * (t2 - ahat * jnp.mean(t2 * ahat, axis=-1, keepdims=True))
        acc[R_DGOA:R_DGOA + 1, :] += jnp.sum(dya * ahat, axis=0, keepdims=True)

    vec = pl.BlockSpec((1, c), lambda i: (0, 0))
    tile = pl.BlockSpec((MERGE_ROWS, c), lambda i: (i, 0))
    return pl.pallas_call(
        body, grid=(t // MERGE_ROWS,),
        in_specs=_halo_specs(t, c, 0) + [pl.BlockSpec((MERGE_ROWS, c), lambda i: (i, 1))]
        + _halo_specs(t, c, 0) + _halo_specs(t, c, 1) + _halo_specs(t, c, 2)
        + [tile, pl.BlockSpec((3, c), lambda i: (0, 0)), vec, vec, vec],
        out_specs=[pl.BlockSpec((MERGE_ROWS, 3 * c), lambda i: (i, 0)), tile, pl.BlockSpec((8, c), lambda i: (0, 0))],
        out_shape=[jax.ShapeDtypeStruct((t, width), MXU_DTYPE), jax.ShapeDtypeStruct((t, c), F32),
                   jax.ShapeDtypeStruct((8, c), F32)],
        compiler_params=_params(), name=name)(dmg, dmg, dmg, dmg, p, p, p, p, p, p, p, p, p, ao, conv_w, conv_b, g_oc, g_oa)


def _local_step(x0, tgt, mods, wts, small, *, ctx_rows, comm=None):
    t, d = x0.shape
    depth = mods.shape[0]
    c = d // 2
    da = d - c
    kw = N_KV * HEAD_DIM
    kv_start = 3 * c + da
    shards = N_CHIP
    tabs = _rope_tables(t - ctx_rows, ctx_rows)
    kwargs = dict(ctx_rows=ctx_rows)

    saved = []
    xs, z_prev = x0, None
    have = {}

    def riding(l, host):
        if l == 0:
            want = {"in": [(0, 1), (1, 0)], "attn": [(0, 2)], "out": [(1, 1)], "mlp1": [(0, 3), (1, 2)], "mlp2": [(1, 3)]}
        else:
            want = {"in": [(l + 1, 0)], "attn": [], "out": [(l + 1, 1)], "mlp1": [(l + 1, 2)], "mlp2": [(l + 1, 3)]}
        keys = [key for key in want[host] if key[0] < depth] if comm is not None else []
        return keys, (_gather_side([comm.halves[a][k] for a, k in keys]) if keys else None)

    def weight(l, k):
        return wts[l][k] if comm is None else comm.weight(k, have[(l, k)])

    if comm is not None:
        have[(0, 0)] = _run_side(_gather_side([comm.halves[0][0]]), "gather_weights_0")[0]
    for l in range(depth):
        if z_prev is None:
            x_in = xs
            h1 = _norm_fwd(xs, None, None, mods[l], small["g_norm1"][l], gate_idx=None, shift_idx=0, scale_idx=1,
                           name=f"norm1_fwd_{l}", **kwargs)
        else:
            x_in, h1 = _norm_fwd(xs, z_prev, mods[l - 1], mods[l], small["g_norm1"][l], gate_idx=5, shift_idx=0,
                                 scale_idx=1, name=f"norm1_fwd_{l}", **kwargs)
        keys, side = riding(l, "in")
        (p,), got = _mm_nn(h1, weight(l, 0), out_dtypes=(F32,), epilogue=lambda v: (v,), name=f"in_proj_{l}", side=side)
        have.update(zip(keys, got))
        qs, k2, v2 = _rope_fwd(p, tabs, da=da, kw=kw, kv_start=kv_start, name=f"rope_fwd_{l}")
        keys, side = riding(l, "attn")
        ao, got = _attn_fwd(qs, k2, v2, small["sink"][l], name=f"attn_fwd_{l}", side=side, **kwargs)
        have.update(zip(keys, got))
        mg = _merge_fwd(p, ao, small["conv_w"][l], small["conv_b"][l], small["g_out_conv"][l], small["g_out_attn"][l],
                        name=f"merge_fwd_{l}", **kwargs)
        keys, side = riding(l, "out")
        (z,), got = _mm_nn(mg, weight(l, 1), out_dtypes=(MXU_DTYPE,), epilogue=lambda v: (v,), name=f"out_proj_{l}", side=side)
        have.update(zip(keys, got))
        x_mid, h2 = _norm_fwd(x_in, z, mods[l], mods[l], small["g_norm2"][l], gate_idx=2, shift_idx=3, scale_idx=4,
                              name=f"norm2_fwd_{l}", **kwargs)
        keys, side = riding(l, "mlp1")
        (a_act, s_act), got = _mm_nn(h2, weight(l, 2), out_dtypes=(MXU_DTYPE, MXU_DTYPE),
                                     epilogue=lambda v: (v, jnp.square(jnp.maximum(v, 0.0))), name=f"mlp1_{l}", side=side)
        have.update(zip(keys, got))
        keys, side = riding(l, "mlp2")
        (o,), got = _mm_nn(s_act, weight(l, 3), out_dtypes=(MXU_DTYPE,), epilogue=lambda v: (v,), name=f"mlp2_{l}", side=side)
        have.update(zip(keys, got))
        saved.append(dict(x_in=x_in, h1=h1, p=p, qs=qs, k2=k2, v2=v2, ao=ao, mg=mg, z=z, x_mid=x_mid, h2=h2, a=a_act, s=s_act, o=o))
        xs, z_prev = x_mid, o

    dx, do, fin = _final(xs, z_prev, mods[depth - 1], small["g_final"], tgt, name="final", **kwargs)

    grads = [None] * depth
    dmods = [[None] * N_MOD for _ in range(depth)]
    sg = {k: [None] * depth for k in ("g_norm1", "g_norm2", "conv_w", "conv_b", "sink", "g_out_conv", "g_out_attn")}
    dmods[depth - 1][5] = fin[R_FGATE:R_FGATE + 2]
    sync = None
    for l in reversed(range(depth)):
        w_in, w_out, w1, w2 = (weight(l, k) for k in range(4))
        sv = saved[l]
        (da_act,), got = _mm_nt(do, w2, out_dtype=MXU_DTYPE, extra=sv["a"],
                                epilogue=lambda v, a: v * (2.0 * jnp.maximum(a.astype(F32), 0.0)), name=f"mlp2_dx_{l}",
                                side=sync and sync.pair_side())
        sync and sync.add(got)
        (g_w2,), got = _mm_tn(sv["s"], do, shards=1, name=f"mlp2_dw_{l}", side=sync and sync.chips_side((2,)))
        sync and sync.land((2,), got)
        (dh2,), got = _mm_nt(da_act, w1, out_dtype=MXU_DTYPE, name=f"mlp1_dx_{l}", side=sync and sync.chips_side((3,)))
        sync and sync.land((3,), got)
        (g_w1,), got = _mm_tn(sv["h2"], da_act, shards=shards, name=f"mlp1_dw_{l}", side=sync and sync.chips_side((0, 1)))
        sync and sync.land((0, 1), got)
        sync and sync.sum()
        own = _GradSync(comm, l, {2: g_w1, 3: g_w2}) if comm is not None and l == 0 else None
        dx, dz, sums2 = _norm_bwd(dx, dh2, sv["x_mid"], mods[l], small["g_norm2"][l], scale_idx=4,
                                  prev=(sv["z"], mods[l], 2), name=f"norm2_bwd_{l}", **kwargs)
        (dmg,), got = _mm_nt(dz, w_out, out_dtype=F32, name=f"out_proj_dx_{l}", side=own and own.pair_side())
        own and own.add(got)
        (g_wo,), _ = _mm_tn(sv["mg"], dz, shards=1, name=f"out_proj_dw_{l}")
        qkv_w = da + 2 * kw
        in_place = (3 * c) % qkv_w == 0
        dpc, dao, msum = _merge_bwd(dmg, sv["p"], sv["ao"], small["conv_w"][l], small["conv_b"][l],
                                    small["g_out_conv"][l], small["g_out_attn"][l],
                                    width=3 * c + qkv_w if in_place else 3 * c, name=f"merge_bwd_{l}", **kwargs)
        (dq, dkl, dvl, dkc, dvc, dsink), got = _attn_bwd(sv["qs"], sv["k2"], sv["v2"], dao, small["sink"][l],
                                                         name=f"attn_bwd_{l}", side=own and own.chips_side(own.keys), **kwargs)
        own and own.land(own.keys, got)
        own and own.sum()
        dk, dv = _kv_reduce(dkl, dvl, dkc, dvc, ctx_rows=ctx_rows, name=f"kv_reduce_{l}")
        if in_place:
            dp = _rope_bwd(dq, dk, dv, tabs, dpc, name=f"rope_bwd_{l}")
        else:
            dqkv = _rope_bwd(dq, dk, dv, tabs, lax.empty((t, qkv_w), MXU_DTYPE), name=f"rope_bwd_{l}")
            dp = jnp.concatenate([dpc, dqkv], axis=1)
        (dh1,), got = _mm_nt(dp, w_in, out_dtype=MXU_DTYPE, name=f"in_proj_dx_{l}", side=sync and sync.share_side())
        sync and sync.adam(got)
        (g_wi,), got = _mm_tn(sv["h1"], dp, shards=shards, name=f"in_proj_dw_{l}", side=own and own.share_side())
        own and own.adam(got)
        if comm is not None:
            sync = _GradSync(comm, l, {0: g_wi, 1: g_wo} if own else {0: g_wi, 1: g_wo, 2: g_w1, 3: g_w2})
        if l > 0:
            dx, do, sums1 = _norm_bwd(dx, dh1, sv["x_in"], mods[l], small["g_norm1"][l], scale_idx=1,
                                      prev=(saved[l - 1]["o"], mods[l - 1], 5), name=f"norm1_bwd_{l}", **kwargs)
            dmods[l - 1][5] = sums1[R_DGATE:R_DGATE + 2]
        else:
            dx, sums1 = _norm_bwd(dx, dh1, sv["x_in"], mods[l], small["g_norm1"][l], scale_idx=1,
                                  name=f"norm1_bwd_{l}", **kwargs)
        grads[l] = (g_wi, g_wo, g_w1, g_w2)
        dmods[l][0] = sums1[R_DSHIFT:R_DSHIFT + 2]
        dmods[l][1] = sums1[R_DSCALE:R_DSCALE + 2]
        dmods[l][2] = sums2[R_DGATE:R_DGATE + 2]
        dmods[l][3] = sums2[R_DSHIFT:R_DSHIFT + 2]
        dmods[l][4] = sums2[R_DSCALE:R_DSCALE + 2]
        sg["g_norm1"][l] = sums1[R_DG]
        sg["g_norm2"][l] = sums2[R_DG]
        sg["g_out_conv"][l] = msum[R_DGOC]
        sg["g_out_attn"][l] = msum[R_DGOA]
        sg["conv_b"][l] = msum[R_DCB]
        sg["conv_w"][l] = msum[R_DCW:R_DCW + 3]
        sg["sink"][l] = dsink[0, :da // HEAD_DIM]
    if sync is not None:
        sync.run_alone()
    dmods = jnp.stack([jnp.stack(row, axis=1) for row in dmods])
    sg = {k: jnp.stack(v) for k, v in sg.items()}
    sg["g_final"] = fin[R_FG]
    return fin[R_FLOSS, 0], dx, grads, dmods, sg


N_DEV = 8
N_CHIP = 4


def _place():
    mx, my, mc = lax.axis_index("x"), lax.axis_index("y"), lax.axis_index("c")
    others = [(1 - mx, my), (mx, 1 - my), (1 - mx, 1 - my)]
    return mx, my, mc, others


def _remote(src, dst, send_sems, recv_sems, k, dev):
    return pltpu.make_async_remote_copy(src_ref=src, dst_ref=dst, send_sem=send_sems.at[k], recv_sem=recv_sems.at[k],
                                        device_id=dev, device_id_type=MESH)


def _allgather8(x, name):
    r, ccols = x.shape

    def body(x_ref, out_ref, send_sems, recv_sems, local_sem):
        mx, my, mc, _ = _place()
        me = 4 * mx + 2 * my + mc
        mine = pltpu.make_async_copy(x_ref, out_ref.at[me], local_sem)
        mine.start()
        sent = []
        for k in range(1, N_DEV):
            fx, fy, fc = (k >> 2) & 1, (k >> 1) & 1, k & 1
            px, py, pc = (1 - mx if fx else mx), (1 - my if fy else my), (1 - mc if fc else mc)
            cp = _remote(x_ref, out_ref.at[me], send_sems, recv_sems, k - 1, (px, py, pc))
            cp.start()
            sent.append((cp, 4 * px + 2 * py + pc, (px, py, pc)))
        for k, (cp, peer, dev) in enumerate(sent):
            _remote(x_ref, out_ref.at[peer], send_sems, recv_sems, k, dev).wait_recv()
        for cp, _, _ in sent:
            cp.wait_send()
        mine.wait()

    vm = pl.BlockSpec(memory_space=pltpu.VMEM)
    return pl.pallas_call(
        body, in_specs=[vm], out_specs=vm, out_shape=jax.ShapeDtypeStruct((N_DEV, r, ccols), x.dtype),
        scratch_shapes=[pltpu.SemaphoreType.DMA((N_DEV - 1,)), pltpu.SemaphoreType.DMA((N_DEV - 1,)),
                        pltpu.SemaphoreType.DMA], name=name)(x)


def _gather_side(halves):
    n = len(halves)

    def copies(ins, outs, sems):
        send_sems, recv_sems, local_sems = sems
        mx, my, mc, others = _place()
        chip = 2 * mx + my
        sib = (mx, my, 1 - mc)

        def src(w):
            return ins[w].at[pl.ds(mc, 1)]

        def slot(w, ch, core):
            return outs[w].at[ch, pl.ds(core, 1)]

        locs = [pltpu.make_async_copy(src(w), slot(w, chip, mc), local_sems.at[w]) for w in range(n)]
        first, landed, passed, from_sib = [], [], [], []
        for w in range(n):
            first.append(_remote(src(w), slot(w, chip, mc), send_sems, recv_sems, 7 * w, sib))
            from_sib.append(_remote(src(w), slot(w, chip, 1 - mc), send_sems, recv_sems, 7 * w, sib))
            for j, (ox, oy) in enumerate(others):
                och = 2 * ox + oy
                first.append(_remote(src(w), slot(w, chip, mc), send_sems, recv_sems, 7 * w + 1 + j, (ox, oy, mc)))
                landed.append(_remote(src(w), slot(w, och, mc), send_sems, recv_sems, 7 * w + 1 + j, (ox, oy, mc)))
                passed.append(_remote(slot(w, och, mc), slot(w, och, mc), send_sems, recv_sems, 7 * w + 4 + j, sib))
                from_sib.append(_remote(src(w), slot(w, och, 1 - mc), send_sems, recv_sems, 7 * w + 4 + j, sib))
        return locs, first, landed, passed, from_sib

    def start(ins, outs, sems):
        locs, first, _, _, _ = copies(ins, outs, sems)
        for cp in locs + first:
            cp.start()

    def middle(ins, outs, sems):
        _, _, landed, passed, _ = copies(ins, outs, sems)
        for cp, fw in zip(landed, passed):
            cp.wait_recv()
            fw.start()

    def finish(ins, outs, sems):
        locs, first, _, passed, from_sib = copies(ins, outs, sems)
        for cp in from_sib:
            cp.wait_recv()
        for cp in first + passed:
            cp.wait_send()
        for cp in locs:
            cp.wait()

    return _Side(halves, [jax.ShapeDtypeStruct((N_CHIP,) + h.shape, h.dtype) for h in halves],
                 [pltpu.SemaphoreType.DMA((7 * n,)), pltpu.SemaphoreType.DMA((7 * n,)), pltpu.SemaphoreType.DMA((n,))],
                 start, finish, middle=middle)


def _pair_side(gs):
    n = len(gs)

    def copies(ins, outs, sems):
        mx, my, mc, _ = _place()
        return [_remote(ins[w].at[:, pl.ds(1 - mc, 1)], outs[w], sems[0], sems[1], w, (mx, my, 1 - mc)) for w in range(n)]

    def start(ins, outs, sems):
        for cp in copies(ins, outs, sems):
            cp.start()

    def finish(ins, outs, sems):
        for cp in copies(ins, outs, sems):
            cp.wait()

    return _Side(gs, [jax.ShapeDtypeStruct((g.shape[0], 1) + g.shape[2:], g.dtype) for g in gs],
                 [pltpu.SemaphoreType.DMA((n,)), pltpu.SemaphoreType.DMA((n,))], start, finish)


def _pair_add(g, got, core, name):
    s, _, rh, ccols = g.shape
    tr = _pick(rh, (256, 128))

    def body(core_ref, g_ref, r_ref, o_ref):
        o_ref[...] = (g_ref[...] + r_ref[...]).astype(o_ref.dtype)

    spec = pltpu.PrefetchScalarGridSpec(
        num_scalar_prefetch=1, grid=(s, rh // tr),
        in_specs=[pl.BlockSpec((None, None, tr, ccols), lambda a, i, cr: (a, cr[0], i, 0)),
                  pl.BlockSpec((None, None, tr, ccols), lambda a, i, cr: (a, 0, i, 0))],
        out_specs=pl.BlockSpec((None, tr, ccols), lambda a, i, cr: (a, i, 0)))
    return pl.pallas_call(body, grid_spec=spec, out_shape=jax.ShapeDtypeStruct((s, rh, ccols), MXU_DTYPE),
                          compiler_params=_params(), name=name)(core, g, got)


def _chips_side(ps):
    n = len(ps)

    def copies(ins, outs, sems):
        send_sems, recv_sems, local_sems = sems
        mx, my, mc, others = _place()
        chip = 2 * mx + my
        locs = [pltpu.make_async_copy(ins[w].at[chip], outs[w].at[chip], local_sems.at[w]) for w in range(n)]
        sends, lands = [], []
        for w in range(n):
            for j, (ox, oy) in enumerate(others):
                och = 2 * ox + oy
                sends.append(_remote(ins[w].at[och], outs[w].at[chip], send_sems, recv_sems, 3 * w + j, (ox, oy, mc)))
                lands.append(_remote(ins[w].at[och], outs[w].at[och], send_sems, recv_sems, 3 * w + j, (ox, oy, mc)))
        return locs, sends, lands

    def start(ins, outs, sems):
        locs, sends, _ = copies(ins, outs, sems)
        for cp in locs + sends:
            cp.start()

    def finish(ins, outs, sems):
        locs, sends, lands = copies(ins, outs, sems)
        for cp in lands:
            cp.wait_recv()
        for cp in sends:
            cp.wait_send()
        for cp in locs:
            cp.wait()

    return _Side(ps, [jax.ShapeDtypeStruct(p.shape, p.dtype) for p in ps],
                 [pltpu.SemaphoreType.DMA((3 * n,)), pltpu.SemaphoreType.DMA((3 * n,)), pltpu.SemaphoreType.DMA((n,))],
                 start, finish)


def _chip_sum(rb, core, name):
    s, rh, ccols = rb.shape
    tr = _pick(rh, (256, 128))

    def body(core_ref, r_ref, o_ref):
        tot = r_ref[0].astype(F32)
        for k in range(1, s):
            tot = tot + r_ref[k].astype(F32)
        o_ref[...] = tot

    spec = pltpu.PrefetchScalarGridSpec(
        num_scalar_prefetch=1, grid=(rh // tr,),
        in_specs=[pl.BlockSpec((s, tr, ccols), lambda i, cr: (0, i, 0))],
        out_specs=pl.BlockSpec((None, tr, ccols), lambda i, cr: (cr[0], i, 0)))
    return pl.pallas_call(body, grid_spec=spec, out_shape=jax.ShapeDtypeStruct((2, rh, ccols), F32),
                          compiler_params=_params(), name=name)(core, rb)


def _share_side(fulls):
    n = len(fulls)

    def copies(ins, outs, sems):
        mx, my, mc, _ = _place()
        sib = (mx, my, 1 - mc)
        sends = [_remote(ins[w].at[mc], outs[w].at[mc], sems[0], sems[1], w, sib) for w in range(n)]
        lands = [_remote(ins[w].at[mc], outs[w].at[1 - mc], sems[0], sems[1], w, sib) for w in range(n)]
        return sends, lands

    def start(ins, outs, sems):
        for cp in copies(ins, outs, sems)[0]:
            cp.start()

    def finish(ins, outs, sems):
        sends, lands = copies(ins, outs, sems)
        for cp in lands:
            cp.wait_recv()
        for cp in sends:
            cp.wait_send()

    return _Side(fulls, [jax.ShapeDtypeStruct(f.shape, f.dtype) for f in fulls],
                 [pltpu.SemaphoreType.DMA((n,)), pltpu.SemaphoreType.DMA((n,))], start, finish,
                 aliases=[(w, w) for w in range(n)])


def _cast(w, name):
    r, ccols = w.shape
    tr = _pick(r, (256, 128))

    def body(w_ref, o_ref):
        o_ref[...] = w_ref[...].astype(o_ref.dtype)

    spec = pl.BlockSpec((tr, ccols), lambda i: (i, 0))
    return pl.pallas_call(body, grid=(r // tr,), in_specs=[spec], out_specs=spec,
                          out_shape=jax.ShapeDtypeStruct((r, ccols), MXU_DTYPE), compiler_params=_params(), name=name)(w)


def _adam_math(g, w, m, v):
    m = ADAM_B1 * m + (1.0 - ADAM_B1) * g
    v = ADAM_B2 * v + (1.0 - ADAM_B2) * jnp.square(g)
    m_hat = m / (1.0 - ADAM_B1 ** ADAM_STEP)
    v_hat = v / (1.0 - ADAM_B2 ** ADAM_STEP)
    return -ADAM_LR * (m_hat / (jnp.sqrt(v_hat) + ADAM_EPS) + ADAM_WD * w), m, v


def _adamw_layer(l, g, w, m, v, bufs, name):
    depth, r, ccols = w.shape
    tr = _pick(r, (128,))

    def body(g_ref, w_ref, m_ref, v_ref, b0, b1, b2, b3, go_ref, d_ref, mo_ref, vo_ref):
        gv = g_ref[...]
        d, m2, v2 = _adam_math(gv, w_ref[...], m_ref[...], v_ref[...])
        go_ref[...] = gv
        d_ref[...] = d
        mo_ref[...] = m2
        vo_ref[...] = v2

    lay = pl.BlockSpec((None, tr, ccols), lambda i: (l, i, 0))
    return pl.pallas_call(
        body, grid=(r // tr,), in_specs=[pl.BlockSpec((tr, ccols), lambda i: (i, 0)), lay, lay, lay] + [ANY] * 4,
        out_specs=[lay] * 4, out_shape=[jax.ShapeDtypeStruct((depth, r, ccols), F32)] * 4,
        input_output_aliases={4: 0, 5: 1, 6: 2, 7: 3}, compiler_params=_params(), name=name)(g, w, m, v, *bufs)


def _adamw_small(g, g2, w, m, v, name):
    two = g2 is not None

    def body(*refs):
        if two:
            g_ref, g2_ref, w_ref, m_ref, v_ref, go_ref, d_ref, mo_ref, vo_ref = refs
            gv = g_ref[...] + g2_ref[...]
        else:
            g_ref, w_ref, m_ref, v_ref, go_ref, d_ref, mo_ref, vo_ref = refs
            gv = g_ref[...]
        d, m2, v2 = _adam_math(gv, w_ref[...], m_ref[...], v_ref[...])
        go_ref[...] = gv
        d_ref[...] = d
        mo_ref[...] = m2
        vo_ref[...] = v2

    args = [g] + ([g2] if two else []) + [w, m, v]
    vm = pl.BlockSpec(memory_space=pltpu.VMEM)
    return pl.pallas_call(body, in_specs=[vm] * len(args), out_specs=[vm] * 4,
                          out_shape=[jax.ShapeDtypeStruct(w.shape, F32)] * 4, name=name)(*args)


def _sum8(g, name):
    def body(g_ref, o_ref):
        tot = g_ref[0]
        for k in range(1, N_DEV):
            tot = tot + g_ref[k]
        o_ref[...] = tot

    vm = pl.BlockSpec(memory_space=pltpu.VMEM)
    return pl.pallas_call(body, in_specs=[vm], out_specs=vm, out_shape=jax.ShapeDtypeStruct(g.shape[1:], F32),
                          compiler_params=_params(), name=name)(g)


def _pack(arrs, width):
    flat = jnp.concatenate([a.reshape(-1).astype(F32) for a in arrs])
    rows = -(-flat.size // (8 * width)) * 8
    return jnp.pad(flat, (0, rows * width - flat.size)).reshape(rows, width)


def _unpack(flat, shapes):
    out, off = [], 0
    for shp in shapes:
        size = 1
        for v in shp:
            size *= v
        out.append(flat[..., off:off + size].reshape(flat.shape[:-1] + tuple(shp)))
        off += size
    return out


class _Comm:
    def __init__(self, core, params):
        self.core, self.params = core, params
        depth = params[0][1].shape[0]
        self.halves = [[_cast(w[l], f"cast_{nm}_{l}").reshape(2, w.shape[1] // 2, w.shape[2]) for nm, w, _, _ in params]
                       for l in range(depth)]
        self.stacked = [[lax.empty(w.shape, F32) for _ in range(4)] for _, w, _, _ in params]

    def weight(self, k, gathered):
        rows, cols = 2 * gathered.shape[2], gathered.shape[3]
        return gathered.reshape(N_CHIP, rows, cols) if k % 2 == 0 else gathered.reshape(1, N_CHIP * rows, cols)


class _GradSync:
    def __init__(self, comm, l, grads):
        self.comm, self.l, self.keys = comm, l, tuple(sorted(grads))
        self.gs = {k: g.reshape(N_CHIP, 2, g.shape[0] * g.shape[1] // (2 * N_CHIP), g.shape[2]) for k, g in grads.items()}
        self.ps, self.rb, self.full = {}, {}, {}

    def pair_side(self):
        return _pair_side([self.gs[k] for k in self.keys])

    def add(self, got):
        for k, r in zip(self.keys, got):
            self.ps[k] = _pair_add(self.gs[k], r, self.comm.core, f"rs_add_{self.l}_{k}")

    def chips_side(self, which):
        return _chips_side([self.ps[k] for k in which])

    def land(self, which, got):
        self.rb.update(zip(which, got))

    def sum(self):
        for k in self.keys:
            self.full[k] = _chip_sum(self.rb[k], self.comm.core, f"rs_sum_{self.l}_{k}")

    def share_side(self):
        return _share_side([self.full[k] for k in self.keys])

    def adam(self, got):
        for k, full in zip(self.keys, got):
            nm, w, m, v = self.comm.params[k]
            gsum = full.reshape(2 * full.shape[1], full.shape[2])
            self.comm.stacked[k] = _adamw_layer(self.l, gsum, w, m, v, self.comm.stacked[k], f"adamw_{nm}_{self.l}")

    def run_alone(self):
        self.add(_run_side(self.pair_side(), f"rs_pair_{self.l}"))
        self.land(self.keys, _run_side(self.chips_side(self.keys), f"rs_chips_{self.l}"))
        self.sum()
        self.adam(_run_side(self.share_side(), f"rs_share_{self.l}"))


COND_ROWS = 16


def _ada_fwd(cond, w_ada, b_cols, name):
    depth, d, ns = w_ada.shape
    tn = _pick(ns, (512, 384, 256, 128))

    def body(c_ref, w_ref, b_ref, o_ref):
        o_ref[...] = _dot(_silu(c_ref[...]), w_ref[...]) + b_ref[...]

    return pl.pallas_call(
        body, grid=(depth, ns // tn),
        in_specs=[pl.BlockSpec((COND_ROWS, d), lambda l, j: (0, 0)), pl.BlockSpec((None, d, tn), lambda l, j: (l, 0, j)),
                  pl.BlockSpec((None, 1, tn), lambda l, j: (l, 0, j))],
        out_specs=pl.BlockSpec((None, COND_ROWS, tn), lambda l, j: (l, 0, j)),
        out_shape=jax.ShapeDtypeStruct((depth, COND_ROWS, ns), F32), compiler_params=_params(), name=name)(cond, w_ada, b_cols)


def _ada_bwd(cond, dmod, w_ada, name):
    depth, d, ns = w_ada.shape
    tn = _pick(ns, (512, 384, 256, 128))

    def body(c_ref, dm_ref, w_ref, gw_ref, dc_ref):
        @pl.when((pl.program_id(0) == 0) & (pl.program_id(1) == 0))
        def _():
            dc_ref[...] = jnp.zeros_like(dc_ref)

        dm = dm_ref[...]
        gw_ref[...] = _dot_tn(_silu(c_ref[...]), dm)
        dc_ref[...] += _dot_nt(dm, w_ref[...])

    return pl.pallas_call(
        body, grid=(depth, ns // tn),
        in_specs=[pl.BlockSpec((COND_ROWS, d), lambda l, j: (0, 0)),
                  pl.BlockSpec((None, COND_ROWS, tn), lambda l, j: (l, 0, j)),
                  pl.BlockSpec((None, d, tn), lambda l, j: (l, 0, j))],
        out_specs=[pl.BlockSpec((None, d, tn), lambda l, j: (l, 0, j)), pl.BlockSpec((COND_ROWS, d), lambda l, j: (0, 0))],
        out_shape=[jax.ShapeDtypeStruct((depth, d, ns), F32), jax.ShapeDtypeStruct((COND_ROWS, d), F32)],
        compiler_params=_params(), name=name)(cond, dmod, w_ada)


def _cctx_grad(parts, c_ctx, name):
    def body(p_ref, c_ref, o_ref):
        tot = p_ref[0, 0:1, :]
        for k in range(1, N_CHIP):
            tot = tot + p_ref[2 * k, 0:1, :]
        z = c_ref[...]
        sg = 1.0 / (1.0 + jnp.exp(-z))
        o_ref[...] = tot * (sg + z * sg * (1.0 - sg))

    vm = pl.BlockSpec(memory_space=pltpu.VMEM)
    return pl.pallas_call(body, in_specs=[vm, vm], out_specs=vm, out_shape=jax.ShapeDtypeStruct(c_ctx.shape, F32),
                          name=name)(parts, c_ctx)


def kernel(x, c, ctx, c_ctx, w_ada, b_ada, g_norm1, g_norm2, w_in, conv_w, conv_b, sink, g_out_conv, g_out_attn, w_out, w_mlp1, w_mlp2, g_final, loss_target, m_c_ctx, m_w_ada, m_b_ada, m_g_norm1, m_g_norm2, m_w_in, m_conv_w, m_conv_b, m_sink, m_g_out_conv, m_g_out_attn, m_w_out, m_w_mlp1, m_w_mlp2, m_g_final, v_c_ctx, v_w_ada, v_b_ada, v_g_norm1, v_g_norm2, v_w_in, v_conv_w, v_conv_b, v_sink, v_g_out_conv, v_g_out_attn, v_w_out, v_w_mlp1, v_w_mlp2, v_g_final):
    mx, my, mc = lax.axis_index("x"), lax.axis_index("y"), lax.axis_index("c")
    chip, rank = 2 * mx + my, 4 * mx + 2 * my + mc
    core = jnp.reshape(mc, (1,)).astype(I32)
    depth, d = g_norm1.shape
    s_len, ctx_rows = x.shape[1], ctx.shape[1]
    cw_cols = conv_w.shape[2]
    c_conv = cw_cols * N_CHIP
    n_heads = sink.shape[1]
    ns_ada = w_ada.shape[2]

    got = _allgather8(_pack([c, conv_w], d), "gather_cond")
    flat = got.reshape(N_DEV, -1)
    conv_w_full = jnp.transpose(flat[::2, d:d + conv_w.size].reshape(N_CHIP, depth, 3, cw_cols), (1, 2, 0, 3))
    conv_w_full = conv_w_full.reshape(depth, 3, c_conv)
    cond = jnp.zeros((COND_ROWS, d), F32).at[:N_DEV].set(flat[:, :d]).at[N_DEV].set(c_ctx)

    b_cols = lax.dynamic_slice_in_dim(b_ada, chip * ns_ada, ns_ada, axis=1)[:, None, :]
    mod_cols = _ada_fwd(cond, w_ada, b_cols, "ada_fwd")
    got = _allgather8(mod_cols.reshape(depth * COND_ROWS, ns_ada), "gather_mod")
    mod_all = jnp.transpose(got[::2].reshape(N_CHIP, depth, COND_ROWS, ns_ada), (1, 2, 0, 3))
    mod_all = mod_all.reshape(depth, COND_ROWS, N_CHIP * ns_ada)
    mod_me = lax.dynamic_index_in_dim(mod_all, rank, axis=1, keepdims=False)
    mods = jnp.stack([mod_all[:, N_DEV], mod_me], axis=1).reshape(depth, 2, N_MOD, 1, d)

    comm = _Comm(core, (("w_in", w_in, m_w_in, v_w_in), ("w_out", w_out, m_w_out, v_w_out),
                        ("w_mlp1", w_mlp1, m_w_mlp1, v_w_mlp1), ("w_mlp2", w_mlp2, m_w_mlp2, v_w_mlp2)))
    small = dict(g_norm1=g_norm1[:, None], g_norm2=g_norm2[:, None], conv_w=conv_w_full, conv_b=conv_b[:, None], sink=sink,
                 g_out_conv=g_out_conv[:, None], g_out_attn=g_out_attn[:, None], g_final=g_final[None])
    x0 = jnp.concatenate([ctx[0], x[0]], axis=0)
    loss_part, dx0, _, dmods, sg = _local_step(x0, loss_target[0], mods, None, small, ctx_rows=ctx_rows, comm=comm)
    loss = lax.psum(loss_part, ("x", "y", "c"))
    grad_x = dx0[ctx_rows:][None]
    stacked = dict(zip(("w_in", "w_out", "w_mlp1", "w_mlp2"), comm.stacked))

    names = ("g_norm1", "g_norm2", "conv_w", "conv_b", "sink", "g_out_conv", "g_out_attn", "g_final")
    shapes = [(depth, 2, N_MOD * d)] + [sg[k].shape for k in names]
    got = _allgather8(_pack([dmods] + [sg[k] for k in names], d), "gather_small")
    tot = _unpack(_sum8(got, "sum_small").reshape(-1), shapes)
    dmod_tot, small_tot = tot[0], dict(zip(names, tot[1:]))
    dmod_lat = _unpack(got.reshape(N_DEV, -1), shapes[:1])[0][:, :, 1]
    dm_rows = jnp.zeros((depth, COND_ROWS, N_MOD * d), F32)
    dm_rows = dm_rows.at[:, :N_DEV].set(jnp.transpose(dmod_lat, (1, 0, 2))).at[:, N_DEV].set(dmod_tot[:, 0])
    dm_cols = lax.dynamic_slice_in_dim(dm_rows, chip * ns_ada, ns_ada, axis=2)
    g_w_ada, dcond = _ada_bwd(cond, dm_cols, w_ada, "ada_bwd")
    got = _allgather8(dcond[N_DEV:N_DEV + 8], "gather_dcond")
    g_c_ctx = _cctx_grad(got, c_ctx[None], "c_ctx_grad")

    res = {}
    ada_bufs = [lax.empty((1,) + (depth * d, ns_ada), F32) for _ in range(4)]
    res["w_ada"] = [r.reshape(w_ada.shape) for r in _adamw_layer(
        0, g_w_ada.reshape(depth * d, ns_ada), w_ada.reshape(1, depth * d, ns_ada),
        m_w_ada.reshape(1, depth * d, ns_ada), v_w_ada.reshape(1, depth * d, ns_ada), ada_bufs, "adamw_w_ada")]
    for nm in ("w_in", "w_out", "w_mlp1", "w_mlp2"):
        res[nm] = stacked[nm]
    res["c_ctx"] = [r[0] for r in _adamw_small(g_c_ctx, None, c_ctx[None], m_c_ctx[None], v_c_ctx[None], "adamw_c_ctx")]
    res["b_ada"] = _adamw_small(dmod_tot[:, 0], dmod_tot[:, 1], b_ada, m_b_ada, v_b_ada, "adamw_b_ada")
    cw_grad = lax.dynamic_slice_in_dim(small_tot["conv_w"], chip * cw_cols, cw_cols, axis=2)
    res["conv_w"] = [r.reshape(conv_w.shape) for r in _adamw_small(
        cw_grad.reshape(depth * 3, cw_cols), None, conv_w.reshape(depth * 3, cw_cols),
        m_conv_w.reshape(depth * 3, cw_cols), v_conv_w.reshape(depth * 3, cw_cols), "adamw_conv_w")]
    for nm, w, m, v in (("g_norm1", g_norm1, m_g_norm1, v_g_norm1), ("g_norm2", g_norm2, m_g_norm2, v_g_norm2),
                        ("conv_b", conv_b, m_conv_b, v_conv_b), ("sink", sink, m_sink, v_sink),
                        ("g_out_conv", g_out_conv, m_g_out_conv, v_g_out_conv),
                        ("g_out_attn", g_out_attn, m_g_out_attn, v_g_out_attn)):
        res[nm] = _adamw_small(small_tot[nm], None, w, m, v, f"adamw_{nm}")
    res["g_final"] = [r[0] for r in _adamw_small(small_tot["g_final"][None], None, g_final[None], m_g_final[None],
                                                 v_g_final[None], "adamw_g_final")]
    order = ("c_ctx", "w_ada", "b_ada", "g_norm1", "g_norm2", "w_in", "conv_w", "conv_b", "sink", "g_out_conv",
             "g_out_attn", "w_out", "w_mlp1", "w_mlp2", "g_final")
    return (loss, grad_x, *[res[n][0] for n in order], *[res[n][1] for n in order], *[res[n][2] for n in order],
            *[res[n][3] for n in order])
```

```python
import functools

import jax
import jax.numpy as jnp
from jax import lax
from jax.experimental import pallas as pl
from jax.experimental.pallas import tpu as pltpu

F32 = jnp.float32
I32 = jnp.int32
MXU_DTYPE = jnp.bfloat16
EPS = 1e-6
HEAD_DIM = 64
N_KV = 4
WINDOW = 128
QBLK = 128
LANES = 128
GRID_W = 64
ROPE_THETA = 10000.0
NEG_INF = -1e30
N_MOD = 6
HALO = 8
ADAM_LR, ADAM_B1, ADAM_B2, ADAM_EPS, ADAM_WD, ADAM_STEP = 0.001, 0.9, 0.999, 1e-08, 0.01, 10
V7X_VMEM_BYTES = 64 * 1024 * 1024
VMEM_LIMIT = V7X_VMEM_BYTES * 3 // 4
ROW_TILES = (1408, 768, 512, 640, 256, 128)
MESH = pl.DeviceIdType.MESH
ANY = pl.BlockSpec(memory_space=pl.ANY)


def _params():
    return pltpu.CompilerParams(vmem_limit_bytes=VMEM_LIMIT)


def _pick(n, cands):
    for c in cands:
        if n % c == 0:
            return c
    raise ValueError(f"no tile of {cands} divides {n}")


def _mx(v):
    return v.astype(MXU_DTYPE)


def _dot(a, b):
    return jnp.dot(_mx(a), _mx(b), preferred_element_type=F32)


def _dot_nt(a, b):
    return lax.dot_general(_mx(a), _mx(b), (((1,), (1,)), ((), ())), preferred_element_type=F32)


def _dot_tn(a, b):
    return lax.dot_general(_mx(a), _mx(b), (((0,), (0,)), ((), ())), preferred_element_type=F32)


def _silu(v):
    return v / (1.0 + jnp.exp(-v))


class _Side:
    def __init__(self, inputs, out_shapes, scratch, start, finish, aliases=(), middle=None):
        self.inputs, self.out_shapes, self.scratch = list(inputs), list(out_shapes), list(scratch)
        self.start, self.finish, self.aliases = start, finish, tuple(aliases)
        self.middle = middle


def _call(body, *, grid, in_specs, out_specs, out_shape, scratch, args, name, side=None, aliases=None):
    aliases = dict(aliases or {})
    if side is None:
        res = pl.pallas_call(body, grid=grid, in_specs=in_specs, out_specs=out_specs, out_shape=out_shape,
                             scratch_shapes=scratch, input_output_aliases=aliases, compiler_params=_params(),
                             name=name)(*args)
        return list(res), []
    ni, no, ns = len(in_specs), len(out_specs), len(scratch)
    si, so = len(side.inputs), len(side.out_shapes)

    def full(*refs):
        ins, sins = refs[:ni], refs[ni:ni + si]
        outs, souts = refs[ni + si:ni + si + no], refs[ni + si + no:ni + si + no + so]
        scr, sems = refs[ni + si + no + so:ni + si + no + so + ns], refs[ni + si + no + so + ns:]
        ids = [pl.program_id(k) for k in range(len(grid))]
        first, last = ids[0] == 0, ids[0] == grid[0] - 1
        for k in range(1, len(grid)):
            first, last = first & (ids[k] == 0), last & (ids[k] == grid[k] - 1)

        @pl.when(first)
        def _():
            side.start(sins, souts, sems)

        body(*ins, *outs, *scr)

        if side.middle is not None:
            lin, total = ids[0], grid[0]
            for k in range(1, len(grid)):
                lin, total = lin * grid[k] + ids[k], total * grid[k]

            @pl.when(lin == (3 * total) // 4)
            def _():
                side.middle(sins, souts, sems)

        @pl.when(last)
        def _():
            side.finish(sins, souts, sems)

    res = pl.pallas_call(
        full, grid=grid, in_specs=list(in_specs) + [ANY] * si, out_specs=list(out_specs) + [ANY] * so,
        out_shape=list(out_shape) + side.out_shapes, scratch_shapes=list(scratch) + side.scratch,
        input_output_aliases={**aliases, **{ni + a: no + b for a, b in side.aliases}}, compiler_params=_params(),
        name=name)(*args, *side.inputs)
    return list(res[:no]), list(res[no:])


def _run_side(side, name):
    si, so = len(side.inputs), len(side.out_shapes)

    def body(*refs):
        side.start(refs[:si], refs[si:si + so], refs[si + so:])
        if side.middle is not None:
            side.middle(refs[:si], refs[si:si + so], refs[si + so:])
        side.finish(refs[:si], refs[si:si + so], refs[si + so:])

    res = pl.pallas_call(body, in_specs=[ANY] * si, out_specs=[ANY] * so, out_shape=side.out_shapes,
                         scratch_shapes=side.scratch, input_output_aliases=dict(side.aliases), name=name)(*side.inputs)
    return list(res)


def _mm_nn(a, b3, *, out_dtypes, epilogue, name, side=None):
    m, k = a.shape
    s, _, ns = b3.shape
    tm = _pick(m, ROW_TILES)
    tn = _pick(ns, (1024, 1152, 640, 512, 256, 128))
    tk = _pick(k, (2048, 1024, 512))
    nbs, nk = ns // tn, k // tk
    n_out = len(out_dtypes)

    def body(a_ref, b_ref, *rest):
        outs = rest[:n_out]

        def write(v):
            for o, r in zip(outs, epilogue(v)):
                o[...] = r.astype(o.dtype)

        if nk == 1:
            write(jnp.dot(a_ref[...], b_ref[...], preferred_element_type=F32))
            return
        acc = rest[n_out]
        kk = pl.program_id(2)

        @pl.when(kk == 0)
        def _():
            acc[...] = jnp.zeros_like(acc)

        acc[...] += jnp.dot(a_ref[...], b_ref[...], preferred_element_type=F32)

        @pl.when(kk == nk - 1)
        def _():
            write(acc[...])

    return _call(
        body, grid=(m // tm, s * nbs, nk),
        in_specs=[pl.BlockSpec((tm, tk), lambda i, j, kk: (i, kk)),
                  pl.BlockSpec((None, tk, tn), lambda i, j, kk: (j // nbs, kk, j % nbs))],
        out_specs=[pl.BlockSpec((tm, tn), lambda i, j, kk: (i, j))] * n_out,
        out_shape=[jax.ShapeDtypeStruct((m, s * ns), dt) for dt in out_dtypes],
        scratch=[pltpu.VMEM((tm, tn), F32)] if nk > 1 else [], args=(a, b3), name=name, side=side)


def _mm_nt(a, b3, *, out_dtype, name, extra=None, epilogue=None, side=None):
    m = a.shape[0]
    s, ko, ns = b3.shape
    tm = _pick(m, ROW_TILES)
    tko = _pick(ko, (1024, 512))
    tn = _pick(ns, (2048, 1152, 1024, 640, 512, 256, 128))
    nbs = ns // tn
    nk = s * nbs
    n_in = 3 if extra is not None else 2

    def body(a_ref, b_ref, *rest):
        x_ref = rest[0] if extra is not None else None
        o_ref = rest[n_in - 2]

        def write(v):
            if epilogue is not None:
                v = epilogue(v, x_ref[...])
            o_ref[...] = v.astype(o_ref.dtype)

        if nk == 1:
            write(lax.dot_general(a_ref[...], b_ref[...], (((1,), (1,)), ((), ())), preferred_element_type=F32))
            return
        acc = rest[n_in - 1]
        kk = pl.program_id(2)

        @pl.when(kk == 0)
        def _():
            acc[...] = jnp.zeros_like(acc)

        acc[...] += lax.dot_general(a_ref[...], b_ref[...], (((1,), (1,)), ((), ())), preferred_element_type=F32)

        @pl.when(kk == nk - 1)
        def _():
            write(acc[...])

    in_specs = [pl.BlockSpec((tm, tn), lambda i, j, kk: (i, kk)),
                pl.BlockSpec((None, tko, tn), lambda i, j, kk: (kk // nbs, j, kk % nbs))]
    args = [a, b3]
    if extra is not None:
        in_specs.append(pl.BlockSpec((tm, tko), lambda i, j, kk: (i, j)))
        args.append(extra)
    return _call(
        body, grid=(m // tm, ko // tko, nk), in_specs=in_specs,
        out_specs=[pl.BlockSpec((tm, tko), lambda i, j, kk: (i, j))],
        out_shape=[jax.ShapeDtypeStruct((m, ko), out_dtype)],
        scratch=[pltpu.VMEM((tm, tko), F32)] if nk > 1 else [], args=args, name=name, side=side)


def _mm_tn(a, b, *, shards, name, side=None):
    t, k = a.shape
    ns = b.shape[1] // shards
    tt = _pick(t, (2 * ROW_TILES[0],) + ROW_TILES)
    tk = _pick(k, (1024, 512))
    tn = _pick(ns, (1024, 1152, 640, 512, 256, 128))
    nbs, nt = ns // tn, t // tt

    def body(a_ref, b_ref, o_ref, acc):
        tt_i = pl.program_id(2)

        @pl.when(tt_i == 0)
        def _():
            acc[...] = jnp.zeros_like(acc)

        acc[...] += lax.dot_general(a_ref[...], b_ref[...], (((0,), (0,)), ((), ())), preferred_element_type=F32)

        @pl.when(tt_i == nt - 1)
        def _():
            o_ref[...] = acc[...]

    return _call(
        body, grid=(k // tk, shards * nbs, nt),
        in_specs=[pl.BlockSpec((tt, tk), lambda i, j, q: (q, i)),
                  pl.BlockSpec((tt, tn), lambda i, j, q: (q, j))],
        out_specs=[pl.BlockSpec((None, tk, tn), lambda i, j, q: (j // nbs, i, j % nbs))],
        out_shape=[jax.ShapeDtypeStruct((shards, k, ns), F32)], scratch=[pltpu.VMEM((tk, tn), F32)], args=(a, b),
        name=name, side=side)


def _row_tile(t, ctx_rows):
    return 256 if t % 256 == 0 and ctx_rows % 256 == 0 else 128


def _mod_spec(d, ncb, idx):
    return pl.BlockSpec((None, None, 1, d), lambda i: (jnp.where(i >= ncb, 1, 0), idx, 0, 0))


def _norm_fwd(x, z, gate_mods, mods, g, *, gate_idx, shift_idx, scale_idx, ctx_rows, name):
    t, d = x.shape
    tr = _row_tile(t, ctx_rows)
    ncb = ctx_rows // tr
    row = pl.BlockSpec((tr, d), lambda i: (i, 0))
    vec = pl.BlockSpec((1, d), lambda i: (0, 0))
    resid = z is not None

    def body(*refs):
        if resid:
            x_ref, z_ref, gt_ref, g_ref, sh_ref, sc_ref, xo_ref, h_ref = refs
            xn = x_ref[...] + gt_ref[...] * z_ref[...].astype(F32)
            xo_ref[...] = xn
        else:
            x_ref, g_ref, sh_ref, sc_ref, h_ref = refs
            xn = x_ref[...]
        r = lax.rsqrt(jnp.mean(xn * xn, axis=-1, keepdims=True) + EPS)
        h_ref[...] = ((xn * r * g_ref[...]) * (1.0 + sc_ref[...]) + sh_ref[...]).astype(h_ref.dtype)

    mspecs = [vec, _mod_spec(d, ncb, shift_idx), _mod_spec(d, ncb, scale_idx)]
    if resid:
        in_specs = [row, row, _mod_spec(d, ncb, gate_idx)] + mspecs
        args = (x, z, gate_mods, g, mods, mods)
        out_specs = [row, row]
        out_shape = [jax.ShapeDtypeStruct((t, d), F32), jax.ShapeDtypeStruct((t, d), MXU_DTYPE)]
    else:
        in_specs = [row] + mspecs
        args = (x, g, mods, mods)
        out_specs = row
        out_shape = jax.ShapeDtypeStruct((t, d), MXU_DTYPE)
    return pl.pallas_call(body, grid=(t // tr,), in_specs=in_specs, out_specs=out_specs, out_shape=out_shape,
                          compiler_params=_params(), name=name)(*args)


R_DSHIFT, R_DSCALE, R_DG, R_DGATE = 0, 2, 4, 5


def _norm_bwd(dx, dh, xin, mods, g, *, scale_idx, ctx_rows, name, prev=None):
    t, d = dx.shape
    tr = _row_tile(t, ctx_rows)
    ncb = ctx_rows // tr
    row = pl.BlockSpec((tr, d), lambda i: (i, 0))
    vec = pl.BlockSpec((1, d), lambda i: (0, 0))
    acc_spec = pl.BlockSpec((8, d), lambda i: (0, 0))
    has_prev = prev is not None

    def body(*refs):
        if has_prev:
            dx_ref, dh_ref, x_ref, g_ref, sc_ref, z_ref, gt_ref, dxo_ref, dz_ref, acc = refs
        else:
            dx_ref, dh_ref, x_ref, g_ref, sc_ref, dxo_ref, acc = refs
        i = pl.program_id(0)

        @pl.when(i == 0)
        def _():
            acc[...] = jnp.zeros_like(acc)

        lat = jnp.where(i >= ncb, 1.0, 0.0)
        x = x_ref[...]
        r = lax.rsqrt(jnp.mean(x * x, axis=-1, keepdims=True) + EPS)
        xhat = x * r
        gv = g_ref[...]
        dhv = dh_ref[...].astype(F32)
        dn = dhv * (1.0 + sc_ref[...])
        dxhat = dn * gv
        dxin = dx_ref[...] + r * (dxhat - xhat * jnp.mean(dxhat * xhat, axis=-1, keepdims=True))
        dxo_ref[...] = dxin
        dshift = jnp.sum(dhv, axis=0, keepdims=True)
        dscale = jnp.sum(dhv * (xhat * gv), axis=0, keepdims=True)
        acc[R_DSHIFT:R_DSHIFT + 1, :] += dshift * (1.0 - lat)
        acc[R_DSHIFT + 1:R_DSHIFT + 2, :] += dshift * lat
        acc[R_DSCALE:R_DSCALE + 1, :] += dscale * (1.0 - lat)
        acc[R_DSCALE + 1:R_DSCALE + 2, :] += dscale * lat
        acc[R_DG:R_DG + 1, :] += jnp.sum(dn * xhat, axis=0, keepdims=True)
        if has_prev:
            dz_ref[...] = (dxin * gt_ref[...]).astype(dz_ref.dtype)
            dgate = jnp.sum(dxin * z_ref[...].astype(F32), axis=0, keepdims=True)
            acc[R_DGATE:R_DGATE + 1, :] += dgate * (1.0 - lat)
            acc[R_DGATE + 1:R_DGATE + 2, :] += dgate * lat

    in_specs = [row, row, row, vec, _mod_spec(d, ncb, scale_idx)]
    args = [dx, dh, xin, g, mods]
    out_specs = [row]
    out_shape = [jax.ShapeDtypeStruct((t, d), F32)]
    if has_prev:
        z, gate_mods, gate_idx = prev
        in_specs += [row, _mod_spec(d, ncb, gate_idx)]
        args += [z, gate_mods]
        out_specs.append(row)
        out_shape.append(jax.ShapeDtypeStruct((t, d), MXU_DTYPE))
    out_specs.append(acc_spec)
    out_shape.append(jax.ShapeDtypeStruct((8, d), F32))
    return pl.pallas_call(body, grid=(t // tr,), in_specs=in_specs, out_specs=out_specs, out_shape=out_shape,
                          compiler_params=_params(), name=name)(*args)


R_FGATE, R_FG, R_FLOSS = 0, 2, 3


def _final(x1, o, gate_mods, g_final, tgt, *, ctx_rows, name):
    t, d = x1.shape
    tr = _row_tile(t, ctx_rows)
    ncb, nb = ctx_rows // tr, t // tr
    row = pl.BlockSpec((tr, d), lambda i: (i, 0))
    vec = pl.BlockSpec((1, d), lambda i: (0, 0))

    def body(x_ref, o_ref, gt_ref, g_ref, t_ref, dx_ref, do_ref, acc, lsum):
        i = pl.program_id(0)

        @pl.when(i == 0)
        def _():
            acc[...] = jnp.zeros_like(acc)
            lsum[...] = jnp.zeros_like(lsum)

        lat = jnp.where(i >= ncb, 1.0, 0.0)
        gt = gt_ref[...]
        ov = o_ref[...].astype(F32)
        x = x_ref[...] + gt * ov
        r = lax.rsqrt(jnp.mean(x * x, axis=-1, keepdims=True) + EPS)
        xhat = x * r
        gv = g_ref[...]
        err = (xhat * gv - t_ref[...]) * lat
        dy = err / d
        dxhat = dy * gv
        dxv = r * (dxhat - xhat * jnp.mean(dxhat * xhat, axis=-1, keepdims=True))
        dx_ref[...] = dxv
        do_ref[...] = (dxv * gt).astype(do_ref.dtype)
        dgate = jnp.sum(dxv * ov, axis=0, keepdims=True)
        acc[R_FGATE:R_FGATE + 1, :] += dgate * (1.0 - lat)
        acc[R_FGATE + 1:R_FGATE + 2, :] += dgate * lat
        acc[R_FG:R_FG + 1, :] += jnp.sum(dy * xhat, axis=0, keepdims=True)
        lsum[...] += jnp.sum(err * err, axis=0, keepdims=True)

        @pl.when(i == nb - 1)
        def _():
            total = (0.5 / d) * jnp.sum(lsum[...], axis=-1, keepdims=True)
            acc[R_FLOSS:R_FLOSS + 1, :] = jnp.broadcast_to(total, (1, d))

    return pl.pallas_call(
        body, grid=(nb,),
        in_specs=[row, row, _mod_spec(d, ncb, 5), vec, pl.BlockSpec((tr, d), lambda i: (jnp.maximum(i - ncb, 0), 0))],
        out_specs=[row, row, pl.BlockSpec((8, d), lambda i: (0, 0))],
        out_shape=[jax.ShapeDtypeStruct((t, d), F32), jax.ShapeDtypeStruct((t, d), MXU_DTYPE),
                   jax.ShapeDtypeStruct((8, d), F32)],
        scratch_shapes=[pltpu.VMEM((1, d), F32)], compiler_params=_params(), name=name)(x1, o, gate_mods, g_final, tgt)


def _rope_tables(s, ctx_rows):
    rows = s // GRID_W
    row_pos = jnp.repeat(jnp.arange(rows, dtype=F32), GRID_W)
    col_pos = jnp.tile(jnp.arange(GRID_W, dtype=F32), rows)
    quarter = HEAD_DIM // 4
    inv = ROPE_THETA ** (-jnp.arange(0, 2 * quarter, 2, dtype=F32) / (2 * quarter))
    ang_r, ang_c = row_pos[:, None] * inv[None, :], col_pos[:, None] * inv[None, :]
    cr, sr, cc, sc = jnp.cos(ang_r), jnp.sin(ang_r), jnp.cos(ang_c), jnp.sin(ang_c)
    zero = jnp.zeros_like(sr)
    cos = jnp.concatenate([cr, cr, cc, cc], axis=1)
    sa = jnp.concatenate([zero, sr, zero, sc], axis=1)
    sb = jnp.concatenate([-sr, zero, -sc, zero], axis=1)

    def full(tab, fill):
        tab = jnp.tile(tab, (1, LANES // HEAD_DIM))
        return jnp.concatenate([jnp.full((ctx_rows, LANES), fill, F32), tab], axis=0)

    return full(cos, 1.0), full(sa, 0.0), full(sb, 0.0)


def _rope_apply(x, cos, sa, sb, transpose):
    n = x.shape[1] // LANES
    cos, sa, sb = (jnp.tile(v, (1, n)) for v in (cos, sa, sb))
    quarter = HEAD_DIM // 4
    width = x.shape[1]
    if transpose:
        return x * cos + pltpu.roll(x * sa, width - quarter, 1) + pltpu.roll(x * sb, quarter, 1)
    return x * cos + pltpu.roll(x, quarter, 1) * sa + pltpu.roll(x, width - quarter, 1) * sb


def _twice(x):
    lane = lax.broadcasted_iota(I32, (x.shape[0], LANES), 1)
    out = []
    for j in range(N_KV):
        blk = x[:, (j // 2) * LANES:(j // 2 + 1) * LANES]
        own = jnp.where((lane < HEAD_DIM) if j % 2 == 0 else (lane >= HEAD_DIM), blk, 0.0)
        out.append(own + pltpu.roll(own, HEAD_DIM, 1))
    return jnp.concatenate(out, axis=1)


def _rope_fwd(p, tabs, *, da, kw, kv_start, name):
    t = p.shape[0]
    tr = _pick(t, (256, 128))

    def body(q_ref, k_ref, v_ref, c_ref, a_ref, b_ref, qo_ref, ko_ref, vo_ref):
        cos, sa, sb = c_ref[...], a_ref[...], b_ref[...]
        qo_ref[...] = (_rope_apply(q_ref[...], cos, sa, sb, False) * (HEAD_DIM ** -0.5)).astype(qo_ref.dtype)
        ko_ref[...] = _twice(_rope_apply(k_ref[...], cos, sa, sb, False)).astype(ko_ref.dtype)
        vo_ref[...] = _twice(v_ref[...]).astype(vo_ref.dtype)

    tab = pl.BlockSpec((tr, LANES), lambda i: (i, 0))
    two = pl.BlockSpec((tr, N_KV * LANES), lambda i: (i, 0))
    return pl.pallas_call(
        body, grid=(t // tr,),
        in_specs=[pl.BlockSpec((tr, da), lambda i: (i, (kv_start - da) // da)),
                  pl.BlockSpec((tr, kw), lambda i: (i, kv_start // kw)),
                  pl.BlockSpec((tr, kw), lambda i: (i, kv_start // kw + 1)), tab, tab, tab],
        out_specs=[pl.BlockSpec((tr, da), lambda i: (i, 0)), two, two],
        out_shape=[jax.ShapeDtypeStruct((t, da), MXU_DTYPE), jax.ShapeDtypeStruct((t, N_KV * LANES), MXU_DTYPE),
                   jax.ShapeDtypeStruct((t, N_KV * LANES), MXU_DTYPE)],
        compiler_params=_params(), name=name)(p, p, p, *tabs)


def _rope_bwd(dq, dk, dv, tabs, dp, *, name):
    t, da = dq.shape
    kw = dk.shape[1]
    width = da + 2 * kw
    tr = _pick(t, (256, 128))
    col = (dp.shape[1] - width) // width
    assert col * width == dp.shape[1] - width

    def body(q_ref, k_ref, v_ref, c_ref, a_ref, b_ref, dp_ref, o_ref):
        cos, sa, sb = c_ref[...], a_ref[...], b_ref[...]
        o_ref[:, :da] = _rope_apply(q_ref[...], cos, sa, sb, True).astype(o_ref.dtype)
        o_ref[:, da:da + kw] = _rope_apply(k_ref[...], cos, sa, sb, True).astype(o_ref.dtype)
        o_ref[:, da + kw:] = v_ref[...].astype(o_ref.dtype)

    tab = pl.BlockSpec((tr, LANES), lambda i: (i, 0))
    return pl.pallas_call(
        body, grid=(t // tr,),
        in_specs=[pl.BlockSpec((tr, da), lambda i: (i, 0)), pl.BlockSpec((tr, kw), lambda i: (i, 0)),
                  pl.BlockSpec((tr, kw), lambda i: (i, 0)), tab, tab, tab, ANY],
        out_specs=pl.BlockSpec((tr, width), lambda i: (i, col)),
        out_shape=jax.ShapeDtypeStruct(dp.shape, dp.dtype), input_output_aliases={6: 0},
        compiler_params=_params(), name=name)(dq, dk, dv, *tabs, dp)


def _attn_specs(t, ctx_rows):
    nblk = t // QBLK

    def clip(i):
        return jnp.clip(i, 0, nblk - 1)

    return [pl.BlockSpec((QBLK, 2 * LANES), lambda b, i: (clip(i - 1), b)),
            pl.BlockSpec((QBLK, 2 * LANES), lambda b, i: (i, b)),
            pl.BlockSpec((QBLK, 2 * LANES), lambda b, i: (clip(i + 1), b)),
            pl.BlockSpec((ctx_rows, 2 * LANES), lambda b, i: (0, b))]


def _band_bias(rows):
    rr = lax.broadcasted_iota(I32, (rows, 3 * QBLK), 0) & (QBLK - 1)
    cc = lax.broadcasted_iota(I32, (rows, 3 * QBLK), 1)
    return jnp.where(jnp.abs(cc - QBLK - rr) <= WINDOW, 0.0, NEG_INF).astype(F32)


def _local_bias(band_ref, i, t, ctx_rows):
    cc = lax.broadcasted_iota(I32, (1, 3 * QBLK), 1)
    keyrow = (i - 1) * QBLK + cc
    first_key = jnp.where(i * QBLK >= ctx_rows, ctx_rows, t)
    return band_ref[...] + jnp.where((keyrow >= first_key) & (keyrow < t), 0.0, NEG_INF)


def _stack_heads(ref, c0, nblocks):
    lane = lax.broadcasted_iota(I32, (QBLK, LANES), 1)
    rows = []
    for qb in range(nblocks):
        blk = ref[:, c0 + qb * LANES:c0 + (qb + 1) * LANES].astype(F32)
        rows += [jnp.where(lane < HEAD_DIM, blk, 0.0), jnp.where(lane < HEAD_DIM, 0.0, blk)]
    return jnp.concatenate(rows, axis=0)


def _unstack_heads(x, nblocks):
    lane = lax.broadcasted_iota(I32, (QBLK, LANES), 1)
    return [jnp.where(lane < HEAD_DIM, x[2 * qb * QBLK:(2 * qb + 1) * QBLK], x[(2 * qb + 1) * QBLK:(2 * qb + 2) * QBLK])
            for qb in range(nblocks)]


def _sink_column(sink_ref, head0, group):
    row = lax.broadcasted_iota(I32, (group * QBLK, 1), 0)
    col = jnp.zeros((group * QBLK, 1), F32)
    for g in range(group):
        col = jnp.where((row >= g * QBLK) & (row < (g + 1) * QBLK), sink_ref[head0 + g], col)
    return col


def _exps(qs, k_loc, k_ctx, bias, snk):
    s_loc = _dot_nt(qs, k_loc) + bias
    s_ctx = _dot_nt(qs, k_ctx)
    m = jnp.maximum(jnp.maximum(jnp.max(s_loc, axis=-1, keepdims=True), jnp.max(s_ctx, axis=-1, keepdims=True)), snk)
    e_loc, e_ctx, e_snk = jnp.exp(s_loc - m), jnp.exp(s_ctx - m), jnp.exp(snk - m)
    inv = 1.0 / (jnp.sum(e_loc, axis=-1, keepdims=True) + jnp.sum(e_ctx, axis=-1, keepdims=True) + e_snk)
    return e_loc, e_ctx, e_snk, inv


def _attn_fwd(qs, k2, v2, sink, *, ctx_rows, name, side=None):
    t, da = qs.shape
    group = da // HEAD_DIM // N_KV
    nqb = group // 2
    kvspecs = _attn_specs(t, ctx_rows)

    def body(sink_ref, band_ref, q_ref, kp, kc, kn, kx, vp, vc, vn, vx, o_ref):
        b, i = pl.program_id(0), pl.program_id(1)
        bias = _local_bias(band_ref, i, t, ctx_rows)
        for half in range(2):
            lanes = slice(half * LANES, (half + 1) * LANES)
            qst = _mx(_stack_heads(q_ref, half * nqb * LANES, nqb))
            k_loc = jnp.concatenate([kp[:, lanes], kc[:, lanes], kn[:, lanes]], axis=0)
            v_loc = jnp.concatenate([vp[:, lanes], vc[:, lanes], vn[:, lanes]], axis=0)
            snk = _sink_column(sink_ref, (2 * b + half) * group, group)
            e_loc, e_ctx, _, inv = _exps(qst, k_loc, kx[:, lanes], bias, snk)
            out = (_dot(e_loc, v_loc) + _dot(e_ctx, vx[:, lanes])) * inv
            for qb, blk in enumerate(_unstack_heads(out, nqb)):
                c0 = (half * nqb + qb) * LANES
                o_ref[:, c0:c0 + LANES] = blk

    qspec = pl.BlockSpec((QBLK, 2 * group * HEAD_DIM), lambda b, i: (i, b))
    band = pl.BlockSpec((group * QBLK, 3 * QBLK), lambda b, i: (0, 0))
    (ao,), got = _call(
        body, grid=(N_KV // 2, t // QBLK),
        in_specs=[pl.BlockSpec(memory_space=pltpu.SMEM), band, qspec] + kvspecs + kvspecs,
        out_specs=[qspec], out_shape=[jax.ShapeDtypeStruct((t, da), F32)], scratch=[],
        args=(sink, _band_bias(group * QBLK), qs, k2, k2, k2, k2, v2, v2, v2, v2), name=name, side=side)
    return ao, got


def _attn_bwd(qs, k2, v2, dao, sink, *, ctx_rows, name, side=None):
    t, da = qs.shape
    nblk = t // QBLK
    group = da // HEAD_DIM // N_KV
    nqb = group // 2
    kw = N_KV * HEAD_DIM
    nloc = 3 * QBLK
    kvspecs = _attn_specs(t, ctx_rows)

    def body(sink_ref, band_ref, q_ref, do_ref, kp, kc, kn, kx, vp, vc, vn, vx,
             dq_ref, dkl_ref, dvl_ref, dkc_ref, dvc_ref, ds_ref):
        b, i = pl.program_id(0), pl.program_id(1)

        @pl.when((b == 0) & (i == 0))
        def _():
            ds_ref[...] = jnp.zeros_like(ds_ref)

        @pl.when(i == 0)
        def _():
            dkc_ref[...] = jnp.zeros_like(dkc_ref)
            dvc_ref[...] = jnp.zeros_like(dvc_ref)

        bias = _local_bias(band_ref, i, t, ctx_rows)
        srow = lax.broadcasted_iota(I32, ds_ref.shape, 0)
        slane = lax.broadcasted_iota(I32, ds_ref.shape, 1)

        def both(x):
            return x + pltpu.roll(x, HEAD_DIM, 1)

        folded = []
        for half in range(2):
            lanes = slice(half * LANES, (half + 1) * LANES)
            head0 = (2 * b + half) * group
            qst = _mx(_stack_heads(q_ref, half * nqb * LANES, nqb))
            dost = _mx(_stack_heads(do_ref, half * nqb * LANES, nqb))
            k_loc = jnp.concatenate([kp[:, lanes], kc[:, lanes], kn[:, lanes]], axis=0)
            v_loc = jnp.concatenate([vp[:, lanes], vc[:, lanes], vn[:, lanes]], axis=0)
            k_ctx, v_ctx = kx[:, lanes], vx[:, lanes]
            e_loc, e_ctx, e_snk, inv = _exps(qst, k_loc, k_ctx, bias, _sink_column(sink_ref, head0, group))
            p_loc, p_ctx = e_loc * inv, e_ctx * inv
            dp_loc, dp_ctx = _dot_nt(dost, v_loc), _dot_nt(dost, v_ctx)
            dsum = jnp.sum(p_loc * dp_loc, axis=-1, keepdims=True) + jnp.sum(p_ctx * dp_ctx, axis=-1, keepdims=True)
            ds_loc, ds_ctx = _mx(p_loc * (dp_loc - dsum)), _mx(p_ctx * (dp_ctx - dsum))
            dq = (_dot(ds_loc, k_loc) + _dot(ds_ctx, k_ctx)) * (HEAD_DIM ** -0.5)
            for qb, blk in enumerate(_unstack_heads(dq, nqb)):
                c0 = (half * nqb + qb) * LANES
                dq_ref[:, c0:c0 + LANES] = blk
            p_loc, p_ctx = _mx(p_loc), _mx(p_ctx)
            folded.append((both(_dot_tn(ds_loc, qst)), both(_dot_tn(p_loc, dost)),
                           both(_dot_tn(ds_ctx, qst)), both(_dot_tn(p_ctx, dost))))
            dsnk = e_snk * inv * dsum
            for g in range(group):
                part = -jnp.sum(dsnk[g * QBLK:(g + 1) * QBLK])
                ds_ref[...] += jnp.where((srow == 0) & (slane == head0 + g), part, 0.0)
        lane_l = lax.broadcasted_iota(I32, (nloc, LANES), 1)
        lane_c = lax.broadcasted_iota(I32, (ctx_rows, LANES), 1)
        dkl_ref[...] = jnp.where(lane_l < HEAD_DIM, folded[0][0], folded[1][0])
        dvl_ref[...] = jnp.where(lane_l < HEAD_DIM, folded[0][1], folded[1][1])
        dkc_ref[...] += jnp.where(lane_c < HEAD_DIM, folded[0][2], folded[1][2])
        dvc_ref[...] += jnp.where(lane_c < HEAD_DIM, folded[0][3], folded[1][3])

    qspec = pl.BlockSpec((QBLK, 2 * group * HEAD_DIM), lambda b, i: (i, b))
    loc = pl.BlockSpec((None, nloc, LANES), lambda b, i: (i, 0, b))
    cspec = pl.BlockSpec((ctx_rows, LANES), lambda b, i: (0, b))
    band = pl.BlockSpec((group * QBLK, nloc), lambda b, i: (0, 0))
    return _call(
        body, grid=(N_KV // 2, nblk),
        in_specs=[pl.BlockSpec(memory_space=pltpu.SMEM), band, qspec, qspec] + kvspecs + kvspecs,
        out_specs=[qspec, loc, loc, cspec, cspec, pl.BlockSpec((8, LANES), lambda b, i: (0, 0))],
        out_shape=[jax.ShapeDtypeStruct((t, da), F32), jax.ShapeDtypeStruct((nblk, nloc, kw), F32),
                   jax.ShapeDtypeStruct((nblk, nloc, kw), F32), jax.ShapeDtypeStruct((ctx_rows, kw), F32),
                   jax.ShapeDtypeStruct((ctx_rows, kw), F32), jax.ShapeDtypeStruct((8, LANES), F32)],
        scratch=[], args=(sink, _band_bias(group * QBLK), qs, dao, k2, k2, k2, k2, v2, v2, v2, v2), name=name, side=side)


def _kv_reduce(dkl, dvl, dkc, dvc, *, ctx_rows, name):
    nblk, _, kw = dkl.shape
    t = nblk * QBLK
    ncb = ctx_rows // QBLK

    def clip(i):
        return jnp.clip(i, 0, nblk - 1)

    def body(ka, kb, kc, kx, va, vb, vc, vx, dk_ref, dv_ref):
        m = pl.program_id(0)
        lat = m >= ncb
        wa = jnp.where(lat & (m + 1 <= nblk - 1), 1.0, 0.0)
        wc = jnp.where(lat & (m - 1 >= ncb), 1.0, 0.0)
        wl = jnp.where(lat, 1.0, 0.0)
        dk_ref[...] = wl * (kb[...] + wa * ka[...] + wc * kc[...]) + (1.0 - wl) * kx[...]
        dv_ref[...] = wl * (vb[...] + wa * va[...] + wc * vc[...]) + (1.0 - wl) * vx[...]

    slots = [pl.BlockSpec((None, QBLK, kw), lambda m: (clip(m + 1), 0, 0)),
             pl.BlockSpec((None, QBLK, kw), lambda m: (m, 1, 0)),
             pl.BlockSpec((None, QBLK, kw), lambda m: (clip(m - 1), 2, 0))]
    cspec = pl.BlockSpec((QBLK, kw), lambda m: (jnp.clip(m, 0, ncb - 1), 0))
    out = pl.BlockSpec((QBLK, kw), lambda m: (m, 0))
    return pl.pallas_call(
        body, grid=(nblk,), in_specs=slots + [cspec] + slots + [cspec], out_specs=[out, out],
        out_shape=[jax.ShapeDtypeStruct((t, kw), F32)] * 2, compiler_params=_params(), name=name)(
            dkl, dkl, dkl, dkc, dvl, dvl, dvl, dvc)


MERGE_ROWS = 128


def _halo_specs(t, c, col):
    hb = MERGE_ROWS // HALO
    return [pl.BlockSpec((HALO, c), lambda i: (jnp.maximum(i * hb - 1, 0), col)),
            pl.BlockSpec((MERGE_ROWS, c), lambda i: (i, col)),
            pl.BlockSpec((HALO, c), lambda i: (jnp.minimum((i + 1) * hb, t // HALO - 1), col))]


def _ext(refs):
    return jnp.concatenate([r[...] for r in refs], axis=0)


def _conv_ext(cg, hh, w_ref, b_ref, i, t, ctx_rows):
    n = cg.shape[0]
    u = cg * hh
    row = i * MERGE_ROWS - HALO + lax.broadcasted_iota(I32, u.shape, 0)
    first = (row == 0) | (row == ctx_rows)
    last = (row == ctx_rows - 1) | (row == t - 1)
    u_dn = jnp.where(first, 0.0, pltpu.roll(u, 1, 0))
    u_up = jnp.where(last, 0.0, pltpu.roll(u, n - 1, 0))
    cv = w_ref[0:1, :] * u_dn + w_ref[1:2, :] * u + w_ref[2:3, :] * u_up + b_ref[...]
    return u, u_dn, u_up, cv, first, last


def _merge_fwd(p, ao, conv_w, conv_b, g_oc, g_oa, *, ctx_rows, name):
    t = p.shape[0]
    c = ao.shape[1]
    main = slice(HALO, HALO + MERGE_ROWS)

    def body(bg_ref, cgp, cgm, cgn, hhp, hhm, hhn, ao_ref, w_ref, b_ref, gc_ref, ga_ref, o_ref):
        i = pl.program_id(0)
        _, _, _, cv, _, _ = _conv_ext(_ext((cgp, cgm, cgn)), _ext((hhp, hhm, hhn)), w_ref, b_ref, i, t, ctx_rows)
        co = bg_ref[...] * cv[main]
        rc = lax.rsqrt(jnp.mean(co * co, axis=-1, keepdims=True) + EPS)
        o_ref[:, :c] = (co * rc * gc_ref[...]).astype(o_ref.dtype)
        av = ao_ref[...]
        ra = lax.rsqrt(jnp.mean(av * av, axis=-1, keepdims=True) + EPS)
        o_ref[:, c:] = (av * ra * ga_ref[...]).astype(o_ref.dtype)

    vec = pl.BlockSpec((1, c), lambda i: (0, 0))
    return pl.pallas_call(
        body, grid=(t // MERGE_ROWS,),
        in_specs=[pl.BlockSpec((MERGE_ROWS, c), lambda i: (i, 0))] + _halo_specs(t, c, 1) + _halo_specs(t, c, 2)
        + [pl.BlockSpec((MERGE_ROWS, c), lambda i: (i, 0)), pl.BlockSpec((3, c), lambda i: (0, 0)), vec, vec, vec],
        out_specs=pl.BlockSpec((MERGE_ROWS, 2 * c), lambda i: (i, 0)),
        out_shape=jax.ShapeDtypeStruct((t, 2 * c), MXU_DTYPE), compiler_params=_params(), name=name)(
            p, p, p, p, p, p, p, ao, conv_w, conv_b, g_oc, g_oa)


R_DGOC, R_DGOA, R_DCB, R_DCW = 0, 1, 2, 3


def _merge_bwd(dmg, p, ao, conv_w, conv_b, g_oc, g_oa, *, width, ctx_rows, name):
    t = p.shape[0]
    c = ao.shape[1]
    main = slice(HALO, HALO + MERGE_ROWS)

    def body(dyp, dym, dyn, dya_ref, bgp, bgm, bgn, cgp, cgm, cgn, hhp, hhm, hhn, ao_ref, w_ref, b_ref, gc_ref, ga_ref,
             dp_ref, dao_ref, acc):
        i = pl.program_id(0)

        @pl.when(i == 0)
        def _():
            acc[...] = jnp.zeros_like(acc)

        bg, cg, hh = _ext((bgp, bgm, bgn)), _ext((cgp, cgm, cgn)), _ext((hhp, hhm, hhn))
        n = bg.shape[0]
        u, u_dn, u_up, cv, first, last = _conv_ext(cg, hh, w_ref, b_ref, i, t, ctx_rows)
        co = bg * cv
        rc = lax.rsqrt(jnp.mean(co * co, axis=-1, keepdims=True) + EPS)
        cohat = co * rc
        dyc = _ext((dyp, dym, dyn))
        t1 = dyc * gc_ref[...]
        dco = rc * (t1 - cohat * jnp.mean(t1 * cohat, axis=-1, keepdims=True))
        dcv = dco * bg
        dcv_next = jnp.where(last, 0.0, pltpu.roll(dcv, n - 1, 0))
        dcv_prev = jnp.where(first, 0.0, pltpu.roll(dcv, 1, 0))
        du = w_ref[1:2, :] * dcv + w_ref[0:1, :] * dcv_next + w_ref[2:3, :] * dcv_prev
        dp_ref[:, :c] = (dco * cv)[main].astype(dp_ref.dtype)
        dp_ref[:, c:2 * c] = (du * hh)[main].astype(dp_ref.dtype)
        dp_ref[:, 2 * c:] = (du * cg)[main].astype(dp_ref.dtype)
        dcv_m = dcv[main]
        acc[R_DGOC:R_DGOC + 1, :] += jnp.sum((dyc * cohat)[main], axis=0, keepdims=True)
        acc[R_DCB:R_DCB + 1, :] += jnp.sum(dcv_m, axis=0, keepdims=True)
        acc[R_DCW:R_DCW + 1, :] += jnp.sum(dcv_m * u_dn[main], axis=0, keepdims=True)
        acc[R_DCW + 1:R_DCW + 2, :] += jnp.sum(dcv_m * u[main], axis=0, keepdims=True)
        acc[R_DCW + 2:R_DCW + 3, :] += jnp.sum(dcv_m * u_up[main], axis=0, keepdims=True)
        av = ao_ref[...]
        ra = lax.rsqrt(jnp.mean(av * av, axis=-1, keepdims=True) + EPS)
        ahat = av * ra
        dya = dya_ref[...]
        t2 = dya * ga_ref[...]
        dao_ref[...] = ra * (t2 - ahat * jnp.mean(t2 * ahat, axis=-1, keepdims=True))
        acc[R_DGOA:R_DGOA + 1, :] += jnp.sum(dya * ahat, axis=0, keepdims=True)

    vec = pl.BlockSpec((1, c), lambda i: (0, 0))
    tile = pl.BlockSpec((MERGE_ROWS, c), lambda i: (i, 0))
    return pl.pallas_call(
        body, grid=(t // MERGE_ROWS,),
        in_specs=_halo_specs(t, c, 0) + [pl.BlockSpec((MERGE_ROWS, c), lambda i: (i, 1))]
        + _halo_specs(t, c, 0) + _halo_specs(t, c, 1) + _halo_specs(t, c, 2)
        + [tile, pl.BlockSpec((3, c), lambda i: (0, 0)), vec, vec, vec],
        out_specs=[pl.BlockSpec((MERGE_ROWS, 3 * c), lambda i: (i, 0)), tile, pl.BlockSpec((8, c), lambda i: (0, 0))],
        out_shape=[jax.ShapeDtypeStruct((t, width), MXU_DTYPE), jax.ShapeDtypeStruct((t, c), F32),
                   jax.ShapeDtypeStruct((8, c), F32)],
        compiler_params=_params(), name=name)(dmg, dmg, dmg, dmg, p, p, p, p, p, p, p, p, p, ao, conv_w, conv_b, g_oc, g_oa)


def _local_step(x0, tgt, mods, wts, small, *, ctx_rows, comm=None):
    t, d = x0.shape
    depth = mods.shape[0]
    c = d // 2
    da = d - c
    kw = N_KV * HEAD_DIM
    kv_start = 3 * c + da
    shards = N_CHIP
    tabs = _rope_tables(t - ctx_rows, ctx_rows)
    kwargs = dict(ctx_rows=ctx_rows)

    saved = []
    xs, z_prev = x0, None
    have = {}

    def riding(l, host):
        want = {"in": [(l, 1), (l + 1, 0)], "attn": [(l, 2)], "out": [], "mlp1": [(l, 3)], "mlp2": []}
        keys = [key for key in want[host] if key[0] < depth] if comm is not None else []
        return keys, (_gather_side([comm.halves[a][k] for a, k in keys]) if keys else None)

    def weight(l, k):
        return wts[l][k] if comm is None else comm.weight(k, have[(l, k)])

    if comm is not None:
        have[(0, 0)] = _run_side(_gather_side([comm.halves[0][0]]), "gather_weights_0")[0]
    for l in range(depth):
        if z_prev is None:
            x_in = xs
            h1 = _norm_fwd(xs, None, None, mods[l], small["g_norm1"][l], gate_idx=None, shift_idx=0, scale_idx=1,
                           name=f"norm1_fwd_{l}", **kwargs)
        else:
            x_in, h1 = _norm_fwd(xs, z_prev, mods[l - 1], mods[l], small["g_norm1"][l], gate_idx=5, shift_idx=0,
                                 scale_idx=1, name=f"norm1_fwd_{l}", **kwargs)
        keys, side = riding(l, "in")
        (p,), got = _mm_nn(h1, weight(l, 0), out_dtypes=(F32,), epilogue=lambda v: (v,), name=f"in_proj_{l}", side=side)
        have.update(zip(keys, got))
        qs, k2, v2 = _rope_fwd(p, tabs, da=da, kw=kw, kv_start=kv_start, name=f"rope_fwd_{l}")
        keys, side = riding(l, "attn")
        ao, got = _attn_fwd(qs, k2, v2, small["sink"][l], name=f"attn_fwd_{l}", side=side, **kwargs)
        have.update(zip(keys, got))
        mg = _merge_fwd(p, ao, small["conv_w"][l], small["conv_b"][l], small["g_out_conv"][l], small["g_out_attn"][l],
                        name=f"merge_fwd_{l}", **kwargs)
        keys, side = riding(l, "out")
        (z,), got = _mm_nn(mg, weight(l, 1), out_dtypes=(MXU_DTYPE,), epilogue=lambda v: (v,), name=f"out_proj_{l}", side=side)
        have.update(zip(keys, got))
        x_mid, h2 = _norm_fwd(x_in, z, mods[l], mods[l], small["g_norm2"][l], gate_idx=2, shift_idx=3, scale_idx=4,
                              name=f"norm2_fwd_{l}", **kwargs)
        keys, side = riding(l, "mlp1")
        (a_act, s_act), got = _mm_nn(h2, weight(l, 2), out_dtypes=(MXU_DTYPE, MXU_DTYPE),
                                     epilogue=lambda v: (v, jnp.square(jnp.maximum(v, 0.0))), name=f"mlp1_{l}", side=side)
        have.update(zip(keys, got))
        keys, side = riding(l, "mlp2")
        (o,), got = _mm_nn(s_act, weight(l, 3), out_dtypes=(MXU_DTYPE,), epilogue=lambda v: (v,), name=f"mlp2_{l}", side=side)
        have.update(zip(keys, got))
        saved.append(dict(x_in=x_in, h1=h1, p=p, qs=qs, k2=k2, v2=v2, ao=ao, mg=mg, z=z, x_mid=x_mid, h2=h2, a=a_act, s=s_act, o=o))
        xs, z_prev = x_mid, o

    dx, do, fin = _final(xs, z_prev, mods[depth - 1], small["g_final"], tgt, name="final", **kwargs)

    grads = [None] * depth
    dmods = [[None] * N_MOD for _ in range(depth)]
    sg = {k: [None] * depth for k in ("g_norm1", "g_norm2", "conv_w", "conv_b", "sink", "g_out_conv", "g_out_attn")}
    dmods[depth - 1][5] = fin[R_FGATE:R_FGATE + 2]
    sync = None
    for l in reversed(range(depth)):
        w_in, w_out, w1, w2 = (weight(l, k) for k in range(4))
        sv = saved[l]
        (da_act,), got = _mm_nt(do, w2, out_dtype=MXU_DTYPE, extra=sv["a"],
                                epilogue=lambda v, a: v * (2.0 * jnp.maximum(a.astype(F32), 0.0)), name=f"mlp2_dx_{l}",
                                side=sync and sync.pair_side())
        sync and sync.add(got)
        (g_w2,), got = _mm_tn(sv["s"], do, shards=1, name=f"mlp2_dw_{l}", side=sync and sync.chips_side((2,)))
        sync and sync.land((2,), got)
        (dh2,), got = _mm_nt(da_act, w1, out_dtype=MXU_DTYPE, name=f"mlp1_dx_{l}", side=sync and sync.chips_side((3,)))
        sync and sync.land((3,), got)
        (g_w1,), got = _mm_tn(sv["h2"], da_act, shards=shards, name=f"mlp1_dw_{l}", side=sync and sync.chips_side((0, 1)))
        sync and sync.land((0, 1), got)
        sync and sync.sum()
        own = _GradSync(comm, l, {2: g_w1, 3: g_w2}) if comm is not None and l == 0 else None
        dx, dz, sums2 = _norm_bwd(dx, dh2, sv["x_mid"], mods[l], small["g_norm2"][l], scale_idx=4,
                                  prev=(sv["z"], mods[l], 2), name=f"norm2_bwd_{l}", **kwargs)
        (dmg,), got = _mm_nt(dz, w_out, out_dtype=F32, name=f"out_proj_dx_{l}", side=own and own.pair_side())
        own and own.add(got)
        (g_wo,), _ = _mm_tn(sv["mg"], dz, shards=1, name=f"out_proj_dw_{l}")
        qkv_w = da + 2 * kw
        in_place = (3 * c) % qkv_w == 0
        dpc, dao, msum = _merge_bwd(dmg, sv["p"], sv["ao"], small["conv_w"][l], small["conv_b"][l],
                                    small["g_out_conv"][l], small["g_out_attn"][l],
                                    width=3 * c + qkv_w if in_place else 3 * c, name=f"merge_bwd_{l}", **kwargs)
        (dq, dkl, dvl, dkc, dvc, dsink), got = _attn_bwd(sv["qs"], sv["k2"], sv["v2"], dao, small["sink"][l],
                                                         name=f"attn_bwd_{l}", side=own and own.chips_side(own.keys), **kwargs)
        own and own.land(own.keys, got)
        own and own.sum()
        dk, dv = _kv_reduce(dkl, dvl, dkc, dvc, ctx_rows=ctx_rows, name=f"kv_reduce_{l}")
        if in_place:
            dp = _rope_bwd(dq, dk, dv, tabs, dpc, name=f"rope_bwd_{l}")
        else:
            dqkv = _rope_bwd(dq, dk, dv, tabs, lax.empty((t, qkv_w), MXU_DTYPE), name=f"rope_bwd_{l}")
            dp = jnp.concatenate([dpc, dqkv], axis=1)
        (dh1,), got = _mm_nt(dp, w_in, out_dtype=MXU_DTYPE, name=f"in_proj_dx_{l}", side=sync and sync.share_side())
        sync and sync.adam(got)
        (g_wi,), got = _mm_tn(sv["h1"], dp, shards=shards, name=f"in_proj_dw_{l}", side=own and own.share_side())
        own and own.adam(got)
        if comm is not None:
            sync = _GradSync(comm, l, {0: g_wi, 1: g_wo} if own else {0: g_wi, 1: g_wo, 2: g_w1, 3: g_w2})
        if l > 0:
            dx, do, sums1 = _norm_bwd(dx, dh1, sv["x_in"], mods[l], small["g_norm1"][l], scale_idx=1,
                                      prev=(saved[l - 1]["o"], mods[l - 1], 5), name=f"norm1_bwd_{l}", **kwargs)
            dmods[l - 1][5] = sums1[R_DGATE:R_DGATE + 2]
        else:
            dx, sums1 = _norm_bwd(dx, dh1, sv["x_in"], mods[l], small["g_norm1"][l], scale_idx=1,
                                  name=f"norm1_bwd_{l}", **kwargs)
        grads[l] = (g_wi, g_wo, g_w1, g_w2)
        dmods[l][0] = sums1[R_DSHIFT:R_DSHIFT + 2]
        dmods[l][1] = sums1[R_DSCALE:R_DSCALE + 2]
        dmods[l][2] = sums2[R_DGATE:R_DGATE + 2]
        dmods[l][3] = sums2[R_DSHIFT:R_DSHIFT + 2]
        dmods[l][4] = sums2[R_DSCALE:R_DSCALE + 2]
        sg["g_norm1"][l] = sums1[R_DG]
        sg["g_norm2"][l] = sums2[R_DG]
        sg["g_out_conv"][l] = msum[R_DGOC]
        sg["g_out_attn"][l] = msum[R_DGOA]
        sg["conv_b"][l] = msum[R_DCB]
        sg["conv_w"][l] = msum[R_DCW:R_DCW + 3]
        sg["sink"][l] = dsink[0, :da // HEAD_DIM]
    dmods = jnp.stack([jnp.stack(row, axis=1) for row in dmods])
    sg = {k: jnp.stack(v) for k, v in sg.items()}
    sg["g_final"] = fin[R_FG]
    return fin[R_FLOSS, 0], dx, grads, dmods, sg, sync


N_DEV = 8
N_CHIP = 4


def _place():
    mx, my, mc = lax.axis_index("x"), lax.axis_index("y"), lax.axis_index("c")
    others = [(1 - mx, my), (mx, 1 - my), (1 - mx, 1 - my)]
    return mx, my, mc, others


def _remote(src, dst, send_sems, recv_sems, k, dev):
    return pltpu.make_async_remote_copy(src_ref=src, dst_ref=dst, send_sem=send_sems.at[k], recv_sem=recv_sems.at[k],
                                        device_id=dev, device_id_type=MESH)


def _allgather8(x, name):
    r, ccols = x.shape

    def body(x_ref, out_ref, send_sems, recv_sems, local_sem):
        mx, my, mc, _ = _place()
        me = 4 * mx + 2 * my + mc
        mine = pltpu.make_async_copy(x_ref, out_ref.at[me], local_sem)
        mine.start()
        sent = []
        for k in range(1, N_DEV):
            fx, fy, fc = (k >> 2) & 1, (k >> 1) & 1, k & 1
            px, py, pc = (1 - mx if fx else mx), (1 - my if fy else my), (1 - mc if fc else mc)
            cp = _remote(x_ref, out_ref.at[me], send_sems, recv_sems, k - 1, (px, py, pc))
            cp.start()
            sent.append((cp, 4 * px + 2 * py + pc, (px, py, pc)))
        for k, (cp, peer, dev) in enumerate(sent):
            _remote(x_ref, out_ref.at[peer], send_sems, recv_sems, k, dev).wait_recv()
        for cp, _, _ in sent:
            cp.wait_send()
        mine.wait()

    vm = pl.BlockSpec(memory_space=pltpu.VMEM)
    return pl.pallas_call(
        body, in_specs=[vm], out_specs=vm, out_shape=jax.ShapeDtypeStruct((N_DEV, r, ccols), x.dtype),
        scratch_shapes=[pltpu.SemaphoreType.DMA((N_DEV - 1,)), pltpu.SemaphoreType.DMA((N_DEV - 1,)),
                        pltpu.SemaphoreType.DMA], name=name)(x)


def _gather_side(halves):
    n = len(halves)

    def copies(ins, outs, sems):
        send_sems, recv_sems, local_sems = sems
        mx, my, mc, others = _place()
        chip = 2 * mx + my
        sib = (mx, my, 1 - mc)

        def src(w):
            return ins[w].at[pl.ds(mc, 1)]

        def slot(w, ch, core):
            return outs[w].at[ch, pl.ds(core, 1)]

        locs = [pltpu.make_async_copy(src(w), slot(w, chip, mc), local_sems.at[w]) for w in range(n)]
        first, landed, passed, from_sib = [], [], [], []
        for w in range(n):
            first.append(_remote(src(w), slot(w, chip, mc), send_sems, recv_sems, 7 * w, sib))
            from_sib.append(_remote(src(w), slot(w, chip, 1 - mc), send_sems, recv_sems, 7 * w, sib))
            for j, (ox, oy) in enumerate(others):
                och = 2 * ox + oy
                first.append(_remote(src(w), slot(w, chip, mc), send_sems, recv_sems, 7 * w + 1 + j, (ox, oy, mc)))
                landed.append(_remote(src(w), slot(w, och, mc), send_sems, recv_sems, 7 * w + 1 + j, (ox, oy, mc)))
                passed.append(_remote(slot(w, och, mc), slot(w, och, mc), send_sems, recv_sems, 7 * w + 4 + j, sib))
                from_sib.append(_remote(src(w), slot(w, och, 1 - mc), send_sems, recv_sems, 7 * w + 4 + j, sib))
        return locs, first, landed, passed, from_sib

    def start(ins, outs, sems):
        locs, first, _, _, _ = copies(ins, outs, sems)
        for cp in locs + first:
            cp.start()

    def middle(ins, outs, sems):
        _, _, landed, passed, _ = copies(ins, outs, sems)
        for cp, fw in zip(landed, passed):
            cp.wait_recv()
            fw.start()

    def finish(ins, outs, sems):
        locs, first, _, passed, from_sib = copies(ins, outs, sems)
        for cp in from_sib:
            cp.wait_recv()
        for cp in first + passed:
            cp.wait_send()
        for cp in locs:
            cp.wait()

    return _Side(halves, [jax.ShapeDtypeStruct((N_CHIP,) + h.shape, h.dtype) for h in halves],
                 [pltpu.SemaphoreType.DMA((7 * n,)), pltpu.SemaphoreType.DMA((7 * n,)), pltpu.SemaphoreType.DMA((n,))],
                 start, finish, middle=middle)


def _pair_side(gs):
    n = len(gs)

    def copies(ins, outs, sems):
        mx, my, mc, _ = _place()
        return [_remote(ins[w].at[:, pl.ds(1 - mc, 1)], outs[w], sems[0], sems[1], w, (mx, my, 1 - mc)) for w in range(n)]

    def start(ins, outs, sems):
        for cp in copies(ins, outs, sems):
            cp.start()

    def finish(ins, outs, sems):
        for cp in copies(ins, outs, sems):
            cp.wait()

    return _Side(gs, [jax.ShapeDtypeStruct((g.shape[0], 1) + g.shape[2:], g.dtype) for g in gs],
                 [pltpu.SemaphoreType.DMA((n,)), pltpu.SemaphoreType.DMA((n,))], start, finish)


def _pair_add(g, got, core, name):
    s, _, rh, ccols = g.shape
    tr = _pick(rh, (256, 128))

    def body(core_ref, g_ref, r_ref, o_ref):
        o_ref[...] = (g_ref[...] + r_ref[...]).astype(o_ref.dtype)

    spec = pltpu.PrefetchScalarGridSpec(
        num_scalar_prefetch=1, grid=(s, rh // tr),
        in_specs=[pl.BlockSpec((None, None, tr, ccols), lambda a, i, cr: (a, cr[0], i, 0)),
                  pl.BlockSpec((None, None, tr, ccols), lambda a, i, cr: (a, 0, i, 0))],
        out_specs=pl.BlockSpec((None, tr, ccols), lambda a, i, cr: (a, i, 0)))
    return pl.pallas_call(body, grid_spec=spec, out_shape=jax.ShapeDtypeStruct((s, rh, ccols), MXU_DTYPE),
                          compiler_params=_params(), name=name)(core, g, got)


def _chips_side(ps):
    n = len(ps)

    def copies(ins, outs, sems):
        send_sems, recv_sems, local_sems = sems
        mx, my, mc, others = _place()
        chip = 2 * mx + my
        locs = [pltpu.make_async_copy(ins[w].at[chip], outs[w].at[chip], local_sems.at[w]) for w in range(n)]
        sends, lands = [], []
        for w in range(n):
            for j, (ox, oy) in enumerate(others):
                och = 2 * ox + oy
                sends.append(_remote(ins[w].at[och], outs[w].at[chip], send_sems, recv_sems, 3 * w + j, (ox, oy, mc)))
                lands.append(_remote(ins[w].at[och], outs[w].at[och], send_sems, recv_sems, 3 * w + j, (ox, oy, mc)))
        return locs, sends, lands

    def start(ins, outs, sems):
        locs, sends, _ = copies(ins, outs, sems)
        for cp in locs + sends:
            cp.start()

    def finish(ins, outs, sems):
        locs, sends, lands = copies(ins, outs, sems)
        for cp in lands:
            cp.wait_recv()
        for cp in sends:
            cp.wait_send()
        for cp in locs:
            cp.wait()

    return _Side(ps, [jax.ShapeDtypeStruct(p.shape, p.dtype) for p in ps],
                 [pltpu.SemaphoreType.DMA((3 * n,)), pltpu.SemaphoreType.DMA((3 * n,)), pltpu.SemaphoreType.DMA((n,))],
                 start, finish)


def _chip_sum(rb, core, name):
    s, rh, ccols = rb.shape
    tr = _pick(rh, (256, 128))

    def body(core_ref, r_ref, o_ref):
        tot = r_ref[0].astype(F32)
        for k in range(1, s):
            tot = tot + r_ref[k].astype(F32)
        o_ref[...] = tot

    spec = pltpu.PrefetchScalarGridSpec(
        num_scalar_prefetch=1, grid=(rh // tr,),
        in_specs=[pl.BlockSpec((s, tr, ccols), lambda i, cr: (0, i, 0))],
        out_specs=pl.BlockSpec((None, tr, ccols), lambda i, cr: (cr[0], i, 0)))
    return pl.pallas_call(body, grid_spec=spec, out_shape=jax.ShapeDtypeStruct((2, rh, ccols), F32),
                          compiler_params=_params(), name=name)(core, rb)


def _share_side(fulls):
    n = len(fulls)

    def copies(ins, outs, sems):
        mx, my, mc, _ = _place()
        sib = (mx, my, 1 - mc)
        sends = [_remote(ins[w].at[mc], outs[w].at[mc], sems[0], sems[1], w, sib) for w in range(n)]
        lands = [_remote(ins[w].at[mc], outs[w].at[1 - mc], sems[0], sems[1], w, sib) for w in range(n)]
        return sends, lands

    def start(ins, outs, sems):
        for cp in copies(ins, outs, sems)[0]:
            cp.start()

    def finish(ins, outs, sems):
        sends, lands = copies(ins, outs, sems)
        for cp in lands:
            cp.wait_recv()
        for cp in sends:
            cp.wait_send()

    return _Side(fulls, [jax.ShapeDtypeStruct(f.shape, f.dtype) for f in fulls],
                 [pltpu.SemaphoreType.DMA((n,)), pltpu.SemaphoreType.DMA((n,))], start, finish,
                 aliases=[(w, w) for w in range(n)])


def _cast(w, name):
    r, ccols = w.shape
    tr = _pick(r, (256, 128))

    def body(w_ref, o_ref):
        o_ref[...] = w_ref[...].astype(o_ref.dtype)

    spec = pl.BlockSpec((tr, ccols), lambda i: (i, 0))
    return pl.pallas_call(body, grid=(r // tr,), in_specs=[spec], out_specs=spec,
                          out_shape=jax.ShapeDtypeStruct((r, ccols), MXU_DTYPE), compiler_params=_params(), name=name)(w)


def _adam_math(g, w, m, v):
    m = ADAM_B1 * m + (1.0 - ADAM_B1) * g
    v = ADAM_B2 * v + (1.0 - ADAM_B2) * jnp.square(g)
    m_hat = m / (1.0 - ADAM_B1 ** ADAM_STEP)
    v_hat = v / (1.0 - ADAM_B2 ** ADAM_STEP)
    return -ADAM_LR * (m_hat / (jnp.sqrt(v_hat) + ADAM_EPS) + ADAM_WD * w), m, v


def _adamw_layer(l, g, w, m, v, bufs, name, side=None):
    depth, r, ccols = w.shape
    tr = _pick(r, (128,))

    def body(g_ref, w_ref, m_ref, v_ref, b0, b1, b2, b3, go_ref, d_ref, mo_ref, vo_ref):
        gv = g_ref[...]
        d, m2, v2 = _adam_math(gv, w_ref[...], m_ref[...], v_ref[...])
        go_ref[...] = gv
        d_ref[...] = d
        mo_ref[...] = m2
        vo_ref[...] = v2

    lay = pl.BlockSpec((None, tr, ccols), lambda i: (l, i, 0))
    return _call(
        body, grid=(r // tr,), in_specs=[pl.BlockSpec((tr, ccols), lambda i: (i, 0)), lay, lay, lay] + [ANY] * 4,
        out_specs=[lay] * 4, out_shape=[jax.ShapeDtypeStruct((depth, r, ccols), F32)] * 4, scratch=[],
        args=(g, w, m, v, *bufs), name=name, side=side, aliases={4: 0, 5: 1, 6: 2, 7: 3})


def _adamw_small(g, g2, w, m, v, name):
    two = g2 is not None

    def body(*refs):
        if two:
            g_ref, g2_ref, w_ref, m_ref, v_ref, go_ref, d_ref, mo_ref, vo_ref = refs
            gv = g_ref[...] + g2_ref[...]
        else:
            g_ref, w_ref, m_ref, v_ref, go_ref, d_ref, mo_ref, vo_ref = refs
            gv = g_ref[...]
        d, m2, v2 = _adam_math(gv, w_ref[...], m_ref[...], v_ref[...])
        go_ref[...] = gv
        d_ref[...] = d
        mo_ref[...] = m2
        vo_ref[...] = v2

    args = [g] + ([g2] if two else []) + [w, m, v]
    vm = pl.BlockSpec(memory_space=pltpu.VMEM)
    return pl.pallas_call(body, in_specs=[vm] * len(args), out_specs=[vm] * 4,
                          out_shape=[jax.ShapeDtypeStruct(w.shape, F32)] * 4, name=name)(*args)


def _sum8(g, name):
    def body(g_ref, o_ref):
        tot = g_ref[0]
        for k in range(1, N_DEV):
            tot = tot + g_ref[k]
        o_ref[...] = tot

    vm = pl.BlockSpec(memory_space=pltpu.VMEM)
    return pl.pallas_call(body, in_specs=[vm], out_specs=vm, out_shape=jax.ShapeDtypeStruct(g.shape[1:], F32),
                          compiler_params=_params(), name=name)(g)


def _pack(arrs, width):
    flat = jnp.concatenate([a.reshape(-1).astype(F32) for a in arrs])
    rows = -(-flat.size // (8 * width)) * 8
    return jnp.pad(flat, (0, rows * width - flat.size)).reshape(rows, width)


def _unpack(flat, shapes):
    out, off = [], 0
    for shp in shapes:
        size = 1
        for v in shp:
            size *= v
        out.append(flat[..., off:off + size].reshape(flat.shape[:-1] + tuple(shp)))
        off += size
    return out


class _Comm:
    def __init__(self, core, params):
        self.core, self.params = core, params
        depth = params[0][1].shape[0]
        self.halves = [[_cast(w[l], f"cast_{nm}_{l}").reshape(2, w.shape[1] // 2, w.shape[2]) for nm, w, _, _ in params]
                       for l in range(depth)]
        self.stacked = [[lax.empty(w.shape, F32) for _ in range(4)] for _, w, _, _ in params]

    def weight(self, k, gathered):
        rows, cols = 2 * gathered.shape[2], gathered.shape[3]
        return gathered.reshape(N_CHIP, rows, cols) if k % 2 == 0 else gathered.reshape(1, N_CHIP * rows, cols)


class _GradSync:
    def __init__(self, comm, l, grads):
        self.comm, self.l, self.keys = comm, l, tuple(sorted(grads))
        self.gs = {k: g.reshape(N_CHIP, 2, g.shape[0] * g.shape[1] // (2 * N_CHIP), g.shape[2]) for k, g in grads.items()}
        self.ps, self.rb, self.full = {}, {}, {}

    def pair_side(self):
        return _pair_side([self.gs[k] for k in self.keys])

    def add(self, got):
        for k, r in zip(self.keys, got):
            self.ps[k] = _pair_add(self.gs[k], r, self.comm.core, f"rs_add_{self.l}_{k}")

    def chips_side(self, which):
        return _chips_side([self.ps[k] for k in which])

    def land(self, which, got):
        self.rb.update(zip(which, got))

    def sum(self):
        for k in self.keys:
            self.full[k] = _chip_sum(self.rb[k], self.comm.core, f"rs_sum_{self.l}_{k}")

    def share_side(self):
        return _share_side([self.full[k] for k in self.keys])

    def adam(self, got):
        for k, full in zip(self.keys, got):
            nm, w, m, v = self.comm.params[k]
            gsum = full.reshape(2 * full.shape[1], full.shape[2])
            self.comm.stacked[k], _ = _adamw_layer(self.l, gsum, w, m, v, self.comm.stacked[k], f"adamw_{nm}_{self.l}")


COND_ROWS = 16


def _ada_fwd(cond, w_ada, b_cols, name):
    depth, d, ns = w_ada.shape
    tn = _pick(ns, (512, 384, 256, 128))

    def body(c_ref, w_ref, b_ref, o_ref):
        o_ref[...] = _dot(_silu(c_ref[...]), w_ref[...]) + b_ref[...]

    return pl.pallas_call(
        body, grid=(depth, ns // tn),
        in_specs=[pl.BlockSpec((COND_ROWS, d), lambda l, j: (0, 0)), pl.BlockSpec((None, d, tn), lambda l, j: (l, 0, j)),
                  pl.BlockSpec((None, 1, tn), lambda l, j: (l, 0, j))],
        out_specs=pl.BlockSpec((None, COND_ROWS, tn), lambda l, j: (l, 0, j)),
        out_shape=jax.ShapeDtypeStruct((depth, COND_ROWS, ns), F32), compiler_params=_params(), name=name)(cond, w_ada, b_cols)


def _ada_bwd(cond, dmod, w_ada, name):
    depth, d, ns = w_ada.shape
    tn = _pick(ns, (512, 384, 256, 128))

    def body(c_ref, dm_ref, w_ref, gw_ref, dc_ref):
        @pl.when((pl.program_id(0) == 0) & (pl.program_id(1) == 0))
        def _():
            dc_ref[...] = jnp.zeros_like(dc_ref)

        dm = dm_ref[...]
        gw_ref[...] = _dot_tn(_silu(c_ref[...]), dm)
        dc_ref[...] += _dot_nt(dm, w_ref[...])

    return pl.pallas_call(
        body, grid=(depth, ns // tn),
        in_specs=[pl.BlockSpec((COND_ROWS, d), lambda l, j: (0, 0)),
                  pl.BlockSpec((None, COND_ROWS, tn), lambda l, j: (l, 0, j)),
                  pl.BlockSpec((None, d, tn), lambda l, j: (l, 0, j))],
        out_specs=[pl.BlockSpec((None, d, tn), lambda l, j: (l, 0, j)), pl.BlockSpec((COND_ROWS, d), lambda l, j: (0, 0))],
        out_shape=[jax.ShapeDtypeStruct((depth, d, ns), F32), jax.ShapeDtypeStruct((COND_ROWS, d), F32)],
        compiler_params=_params(), name=name)(cond, dmod, w_ada)


def _cctx_grad(parts, c_ctx, name):
    def body(p_ref, c_ref, o_ref):
        tot = p_ref[0, 0:1, :]
        for k in range(1, N_CHIP):
            tot = tot + p_ref[2 * k, 0:1, :]
        z = c_ref[...]
        sg = 1.0 / (1.0 + jnp.exp(-z))
        o_ref[...] = tot * (sg + z * sg * (1.0 - sg))

    vm = pl.BlockSpec(memory_space=pltpu.VMEM)
    return pl.pallas_call(body, in_specs=[vm, vm], out_specs=vm, out_shape=jax.ShapeDtypeStruct(c_ctx.shape, F32),
                          name=name)(parts, c_ctx)


def kernel(x, c, ctx, c_ctx, w_ada, b_ada, g_norm1, g_norm2, w_in, conv_w, conv_b, sink, g_out_conv, g_out_attn, w_out, w_mlp1, w_mlp2, g_final, loss_target, m_c_ctx, m_w_ada, m_b_ada, m_g_norm1, m_g_norm2, m_w_in, m_conv_w, m_conv_b, m_sink, m_g_out_conv, m_g_out_attn, m_w_out, m_w_mlp1, m_w_mlp2, m_g_final, v_c_ctx, v_w_ada, v_b_ada, v_g_norm1, v_g_norm2, v_w_in, v_conv_w, v_conv_b, v_sink, v_g_out_conv, v_g_out_attn, v_w_out, v_w_mlp1, v_w_mlp2, v_g_final):
    mx, my, mc = lax.axis_index("x"), lax.axis_index("y"), lax.axis_index("c")
    chip, rank = 2 * mx + my, 4 * mx + 2 * my + mc
    core = jnp.reshape(mc, (1,)).astype(I32)
    depth, d = g_norm1.shape
    s_len, ctx_rows = x.shape[1], ctx.shape[1]
    cw_cols = conv_w.shape[2]
    c_conv = cw_cols * N_CHIP
    n_heads = sink.shape[1]
    ns_ada = w_ada.shape[2]

    got = _allgather8(_pack([c, conv_w], d), "gather_cond")
    flat = got.reshape(N_DEV, -1)
    conv_w_full = jnp.transpose(flat[::2, d:d + conv_w.size].reshape(N_CHIP, depth, 3, cw_cols), (1, 2, 0, 3))
    conv_w_full = conv_w_full.reshape(depth, 3, c_conv)
    cond = jnp.zeros((COND_ROWS, d), F32).at[:N_DEV].set(flat[:, :d]).at[N_DEV].set(c_ctx)

    b_cols = lax.dynamic_slice_in_dim(b_ada, chip * ns_ada, ns_ada, axis=1)[:, None, :]
    mod_cols = _ada_fwd(cond, w_ada, b_cols, "ada_fwd")
    got = _allgather8(mod_cols.reshape(depth * COND_ROWS, ns_ada), "gather_mod")
    mod_all = jnp.transpose(got[::2].reshape(N_CHIP, depth, COND_ROWS, ns_ada), (1, 2, 0, 3))
    mod_all = mod_all.reshape(depth, COND_ROWS, N_CHIP * ns_ada)
    mod_me = lax.dynamic_index_in_dim(mod_all, rank, axis=1, keepdims=False)
    mods = jnp.stack([mod_all[:, N_DEV], mod_me], axis=1).reshape(depth, 2, N_MOD, 1, d)

    comm = _Comm(core, (("w_in", w_in, m_w_in, v_w_in), ("w_out", w_out, m_w_out, v_w_out),
                        ("w_mlp1", w_mlp1, m_w_mlp1, v_w_mlp1), ("w_mlp2", w_mlp2, m_w_mlp2, v_w_mlp2)))
    small = dict(g_norm1=g_norm1[:, None], g_norm2=g_norm2[:, None], conv_w=conv_w_full, conv_b=conv_b[:, None], sink=sink,
                 g_out_conv=g_out_conv[:, None], g_out_attn=g_out_attn[:, None], g_final=g_final[None])
    x0 = jnp.concatenate([ctx[0], x[0]], axis=0)
    loss_part, dx0, _, dmods, sg, last = _local_step(x0, loss_target[0], mods, None, small, ctx_rows=ctx_rows, comm=comm)
    loss = lax.psum(loss_part, ("x", "y", "c"))
    grad_x = dx0[ctx_rows:][None]
    last.add(_run_side(last.pair_side(), "rs_pair_last"))

    names = ("g_norm1", "g_norm2", "conv_w", "conv_b", "sink", "g_out_conv", "g_out_attn", "g_final")
    shapes = [(depth, 2, N_MOD * d)] + [sg[k].shape for k in names]
    got = _allgather8(_pack([dmods] + [sg[k] for k in names], d), "gather_small")
    tot = _unpack(_sum8(got, "sum_small").reshape(-1), shapes)
    dmod_tot, small_tot = tot[0], dict(zip(names, tot[1:]))
    dmod_lat = _unpack(got.reshape(N_DEV, -1), shapes[:1])[0][:, :, 1]
    dm_rows = jnp.zeros((depth, COND_ROWS, N_MOD * d), F32)
    dm_rows = dm_rows.at[:, :N_DEV].set(jnp.transpose(dmod_lat, (1, 0, 2))).at[:, N_DEV].set(dmod_tot[:, 0])
    dm_cols = lax.dynamic_slice_in_dim(dm_rows, chip * ns_ada, ns_ada, axis=2)
    g_w_ada, dcond = _ada_bwd(cond, dm_cols, w_ada, "ada_bwd")
    got = _allgather8(dcond[N_DEV:N_DEV + 8], "gather_dcond")
    g_c_ctx = _cctx_grad(got, c_ctx[None], "c_ctx_grad")

    res = {}
    ada_bufs = [lax.empty((1,) + (depth * d, ns_ada), F32) for _ in range(4)]
    ada_res, got = _adamw_layer(
        0, g_w_ada.reshape(depth * d, ns_ada), w_ada.reshape(1, depth * d, ns_ada),
        m_w_ada.reshape(1, depth * d, ns_ada), v_w_ada.reshape(1, depth * d, ns_ada), ada_bufs, "adamw_w_ada",
        side=last.chips_side(last.keys))
    res["w_ada"] = [r.reshape(w_ada.shape) for r in ada_res]
    last.land(last.keys, got)
    last.sum()
    last.adam(_run_side(last.share_side(), "rs_share_last"))
    res.update(zip(("w_in", "w_out", "w_mlp1", "w_mlp2"), comm.stacked))
    res["c_ctx"] = [r[0] for r in _adamw_small(g_c_ctx, None, c_ctx[None], m_c_ctx[None], v_c_ctx[None], "adamw_c_ctx")]
    res["b_ada"] = _adamw_small(dmod_tot[:, 0], dmod_tot[:, 1], b_ada, m_b_ada, v_b_ada, "adamw_b_ada")
    cw_grad = lax.dynamic_slice_in_dim(small_tot["conv_w"], chip * cw_cols, cw_cols, axis=2)
    res["conv_w"] = [r.reshape(conv_w.shape) for r in _adamw_small(
        cw_grad.reshape(depth * 3, cw_cols), None, conv_w.reshape(depth * 3, cw_cols),
        m_conv_w.reshape(depth * 3, cw_cols), v_conv_w.reshape(depth * 3, cw_cols), "adamw_conv_w")]
    for nm, w, m, v in (("g_norm1", g_norm1, m_g_norm1, v_g_norm1), ("g_norm2", g_norm2, m_g_norm2, v_g_norm2),
                        ("conv_b", conv_b, m_conv_b, v_conv_b), ("sink", sink, m_sink, v_sink),
                        ("g_out_conv", g_out_conv, m_g_out_conv, v_g_out_conv),
                        ("g_out_attn", g_out_attn, m_g_out_attn, v_g_out_attn)):
        res[nm] = _adamw_small(small_tot[nm], None, w, m, v, f"adamw_{nm}")
    res["g_final"] = [r[0] for r in _adamw_small(small_tot["g_final"][None], None, g_final[None], m_g_final[None],
                                                 v_g_final[None], "adamw_g_final")]
    order = ("c_ctx", "w_ada", "b_ada", "g_norm1", "g_norm2", "w_in", "conv_w", "conv_b", "sink", "g_out_conv",
             "g_out_attn", "w_out", "w_mlp1", "w_mlp2", "g_final")
    return (loss, grad_x, *[res[n][0] for n in order], *[res[n][1] for n in order], *[res[n][2] for n in order],
            *[res[n][3] for n in order])
```

```python
import functools

import jax
import jax.numpy as jnp
from jax import lax
from jax.experimental import pallas as pl
from jax.experimental.pallas import tpu as pltpu

F32 = jnp.float32
I32 = jnp.int32
MXU_DTYPE = jnp.bfloat16
EPS = 1e-6
HEAD_DIM = 64
N_KV = 4
WINDOW = 128
QBLK = 128
LANES = 128
GRID_W = 64
ROPE_THETA = 10000.0
NEG_INF = -1e30
N_MOD = 6
HALO = 8
ADAM_LR, ADAM_B1, ADAM_B2, ADAM_EPS, ADAM_WD, ADAM_STEP = 0.001, 0.9, 0.999, 1e-08, 0.01, 10
V7X_VMEM_BYTES = 64 * 1024 * 1024
VMEM_LIMIT = V7X_VMEM_BYTES * 3 // 4
ROW_TILES = (1408, 768, 512, 640, 256, 128)
MESH = pl.DeviceIdType.MESH
ANY = pl.BlockSpec(memory_space=pl.ANY)


def _params():
    return pltpu.CompilerParams(vmem_limit_bytes=VMEM_LIMIT)


def _pick(n, cands):
    for c in cands:
        if n % c == 0:
            return c
    raise ValueError(f"no tile of {cands} divides {n}")


def _mx(v):
    return v.astype(MXU_DTYPE)


def _dot(a, b):
    return jnp.dot(_mx(a), _mx(b), preferred_element_type=F32)


def _dot_nt(a, b):
    return lax.dot_general(_mx(a), _mx(b), (((1,), (1,)), ((), ())), preferred_element_type=F32)


def _dot_tn(a, b):
    return lax.dot_general(_mx(a), _mx(b), (((0,), (0,)), ((), ())), preferred_element_type=F32)


def _silu(v):
    return v / (1.0 + jnp.exp(-v))


class _Side:
    def __init__(self, inputs, out_shapes, scratch, start, finish, aliases=(), middle=None):
        self.inputs, self.out_shapes, self.scratch = list(inputs), list(out_shapes), list(scratch)
        self.start, self.finish, self.aliases = start, finish, tuple(aliases)
        self.middle = middle


def _call(body, *, grid, in_specs, out_specs, out_shape, scratch, args, name, side=None, aliases=None):
    aliases = dict(aliases or {})
    if side is None:
        res = pl.pallas_call(body, grid=grid, in_specs=in_specs, out_specs=out_specs, out_shape=out_shape,
                             scratch_shapes=scratch, input_output_aliases=aliases, compiler_params=_params(),
                             name=name)(*args)
        return list(res), []
    ni, no, ns = len(in_specs), len(out_specs), len(scratch)
    si, so = len(side.inputs), len(side.out_shapes)

    def full(*refs):
        ins, sins = refs[:ni], refs[ni:ni + si]
        outs, souts = refs[ni + si:ni + si + no], refs[ni + si + no:ni + si + no + so]
        scr, sems = refs[ni + si + no + so:ni + si + no + so + ns], refs[ni + si + no + so + ns:]
        ids = [pl.program_id(k) for k in range(len(grid))]
        first, last = ids[0] == 0, ids[0] == grid[0] - 1
        for k in range(1, len(grid)):
            first, last = first & (ids[k] == 0), last & (ids[k] == grid[k] - 1)

        @pl.when(first)
        def _():
            side.start(sins, souts, sems)

        body(*ins, *outs, *scr)

        if side.middle is not None:
            lin, total = ids[0], grid[0]
            for k in range(1, len(grid)):
                lin, total = lin * grid[k] + ids[k], total * grid[k]

            @pl.when(lin == (3 * total) // 4)
            def _():
                side.middle(sins, souts, sems)

        @pl.when(last)
        def _():
            side.finish(sins, souts, sems)

    res = pl.pallas_call(
        full, grid=grid, in_specs=list(in_specs) + [ANY] * si, out_specs=list(out_specs) + [ANY] * so,
        out_shape=list(out_shape) + side.out_shapes, scratch_shapes=list(scratch) + side.scratch,
        input_output_aliases={**aliases, **{ni + a: no + b for a, b in side.aliases}}, compiler_params=_params(),
        name=name)(*args, *side.inputs)
    return list(res[:no]), list(res[no:])


def _run_side(side, name):
    si, so = len(side.inputs), len(side.out_shapes)

    def body(*refs):
        side.start(refs[:si], refs[si:si + so], refs[si + so:])
        if side.middle is not None:
            side.middle(refs[:si], refs[si:si + so], refs[si + so:])
        side.finish(refs[:si], refs[si:si + so], refs[si + so:])

    res = pl.pallas_call(body, in_specs=[ANY] * si, out_specs=[ANY] * so, out_shape=side.out_shapes,
                         scratch_shapes=side.scratch, input_output_aliases=dict(side.aliases), name=name)(*side.inputs)
    return list(res)


def _mm_nn(a, b3, *, out_dtypes, epilogue, name, side=None):
    m, k = a.shape
    s, _, ns = b3.shape
    tm = _pick(m, ROW_TILES)
    tn = _pick(ns, (1024, 1152, 640, 512, 256, 128))
    tk = _pick(k, (2048, 1024, 512))
    nbs, nk = ns // tn, k // tk
    n_out = len(out_dtypes)

    def body(a_ref, b_ref, *rest):
        outs = rest[:n_out]

        def write(v):
            for o, r in zip(outs, epilogue(v)):
                o[...] = r.astype(o.dtype)

        if nk == 1:
            write(jnp.dot(a_ref[...], b_ref[...], preferred_element_type=F32))
            return
        acc = rest[n_out]
        kk = pl.program_id(2)

        @pl.when(kk == 0)
        def _():
            acc[...] = jnp.zeros_like(acc)

        acc[...] += jnp.dot(a_ref[...], b_ref[...], preferred_element_type=F32)

        @pl.when(kk == nk - 1)
        def _():
            write(acc[...])

    return _call(
        body, grid=(m // tm, s * nbs, nk),
        in_specs=[pl.BlockSpec((tm, tk), lambda i, j, kk: (i, kk)),
                  pl.BlockSpec((None, tk, tn), lambda i, j, kk: (j // nbs, kk, j % nbs))],
        out_specs=[pl.BlockSpec((tm, tn), lambda i, j, kk: (i, j))] * n_out,
        out_shape=[jax.ShapeDtypeStruct((m, s * ns), dt) for dt in out_dtypes],
        scratch=[pltpu.VMEM((tm, tn), F32)] if nk > 1 else [], args=(a, b3), name=name, side=side)


def _mm_nt(a, b3, *, out_dtype, name, extra=None, epilogue=None, side=None):
    m = a.shape[0]
    s, ko, ns = b3.shape
    tm = _pick(m, ROW_TILES)
    tko = _pick(ko, (1024, 512))
    tn = _pick(ns, (2048, 1152, 1024, 640, 512, 256, 128))
    nbs = ns // tn
    nk = s * nbs
    n_in = 3 if extra is not None else 2

    def body(a_ref, b_ref, *rest):
        x_ref = rest[0] if extra is not None else None
        o_ref = rest[n_in - 2]

        def write(v):
            if epilogue is not None:
                v = epilogue(v, x_ref[...])
            o_ref[...] = v.astype(o_ref.dtype)

        if nk == 1:
            write(lax.dot_general(a_ref[...], b_ref[...], (((1,), (1,)), ((), ())), preferred_element_type=F32))
            return
        acc = rest[n_in - 1]
        kk = pl.program_id(2)

        @pl.when(kk == 0)
        def _():
            acc[...] = jnp.zeros_like(acc)

        acc[...] += lax.dot_general(a_ref[...], b_ref[...], (((1,), (1,)), ((), ())), preferred_element_type=F32)

        @pl.when(kk == nk - 1)
        def _():
            write(acc[...])

    in_specs = [pl.BlockSpec((tm, tn), lambda i, j, kk: (i, kk)),
                pl.BlockSpec((None, tko, tn), lambda i, j, kk: (kk // nbs, j, kk % nbs))]
    args = [a, b3]
    if extra is not None:
        in_specs.append(pl.BlockSpec((tm, tko), lambda i, j, kk: (i, j)))
        args.append(extra)
    return _call(
        body, grid=(m // tm, ko // tko, nk), in_specs=in_specs,
        out_specs=[pl.BlockSpec((tm, tko), lambda i, j, kk: (i, j))],
        out_shape=[jax.ShapeDtypeStruct((m, ko), out_dtype)],
        scratch=[pltpu.VMEM((tm, tko), F32)] if nk > 1 else [], args=args, name=name, side=side)


def _mm_tn(a, b, *, shards, name, side=None):
    t, k = a.shape
    ns = b.shape[1] // shards
    tt = _pick(t, (2 * ROW_TILES[0],) + ROW_TILES)
    tk = _pick(k, (1024, 512))
    tn = _pick(ns, (1024, 1152, 640, 512, 256, 128))
    nbs, nt = ns // tn, t // tt

    def body(a_ref, b_ref, o_ref, acc):
        tt_i = pl.program_id(2)

        @pl.when(tt_i == 0)
        def _():
            acc[...] = jnp.zeros_like(acc)

        acc[...] += lax.dot_general(a_ref[...], b_ref[...], (((0,), (0,)), ((), ())), preferred_element_type=F32)

        @pl.when(tt_i == nt - 1)
        def _():
            o_ref[...] = acc[...]

    return _call(
        body, grid=(k // tk, shards * nbs, nt),
        in_specs=[pl.BlockSpec((tt, tk), lambda i, j, q: (q, i)),
                  pl.BlockSpec((tt, tn), lambda i, j, q: (q, j))],
        out_specs=[pl.BlockSpec((None, tk, tn), lambda i, j, q: (j // nbs, i, j % nbs))],
        out_shape=[jax.ShapeDtypeStruct((shards, k, ns), F32)], scratch=[pltpu.VMEM((tk, tn), F32)], args=(a, b),
        name=name, side=side)


def _row_tile(t, ctx_rows):
    return 256 if t % 256 == 0 and ctx_rows % 256 == 0 else 128


def _mod_spec(d, ncb, idx):
    return pl.BlockSpec((None, None, 1, d), lambda i: (jnp.where(i >= ncb, 1, 0), idx, 0, 0))


def _norm_fwd(x, z, gate_mods, mods, g, *, gate_idx, shift_idx, scale_idx, ctx_rows, name):
    t, d = x.shape
    tr = _row_tile(t, ctx_rows)
    ncb = ctx_rows // tr
    row = pl.BlockSpec((tr, d), lambda i: (i, 0))
    vec = pl.BlockSpec((1, d), lambda i: (0, 0))
    resid = z is not None

    def body(*refs):
        if resid:
            x_ref, z_ref, gt_ref, g_ref, sh_ref, sc_ref, xo_ref, h_ref = refs
            xn = x_ref[...] + gt_ref[...] * z_ref[...].astype(F32)
            xo_ref[...] = xn
        else:
            x_ref, g_ref, sh_ref, sc_ref, h_ref = refs
            xn = x_ref[...]
        r = lax.rsqrt(jnp.mean(xn * xn, axis=-1, keepdims=True) + EPS)
        h_ref[...] = ((xn * r * g_ref[...]) * (1.0 + sc_ref[...]) + sh_ref[...]).astype(h_ref.dtype)

    mspecs = [vec, _mod_spec(d, ncb, shift_idx), _mod_spec(d, ncb, scale_idx)]
    if resid:
        in_specs = [row, row, _mod_spec(d, ncb, gate_idx)] + mspecs
        args = (x, z, gate_mods, g, mods, mods)
        out_specs = [row, row]
        out_shape = [jax.ShapeDtypeStruct((t, d), F32), jax.ShapeDtypeStruct((t, d), MXU_DTYPE)]
    else:
        in_specs = [row] + mspecs
        args = (x, g, mods, mods)
        out_specs = row
        out_shape = jax.ShapeDtypeStruct((t, d), MXU_DTYPE)
    return pl.pallas_call(body, grid=(t // tr,), in_specs=in_specs, out_specs=out_specs, out_shape=out_shape,
                          compiler_params=_params(), name=name)(*args)


R_DSHIFT, R_DSCALE, R_DG, R_DGATE = 0, 2, 4, 5


def _norm_bwd(dx, dh, xin, mods, g, *, scale_idx, ctx_rows, name, prev=None, side=None):
    t, d = dx.shape
    tr = _row_tile(t, ctx_rows)
    ncb = ctx_rows // tr
    row = pl.BlockSpec((tr, d), lambda i: (i, 0))
    vec = pl.BlockSpec((1, d), lambda i: (0, 0))
    acc_spec = pl.BlockSpec((8, d), lambda i: (0, 0))
    has_prev = prev is not None

    def body(*refs):
        if has_prev:
            dx_ref, dh_ref, x_ref, g_ref, sc_ref, z_ref, gt_ref, dxo_ref, dz_ref, acc = refs
        else:
            dx_ref, dh_ref, x_ref, g_ref, sc_ref, dxo_ref, acc = refs
        i = pl.program_id(0)

        @pl.when(i == 0)
        def _():
            acc[...] = jnp.zeros_like(acc)

        lat = jnp.where(i >= ncb, 1.0, 0.0)
        x = x_ref[...]
        r = lax.rsqrt(jnp.mean(x * x, axis=-1, keepdims=True) + EPS)
        xhat = x * r
        gv = g_ref[...]
        dhv = dh_ref[...].astype(F32)
        dn = dhv * (1.0 + sc_ref[...])
        dxhat = dn * gv
        dxin = dx_ref[...] + r * (dxhat - xhat * jnp.mean(dxhat * xhat, axis=-1, keepdims=True))
        dxo_ref[...] = dxin
        dshift = jnp.sum(dhv, axis=0, keepdims=True)
        dscale = jnp.sum(dhv * (xhat * gv), axis=0, keepdims=True)
        acc[R_DSHIFT:R_DSHIFT + 1, :] += dshift * (1.0 - lat)
        acc[R_DSHIFT + 1:R_DSHIFT + 2, :] += dshift * lat
        acc[R_DSCALE:R_DSCALE + 1, :] += dscale * (1.0 - lat)
        acc[R_DSCALE + 1:R_DSCALE + 2, :] += dscale * lat
        acc[R_DG:R_DG + 1, :] += jnp.sum(dn * xhat, axis=0, keepdims=True)
        if has_prev:
            dz_ref[...] = (dxin * gt_ref[...]).astype(dz_ref.dtype)
            dgate = jnp.sum(dxin * z_ref[...].astype(F32), axis=0, keepdims=True)
            acc[R_DGATE:R_DGATE + 1, :] += dgate * (1.0 - lat)
            acc[R_DGATE + 1:R_DGATE + 2, :] += dgate * lat

    in_specs = [row, row, row, vec, _mod_spec(d, ncb, scale_idx)]
    args = [dx, dh, xin, g, mods]
    out_specs = [row]
    out_shape = [jax.ShapeDtypeStruct((t, d), F32)]
    if has_prev:
        z, gate_mods, gate_idx = prev
        in_specs += [row, _mod_spec(d, ncb, gate_idx)]
        args += [z, gate_mods]
        out_specs.append(row)
        out_shape.append(jax.ShapeDtypeStruct((t, d), MXU_DTYPE))
    out_specs.append(acc_spec)
    out_shape.append(jax.ShapeDtypeStruct((8, d), F32))
    outs, got = _call(body, grid=(t // tr,), in_specs=in_specs, out_specs=out_specs, out_shape=out_shape, scratch=[],
                      args=args, name=name, side=side)
    return outs if side is None else (outs, got)


R_FGATE, R_FG, R_FLOSS = 0, 2, 3


def _final(x1, o, gate_mods, g_final, tgt, *, ctx_rows, name):
    t, d = x1.shape
    tr = _row_tile(t, ctx_rows)
    ncb, nb = ctx_rows // tr, t // tr
    row = pl.BlockSpec((tr, d), lambda i: (i, 0))
    vec = pl.BlockSpec((1, d), lambda i: (0, 0))

    def body(x_ref, o_ref, gt_ref, g_ref, t_ref, dx_ref, do_ref, acc, lsum):
        i = pl.program_id(0)

        @pl.when(i == 0)
        def _():
            acc[...] = jnp.zeros_like(acc)
            lsum[...] = jnp.zeros_like(lsum)

        lat = jnp.where(i >= ncb, 1.0, 0.0)
        gt = gt_ref[...]
        ov = o_ref[...].astype(F32)
        x = x_ref[...] + gt * ov
        r = lax.rsqrt(jnp.mean(x * x, axis=-1, keepdims=True) + EPS)
        xhat = x * r
        gv = g_ref[...]
        err = (xhat * gv - t_ref[...]) * lat
        dy = err / d
        dxhat = dy * gv
        dxv = r * (dxhat - xhat * jnp.mean(dxhat * xhat, axis=-1, keepdims=True))
        dx_ref[...] = dxv
        do_ref[...] = (dxv * gt).astype(do_ref.dtype)
        dgate = jnp.sum(dxv * ov, axis=0, keepdims=True)
        acc[R_FGATE:R_FGATE + 1, :] += dgate * (1.0 - lat)
        acc[R_FGATE + 1:R_FGATE + 2, :] += dgate * lat
        acc[R_FG:R_FG + 1, :] += jnp.sum(dy * xhat, axis=0, keepdims=True)
        lsum[...] += jnp.sum(err * err, axis=0, keepdims=True)

        @pl.when(i == nb - 1)
        def _():
            total = (0.5 / d) * jnp.sum(lsum[...], axis=-1, keepdims=True)
            acc[R_FLOSS:R_FLOSS + 1, :] = jnp.broadcast_to(total, (1, d))

    return pl.pallas_call(
        body, grid=(nb,),
        in_specs=[row, row, _mod_spec(d, ncb, 5), vec, pl.BlockSpec((tr, d), lambda i: (jnp.maximum(i - ncb, 0), 0))],
        out_specs=[row, row, pl.BlockSpec((8, d), lambda i: (0, 0))],
        out_shape=[jax.ShapeDtypeStruct((t, d), F32), jax.ShapeDtypeStruct((t, d), MXU_DTYPE),
                   jax.ShapeDtypeStruct((8, d), F32)],
        scratch_shapes=[pltpu.VMEM((1, d), F32)], compiler_params=_params(), name=name)(x1, o, gate_mods, g_final, tgt)


def _rope_tables(s, ctx_rows):
    rows = s // GRID_W
    row_pos = jnp.repeat(jnp.arange(rows, dtype=F32), GRID_W)
    col_pos = jnp.tile(jnp.arange(GRID_W, dtype=F32), rows)
    quarter = HEAD_DIM // 4
    inv = ROPE_THETA ** (-jnp.arange(0, 2 * quarter, 2, dtype=F32) / (2 * quarter))
    ang_r, ang_c = row_pos[:, None] * inv[None, :], col_pos[:, None] * inv[None, :]
    cr, sr, cc, sc = jnp.cos(ang_r), jnp.sin(ang_r), jnp.cos(ang_c), jnp.sin(ang_c)
    zero = jnp.zeros_like(sr)
    cos = jnp.concatenate([cr, cr, cc, cc], axis=1)
    sa = jnp.concatenate([zero, sr, zero, sc], axis=1)
    sb = jnp.concatenate([-sr, zero, -sc, zero], axis=1)

    def full(tab, fill):
        tab = jnp.tile(tab, (1, LANES // HEAD_DIM))
        return jnp.concatenate([jnp.full((ctx_rows, LANES), fill, F32), tab], axis=0)

    return full(cos, 1.0), full(sa, 0.0), full(sb, 0.0)


def _rope_apply(x, cos, sa, sb, transpose):
    n = x.shape[1] // LANES
    cos, sa, sb = (jnp.tile(v, (1, n)) for v in (cos, sa, sb))
    quarter = HEAD_DIM // 4
    width = x.shape[1]
    if transpose:
        return x * cos + pltpu.roll(x * sa, width - quarter, 1) + pltpu.roll(x * sb, quarter, 1)
    return x * cos + pltpu.roll(x, quarter, 1) * sa + pltpu.roll(x, width - quarter, 1) * sb


def _twice(x):
    lane = lax.broadcasted_iota(I32, (x.shape[0], LANES), 1)
    out = []
    for j in range(N_KV):
        blk = x[:, (j // 2) * LANES:(j // 2 + 1) * LANES]
        own = jnp.where((lane < HEAD_DIM) if j % 2 == 0 else (lane >= HEAD_DIM), blk, 0.0)
        out.append(own + pltpu.roll(own, HEAD_DIM, 1))
    return jnp.concatenate(out, axis=1)


def _rope_fwd(p, tabs, *, da, kw, kv_start, name):
    t = p.shape[0]
    tr = _pick(t, (256, 128))

    def body(q_ref, k_ref, v_ref, c_ref, a_ref, b_ref, qo_ref, ko_ref, vo_ref):
        cos, sa, sb = c_ref[...], a_ref[...], b_ref[...]
        qo_ref[...] = (_rope_apply(q_ref[...], cos, sa, sb, False) * (HEAD_DIM ** -0.5)).astype(qo_ref.dtype)
        ko_ref[...] = _twice(_rope_apply(k_ref[...], cos, sa, sb, False)).astype(ko_ref.dtype)
        vo_ref[...] = _twice(v_ref[...]).astype(vo_ref.dtype)

    tab = pl.BlockSpec((tr, LANES), lambda i: (i, 0))
    two = pl.BlockSpec((tr, N_KV * LANES), lambda i: (i, 0))
    return pl.pallas_call(
        body, grid=(t // tr,),
        in_specs=[pl.BlockSpec((tr, da), lambda i: (i, (kv_start - da) // da)),
                  pl.BlockSpec((tr, kw), lambda i: (i, kv_start // kw)),
                  pl.BlockSpec((tr, kw), lambda i: (i, kv_start // kw + 1)), tab, tab, tab],
        out_specs=[pl.BlockSpec((tr, da), lambda i: (i, 0)), two, two],
        out_shape=[jax.ShapeDtypeStruct((t, da), MXU_DTYPE), jax.ShapeDtypeStruct((t, N_KV * LANES), MXU_DTYPE),
                   jax.ShapeDtypeStruct((t, N_KV * LANES), MXU_DTYPE)],
        compiler_params=_params(), name=name)(p, p, p, *tabs)


def _rope_bwd(dq, dk, dv, tabs, dp, *, name):
    t, da = dq.shape
    kw = dk.shape[1]
    width = da + 2 * kw
    tr = _pick(t, (256, 128))
    col = (dp.shape[1] - width) // width
    assert col * width == dp.shape[1] - width

    def body(q_ref, k_ref, v_ref, c_ref, a_ref, b_ref, dp_ref, o_ref):
        cos, sa, sb = c_ref[...], a_ref[...], b_ref[...]
        o_ref[:, :da] = _rope_apply(q_ref[...], cos, sa, sb, True).astype(o_ref.dtype)
        o_ref[:, da:da + kw] = _rope_apply(k_ref[...], cos, sa, sb, True).astype(o_ref.dtype)
        o_ref[:, da + kw:] = v_ref[...].astype(o_ref.dtype)

    tab = pl.BlockSpec((tr, LANES), lambda i: (i, 0))
    return pl.pallas_call(
        body, grid=(t // tr,),
        in_specs=[pl.BlockSpec((tr, da), lambda i: (i, 0)), pl.BlockSpec((tr, kw), lambda i: (i, 0)),
                  pl.BlockSpec((tr, kw), lambda i: (i, 0)), tab, tab, tab, ANY],
        out_specs=pl.BlockSpec((tr, width), lambda i: (i, col)),
        out_shape=jax.ShapeDtypeStruct(dp.shape, dp.dtype), input_output_aliases={6: 0},
        compiler_params=_params(), name=name)(dq, dk, dv, *tabs, dp)


def _attn_specs(t, ctx_rows):
    nblk = t // QBLK

    def clip(i):
        return jnp.clip(i, 0, nblk - 1)

    return [pl.BlockSpec((QBLK, 2 * LANES), lambda b, i: (clip(i - 1), b)),
            pl.BlockSpec((QBLK, 2 * LANES), lambda b, i: (i, b)),
            pl.BlockSpec((QBLK, 2 * LANES), lambda b, i: (clip(i + 1), b)),
            pl.BlockSpec((ctx_rows, 2 * LANES), lambda b, i: (0, b))]


def _band_bias(rows):
    rr = lax.broadcasted_iota(I32, (rows, 3 * QBLK), 0) & (QBLK - 1)
    cc = lax.broadcasted_iota(I32, (rows, 3 * QBLK), 1)
    return jnp.where(jnp.abs(cc - QBLK - rr) <= WINDOW, 0.0, NEG_INF).astype(F32)


def _local_bias(band_ref, i, t, ctx_rows):
    cc = lax.broadcasted_iota(I32, (1, 3 * QBLK), 1)
    keyrow = (i - 1) * QBLK + cc
    first_key = jnp.where(i * QBLK >= ctx_rows, ctx_rows, t)
    return band_ref[...] + jnp.where((keyrow >= first_key) & (keyrow < t), 0.0, NEG_INF)


def _stack_heads(ref, c0, nblocks):
    lane = lax.broadcasted_iota(I32, (QBLK, LANES), 1)
    rows = []
    for qb in range(nblocks):
        blk = ref[:, c0 + qb * LANES:c0 + (qb + 1) * LANES].astype(F32)
        rows += [jnp.where(lane < HEAD_DIM, blk, 0.0), jnp.where(lane < HEAD_DIM, 0.0, blk)]
    return jnp.concatenate(rows, axis=0)


def _unstack_heads(x, nblocks):
    lane = lax.broadcasted_iota(I32, (QBLK, LANES), 1)
    return [jnp.where(lane < HEAD_DIM, x[2 * qb * QBLK:(2 * qb + 1) * QBLK], x[(2 * qb + 1) * QBLK:(2 * qb + 2) * QBLK])
            for qb in range(nblocks)]


def _sink_column(sink_ref, head0, group):
    row = lax.broadcasted_iota(I32, (group * QBLK, 1), 0)
    col = jnp.zeros((group * QBLK, 1), F32)
    for g in range(group):
        col = jnp.where((row >= g * QBLK) & (row < (g + 1) * QBLK), sink_ref[head0 + g], col)
    return col


def _exps(qs, k_loc, k_ctx, bias, snk):
    s_loc = _dot_nt(qs, k_loc) + bias
    s_ctx = _dot_nt(qs, k_ctx)
    m = jnp.maximum(jnp.maximum(jnp.max(s_loc, axis=-1, keepdims=True), jnp.max(s_ctx, axis=-1, keepdims=True)), snk)
    e_loc, e_ctx, e_snk = jnp.exp(s_loc - m), jnp.exp(s_ctx - m), jnp.exp(snk - m)
    inv = 1.0 / (jnp.sum(e_loc, axis=-1, keepdims=True) + jnp.sum(e_ctx, axis=-1, keepdims=True) + e_snk)
    return e_loc, e_ctx, e_snk, inv


def _attn_fwd(qs, k2, v2, sink, *, ctx_rows, name, side=None):
    t, da = qs.shape
    group = da // HEAD_DIM // N_KV
    nqb = group // 2
    kvspecs = _attn_specs(t, ctx_rows)

    def body(sink_ref, band_ref, q_ref, kp, kc, kn, kx, vp, vc, vn, vx, o_ref):
        b, i = pl.program_id(0), pl.program_id(1)
        bias = _local_bias(band_ref, i, t, ctx_rows)
        for half in range(2):
            lanes = slice(half * LANES, (half + 1) * LANES)
            qst = _mx(_stack_heads(q_ref, half * nqb * LANES, nqb))
            k_loc = jnp.concatenate([kp[:, lanes], kc[:, lanes], kn[:, lanes]], axis=0)
            v_loc = jnp.concatenate([vp[:, lanes], vc[:, lanes], vn[:, lanes]], axis=0)
            snk = _sink_column(sink_ref, (2 * b + half) * group, group)
            e_loc, e_ctx, _, inv = _exps(qst, k_loc, kx[:, lanes], bias, snk)
            out = (_dot(e_loc, v_loc) + _dot(e_ctx, vx[:, lanes])) * inv
            for qb, blk in enumerate(_unstack_heads(out, nqb)):
                c0 = (half * nqb + qb) * LANES
                o_ref[:, c0:c0 + LANES] = blk

    qspec = pl.BlockSpec((QBLK, 2 * group * HEAD_DIM), lambda b, i: (i, b))
    band = pl.BlockSpec((group * QBLK, 3 * QBLK), lambda b, i: (0, 0))
    (ao,), got = _call(
        body, grid=(N_KV // 2, t // QBLK),
        in_specs=[pl.BlockSpec(memory_space=pltpu.SMEM), band, qspec] + kvspecs + kvspecs,
        out_specs=[qspec], out_shape=[jax.ShapeDtypeStruct((t, da), F32)], scratch=[],
        args=(sink, _band_bias(group * QBLK), qs, k2, k2, k2, k2, v2, v2, v2, v2), name=name, side=side)
    return ao, got


def _attn_bwd(qs, k2, v2, dao, sink, *, ctx_rows, name, side=None):
    t, da = qs.shape
    nblk = t // QBLK
    group = da // HEAD_DIM // N_KV
    nqb = group // 2
    kw = N_KV * HEAD_DIM
    nloc = 3 * QBLK
    kvspecs = _attn_specs(t, ctx_rows)

    def body(sink_ref, band_ref, q_ref, do_ref, kp, kc, kn, kx, vp, vc, vn, vx,
             dq_ref, dkl_ref, dvl_ref, dkc_ref, dvc_ref, ds_ref):
        b, i = pl.program_id(0), pl.program_id(1)

        @pl.when((b == 0) & (i == 0))
        def _():
            ds_ref[...] = jnp.zeros_like(ds_ref)

        @pl.when(i == 0)
        def _():
            dkc_ref[...] = jnp.zeros_like(dkc_ref)
            dvc_ref[...] = jnp.zeros_like(dvc_ref)

        bias = _local_bias(band_ref, i, t, ctx_rows)
        srow = lax.broadcasted_iota(I32, ds_ref.shape, 0)
        slane = lax.broadcasted_iota(I32, ds_ref.shape, 1)

        def both(x):
            return x + pltpu.roll(x, HEAD_DIM, 1)

        folded = []
        for half in range(2):
            lanes = slice(half * LANES, (half + 1) * LANES)
            head0 = (2 * b + half) * group
            qst = _mx(_stack_heads(q_ref, half * nqb * LANES, nqb))
            dost = _mx(_stack_heads(do_ref, half * nqb * LANES, nqb))
            k_loc = jnp.concatenate([kp[:, lanes], kc[:, lanes], kn[:, lanes]], axis=0)
            v_loc = jnp.concatenate([vp[:, lanes], vc[:, lanes], vn[:, lanes]], axis=0)
            k_ctx, v_ctx = kx[:, lanes], vx[:, lanes]
            e_loc, e_ctx, e_snk, inv = _exps(qst, k_loc, k_ctx, bias, _sink_column(sink_ref, head0, group))
            p_loc, p_ctx = e_loc * inv, e_ctx * inv
            dp_loc, dp_ctx = _dot_nt(dost, v_loc), _dot_nt(dost, v_ctx)
            dsum = jnp.sum(p_loc * dp_loc, axis=-1, keepdims=True) + jnp.sum(p_ctx * dp_ctx, axis=-1, keepdims=True)
            ds_loc, ds_ctx = _mx(p_loc * (dp_loc - dsum)), _mx(p_ctx * (dp_ctx - dsum))
            dq = (_dot(ds_loc, k_loc) + _dot(ds_ctx, k_ctx)) * (HEAD_DIM ** -0.5)
            for qb, blk in enumerate(_unstack_heads(dq, nqb)):
                c0 = (half * nqb + qb) * LANES
                dq_ref[:, c0:c0 + LANES] = blk
            p_loc, p_ctx = _mx(p_loc), _mx(p_ctx)
            folded.append((both(_dot_tn(ds_loc, qst)), both(_dot_tn(p_loc, dost)),
                           both(_dot_tn(ds_ctx, qst)), both(_dot_tn(p_ctx, dost))))
            dsnk = e_snk * inv * dsum
            for g in range(group):
                part = -jnp.sum(dsnk[g * QBLK:(g + 1) * QBLK])
                ds_ref[...] += jnp.where((srow == 0) & (slane == head0 + g), part, 0.0)
        lane_l = lax.broadcasted_iota(I32, (nloc, LANES), 1)
        lane_c = lax.broadcasted_iota(I32, (ctx_rows, LANES), 1)
        dkl_ref[...] = jnp.where(lane_l < HEAD_DIM, folded[0][0], folded[1][0])
        dvl_ref[...] = jnp.where(lane_l < HEAD_DIM, folded[0][1], folded[1][1])
        dkc_ref[...] += jnp.where(lane_c < HEAD_DIM, folded[0][2], folded[1][2])
        dvc_ref[...] += jnp.where(lane_c < HEAD_DIM, folded[0][3], folded[1][3])

    qspec = pl.BlockSpec((QBLK, 2 * group * HEAD_DIM), lambda b, i: (i, b))
    loc = pl.BlockSpec((None, nloc, LANES), lambda b, i: (i, 0, b))
    cspec = pl.BlockSpec((ctx_rows, LANES), lambda b, i: (0, b))
    band = pl.BlockSpec((group * QBLK, nloc), lambda b, i: (0, 0))
    return _call(
        body, grid=(N_KV // 2, nblk),
        in_specs=[pl.BlockSpec(memory_space=pltpu.SMEM), band, qspec, qspec] + kvspecs + kvspecs,
        out_specs=[qspec, loc, loc, cspec, cspec, pl.BlockSpec((8, LANES), lambda b, i: (0, 0))],
        out_shape=[jax.ShapeDtypeStruct((t, da), F32), jax.ShapeDtypeStruct((nblk, nloc, kw), F32),
                   jax.ShapeDtypeStruct((nblk, nloc, kw), F32), jax.ShapeDtypeStruct((ctx_rows, kw), F32),
                   jax.ShapeDtypeStruct((ctx_rows, kw), F32), jax.ShapeDtypeStruct((8, LANES), F32)],
        scratch=[], args=(sink, _band_bias(group * QBLK), qs, dao, k2, k2, k2, k2, v2, v2, v2, v2), name=name, side=side)


def _kv_reduce(dkl, dvl, dkc, dvc, *, ctx_rows, name):
    nblk, _, kw = dkl.shape
    t = nblk * QBLK
    ncb = ctx_rows // QBLK

    def clip(i):
        return jnp.clip(i, 0, nblk - 1)

    def body(ka, kb, kc, kx, va, vb, vc, vx, dk_ref, dv_ref):
        m = pl.program_id(0)
        lat = m >= ncb
        wa = jnp.where(lat & (m + 1 <= nblk - 1), 1.0, 0.0)
        wc = jnp.where(lat & (m - 1 >= ncb), 1.0, 0.0)
        wl = jnp.where(lat, 1.0, 0.0)
        dk_ref[...] = wl * (kb[...] + wa * ka[...] + wc * kc[...]) + (1.0 - wl) * kx[...]
        dv_ref[...] = wl * (vb[...] + wa * va[...] + wc * vc[...]) + (1.0 - wl) * vx[...]

    slots = [pl.BlockSpec((None, QBLK, kw), lambda m: (clip(m + 1), 0, 0)),
             pl.BlockSpec((None, QBLK, kw), lambda m: (m, 1, 0)),
             pl.BlockSpec((None, QBLK, kw), lambda m: (clip(m - 1), 2, 0))]
    cspec = pl.BlockSpec((QBLK, kw), lambda m: (jnp.clip(m, 0, ncb - 1), 0))
    out = pl.BlockSpec((QBLK, kw), lambda m: (m, 0))
    return pl.pallas_call(
        body, grid=(nblk,), in_specs=slots + [cspec] + slots + [cspec], out_specs=[out, out],
        out_shape=[jax.ShapeDtypeStruct((t, kw), F32)] * 2, compiler_params=_params(), name=name)(
            dkl, dkl, dkl, dkc, dvl, dvl, dvl, dvc)


MERGE_ROWS = 128


def _halo_specs(t, c, col):
    hb = MERGE_ROWS // HALO
    return [pl.BlockSpec((HALO, c), lambda i: (jnp.maximum(i * hb - 1, 0), col)),
            pl.BlockSpec((MERGE_ROWS, c), lambda i: (i, col)),
            pl.BlockSpec((HALO, c), lambda i: (jnp.minimum((i + 1) * hb, t // HALO - 1), col))]


def _ext(refs):
    return jnp.concatenate([r[...] for r in refs], axis=0)


def _conv_ext(cg, hh, w_ref, b_ref, i, t, ctx_rows):
    n = cg.shape[0]
    u = cg * hh
    row = i * MERGE_ROWS - HALO + lax.broadcasted_iota(I32, u.shape, 0)
    first = (row == 0) | (row == ctx_rows)
    last = (row == ctx_rows - 1) | (row == t - 1)
    u_dn = jnp.where(first, 0.0, pltpu.roll(u, 1, 0))
    u_up = jnp.where(last, 0.0, pltpu.roll(u, n - 1, 0))
    cv = w_ref[0:1, :] * u_dn + w_ref[1:2, :] * u + w_ref[2:3, :] * u_up + b_ref[...]
    return u, u_dn, u_up, cv, first, last


def _merge_fwd(p, ao, conv_w, conv_b, g_oc, g_oa, *, ctx_rows, name):
    t = p.shape[0]
    c = ao.shape[1]
    main = slice(HALO, HALO + MERGE_ROWS)

    def body(bg_ref, cgp, cgm, cgn, hhp, hhm, hhn, ao_ref, w_ref, b_ref, gc_ref, ga_ref, o_ref):
        i = pl.program_id(0)
        _, _, _, cv, _, _ = _conv_ext(_ext((cgp, cgm, cgn)), _ext((hhp, hhm, hhn)), w_ref, b_ref, i, t, ctx_rows)
        co = bg_ref[...] * cv[main]
        rc = lax.rsqrt(jnp.mean(co * co, axis=-1, keepdims=True) + EPS)
        o_ref[:, :c] = (co * rc * gc_ref[...]).astype(o_ref.dtype)
        av = ao_ref[...]
        ra = lax.rsqrt(jnp.mean(av * av, axis=-1, keepdims=True) + EPS)
        o_ref[:, c:] = (av * ra * ga_ref[...]).astype(o_ref.dtype)

    vec = pl.BlockSpec((1, c), lambda i: (0, 0))
    return pl.pallas_call(
        body, grid=(t // MERGE_ROWS,),
        in_specs=[pl.BlockSpec((MERGE_ROWS, c), lambda i: (i, 0))] + _halo_specs(t, c, 1) + _halo_specs(t, c, 2)
        + [pl.BlockSpec((MERGE_ROWS, c), lambda i: (i, 0)), pl.BlockSpec((3, c), lambda i: (0, 0)), vec, vec, vec],
        out_specs=pl.BlockSpec((MERGE_ROWS, 2 * c), lambda i: (i, 0)),
        out_shape=jax.ShapeDtypeStruct((t, 2 * c), MXU_DTYPE), compiler_params=_params(), name=name)(
            p, p, p, p, p, p, p, ao, conv_w, conv_b, g_oc, g_oa)


R_DGOC, R_DGOA, R_DCB, R_DCW = 0, 1, 2, 3


def _merge_bwd(dmg, p, ao, conv_w, conv_b, g_oc, g_oa, *, width, ctx_rows, name):
    t = p.shape[0]
    c = ao.shape[1]
    main = slice(HALO, HALO + MERGE_ROWS)

    def body(dyp, dym, dyn, dya_ref, bgp, bgm, bgn, cgp, cgm, cgn, hhp, hhm, hhn, ao_ref, w_ref, b_ref, gc_ref, ga_ref,
             dp_ref, dao_ref, acc):
        i = pl.program_id(0)

        @pl.when(i == 0)
        def _():
            acc[...] = jnp.zeros_like(acc)

        bg, cg, hh = _ext((bgp, bgm, bgn)), _ext((cgp, cgm, cgn)), _ext((hhp, hhm, hhn))
        n = bg.shape[0]
        u, u_dn, u_up, cv, first, last = _conv_ext(cg, hh, w_ref, b_ref, i, t, ctx_rows)
        co = bg * cv
        rc = lax.rsqrt(jnp.mean(co * co, axis=-1, keepdims=True) + EPS)
        cohat = co * rc
        dyc = _ext((dyp, dym, dyn))
        t1 = dyc * gc_ref[...]
        dco = rc * (t1 - cohat * jnp.mean(t1 * cohat, axis=-1, keepdims=True))
        dcv = dco * bg
        dcv_next = jnp.where(last, 0.0, pltpu.roll(dcv, n - 1, 0))
        dcv_prev = jnp.where(first, 0.0, pltpu.roll(dcv, 1, 0))
        du = w_ref[1:2, :] * dcv + w_ref[0:1, :] * dcv_next + w_ref[2:3, :] * dcv_prev
        dp_ref[:, :c] = (dco * cv)[main].astype(dp_ref.dtype)
        dp_ref[:, c:2 * c] = (du * hh)[main].astype(dp_ref.dtype)
        dp_ref[:, 2 * c:] = (du * cg)[main].astype(dp_ref.dtype)
        dcv_m = dcv[main]
        acc[R_DGOC:R_DGOC + 1, :] += jnp.sum((dyc * cohat)[main], axis=0, keepdims=True)
        acc[R_DCB:R_DCB + 1, :] += jnp.sum(dcv_m, axis=0, keepdims=True)
        acc[R_DCW:R_DCW + 1, :] += jnp.sum(dcv_m * u_dn[main], axis=0, keepdims=True)
        acc[R_DCW + 1:R_DCW + 2, :] += jnp.sum(dcv_m * u[main], axis=0, keepdims=True)
        acc[R_DCW + 2:R_DCW + 3, :] += jnp.sum(dcv_m * u_up[main], axis=0, keepdims=True)
        av = ao_ref[...]
        ra = lax.rsqrt(jnp.mean(av * av, axis=-1, keepdims=True) + EPS)
        ahat = av * ra
        dya = dya_ref[...]
        t2 = dya * ga_ref[...]
        dao_ref[...] = ra * (t2 - ahat * jnp.mean(t2 * ahat, axis=-1, keepdims=True))
        acc[R_DGOA:R_DGOA + 1, :] += jnp.sum(dya * ahat, axis=0, keepdims=True)

    vec = pl.BlockSpec((1, c), lambda i: (0, 0))
    tile = pl.BlockSpec((MERGE_ROWS, c), lambda i: (i, 0))
    return pl.pallas_call(
        body, grid=(t // MERGE_ROWS,),
        in_specs=_halo_specs(t, c, 0) + [pl.BlockSpec((MERGE_ROWS, c), lambda i: (i, 1))]
        + _halo_specs(t, c, 0) + _halo_specs(t, c, 1) + _halo_specs(t, c, 2)
        + [tile, pl.BlockSpec((3, c), lambda i: (0, 0)), vec, vec, vec],
        out_specs=[pl.BlockSpec((MERGE_ROWS, 3 * c), lambda i: (i, 0)), tile, pl.BlockSpec((8, c), lambda i: (0, 0))],
        out_shape=[jax.ShapeDtypeStruct((t, width), MXU_DTYPE), jax.ShapeDtypeStruct((t, c), F32),
                   jax.ShapeDtypeStruct((8, c), F32)],
        compiler_params=_params(), name=name)(dmg, dmg, dmg, dmg, p, p, p, p, p, p, p, p, p, ao, conv_w, conv_b, g_oc, g_oa)


def _local_step(x0, tgt, mods, wts, small, *, ctx_rows, comm=None):
    t, d = x0.shape
    depth = mods.shape[0]
    c = d // 2
    da = d - c
    kw = N_KV * HEAD_DIM
    kv_start = 3 * c + da
    shards = N_CHIP
    tabs = _rope_tables(t - ctx_rows, ctx_rows)
    kwargs = dict(ctx_rows=ctx_rows)

    saved = []
    xs, z_prev = x0, None
    have = {}

    def riding(l, host):
        want = {"in": [(l, 1), (l + 1, 0)], "attn": [(l, 2)], "out": [], "mlp1": [(l, 3)], "mlp2": []}
        keys = [key for key in want[host] if key[0] < depth] if comm is not None else []
        return keys, (_gather_side([comm.halves[a][k] for a, k in keys]) if keys else None)

    def weight(l, k):
        return wts[l][k] if comm is None else comm.weight(k, have[(l, k)])

    if comm is not None:
        have[(0, 0)] = _run_side(_gather_side([comm.halves[0][0]]), "gather_weights_0")[0]
    for l in range(depth):
        if z_prev is None:
            x_in = xs
            h1 = _norm_fwd(xs, None, None, mods[l], small["g_norm1"][l], gate_idx=None, shift_idx=0, scale_idx=1,
                           name=f"norm1_fwd_{l}", **kwargs)
        else:
            x_in, h1 = _norm_fwd(xs, z_prev, mods[l - 1], mods[l], small["g_norm1"][l], gate_idx=5, shift_idx=0,
                                 scale_idx=1, name=f"norm1_fwd_{l}", **kwargs)
        keys, side = riding(l, "in")
        (p,), got = _mm_nn(h1, weight(l, 0), out_dtypes=(F32,), epilogue=lambda v: (v,), name=f"in_proj_{l}", side=side)
        have.update(zip(keys, got))
        qs, k2, v2 = _rope_fwd(p, tabs, da=da, kw=kw, kv_start=kv_start, name=f"rope_fwd_{l}")
        keys, side = riding(l, "attn")
        ao, got = _attn_fwd(qs, k2, v2, small["sink"][l], name=f"attn_fwd_{l}", side=side, **kwargs)
        have.update(zip(keys, got))
        mg = _merge_fwd(p, ao, small["conv_w"][l], small["conv_b"][l], small["g_out_conv"][l], small["g_out_attn"][l],
                        name=f"merge_fwd_{l}", **kwargs)
        keys, side = riding(l, "out")
        (z,), got = _mm_nn(mg, weight(l, 1), out_dtypes=(MXU_DTYPE,), epilogue=lambda v: (v,), name=f"out_proj_{l}", side=side)
        have.update(zip(keys, got))
        x_mid, h2 = _norm_fwd(x_in, z, mods[l], mods[l], small["g_norm2"][l], gate_idx=2, shift_idx=3, scale_idx=4,
                              name=f"norm2_fwd_{l}", **kwargs)
        keys, side = riding(l, "mlp1")
        (a_act, s_act), got = _mm_nn(h2, weight(l, 2), out_dtypes=(MXU_DTYPE, MXU_DTYPE),
                                     epilogue=lambda v: (v, jnp.square(jnp.maximum(v, 0.0))), name=f"mlp1_{l}", side=side)
        have.update(zip(keys, got))
        keys, side = riding(l, "mlp2")
        (o,), got = _mm_nn(s_act, weight(l, 3), out_dtypes=(MXU_DTYPE,), epilogue=lambda v: (v,), name=f"mlp2_{l}", side=side)
        have.update(zip(keys, got))
        saved.append(dict(x_in=x_in, h1=h1, p=p, qs=qs, k2=k2, v2=v2, ao=ao, mg=mg, z=z, x_mid=x_mid, h2=h2, a=a_act, s=s_act, o=o))
        xs, z_prev = x_mid, o

    dx, do, fin = _final(xs, z_prev, mods[depth - 1], small["g_final"], tgt, name="final", **kwargs)

    grads = [None] * depth
    dmods = [[None] * N_MOD for _ in range(depth)]
    sg = {k: [None] * depth for k in ("g_norm1", "g_norm2", "conv_w", "conv_b", "sink", "g_out_conv", "g_out_attn")}
    dmods[depth - 1][5] = fin[R_FGATE:R_FGATE + 2]
    sync = None
    for l in reversed(range(depth)):
        w_in, w_out, w1, w2 = (weight(l, k) for k in range(4))
        sv = saved[l]
        (da_act,), got = _mm_nt(do, w2, out_dtype=MXU_DTYPE, extra=sv["a"],
                                epilogue=lambda v, a: v * (2.0 * jnp.maximum(a.astype(F32), 0.0)), name=f"mlp2_dx_{l}",
                                side=sync and sync.pair_side())
        sync and sync.add(got)
        (g_w2,), got = _mm_tn(sv["s"], do, shards=1, name=f"mlp2_dw_{l}", side=sync and sync.chips_side((2,)))
        sync and sync.land((2,), got)
        (dh2,), got = _mm_nt(da_act, w1, out_dtype=MXU_DTYPE, name=f"mlp1_dx_{l}", side=sync and sync.chips_side((3,)))
        sync and sync.land((3,), got)
        (g_w1,), got = _mm_tn(sv["h2"], da_act, shards=shards, name=f"mlp1_dw_{l}", side=sync and sync.chips_side((0, 1)))
        sync and sync.land((0, 1), got)
        sync and sync.sum()
        own = _GradSync(comm, l, {2: g_w1, 3: g_w2}) if comm is not None and l == 0 else None
        dx, dz, sums2 = _norm_bwd(dx, dh2, sv["x_mid"], mods[l], small["g_norm2"][l], scale_idx=4,
                                  prev=(sv["z"], mods[l], 2), name=f"norm2_bwd_{l}", **kwargs)
        (dmg,), got = _mm_nt(dz, w_out, out_dtype=F32, name=f"out_proj_dx_{l}", side=own and own.pair_side())
        own and own.add(got)
        (g_wo,), _ = _mm_tn(sv["mg"], dz, shards=1, name=f"out_proj_dw_{l}")
        qkv_w = da + 2 * kw
        in_place = (3 * c) % qkv_w == 0
        dpc, dao, msum = _merge_bwd(dmg, sv["p"], sv["ao"], small["conv_w"][l], small["conv_b"][l],
                                    small["g_out_conv"][l], small["g_out_attn"][l],
                                    width=3 * c + qkv_w if in_place else 3 * c, name=f"merge_bwd_{l}", **kwargs)
        (dq, dkl, dvl, dkc, dvc, dsink), got = _attn_bwd(sv["qs"], sv["k2"], sv["v2"], dao, small["sink"][l],
                                                         name=f"attn_bwd_{l}", side=own and own.chips_side(own.keys), **kwargs)
        own and own.land(own.keys, got)
        own and own.sum()
        dk, dv = _kv_reduce(dkl, dvl, dkc, dvc, ctx_rows=ctx_rows, name=f"kv_reduce_{l}")
        if in_place:
            dp = _rope_bwd(dq, dk, dv, tabs, dpc, name=f"rope_bwd_{l}")
        else:
            dqkv = _rope_bwd(dq, dk, dv, tabs, lax.empty((t, qkv_w), MXU_DTYPE), name=f"rope_bwd_{l}")
            dp = jnp.concatenate([dpc, dqkv], axis=1)
        (dh1,), got = _mm_nt(dp, w_in, out_dtype=MXU_DTYPE, name=f"in_proj_dx_{l}", side=sync and sync.share_side())
        sync and sync.adam(got)
        (g_wi,), got = _mm_tn(sv["h1"], dp, shards=shards, name=f"in_proj_dw_{l}", side=own and own.share_side())
        own and own.adam(got)
        if comm is not None:
            sync = _GradSync(comm, l, {0: g_wi, 1: g_wo} if own else {0: g_wi, 1: g_wo, 2: g_w1, 3: g_w2})
        if l > 0:
            dx, do, sums1 = _norm_bwd(dx, dh1, sv["x_in"], mods[l], small["g_norm1"][l], scale_idx=1,
                                      prev=(saved[l - 1]["o"], mods[l - 1], 5), name=f"norm1_bwd_{l}", **kwargs)
            dmods[l - 1][5] = sums1[R_DGATE:R_DGATE + 2]
        elif sync is not None:
            sync.add(_run_side(sync.pair_side(), "rs_pair_last"))
            (dx, sums1), got = _norm_bwd(dx, dh1, sv["x_in"], mods[l], small["g_norm1"][l], scale_idx=1,
                                         name=f"norm1_bwd_{l}", side=sync.chips_side((0,)), **kwargs)
            sync.land((0,), got)
        else:
            dx, sums1 = _norm_bwd(dx, dh1, sv["x_in"], mods[l], small["g_norm1"][l], scale_idx=1,
                                  name=f"norm1_bwd_{l}", **kwargs)
        grads[l] = (g_wi, g_wo, g_w1, g_w2)
        dmods[l][0] = sums1[R_DSHIFT:R_DSHIFT + 2]
        dmods[l][1] = sums1[R_DSCALE:R_DSCALE + 2]
        dmods[l][2] = sums2[R_DGATE:R_DGATE + 2]
        dmods[l][3] = sums2[R_DSHIFT:R_DSHIFT + 2]
        dmods[l][4] = sums2[R_DSCALE:R_DSCALE + 2]
        sg["g_norm1"][l] = sums1[R_DG]
        sg["g_norm2"][l] = sums2[R_DG]
        sg["g_out_conv"][l] = msum[R_DGOC]
        sg["g_out_attn"][l] = msum[R_DGOA]
        sg["conv_b"][l] = msum[R_DCB]
        sg["conv_w"][l] = msum[R_DCW:R_DCW + 3]
        sg["sink"][l] = dsink[0, :da // HEAD_DIM]
    dmods = jnp.stack([jnp.stack(row, axis=1) for row in dmods])
    sg = {k: jnp.stack(v) for k, v in sg.items()}
    sg["g_final"] = fin[R_FG]
    return fin[R_FLOSS, 0], dx, grads, dmods, sg, sync


N_DEV = 8
N_CHIP = 4


def _place():
    mx, my, mc = lax.axis_index("x"), lax.axis_index("y"), lax.axis_index("c")
    others = [(1 - mx, my), (mx, 1 - my), (1 - mx, 1 - my)]
    return mx, my, mc, others


def _remote(src, dst, send_sems, recv_sems, k, dev):
    return pltpu.make_async_remote_copy(src_ref=src, dst_ref=dst, send_sem=send_sems.at[k], recv_sem=recv_sems.at[k],
                                        device_id=dev, device_id_type=MESH)


def _allgather8(x, name):
    r, ccols = x.shape

    def body(x_ref, out_ref, send_sems, recv_sems, local_sem):
        mx, my, mc, _ = _place()
        me = 4 * mx + 2 * my + mc
        mine = pltpu.make_async_copy(x_ref, out_ref.at[me], local_sem)
        mine.start()
        sent = []
        for k in range(1, N_DEV):
            fx, fy, fc = (k >> 2) & 1, (k >> 1) & 1, k & 1
            px, py, pc = (1 - mx if fx else mx), (1 - my if fy else my), (1 - mc if fc else mc)
            cp = _remote(x_ref, out_ref.at[me], send_sems, recv_sems, k - 1, (px, py, pc))
            cp.start()
            sent.append((cp, 4 * px + 2 * py + pc, (px, py, pc)))
        for k, (cp, peer, dev) in enumerate(sent):
            _remote(x_ref, out_ref.at[peer], send_sems, recv_sems, k, dev).wait_recv()
        for cp, _, _ in sent:
            cp.wait_send()
        mine.wait()

    vm = pl.BlockSpec(memory_space=pltpu.VMEM)
    return pl.pallas_call(
        body, in_specs=[vm], out_specs=vm, out_shape=jax.ShapeDtypeStruct((N_DEV, r, ccols), x.dtype),
        scratch_shapes=[pltpu.SemaphoreType.DMA((N_DEV - 1,)), pltpu.SemaphoreType.DMA((N_DEV - 1,)),
                        pltpu.SemaphoreType.DMA], name=name)(x)


def _gather_side(halves):
    n = len(halves)

    def copies(ins, outs, sems):
        send_sems, recv_sems, local_sems = sems
        mx, my, mc, others = _place()
        chip = 2 * mx + my
        sib = (mx, my, 1 - mc)

        def src(w):
            return ins[w].at[pl.ds(mc, 1)]

        def slot(w, ch, core):
            return outs[w].at[ch, pl.ds(core, 1)]

        locs = [pltpu.make_async_copy(src(w), slot(w, chip, mc), local_sems.at[w]) for w in range(n)]
        first, landed, passed, from_sib = [], [], [], []
        for w in range(n):
            first.append(_remote(src(w), slot(w, chip, mc), send_sems, recv_sems, 7 * w, sib))
            from_sib.append(_remote(src(w), slot(w, chip, 1 - mc), send_sems, recv_sems, 7 * w, sib))
            for j, (ox, oy) in enumerate(others):
                och = 2 * ox + oy
                first.append(_remote(src(w), slot(w, chip, mc), send_sems, recv_sems, 7 * w + 1 + j, (ox, oy, mc)))
                landed.append(_remote(src(w), slot(w, och, mc), send_sems, recv_sems, 7 * w + 1 + j, (ox, oy, mc)))
                passed.append(_remote(slot(w, och, mc), slot(w, och, mc), send_sems, recv_sems, 7 * w + 4 + j, sib))
                from_sib.append(_remote(src(w), slot(w, och, 1 - mc), send_sems, recv_sems, 7 * w + 4 + j, sib))
        return locs, first, landed, passed, from_sib

    def start(ins, outs, sems):
        locs, first, _, _, _ = copies(ins, outs, sems)
        for cp in locs + first:
            cp.start()

    def middle(ins, outs, sems):
        _, _, landed, passed, _ = copies(ins, outs, sems)
        for cp, fw in zip(landed, passed):
            cp.wait_recv()
            fw.start()

    def finish(ins, outs, sems):
        locs, first, _, passed, from_sib = copies(ins, outs, sems)
        for cp in from_sib:
            cp.wait_recv()
        for cp in first + passed:
            cp.wait_send()
        for cp in locs:
            cp.wait()

    return _Side(halves, [jax.ShapeDtypeStruct((N_CHIP,) + h.shape, h.dtype) for h in halves],
                 [pltpu.SemaphoreType.DMA((7 * n,)), pltpu.SemaphoreType.DMA((7 * n,)), pltpu.SemaphoreType.DMA((n,))],
                 start, finish, middle=middle)


def _pair_side(gs):
    n = len(gs)

    def copies(ins, outs, sems):
        mx, my, mc, _ = _place()
        return [_remote(ins[w].at[:, pl.ds(1 - mc, 1)], outs[w], sems[0], sems[1], w, (mx, my, 1 - mc)) for w in range(n)]

    def start(ins, outs, sems):
        for cp in copies(ins, outs, sems):
            cp.start()

    def finish(ins, outs, sems):
        for cp in copies(ins, outs, sems):
            cp.wait()

    return _Side(gs, [jax.ShapeDtypeStruct((g.shape[0], 1) + g.shape[2:], g.dtype) for g in gs],
                 [pltpu.SemaphoreType.DMA((n,)), pltpu.SemaphoreType.DMA((n,))], start, finish)


def _pair_add(g, got, core, name):
    s, _, rh, ccols = g.shape
    tr = _pick(rh, (256, 128))

    def body(core_ref, g_ref, r_ref, o_ref):
        o_ref[...] = (g_ref[...] + r_ref[...]).astype(o_ref.dtype)

    spec = pltpu.PrefetchScalarGridSpec(
        num_scalar_prefetch=1, grid=(s, rh // tr),
        in_specs=[pl.BlockSpec((None, None, tr, ccols), lambda a, i, cr: (a, cr[0], i, 0)),
                  pl.BlockSpec((None, None, tr, ccols), lambda a, i, cr: (a, 0, i, 0))],
        out_specs=pl.BlockSpec((None, tr, ccols), lambda a, i, cr: (a, i, 0)))
    return pl.pallas_call(body, grid_spec=spec, out_shape=jax.ShapeDtypeStruct((s, rh, ccols), MXU_DTYPE),
                          compiler_params=_params(), name=name)(core, g, got)


def _chips_side(ps):
    n = len(ps)

    def copies(ins, outs, sems):
        send_sems, recv_sems, local_sems = sems
        mx, my, mc, others = _place()
        chip = 2 * mx + my
        locs = [pltpu.make_async_copy(ins[w].at[chip], outs[w].at[chip], local_sems.at[w]) for w in range(n)]
        sends, lands = [], []
        for w in range(n):
            for j, (ox, oy) in enumerate(others):
                och = 2 * ox + oy
                sends.append(_remote(ins[w].at[och], outs[w].at[chip], send_sems, recv_sems, 3 * w + j, (ox, oy, mc)))
                lands.append(_remote(ins[w].at[och], outs[w].at[och], send_sems, recv_sems, 3 * w + j, (ox, oy, mc)))
        return locs, sends, lands

    def start(ins, outs, sems):
        locs, sends, _ = copies(ins, outs, sems)
        for cp in locs + sends:
            cp.start()

    def finish(ins, outs, sems):
        locs, sends, lands = copies(ins, outs, sems)
        for cp in lands:
            cp.wait_recv()
        for cp in sends:
            cp.wait_send()
        for cp in locs:
            cp.wait()

    return _Side(ps, [jax.ShapeDtypeStruct(p.shape, p.dtype) for p in ps],
                 [pltpu.SemaphoreType.DMA((3 * n,)), pltpu.SemaphoreType.DMA((3 * n,)), pltpu.SemaphoreType.DMA((n,))],
                 start, finish)


def _chip_sum(rb, core, name):
    s, rh, ccols = rb.shape
    tr = _pick(rh, (256, 128))

    def body(core_ref, r_ref, o_ref):
        tot = r_ref[0].astype(F32)
        for k in range(1, s):
            tot = tot + r_ref[k].astype(F32)
        o_ref[...] = tot

    spec = pltpu.PrefetchScalarGridSpec(
        num_scalar_prefetch=1, grid=(rh // tr,),
        in_specs=[pl.BlockSpec((s, tr, ccols), lambda i, cr: (0, i, 0))],
        out_specs=pl.BlockSpec((None, tr, ccols), lambda i, cr: (cr[0], i, 0)))
    return pl.pallas_call(body, grid_spec=spec, out_shape=jax.ShapeDtypeStruct((2, rh, ccols), F32),
                          compiler_params=_params(), name=name)(core, rb)


def _share_side(fulls):
    n = len(fulls)

    def copies(ins, outs, sems):
        mx, my, mc, _ = _place()
        sib = (mx, my, 1 - mc)
        sends = [_remote(ins[w].at[mc], outs[w].at[mc], sems[0], sems[1], w, sib) for w in range(n)]
        lands = [_remote(ins[w].at[mc], outs[w].at[1 - mc], sems[0], sems[1], w, sib) for w in range(n)]
        return sends, lands

    def start(ins, outs, sems):
        for cp in copies(ins, outs, sems)[0]:
            cp.start()

    def finish(ins, outs, sems):
        sends, lands = copies(ins, outs, sems)
        for cp in lands:
            cp.wait_recv()
        for cp in sends:
            cp.wait_send()

    return _Side(fulls, [jax.ShapeDtypeStruct(f.shape, f.dtype) for f in fulls],
                 [pltpu.SemaphoreType.DMA((n,)), pltpu.SemaphoreType.DMA((n,))], start, finish,
                 aliases=[(w, w) for w in range(n)])


def _cast(w, l, name):
    _, r, ccols = w.shape
    tr = _pick(r, (256, 128))

    def body(w_ref, o_ref):
        o_ref[...] = w_ref[...].astype(o_ref.dtype)

    return pl.pallas_call(body, grid=(r // tr,), in_specs=[pl.BlockSpec((None, tr, ccols), lambda i: (l, i, 0))],
                          out_specs=pl.BlockSpec((tr, ccols), lambda i: (i, 0)),
                          out_shape=jax.ShapeDtypeStruct((r, ccols), MXU_DTYPE), compiler_params=_params(), name=name)(w)


def _adam_math(g, w, m, v):
    m = ADAM_B1 * m + (1.0 - ADAM_B1) * g
    v = ADAM_B2 * v + (1.0 - ADAM_B2) * jnp.square(g)
    m_hat = m / (1.0 - ADAM_B1 ** ADAM_STEP)
    v_hat = v / (1.0 - ADAM_B2 ** ADAM_STEP)
    return -ADAM_LR * (m_hat / (jnp.sqrt(v_hat) + ADAM_EPS) + ADAM_WD * w), m, v


def _adamw_layer(l, g, w, m, v, bufs, name, side=None):
    depth, r, ccols = w.shape
    tr = _pick(r, (128,))

    def body(g_ref, w_ref, m_ref, v_ref, b0, b1, b2, b3, go_ref, d_ref, mo_ref, vo_ref):
        gv = g_ref[...]
        d, m2, v2 = _adam_math(gv, w_ref[...], m_ref[...], v_ref[...])
        go_ref[...] = gv
        d_ref[...] = d
        mo_ref[...] = m2
        vo_ref[...] = v2

    lay = pl.BlockSpec((None, tr, ccols), lambda i: (l, i, 0))
    return _call(
        body, grid=(r // tr,), in_specs=[pl.BlockSpec((tr, ccols), lambda i: (i, 0)), lay, lay, lay] + [ANY] * 4,
        out_specs=[lay] * 4, out_shape=[jax.ShapeDtypeStruct((depth, r, ccols), F32)] * 4, scratch=[],
        args=(g, w, m, v, *bufs), name=name, side=side, aliases={4: 0, 5: 1, 6: 2, 7: 3})


def _adamw_small(g, g2, w, m, v, name):
    two = g2 is not None

    def body(*refs):
        if two:
            g_ref, g2_ref, w_ref, m_ref, v_ref, go_ref, d_ref, mo_ref, vo_ref = refs
            gv = g_ref[...] + g2_ref[...]
        else:
            g_ref, w_ref, m_ref, v_ref, go_ref, d_ref, mo_ref, vo_ref = refs
            gv = g_ref[...]
        d, m2, v2 = _adam_math(gv, w_ref[...], m_ref[...], v_ref[...])
        go_ref[...] = gv
        d_ref[...] = d
        mo_ref[...] = m2
        vo_ref[...] = v2

    args = [g] + ([g2] if two else []) + [w, m, v]
    vm = pl.BlockSpec(memory_space=pltpu.VMEM)
    return pl.pallas_call(body, in_specs=[vm] * len(args), out_specs=[vm] * 4,
                          out_shape=[jax.ShapeDtypeStruct(w.shape, F32)] * 4, name=name)(*args)


def _sum8(g, name):
    def body(g_ref, o_ref):
        tot = g_ref[0]
        for k in range(1, N_DEV):
            tot = tot + g_ref[k]
        o_ref[...] = tot

    vm = pl.BlockSpec(memory_space=pltpu.VMEM)
    return pl.pallas_call(body, in_specs=[vm], out_specs=vm, out_shape=jax.ShapeDtypeStruct(g.shape[1:], F32),
                          compiler_params=_params(), name=name)(g)


def _pack(arrs, width):
    flat = jnp.concatenate([a.reshape(-1).astype(F32) for a in arrs])
    rows = -(-flat.size // (8 * width)) * 8
    return jnp.pad(flat, (0, rows * width - flat.size)).reshape(rows, width)


def _unpack(flat, shapes):
    out, off = [], 0
    for shp in shapes:
        size = 1
        for v in shp:
            size *= v
        out.append(flat[..., off:off + size].reshape(flat.shape[:-1] + tuple(shp)))
        off += size
    return out


class _Comm:
    def __init__(self, core, params):
        self.core, self.params = core, params
        depth = params[0][1].shape[0]
        self.halves = [[_cast(w, l, f"cast_{nm}_{l}").reshape(2, w.shape[1] // 2, w.shape[2]) for nm, w, _, _ in params]
                       for l in range(depth)]
        self.stacked = [[lax.empty(w.shape, F32) for _ in range(4)] for _, w, _, _ in params]

    def weight(self, k, gathered):
        rows, cols = 2 * gathered.shape[2], gathered.shape[3]
        return gathered.reshape(N_CHIP, rows, cols) if k % 2 == 0 else gathered.reshape(1, N_CHIP * rows, cols)


class _GradSync:
    def __init__(self, comm, l, grads):
        self.comm, self.l, self.keys = comm, l, tuple(sorted(grads))
        self.gs = {k: g.reshape(N_CHIP, 2, g.shape[0] * g.shape[1] // (2 * N_CHIP), g.shape[2]) for k, g in grads.items()}
        self.ps, self.rb, self.full = {}, {}, {}

    def pair_side(self):
        return _pair_side([self.gs[k] for k in self.keys])

    def add(self, got):
        for k, r in zip(self.keys, got):
            self.ps[k] = _pair_add(self.gs[k], r, self.comm.core, f"rs_add_{self.l}_{k}")

    def chips_side(self, which):
        return _chips_side([self.ps[k] for k in which])

    def land(self, which, got):
        self.rb.update(zip(which, got))

    def sum(self):
        for k in self.keys:
            self.full[k] = _chip_sum(self.rb[k], self.comm.core, f"rs_sum_{self.l}_{k}")

    def share_side(self):
        return _share_side([self.full[k] for k in self.keys])

    def adam(self, got):
        for k, full in zip(self.keys, got):
            nm, w, m, v = self.comm.params[k]
            gsum = full.reshape(2 * full.shape[1], full.shape[2])
            self.comm.stacked[k], _ = _adamw_layer(self.l, gsum, w, m, v, self.comm.stacked[k], f"adamw_{nm}_{self.l}")


COND_ROWS = 16


def _ada_fwd(cond, w_ada, b_cols, name):
    depth, d, ns = w_ada.shape
    tn = _pick(ns, (512, 384, 256, 128))

    def body(c_ref, w_ref, b_ref, o_ref):
        o_ref[...] = _dot(_silu(c_ref[...]), w_ref[...]) + b_ref[...]

    return pl.pallas_call(
        body, grid=(depth, ns // tn),
        in_specs=[pl.BlockSpec((COND_ROWS, d), lambda l, j: (0, 0)), pl.BlockSpec((None, d, tn), lambda l, j: (l, 0, j)),
                  pl.BlockSpec((None, 1, tn), lambda l, j: (l, 0, j))],
        out_specs=pl.BlockSpec((None, COND_ROWS, tn), lambda l, j: (l, 0, j)),
        out_shape=jax.ShapeDtypeStruct((depth, COND_ROWS, ns), F32), compiler_params=_params(), name=name)(cond, w_ada, b_cols)


def _ada_bwd(cond, dmod, w_ada, name, side=None):
    depth, d, ns = w_ada.shape
    tn = _pick(ns, (512, 384, 256, 128))

    def body(c_ref, dm_ref, w_ref, gw_ref, dc_ref):
        @pl.when((pl.program_id(0) == 0) & (pl.program_id(1) == 0))
        def _():
            dc_ref[...] = jnp.zeros_like(dc_ref)

        dm = dm_ref[...]
        gw_ref[...] = _dot_tn(_silu(c_ref[...]), dm)
        dc_ref[...] += _dot_nt(dm, w_ref[...])

    return _call(
        body, grid=(depth, ns // tn),
        in_specs=[pl.BlockSpec((COND_ROWS, d), lambda l, j: (0, 0)),
                  pl.BlockSpec((None, COND_ROWS, tn), lambda l, j: (l, 0, j)),
                  pl.BlockSpec((None, d, tn), lambda l, j: (l, 0, j))],
        out_specs=[pl.BlockSpec((None, d, tn), lambda l, j: (l, 0, j)), pl.BlockSpec((COND_ROWS, d), lambda l, j: (0, 0))],
        out_shape=[jax.ShapeDtypeStruct((depth, d, ns), F32), jax.ShapeDtypeStruct((COND_ROWS, d), F32)],
        scratch=[], args=(cond, dmod, w_ada), name=name, side=side)


def _cctx_grad(parts, c_ctx, name):
    def body(p_ref, c_ref, o_ref):
        tot = p_ref[0, 0:1, :]
        for k in range(1, N_CHIP):
            tot = tot + p_ref[2 * k, 0:1, :]
        z = c_ref[...]
        sg = 1.0 / (1.0 + jnp.exp(-z))
        o_ref[...] = tot * (sg + z * sg * (1.0 - sg))

    vm = pl.BlockSpec(memory_space=pltpu.VMEM)
    return pl.pallas_call(body, in_specs=[vm, vm], out_specs=vm, out_shape=jax.ShapeDtypeStruct(c_ctx.shape, F32),
                          name=name)(parts, c_ctx)


def kernel(x, c, ctx, c_ctx, w_ada, b_ada, g_norm1, g_norm2, w_in, conv_w, conv_b, sink, g_out_conv, g_out_attn, w_out, w_mlp1, w_mlp2, g_final, loss_target, m_c_ctx, m_w_ada, m_b_ada, m_g_norm1, m_g_norm2, m_w_in, m_conv_w, m_conv_b, m_sink, m_g_out_conv, m_g_out_attn, m_w_out, m_w_mlp1, m_w_mlp2, m_g_final, v_c_ctx, v_w_ada, v_b_ada, v_g_norm1, v_g_norm2, v_w_in, v_conv_w, v_conv_b, v_sink, v_g_out_conv, v_g_out_attn, v_w_out, v_w_mlp1, v_w_mlp2, v_g_final):
    mx, my, mc = lax.axis_index("x"), lax.axis_index("y"), lax.axis_index("c")
    chip, rank = 2 * mx + my, 4 * mx + 2 * my + mc
    core = jnp.reshape(mc, (1,)).astype(I32)
    depth, d = g_norm1.shape
    s_len, ctx_rows = x.shape[1], ctx.shape[1]
    cw_cols = conv_w.shape[2]
    c_conv = cw_cols * N_CHIP
    n_heads = sink.shape[1]
    ns_ada = w_ada.shape[2]

    got = _allgather8(_pack([c, conv_w], d), "gather_cond")
    flat = got.reshape(N_DEV, -1)
    conv_w_full = jnp.transpose(flat[::2, d:d + conv_w.size].reshape(N_CHIP, depth, 3, cw_cols), (1, 2, 0, 3))
    conv_w_full = conv_w_full.reshape(depth, 3, c_conv)
    cond = jnp.zeros((COND_ROWS, d), F32).at[:N_DEV].set(flat[:, :d]).at[N_DEV].set(c_ctx)

    b_cols = lax.dynamic_slice_in_dim(b_ada, chip * ns_ada, ns_ada, axis=1)[:, None, :]
    mod_cols = _ada_fwd(cond, w_ada, b_cols, "ada_fwd")
    got = _allgather8(mod_cols.reshape(depth * COND_ROWS, ns_ada), "gather_mod")
    mod_all = jnp.transpose(got[::2].reshape(N_CHIP, depth, COND_ROWS, ns_ada), (1, 2, 0, 3))
    mod_all = mod_all.reshape(depth, COND_ROWS, N_CHIP * ns_ada)
    mod_me = lax.dynamic_index_in_dim(mod_all, rank, axis=1, keepdims=False)
    mods = jnp.stack([mod_all[:, N_DEV], mod_me], axis=1).reshape(depth, 2, N_MOD, 1, d)

    comm = _Comm(core, (("w_in", w_in, m_w_in, v_w_in), ("w_out", w_out, m_w_out, v_w_out),
                        ("w_mlp1", w_mlp1, m_w_mlp1, v_w_mlp1), ("w_mlp2", w_mlp2, m_w_mlp2, v_w_mlp2)))
    small = dict(g_norm1=g_norm1[:, None], g_norm2=g_norm2[:, None], conv_w=conv_w_full, conv_b=conv_b[:, None], sink=sink,
                 g_out_conv=g_out_conv[:, None], g_out_attn=g_out_attn[:, None], g_final=g_final[None])
    x0 = jnp.concatenate([ctx[0], x[0]], axis=0)
    loss_part, dx0, _, dmods, sg, last = _local_step(x0, loss_target[0], mods, None, small, ctx_rows=ctx_rows, comm=comm)
    loss = lax.psum(loss_part, ("x", "y", "c"))
    grad_x = dx0[ctx_rows:][None]

    names = ("g_norm1", "g_norm2", "conv_w", "conv_b", "sink", "g_out_conv", "g_out_attn", "g_final")
    shapes = [(depth, 2, N_MOD * d)] + [sg[k].shape for k in names]
    got = _allgather8(_pack([dmods] + [sg[k] for k in names], d), "gather_small")
    tot = _unpack(_sum8(got, "sum_small").reshape(-1), shapes)
    dmod_tot, small_tot = tot[0], dict(zip(names, tot[1:]))
    dmod_lat = _unpack(got.reshape(N_DEV, -1), shapes[:1])[0][:, :, 1]
    dm_rows = jnp.zeros((depth, COND_ROWS, N_MOD * d), F32)
    dm_rows = dm_rows.at[:, :N_DEV].set(jnp.transpose(dmod_lat, (1, 0, 2))).at[:, N_DEV].set(dmod_tot[:, 0])
    dm_cols = lax.dynamic_slice_in_dim(dm_rows, chip * ns_ada, ns_ada, axis=2)
    (g_w_ada, dcond), got = _ada_bwd(cond, dm_cols, w_ada, "ada_bwd", side=last.chips_side((1,)))
    last.land((1,), got)
    last.sum()
    last.adam(_run_side(last.share_side(), "rs_share_last"))
    got = _allgather8(dcond[N_DEV:N_DEV + 8], "gather_dcond")
    g_c_ctx = _cctx_grad(got, c_ctx[None], "c_ctx_grad")

    res = {}
    ada_bufs = [lax.empty((1,) + (depth * d, ns_ada), F32) for _ in range(4)]
    ada_res, _ = _adamw_layer(
        0, g_w_ada.reshape(depth * d, ns_ada), w_ada.reshape(1, depth * d, ns_ada),
        m_w_ada.reshape(1, depth * d, ns_ada), v_w_ada.reshape(1, depth * d, ns_ada), ada_bufs, "adamw_w_ada")
    res["w_ada"] = [r.reshape(w_ada.shape) for r in ada_res]
    res.update(zip(("w_in", "w_out", "w_mlp1", "w_mlp2"), comm.stacked))
    res["c_ctx"] = [r[0] for r in _adamw_small(g_c_ctx, None, c_ctx[None], m_c_ctx[None], v_c_ctx[None], "adamw_c_ctx")]
    res["b_ada"] = _adamw_small(dmod_tot[:, 0], dmod_tot[:, 1], b_ada, m_b_ada, v_b_ada, "adamw_b_ada")
    cw_grad = lax.dynamic_slice_in_dim(small_tot["conv_w"], chip * cw_cols, cw_cols, axis=2)
    res["conv_w"] = [r.reshape(conv_w.shape) for r in _adamw_small(
        cw_grad.reshape(depth * 3, cw_cols), None, conv_w.reshape(depth * 3, cw_cols),
        m_conv_w.reshape(depth * 3, cw_cols), v_conv_w.reshape(depth * 3, cw_cols), "adamw_conv_w")]
    for nm, w, m, v in (("g_norm1", g_norm1, m_g_norm1, v_g_norm1), ("g_norm2", g_norm2, m_g_norm2, v_g_norm2),
                        ("conv_b", conv_b, m_conv_b, v_conv_b), ("sink", sink, m_sink, v_sink),
                        ("g_out_conv", g_out_conv, m_g_out_conv, v_g_out_conv),
                        ("g_out_attn", g_out_attn, m_g_out_attn, v_g_out_attn)):
        res[nm] = _adamw_small(small_tot[nm], None, w, m, v, f"adamw_{nm}")
    res["g_final"] = [r[0] for r in _adamw_small(small_tot["g_final"][None], None, g_final[None], m_g_final[None],
                                                 v_g_final[None], "adamw_g_final")]
    order = ("c_ctx", "w_ada", "b_ada", "g_norm1", "g_norm2", "w_in", "conv_w", "conv_b", "sink", "g_out_conv",
             "g_out_attn", "w_out", "w_mlp1", "w_mlp2", "g_final")
    return (loss, grad_x, *[res[n][0] for n in order], *[res[n][1] for n in order], *[res[n][2] for n in order],
            *[res[n][3] for n in order])
```

```python
import functools

import jax
import jax.numpy as jnp
from jax import lax
from jax.experimental import pallas as pl
from jax.experimental.pallas import tpu as pltpu

F32 = jnp.float32
I32 = jnp.int32
MXU_DTYPE = jnp.bfloat16
EPS = 1e-6
HEAD_DIM = 64
N_KV = 4
WINDOW = 128
QBLK = 128
LANES = 128
GRID_W = 64
ROPE_THETA = 10000.0
NEG_INF = -1e30
N_MOD = 6
HALO = 8
ADAM_LR, ADAM_B1, ADAM_B2, ADAM_EPS, ADAM_WD, ADAM_STEP = 0.001, 0.9, 0.999, 1e-08, 0.01, 10
V7X_VMEM_BYTES = 64 * 1024 * 1024
VMEM_LIMIT = V7X_VMEM_BYTES * 3 // 4
ROW_TILES = (1408, 768, 512, 640, 256, 128)
MESH = pl.DeviceIdType.MESH
ANY = pl.BlockSpec(memory_space=pl.ANY)


def _params():
    return pltpu.CompilerParams(vmem_limit_bytes=VMEM_LIMIT)


def _pick(n, cands):
    for c in cands:
        if n % c == 0:
            return c
    raise ValueError(f"no tile of {cands} divides {n}")


def _mx(v):
    return v.astype(MXU_DTYPE)


def _dot(a, b):
    return jnp.dot(_mx(a), _mx(b), preferred_element_type=F32)


def _dot_nt(a, b):
    return lax.dot_general(_mx(a), _mx(b), (((1,), (1,)), ((), ())), preferred_element_type=F32)


def _dot_tn(a, b):
    return lax.dot_general(_mx(a), _mx(b), (((0,), (0,)), ((), ())), preferred_element_type=F32)


def _silu(v):
    return v / (1.0 + jnp.exp(-v))


class _Side:
    def __init__(self, inputs, out_shapes, scratch, start, finish, aliases=(), middle=None):
        self.inputs, self.out_shapes, self.scratch = list(inputs), list(out_shapes), list(scratch)
        self.start, self.finish, self.aliases = start, finish, tuple(aliases)
        self.middle = middle


def _call(body, *, grid, in_specs, out_specs, out_shape, scratch, args, name, side=None, aliases=None):
    aliases = dict(aliases or {})
    if side is None:
        res = pl.pallas_call(body, grid=grid, in_specs=in_specs, out_specs=out_specs, out_shape=out_shape,
                             scratch_shapes=scratch, input_output_aliases=aliases, compiler_params=_params(),
                             name=name)(*args)
        return list(res), []
    ni, no, ns = len(in_specs), len(out_specs), len(scratch)
    si, so = len(side.inputs), len(side.out_shapes)

    def full(*refs):
        ins, sins = refs[:ni], refs[ni:ni + si]
        outs, souts = refs[ni + si:ni + si + no], refs[ni + si + no:ni + si + no + so]
        scr, sems = refs[ni + si + no + so:ni + si + no + so + ns], refs[ni + si + no + so + ns:]
        ids = [pl.program_id(k) for k in range(len(grid))]
        first, last = ids[0] == 0, ids[0] == grid[0] - 1
        for k in range(1, len(grid)):
            first, last = first & (ids[k] == 0), last & (ids[k] == grid[k] - 1)

        @pl.when(first)
        def _():
            side.start(sins, souts, sems)

        body(*ins, *outs, *scr)

        if side.middle is not None:
            lin, total = ids[0], grid[0]
            for k in range(1, len(grid)):
                lin, total = lin * grid[k] + ids[k], total * grid[k]

            @pl.when(lin == (3 * total) // 4)
            def _():
                side.middle(sins, souts, sems)

        @pl.when(last)
        def _():
            side.finish(sins, souts, sems)

    res = pl.pallas_call(
        full, grid=grid, in_specs=list(in_specs) + [ANY] * si, out_specs=list(out_specs) + [ANY] * so,
        out_shape=list(out_shape) + side.out_shapes, scratch_shapes=list(scratch) + side.scratch,
        input_output_aliases={**aliases, **{ni + a: no + b for a, b in side.aliases}}, compiler_params=_params(),
        name=name)(*args, *side.inputs)
    return list(res[:no]), list(res[no:])


def _run_side(side, name):
    si, so = len(side.inputs), len(side.out_shapes)

    def body(*refs):
        side.start(refs[:si], refs[si:si + so], refs[si + so:])
        if side.middle is not None:
            side.middle(refs[:si], refs[si:si + so], refs[si + so:])
        side.finish(refs[:si], refs[si:si + so], refs[si + so:])

    res = pl.pallas_call(body, in_specs=[ANY] * si, out_specs=[ANY] * so, out_shape=side.out_shapes,
                         scratch_shapes=side.scratch, input_output_aliases=dict(side.aliases), name=name)(*side.inputs)
    return list(res)


def _mm_nn(a, b3, *, out_dtypes, epilogue, name, side=None):
    m, k = a.shape
    s, _, ns = b3.shape
    tm = _pick(m, ROW_TILES)
    tn = _pick(ns, (1024, 1152, 640, 512, 256, 128))
    tk = _pick(k, (2048, 1024, 512))
    nbs, nk = ns // tn, k // tk
    n_out = len(out_dtypes)

    def body(a_ref, b_ref, *rest):
        outs = rest[:n_out]

        def write(v):
            for o, r in zip(outs, epilogue(v)):
                o[...] = r.astype(o.dtype)

        if nk == 1:
            write(jnp.dot(a_ref[...], b_ref[...], preferred_element_type=F32))
            return
        acc = rest[n_out]
        kk = pl.program_id(2)

        @pl.when(kk == 0)
        def _():
            acc[...] = jnp.zeros_like(acc)

        acc[...] += jnp.dot(a_ref[...], b_ref[...], preferred_element_type=F32)

        @pl.when(kk == nk - 1)
        def _():
            write(acc[...])

    return _call(
        body, grid=(m // tm, s * nbs, nk),
        in_specs=[pl.BlockSpec((tm, tk), lambda i, j, kk: (i, kk)),
                  pl.BlockSpec((None, tk, tn), lambda i, j, kk: (j // nbs, kk, j % nbs))],
        out_specs=[pl.BlockSpec((tm, tn), lambda i, j, kk: (i, j))] * n_out,
        out_shape=[jax.ShapeDtypeStruct((m, s * ns), dt) for dt in out_dtypes],
        scratch=[pltpu.VMEM((tm, tn), F32)] if nk > 1 else [], args=(a, b3), name=name, side=side)


def _mm_nt(a, b3, *, out_dtype, name, extra=None, epilogue=None, side=None):
    m = a.shape[0]
    s, ko, ns = b3.shape
    tm = _pick(m, ROW_TILES)
    tko = _pick(ko, (1024, 512))
    tn = _pick(ns, (2048, 1152, 1024, 640, 512, 256, 128))
    nbs = ns // tn
    nk = s * nbs
    n_in = 3 if extra is not None else 2

    def body(a_ref, b_ref, *rest):
        x_ref = rest[0] if extra is not None else None
        o_ref = rest[n_in - 2]

        def write(v):
            if epilogue is not None:
                v = epilogue(v, x_ref[...])
            o_ref[...] = v.astype(o_ref.dtype)

        if nk == 1:
            write(lax.dot_general(a_ref[...], b_ref[...], (((1,), (1,)), ((), ())), preferred_element_type=F32))
            return
        acc = rest[n_in - 1]
        kk = pl.program_id(2)

        @pl.when(kk == 0)
        def _():
            acc[...] = jnp.zeros_like(acc)

        acc[...] += lax.dot_general(a_ref[...], b_ref[...], (((1,), (1,)), ((), ())), preferred_element_type=F32)

        @pl.when(kk == nk - 1)
        def _():
            write(acc[...])

    in_specs = [pl.BlockSpec((tm, tn), lambda i, j, kk: (i, kk)),
                pl.BlockSpec((None, tko, tn), lambda i, j, kk: (kk // nbs, j, kk % nbs))]
    args = [a, b3]
    if extra is not None:
        in_specs.append(pl.BlockSpec((tm, tko), lambda i, j, kk: (i, j)))
        args.append(extra)
    return _call(
        body, grid=(m // tm, ko // tko, nk), in_specs=in_specs,
        out_specs=[pl.BlockSpec((tm, tko), lambda i, j, kk: (i, j))],
        out_shape=[jax.ShapeDtypeStruct((m, ko), out_dtype)],
        scratch=[pltpu.VMEM((tm, tko), F32)] if nk > 1 else [], args=args, name=name, side=side)


def _mm_tn(a, b, *, shards, name, side=None):
    t, k = a.shape
    ns = b.shape[1] // shards
    tt = _pick(t, (2 * ROW_TILES[0],) + ROW_TILES)
    tk = _pick(k, (1024, 512))
    tn = _pick(ns, (1024, 1152, 640, 512, 256, 128))
    nbs, nt = ns // tn, t // tt

    def body(a_ref, b_ref, o_ref, acc):
        tt_i = pl.program_id(2)

        @pl.when(tt_i == 0)
        def _():
            acc[...] = jnp.zeros_like(acc)

        acc[...] += lax.dot_general(a_ref[...], b_ref[...], (((0,), (0,)), ((), ())), preferred_element_type=F32)

        @pl.when(tt_i == nt - 1)
        def _():
            o_ref[...] = acc[...].astype(o_ref.dtype)

    return _call(
        body, grid=(k // tk, shards * nbs, nt),
        in_specs=[pl.BlockSpec((tt, tk), lambda i, j, q: (q, i)),
                  pl.BlockSpec((tt, tn), lambda i, j, q: (q, j))],
        out_specs=[pl.BlockSpec((None, tk, tn), lambda i, j, q: (j // nbs, i, j % nbs))],
        out_shape=[jax.ShapeDtypeStruct((shards, k, ns), MXU_DTYPE)], scratch=[pltpu.VMEM((tk, tn), F32)], args=(a, b),
        name=name, side=side)


def _row_tile(t, ctx_rows):
    return 256 if t % 256 == 0 and ctx_rows % 256 == 0 else 128


def _mod_spec(d, ncb, idx):
    return pl.BlockSpec((None, None, 1, d), lambda i: (jnp.where(i >= ncb, 1, 0), idx, 0, 0))


def _norm_fwd(x, z, gate_mods, mods, g, *, gate_idx, shift_idx, scale_idx, ctx_rows, name):
    t, d = x.shape
    tr = _row_tile(t, ctx_rows)
    ncb = ctx_rows // tr
    row = pl.BlockSpec((tr, d), lambda i: (i, 0))
    vec = pl.BlockSpec((1, d), lambda i: (0, 0))
    resid = z is not None

    def body(*refs):
        if resid:
            x_ref, z_ref, gt_ref, g_ref, sh_ref, sc_ref, xo_ref, h_ref = refs
            xn = x_ref[...] + gt_ref[...] * z_ref[...].astype(F32)
            xo_ref[...] = xn
        else:
            x_ref, g_ref, sh_ref, sc_ref, h_ref = refs
            xn = x_ref[...]
        r = lax.rsqrt(jnp.mean(xn * xn, axis=-1, keepdims=True) + EPS)
        h_ref[...] = ((xn * r * g_ref[...]) * (1.0 + sc_ref[...]) + sh_ref[...]).astype(h_ref.dtype)

    mspecs = [vec, _mod_spec(d, ncb, shift_idx), _mod_spec(d, ncb, scale_idx)]
    if resid:
        in_specs = [row, row, _mod_spec(d, ncb, gate_idx)] + mspecs
        args = (x, z, gate_mods, g, mods, mods)
        out_specs = [row, row]
        out_shape = [jax.ShapeDtypeStruct((t, d), F32), jax.ShapeDtypeStruct((t, d), MXU_DTYPE)]
    else:
        in_specs = [row] + mspecs
        args = (x, g, mods, mods)
        out_specs = row
        out_shape = jax.ShapeDtypeStruct((t, d), MXU_DTYPE)
    return pl.pallas_call(body, grid=(t // tr,), in_specs=in_specs, out_specs=out_specs, out_shape=out_shape,
                          compiler_params=_params(), name=name)(*args)


R_DSHIFT, R_DSCALE, R_DG, R_DGATE = 0, 2, 4, 5


def _norm_bwd(dx, dh, xin, mods, g, *, scale_idx, ctx_rows, name, prev=None, side=None):
    t, d = dx.shape
    tr = _row_tile(t, ctx_rows)
    ncb = ctx_rows // tr
    row = pl.BlockSpec((tr, d), lambda i: (i, 0))
    vec = pl.BlockSpec((1, d), lambda i: (0, 0))
    acc_spec = pl.BlockSpec((8, d), lambda i: (0, 0))
    has_prev = prev is not None

    def body(*refs):
        if has_prev:
            dx_ref, dh_ref, x_ref, g_ref, sc_ref, z_ref, gt_ref, dxo_ref, dz_ref, acc = refs
        else:
            dx_ref, dh_ref, x_ref, g_ref, sc_ref, dxo_ref, acc = refs
        i = pl.program_id(0)

        @pl.when(i == 0)
        def _():
            acc[...] = jnp.zeros_like(acc)

        lat = jnp.where(i >= ncb, 1.0, 0.0)
        x = x_ref[...]
        r = lax.rsqrt(jnp.mean(x * x, axis=-1, keepdims=True) + EPS)
        xhat = x * r
        gv = g_ref[...]
        dhv = dh_ref[...].astype(F32)
        dn = dhv * (1.0 + sc_ref[...])
        dxhat = dn * gv
        dxin = dx_ref[...] + r * (dxhat - xhat * jnp.mean(dxhat * xhat, axis=-1, keepdims=True))
        dxo_ref[...] = dxin
        dshift = jnp.sum(dhv, axis=0, keepdims=True)
        dscale = jnp.sum(dhv * (xhat * gv), axis=0, keepdims=True)
        acc[R_DSHIFT:R_DSHIFT + 1, :] += dshift * (1.0 - lat)
        acc[R_DSHIFT + 1:R_DSHIFT + 2, :] += dshift * lat
        acc[R_DSCALE:R_DSCALE + 1, :] += dscale * (1.0 - lat)
        acc[R_DSCALE + 1:R_DSCALE + 2, :] += dscale * lat
        acc[R_DG:R_DG + 1, :] += jnp.sum(dn * xhat, axis=0, keepdims=True)
        if has_prev:
            dz_ref[...] = (dxin * gt_ref[...]).astype(dz_ref.dtype)
            dgate = jnp.sum(dxin * z_ref[...].astype(F32), axis=0, keepdims=True)
            acc[R_DGATE:R_DGATE + 1, :] += dgate * (1.0 - lat)
            acc[R_DGATE + 1:R_DGATE + 2, :] += dgate * lat

    in_specs = [row, row, row, vec, _mod_spec(d, ncb, scale_idx)]
    args = [dx, dh, xin, g, mods]
    out_specs = [row]
    out_shape = [jax.ShapeDtypeStruct((t, d), F32)]
    if has_prev:
        z, gate_mods, gate_idx = prev
        in_specs += [row, _mod_spec(d, ncb, gate_idx)]
        args += [z, gate_mods]
        out_specs.append(row)
        out_shape.append(jax.ShapeDtypeStruct((t, d), MXU_DTYPE))
    out_specs.append(acc_spec)
    out_shape.append(jax.ShapeDtypeStruct((8, d), F32))
    outs, got = _call(body, grid=(t // tr,), in_specs=in_specs, out_specs=out_specs, out_shape=out_shape, scratch=[],
                      args=args, name=name, side=side)
    return outs if side is None else (outs, got)


R_FGATE, R_FG, R_FLOSS = 0, 2, 3


def _final(x1, o, gate_mods, g_final, tgt, *, ctx_rows, name):
    t, d = x1.shape
    tr = _row_tile(t, ctx_rows)
    ncb, nb = ctx_rows // tr, t // tr
    row = pl.BlockSpec((tr, d), lambda i: (i, 0))
    vec = pl.BlockSpec((1, d), lambda i: (0, 0))

    def body(x_ref, o_ref, gt_ref, g_ref, t_ref, dx_ref, do_ref, acc, lsum):
        i = pl.program_id(0)

        @pl.when(i == 0)
        def _():
            acc[...] = jnp.zeros_like(acc)
            lsum[...] = jnp.zeros_like(lsum)

        lat = jnp.where(i >= ncb, 1.0, 0.0)
        gt = gt_ref[...]
        ov = o_ref[...].astype(F32)
        x = x_ref[...] + gt * ov
        r = lax.rsqrt(jnp.mean(x * x, axis=-1, keepdims=True) + EPS)
        xhat = x * r
        gv = g_ref[...]
        err = (xhat * gv - t_ref[...]) * lat
        dy = err / d
        dxhat = dy * gv
        dxv = r * (dxhat - xhat * jnp.mean(dxhat * xhat, axis=-1, keepdims=True))
        dx_ref[...] = dxv
        do_ref[...] = (dxv * gt).astype(do_ref.dtype)
        dgate = jnp.sum(dxv * ov, axis=0, keepdims=True)
        acc[R_FGATE:R_FGATE + 1, :] += dgate * (1.0 - lat)
        acc[R_FGATE + 1:R_FGATE + 2, :] += dgate * lat
        acc[R_FG:R_FG + 1, :] += jnp.sum(dy * xhat, axis=0, keepdims=True)
        lsum[...] += jnp.sum(err * err, axis=0, keepdims=True)

        @pl.when(i == nb - 1)
        def _():
            total = (0.5 / d) * jnp.sum(lsum[...], axis=-1, keepdims=True)
            acc[R_FLOSS:R_FLOSS + 1, :] = jnp.broadcast_to(total, (1, d))

    return pl.pallas_call(
        body, grid=(nb,),
        in_specs=[row, row, _mod_spec(d, ncb, 5), vec, pl.BlockSpec((tr, d), lambda i: (jnp.maximum(i - ncb, 0), 0))],
        out_specs=[row, row, pl.BlockSpec((8, d), lambda i: (0, 0))],
        out_shape=[jax.ShapeDtypeStruct((t, d), F32), jax.ShapeDtypeStruct((t, d), MXU_DTYPE),
                   jax.ShapeDtypeStruct((8, d), F32)],
        scratch_shapes=[pltpu.VMEM((1, d), F32)], compiler_params=_params(), name=name)(x1, o, gate_mods, g_final, tgt)


def _rope_tables(s, ctx_rows):
    rows = s // GRID_W
    row_pos = jnp.repeat(jnp.arange(rows, dtype=F32), GRID_W)
    col_pos = jnp.tile(jnp.arange(GRID_W, dtype=F32), rows)
    quarter = HEAD_DIM // 4
    inv = ROPE_THETA ** (-jnp.arange(0, 2 * quarter, 2, dtype=F32) / (2 * quarter))
    ang_r, ang_c = row_pos[:, None] * inv[None, :], col_pos[:, None] * inv[None, :]
    cr, sr, cc, sc = jnp.cos(ang_r), jnp.sin(ang_r), jnp.cos(ang_c), jnp.sin(ang_c)
    zero = jnp.zeros_like(sr)
    cos = jnp.concatenate([cr, cr, cc, cc], axis=1)
    sa = jnp.concatenate([zero, sr, zero, sc], axis=1)
    sb = jnp.concatenate([-sr, zero, -sc, zero], axis=1)

    def full(tab, fill):
        tab = jnp.tile(tab, (1, LANES // HEAD_DIM))
        return jnp.concatenate([jnp.full((ctx_rows, LANES), fill, F32), tab], axis=0)

    return full(cos, 1.0), full(sa, 0.0), full(sb, 0.0)


def _rope_apply(x, cos, sa, sb, transpose):
    n = x.shape[1] // LANES
    cos, sa, sb = (jnp.tile(v, (1, n)) for v in (cos, sa, sb))
    quarter = HEAD_DIM // 4
    width = x.shape[1]
    if transpose:
        return x * cos + pltpu.roll(x * sa, width - quarter, 1) + pltpu.roll(x * sb, quarter, 1)
    return x * cos + pltpu.roll(x, quarter, 1) * sa + pltpu.roll(x, width - quarter, 1) * sb


def _twice(x):
    lane = lax.broadcasted_iota(I32, (x.shape[0], LANES), 1)
    out = []
    for j in range(N_KV):
        blk = x[:, (j // 2) * LANES:(j // 2 + 1) * LANES]
        own = jnp.where((lane < HEAD_DIM) if j % 2 == 0 else (lane >= HEAD_DIM), blk, 0.0)
        out.append(own + pltpu.roll(own, HEAD_DIM, 1))
    return jnp.concatenate(out, axis=1)


def _rope_fwd(p, tabs, *, da, kw, kv_start, name):
    t = p.shape[0]
    tr = _pick(t, (768, 256, 128))

    def body(q_ref, k_ref, v_ref, c_ref, a_ref, b_ref, qo_ref, ko_ref, vo_ref):
        cos, sa, sb = c_ref[...], a_ref[...], b_ref[...]
        qo_ref[...] = (_rope_apply(q_ref[...], cos, sa, sb, False) * (HEAD_DIM ** -0.5)).astype(qo_ref.dtype)
        ko_ref[...] = _twice(_rope_apply(k_ref[...], cos, sa, sb, False)).astype(ko_ref.dtype)
        vo_ref[...] = _twice(v_ref[...]).astype(vo_ref.dtype)

    tab = pl.BlockSpec((tr, LANES), lambda i: (i, 0))
    two = pl.BlockSpec((tr, N_KV * LANES), lambda i: (i, 0))
    return pl.pallas_call(
        body, grid=(t // tr,),
        in_specs=[pl.BlockSpec((tr, da), lambda i: (i, (kv_start - da) // da)),
                  pl.BlockSpec((tr, kw), lambda i: (i, kv_start // kw)),
                  pl.BlockSpec((tr, kw), lambda i: (i, kv_start // kw + 1)), tab, tab, tab],
        out_specs=[pl.BlockSpec((tr, da), lambda i: (i, 0)), two, two],
        out_shape=[jax.ShapeDtypeStruct((t, da), MXU_DTYPE), jax.ShapeDtypeStruct((t, N_KV * LANES), MXU_DTYPE),
                   jax.ShapeDtypeStruct((t, N_KV * LANES), MXU_DTYPE)],
        compiler_params=_params(), name=name)(p, p, p, *tabs)


def _rope_bwd(dq, dk, dv, tabs, dp, *, name):
    t, da = dq.shape
    kw = dk.shape[1]
    width = da + 2 * kw
    tr = _pick(t, (768, 256, 128))
    col = (dp.shape[1] - width) // width
    assert col * width == dp.shape[1] - width

    def body(q_ref, k_ref, v_ref, c_ref, a_ref, b_ref, dp_ref, o_ref):
        cos, sa, sb = c_ref[...], a_ref[...], b_ref[...]
        o_ref[:, :da] = _rope_apply(q_ref[...], cos, sa, sb, True).astype(o_ref.dtype)
        o_ref[:, da:da + kw] = _rope_apply(k_ref[...], cos, sa, sb, True).astype(o_ref.dtype)
        o_ref[:, da + kw:] = v_ref[...].astype(o_ref.dtype)

    tab = pl.BlockSpec((tr, LANES), lambda i: (i, 0))
    return pl.pallas_call(
        body, grid=(t // tr,),
        in_specs=[pl.BlockSpec((tr, da), lambda i: (i, 0)), pl.BlockSpec((tr, kw), lambda i: (i, 0)),
                  pl.BlockSpec((tr, kw), lambda i: (i, 0)), tab, tab, tab, ANY],
        out_specs=pl.BlockSpec((tr, width), lambda i: (i, col)),
        out_shape=jax.ShapeDtypeStruct(dp.shape, dp.dtype), input_output_aliases={6: 0},
        compiler_params=_params(), name=name)(dq, dk, dv, *tabs, dp)


def _attn_specs(t, ctx_rows):
    nblk = t // QBLK

    def clip(i):
        return jnp.clip(i, 0, nblk - 1)

    return [pl.BlockSpec((QBLK, 2 * LANES), lambda b, i: (clip(i - 1), b)),
            pl.BlockSpec((QBLK, 2 * LANES), lambda b, i: (i, b)),
            pl.BlockSpec((QBLK, 2 * LANES), lambda b, i: (clip(i + 1), b)),
            pl.BlockSpec((ctx_rows, 2 * LANES), lambda b, i: (0, b))]


def _band_bias(rows):
    rr = lax.broadcasted_iota(I32, (rows, 3 * QBLK), 0) & (QBLK - 1)
    cc = lax.broadcasted_iota(I32, (rows, 3 * QBLK), 1)
    return jnp.where(jnp.abs(cc - QBLK - rr) <= WINDOW, 0.0, NEG_INF).astype(F32)


def _local_bias(band_ref, i, t, ctx_rows):
    cc = lax.broadcasted_iota(I32, (1, 3 * QBLK), 1)
    keyrow = (i - 1) * QBLK + cc
    first_key = jnp.where(i * QBLK >= ctx_rows, ctx_rows, t)
    return band_ref[...] + jnp.where((keyrow >= first_key) & (keyrow < t), 0.0, NEG_INF)


def _stack_heads(ref, c0, nblocks):
    lane = lax.broadcasted_iota(I32, (QBLK, LANES), 1)
    rows = []
    for qb in range(nblocks):
        blk = ref[:, c0 + qb * LANES:c0 + (qb + 1) * LANES].astype(F32)
        rows += [jnp.where(lane < HEAD_DIM, blk, 0.0), jnp.where(lane < HEAD_DIM, 0.0, blk)]
    return jnp.concatenate(rows, axis=0)


def _unstack_heads(x, nblocks):
    lane = lax.broadcasted_iota(I32, (QBLK, LANES), 1)
    return [jnp.where(lane < HEAD_DIM, x[2 * qb * QBLK:(2 * qb + 1) * QBLK], x[(2 * qb + 1) * QBLK:(2 * qb + 2) * QBLK])
            for qb in range(nblocks)]


def _sink_column(sink_ref, head0, group):
    row = lax.broadcasted_iota(I32, (group * QBLK, 1), 0)
    col = jnp.zeros((group * QBLK, 1), F32)
    for g in range(group):
        col = jnp.where((row >= g * QBLK) & (row < (g + 1) * QBLK), sink_ref[head0 + g], col)
    return col


def _exps(qs, k_loc, k_ctx, bias, snk):
    s_loc = _dot_nt(qs, k_loc) + bias
    s_ctx = _dot_nt(qs, k_ctx)
    m = jnp.maximum(jnp.maximum(jnp.max(s_loc, axis=-1, keepdims=True), jnp.max(s_ctx, axis=-1, keepdims=True)), snk)
    e_loc, e_ctx, e_snk = jnp.exp(s_loc - m), jnp.exp(s_ctx - m), jnp.exp(snk - m)
    inv = 1.0 / (jnp.sum(e_loc, axis=-1, keepdims=True) + jnp.sum(e_ctx, axis=-1, keepdims=True) + e_snk)
    return e_loc, e_ctx, e_snk, inv


def _attn_fwd(qs, k2, v2, sink, *, ctx_rows, name, side=None):
    t, da = qs.shape
    group = da // HEAD_DIM // N_KV
    nqb = group // 2
    kvspecs = _attn_specs(t, ctx_rows)

    def body(sink_ref, band_ref, q_ref, kp, kc, kn, kx, vp, vc, vn, vx, o_ref):
        b, i = pl.program_id(0), pl.program_id(1)
        bias = _local_bias(band_ref, i, t, ctx_rows)
        for half in range(2):
            lanes = slice(half * LANES, (half + 1) * LANES)
            qst = _mx(_stack_heads(q_ref, half * nqb * LANES, nqb))
            k_loc = jnp.concatenate([kp[:, lanes], kc[:, lanes], kn[:, lanes]], axis=0)
            v_loc = jnp.concatenate([vp[:, lanes], vc[:, lanes], vn[:, lanes]], axis=0)
            snk = _sink_column(sink_ref, (2 * b + half) * group, group)
            e_loc, e_ctx, _, inv = _exps(qst, k_loc, kx[:, lanes], bias, snk)
            out = (_dot(e_loc, v_loc) + _dot(e_ctx, vx[:, lanes])) * inv
            for qb, blk in enumerate(_unstack_heads(out, nqb)):
                c0 = (half * nqb + qb) * LANES
                o_ref[:, c0:c0 + LANES] = blk

    qspec = pl.BlockSpec((QBLK, 2 * group * HEAD_DIM), lambda b, i: (i, b))
    band = pl.BlockSpec((group * QBLK, 3 * QBLK), lambda b, i: (0, 0))
    (ao,), got = _call(
        body, grid=(N_KV // 2, t // QBLK),
        in_specs=[pl.BlockSpec(memory_space=pltpu.SMEM), band, qspec] + kvspecs + kvspecs,
        out_specs=[qspec], out_shape=[jax.ShapeDtypeStruct((t, da), F32)], scratch=[],
        args=(sink, _band_bias(group * QBLK), qs, k2, k2, k2, k2, v2, v2, v2, v2), name=name, side=side)
    return ao, got


def _attn_bwd(qs, k2, v2, dao, sink, *, ctx_rows, name, side=None):
    t, da = qs.shape
    nblk = t // QBLK
    group = da // HEAD_DIM // N_KV
    nqb = group // 2
    kw = N_KV * HEAD_DIM
    nloc = 3 * QBLK
    kvspecs = _attn_specs(t, ctx_rows)

    def body(sink_ref, band_ref, q_ref, do_ref, kp, kc, kn, kx, vp, vc, vn, vx,
             dq_ref, dkl_ref, dvl_ref, dkc_ref, dvc_ref, ds_ref):
        b, i = pl.program_id(0), pl.program_id(1)

        @pl.when((b == 0) & (i == 0))
        def _():
            ds_ref[...] = jnp.zeros_like(ds_ref)

        @pl.when(i == 0)
        def _():
            dkc_ref[...] = jnp.zeros_like(dkc_ref)
            dvc_ref[...] = jnp.zeros_like(dvc_ref)

        bias = _local_bias(band_ref, i, t, ctx_rows)
        srow = lax.broadcasted_iota(I32, ds_ref.shape, 0)
        slane = lax.broadcasted_iota(I32, ds_ref.shape, 1)

        def both(x):
            return x + pltpu.roll(x, HEAD_DIM, 1)

        folded = []
        for half in range(2):
            lanes = slice(half * LANES, (half + 1) * LANES)
            head0 = (2 * b + half) * group
            qst = _mx(_stack_heads(q_ref, half * nqb * LANES, nqb))
            dost = _mx(_stack_heads(do_ref, half * nqb * LANES, nqb))
            k_loc = jnp.concatenate([kp[:, lanes], kc[:, lanes], kn[:, lanes]], axis=0)
            v_loc = jnp.concatenate([vp[:, lanes], vc[:, lanes], vn[:, lanes]], axis=0)
            k_ctx, v_ctx = kx[:, lanes], vx[:, lanes]
            e_loc, e_ctx, e_snk, inv = _exps(qst, k_loc, k_ctx, bias, _sink_column(sink_ref, head0, group))
            p_loc, p_ctx = e_loc * inv, e_ctx * inv
            dp_loc, dp_ctx = _dot_nt(dost, v_loc), _dot_nt(dost, v_ctx)
            dsum = jnp.sum(p_loc * dp_loc, axis=-1, keepdims=True) + jnp.sum(p_ctx * dp_ctx, axis=-1, keepdims=True)
            ds_loc, ds_ctx = _mx(p_loc * (dp_loc - dsum)), _mx(p_ctx * (dp_ctx - dsum))
            dq = (_dot(ds_loc, k_loc) + _dot(ds_ctx, k_ctx)) * (HEAD_DIM ** -0.5)
            for qb, blk in enumerate(_unstack_heads(dq, nqb)):
                c0 = (half * nqb + qb) * LANES
                dq_ref[:, c0:c0 + LANES] = blk
            p_loc, p_ctx = _mx(p_loc), _mx(p_ctx)
            folded.append((both(_dot_tn(ds_loc, qst)), both(_dot_tn(p_loc, dost)),
                           both(_dot_tn(ds_ctx, qst)), both(_dot_tn(p_ctx, dost))))
            dsnk = e_snk * inv * dsum
            for g in range(group):
                part = -jnp.sum(dsnk[g * QBLK:(g + 1) * QBLK])
                ds_ref[...] += jnp.where((srow == 0) & (slane == head0 + g), part, 0.0)
        lane_l = lax.broadcasted_iota(I32, (nloc, LANES), 1)
        lane_c = lax.broadcasted_iota(I32, (ctx_rows, LANES), 1)
        dkl_ref[...] = jnp.where(lane_l < HEAD_DIM, folded[0][0], folded[1][0])
        dvl_ref[...] = jnp.where(lane_l < HEAD_DIM, folded[0][1], folded[1][1])
        dkc_ref[...] += jnp.where(lane_c < HEAD_DIM, folded[0][2], folded[1][2])
        dvc_ref[...] += jnp.where(lane_c < HEAD_DIM, folded[0][3], folded[1][3])

    qspec = pl.BlockSpec((QBLK, 2 * group * HEAD_DIM), lambda b, i: (i, b))
    loc = pl.BlockSpec((None, nloc, LANES), lambda b, i: (i, 0, b))
    cspec = pl.BlockSpec((ctx_rows, LANES), lambda b, i: (0, b))
    band = pl.BlockSpec((group * QBLK, nloc), lambda b, i: (0, 0))
    return _call(
        body, grid=(N_KV // 2, nblk),
        in_specs=[pl.BlockSpec(memory_space=pltpu.SMEM), band, qspec, qspec] + kvspecs + kvspecs,
        out_specs=[qspec, loc, loc, cspec, cspec, pl.BlockSpec((8, LANES), lambda b, i: (0, 0))],
        out_shape=[jax.ShapeDtypeStruct((t, da), F32), jax.ShapeDtypeStruct((nblk, nloc, kw), F32),
                   jax.ShapeDtypeStruct((nblk, nloc, kw), F32), jax.ShapeDtypeStruct((ctx_rows, kw), F32),
                   jax.ShapeDtypeStruct((ctx_rows, kw), F32), jax.ShapeDtypeStruct((8, LANES), F32)],
        scratch=[], args=(sink, _band_bias(group * QBLK), qs, dao, k2, k2, k2, k2, v2, v2, v2, v2), name=name, side=side)


def _kv_reduce(dkl, dvl, dkc, dvc, *, ctx_rows, name):
    nblk, _, kw = dkl.shape
    t = nblk * QBLK
    ncb = ctx_rows // QBLK

    def clip(i):
        return jnp.clip(i, 0, nblk - 1)

    def body(ka, kb, kc, kx, va, vb, vc, vx, dk_ref, dv_ref):
        m = pl.program_id(0)
        lat = m >= ncb
        wa = jnp.where(lat & (m + 1 <= nblk - 1), 1.0, 0.0)
        wc = jnp.where(lat & (m - 1 >= ncb), 1.0, 0.0)
        wl = jnp.where(lat, 1.0, 0.0)
        dk_ref[...] = wl * (kb[...] + wa * ka[...] + wc * kc[...]) + (1.0 - wl) * kx[...]
        dv_ref[...] = wl * (vb[...] + wa * va[...] + wc * vc[...]) + (1.0 - wl) * vx[...]

    slots = [pl.BlockSpec((None, QBLK, kw), lambda m: (clip(m + 1), 0, 0)),
             pl.BlockSpec((None, QBLK, kw), lambda m: (m, 1, 0)),
             pl.BlockSpec((None, QBLK, kw), lambda m: (clip(m - 1), 2, 0))]
    cspec = pl.BlockSpec((QBLK, kw), lambda m: (jnp.clip(m, 0, ncb - 1), 0))
    out = pl.BlockSpec((QBLK, kw), lambda m: (m, 0))
    return pl.pallas_call(
        body, grid=(nblk,), in_specs=slots + [cspec] + slots + [cspec], out_specs=[out, out],
        out_shape=[jax.ShapeDtypeStruct((t, kw), F32)] * 2, compiler_params=_params(), name=name)(
            dkl, dkl, dkl, dkc, dvl, dvl, dvl, dvc)


MERGE_ROWS = 128


def _halo_specs(t, c, col):
    hb = MERGE_ROWS // HALO
    return [pl.BlockSpec((HALO, c), lambda i: (jnp.maximum(i * hb - 1, 0), col)),
            pl.BlockSpec((MERGE_ROWS, c), lambda i: (i, col)),
            pl.BlockSpec((HALO, c), lambda i: (jnp.minimum((i + 1) * hb, t // HALO - 1), col))]


def _ext(refs):
    return jnp.concatenate([r[...] for r in refs], axis=0)


def _conv_ext(cg, hh, w_ref, b_ref, i, t, ctx_rows):
    n = cg.shape[0]
    u = cg * hh
    row = i * MERGE_ROWS - HALO + lax.broadcasted_iota(I32, u.shape, 0)
    first = (row == 0) | (row == ctx_rows)
    last = (row == ctx_rows - 1) | (row == t - 1)
    u_dn = jnp.where(first, 0.0, pltpu.roll(u, 1, 0))
    u_up = jnp.where(last, 0.0, pltpu.roll(u, n - 1, 0))
    cv = w_ref[0:1, :] * u_dn + w_ref[1:2, :] * u + w_ref[2:3, :] * u_up + b_ref[...]
    return u, u_dn, u_up, cv, first, last


def _merge_fwd(p, ao, conv_w, conv_b, g_oc, g_oa, *, ctx_rows, name):
    t = p.shape[0]
    c = ao.shape[1]
    main = slice(HALO, HALO + MERGE_ROWS)

    def body(bg_ref, cgp, cgm, cgn, hhp, hhm, hhn, ao_ref, w_ref, b_ref, gc_ref, ga_ref, o_ref):
        i = pl.program_id(0)
        _, _, _, cv, _, _ = _conv_ext(_ext((cgp, cgm, cgn)), _ext((hhp, hhm, hhn)), w_ref, b_ref, i, t, ctx_rows)
        co = bg_ref[...] * cv[main]
        rc = lax.rsqrt(jnp.mean(co * co, axis=-1, keepdims=True) + EPS)
        o_ref[:, :c] = (co * rc * gc_ref[...]).astype(o_ref.dtype)
        av = ao_ref[...]
        ra = lax.rsqrt(jnp.mean(av * av, axis=-1, keepdims=True) + EPS)
        o_ref[:, c:] = (av * ra * ga_ref[...]).astype(o_ref.dtype)

    vec = pl.BlockSpec((1, c), lambda i: (0, 0))
    return pl.pallas_call(
        body, grid=(t // MERGE_ROWS,),
        in_specs=[pl.BlockSpec((MERGE_ROWS, c), lambda i: (i, 0))] + _halo_specs(t, c, 1) + _halo_specs(t, c, 2)
        + [pl.BlockSpec((MERGE_ROWS, c), lambda i: (i, 0)), pl.BlockSpec((3, c), lambda i: (0, 0)), vec, vec, vec],
        out_specs=pl.BlockSpec((MERGE_ROWS, 2 * c), lambda i: (i, 0)),
        out_shape=jax.ShapeDtypeStruct((t, 2 * c), MXU_DTYPE), compiler_params=_params(), name=name)(
            p, p, p, p, p, p, p, ao, conv_w, conv_b, g_oc, g_oa)


R_DGOC, R_DGOA, R_DCB, R_DCW = 0, 1, 2, 3


def _merge_bwd(dmg, p, ao, conv_w, conv_b, g_oc, g_oa, *, width, ctx_rows, name):
    t = p.shape[0]
    c = ao.shape[1]
    main = slice(HALO, HALO + MERGE_ROWS)

    def body(dyp, dym, dyn, dya_ref, bgp, bgm, bgn, cgp, cgm, cgn, hhp, hhm, hhn, ao_ref, w_ref, b_ref, gc_ref, ga_ref,
             dp_ref, dao_ref, acc):
        i = pl.program_id(0)

        @pl.when(i == 0)
        def _():
            acc[...] = jnp.zeros_like(acc)

        bg, cg, hh = _ext((bgp, bgm, bgn)), _ext((cgp, cgm, cgn)), _ext((hhp, hhm, hhn))
        n = bg.shape[0]
        u, u_dn, u_up, cv, first, last = _conv_ext(cg, hh, w_ref, b_ref, i, t, ctx_rows)
        co = bg * cv
        rc = lax.rsqrt(jnp.mean(co * co, axis=-1, keepdims=True) + EPS)
        cohat = co * rc
        dyc = _ext((dyp, dym, dyn))
        t1 = dyc * gc_ref[...]
        dco = rc * (t1 - cohat * jnp.mean(t1 * cohat, axis=-1, keepdims=True))
        dcv = dco * bg
        dcv_next = jnp.where(last, 0.0, pltpu.roll(dcv, n - 1, 0))
        dcv_prev = jnp.where(first, 0.0, pltpu.roll(dcv, 1, 0))
        du = w_ref[1:2, :] * dcv + w_ref[0:1, :] * dcv_next + w_ref[2:3, :] * dcv_prev
        dp_ref[:, :c] = (dco * cv)[main].astype(dp_ref.dtype)
        dp_ref[:, c:2 * c] = (du * hh)[main].astype(dp_ref.dtype)
        dp_ref[:, 2 * c:] = (du * cg)[main].astype(dp_ref.dtype)
        dcv_m = dcv[main]
        acc[R_DGOC:R_DGOC + 1, :] += jnp.sum((dyc * cohat)[main], axis=0, keepdims=True)
        acc[R_DCB:R_DCB + 1, :] += jnp.sum(dcv_m, axis=0, keepdims=True)
        acc[R_DCW:R_DCW + 1, :] += jnp.sum(dcv_m * u_dn[main], axis=0, keepdims=True)
        acc[R_DCW + 1:R_DCW + 2, :] += jnp.sum(dcv_m * u[main], axis=0, keepdims=True)
        acc[R_DCW + 2:R_DCW + 3, :] += jnp.sum(dcv_m * u_up[main], axis=0, keepdims=True)
        av = ao_ref[...]
        ra = lax.rsqrt(jnp.mean(av * av, axis=-1, keepdims=True) + EPS)
        ahat = av * ra
        dya = dya_ref[...]
        t2 = dya * ga_ref[...]
        dao_ref[...] = ra * (t2 - ahat * jnp.mean(t2 * ahat, axis=-1, keepdims=True))
        acc[R_DGOA:R_DGOA + 1, :] += jnp.sum(dya * ahat, axis=0, keepdims=True)

    vec = pl.BlockSpec((1, c), lambda i: (0, 0))
    tile = pl.BlockSpec((MERGE_ROWS, c), lambda i: (i, 0))
    return pl.pallas_call(
        body, grid=(t // MERGE_ROWS,),
        in_specs=_halo_specs(t, c, 0) + [pl.BlockSpec((MERGE_ROWS, c), lambda i: (i, 1))]
        + _halo_specs(t, c, 0) + _halo_specs(t, c, 1) + _halo_specs(t, c, 2)
        + [tile, pl.BlockSpec((3, c), lambda i: (0, 0)), vec, vec, vec],
        out_specs=[pl.BlockSpec((MERGE_ROWS, 3 * c), lambda i: (i, 0)), tile, pl.BlockSpec((8, c), lambda i: (0, 0))],
        out_shape=[jax.ShapeDtypeStruct((t, width), MXU_DTYPE), jax.ShapeDtypeStruct((t, c), F32),
                   jax.ShapeDtypeStruct((8, c), F32)],
        compiler_params=_params(), name=name)(dmg, dmg, dmg, dmg, p, p, p, p, p, p, p, p, p, ao, conv_w, conv_b, g_oc, g_oa)


def _local_step(x0, tgt, mods, wts, small, *, ctx_rows, comm=None):
    t, d = x0.shape
    depth = mods.shape[0]
    c = d // 2
    da = d - c
    kw = N_KV * HEAD_DIM
    kv_start = 3 * c + da
    shards = N_CHIP
    tabs = _rope_tables(t - ctx_rows, ctx_rows)
    kwargs = dict(ctx_rows=ctx_rows)

    saved = []
    xs, z_prev = x0, None
    have = {}

    def riding(l, host):
        want = {"in": [(l, 1), (l + 1, 0)], "attn": [(l, 2)], "out": [], "mlp1": [(l, 3)], "mlp2": []}
        keys = [key for key in want[host] if key[0] < depth] if comm is not None else []
        return keys, (_gather_side([comm.halves[a][k] for a, k in keys]) if keys else None)

    def weight(l, k):
        return wts[l][k] if comm is None else comm.weight(k, have[(l, k)])

    if comm is not None:
        have[(0, 0)] = _run_side(_gather_side([comm.halves[0][0]]), "gather_weights_0")[0]
    for l in range(depth):
        if z_prev is None:
            x_in = xs
            h1 = _norm_fwd(xs, None, None, mods[l], small["g_norm1"][l], gate_idx=None, shift_idx=0, scale_idx=1,
                           name=f"norm1_fwd_{l}", **kwargs)
        else:
            x_in, h1 = _norm_fwd(xs, z_prev, mods[l - 1], mods[l], small["g_norm1"][l], gate_idx=5, shift_idx=0,
                                 scale_idx=1, name=f"norm1_fwd_{l}", **kwargs)
        keys, side = riding(l, "in")
        (p,), got = _mm_nn(h1, weight(l, 0), out_dtypes=(F32,), epilogue=lambda v: (v,), name=f"in_proj_{l}", side=side)
        have.update(zip(keys, got))
        qs, k2, v2 = _rope_fwd(p, tabs, da=da, kw=kw, kv_start=kv_start, name=f"rope_fwd_{l}")
        keys, side = riding(l, "attn")
        ao, got = _attn_fwd(qs, k2, v2, small["sink"][l], name=f"attn_fwd_{l}", side=side, **kwargs)
        have.update(zip(keys, got))
        mg = _merge_fwd(p, ao, small["conv_w"][l], small["conv_b"][l], small["g_out_conv"][l], small["g_out_attn"][l],
                        name=f"merge_fwd_{l}", **kwargs)
        keys, side = riding(l, "out")
        (z,), got = _mm_nn(mg, weight(l, 1), out_dtypes=(MXU_DTYPE,), epilogue=lambda v: (v,), name=f"out_proj_{l}", side=side)
        have.update(zip(keys, got))
        x_mid, h2 = _norm_fwd(x_in, z, mods[l], mods[l], small["g_norm2"][l], gate_idx=2, shift_idx=3, scale_idx=4,
                              name=f"norm2_fwd_{l}", **kwargs)
        keys, side = riding(l, "mlp1")
        (a_act, s_act), got = _mm_nn(h2, weight(l, 2), out_dtypes=(MXU_DTYPE, MXU_DTYPE),
                                     epilogue=lambda v: (v, jnp.square(jnp.maximum(v, 0.0))), name=f"mlp1_{l}", side=side)
        have.update(zip(keys, got))
        keys, side = riding(l, "mlp2")
        (o,), got = _mm_nn(s_act, weight(l, 3), out_dtypes=(MXU_DTYPE,), epilogue=lambda v: (v,), name=f"mlp2_{l}", side=side)
        have.update(zip(keys, got))
        saved.append(dict(x_in=x_in, h1=h1, p=p, qs=qs, k2=k2, v2=v2, ao=ao, mg=mg, z=z, x_mid=x_mid, h2=h2, a=a_act, s=s_act, o=o))
        xs, z_prev = x_mid, o

    dx, do, fin = _final(xs, z_prev, mods[depth - 1], small["g_final"], tgt, name="final", **kwargs)

    grads = [None] * depth
    dmods = [[None] * N_MOD for _ in range(depth)]
    sg = {k: [None] * depth for k in ("g_norm1", "g_norm2", "conv_w", "conv_b", "sink", "g_out_conv", "g_out_attn")}
    dmods[depth - 1][5] = fin[R_FGATE:R_FGATE + 2]
    sync = None
    for l in reversed(range(depth)):
        w_in, w_out, w1, w2 = (weight(l, k) for k in range(4))
        sv = saved[l]
        (da_act,), got = _mm_nt(do, w2, out_dtype=MXU_DTYPE, extra=sv["a"],
                                epilogue=lambda v, a: v * (2.0 * jnp.maximum(a.astype(F32), 0.0)), name=f"mlp2_dx_{l}",
                                side=sync and sync.pair_side())
        sync and sync.add(got)
        (g_w2,), got = _mm_tn(sv["s"], do, shards=1, name=f"mlp2_dw_{l}", side=sync and sync.chips_side((2,)))
        sync and sync.land((2,), got)
        (dh2,), got = _mm_nt(da_act, w1, out_dtype=MXU_DTYPE, name=f"mlp1_dx_{l}", side=sync and sync.chips_side((3,)))
        sync and sync.land((3,), got)
        (g_w1,), got = _mm_tn(sv["h2"], da_act, shards=shards, name=f"mlp1_dw_{l}", side=sync and sync.chips_side((0, 1)))
        sync and sync.land((0, 1), got)
        sync and sync.sum()
        own = _GradSync(comm, l, {2: g_w1, 3: g_w2}) if comm is not None and l == 0 else None
        dx, dz, sums2 = _norm_bwd(dx, dh2, sv["x_mid"], mods[l], small["g_norm2"][l], scale_idx=4,
                                  prev=(sv["z"], mods[l], 2), name=f"norm2_bwd_{l}", **kwargs)
        (dmg,), got = _mm_nt(dz, w_out, out_dtype=F32, name=f"out_proj_dx_{l}", side=own and own.pair_side())
        own and own.add(got)
        (g_wo,), _ = _mm_tn(sv["mg"], dz, shards=1, name=f"out_proj_dw_{l}")
        qkv_w = da + 2 * kw
        in_place = (3 * c) % qkv_w == 0
        dpc, dao, msum = _merge_bwd(dmg, sv["p"], sv["ao"], small["conv_w"][l], small["conv_b"][l],
                                    small["g_out_conv"][l], small["g_out_attn"][l],
                                    width=3 * c + qkv_w if in_place else 3 * c, name=f"merge_bwd_{l}", **kwargs)
        (dq, dkl, dvl, dkc, dvc, dsink), got = _attn_bwd(sv["qs"], sv["k2"], sv["v2"], dao, small["sink"][l],
                                                         name=f"attn_bwd_{l}", side=own and own.chips_side(own.keys), **kwargs)
        own and own.land(own.keys, got)
        own and own.sum()
        dk, dv = _kv_reduce(dkl, dvl, dkc, dvc, ctx_rows=ctx_rows, name=f"kv_reduce_{l}")
        if in_place:
            dp = _rope_bwd(dq, dk, dv, tabs, dpc, name=f"rope_bwd_{l}")
        else:
            dqkv = _rope_bwd(dq, dk, dv, tabs, lax.empty((t, qkv_w), MXU_DTYPE), name=f"rope_bwd_{l}")
            dp = jnp.concatenate([dpc, dqkv], axis=1)
        (dh1,), got = _mm_nt(dp, w_in, out_dtype=MXU_DTYPE, name=f"in_proj_dx_{l}", side=sync and sync.share_side())
        sync and sync.adam(got)
        (g_wi,), got = _mm_tn(sv["h1"], dp, shards=shards, name=f"in_proj_dw_{l}", side=own and own.share_side())
        own and own.adam(got)
        if comm is not None:
            sync = _GradSync(comm, l, {0: g_wi, 1: g_wo} if own else {0: g_wi, 1: g_wo, 2: g_w1, 3: g_w2})
        if l > 0:
            dx, do, sums1 = _norm_bwd(dx, dh1, sv["x_in"], mods[l], small["g_norm1"][l], scale_idx=1,
                                      prev=(saved[l - 1]["o"], mods[l - 1], 5), name=f"norm1_bwd_{l}", **kwargs)
            dmods[l - 1][5] = sums1[R_DGATE:R_DGATE + 2]
        elif sync is not None:
            sync.add(_run_side(sync.pair_side(), "rs_pair_last"))
            (dx, sums1), got = _norm_bwd(dx, dh1, sv["x_in"], mods[l], small["g_norm1"][l], scale_idx=1,
                                         name=f"norm1_bwd_{l}", side=sync.chips_side((0,)), **kwargs)
            sync.land((0,), got)
        else:
            dx, sums1 = _norm_bwd(dx, dh1, sv["x_in"], mods[l], small["g_norm1"][l], scale_idx=1,
                                  name=f"norm1_bwd_{l}", **kwargs)
        grads[l] = (g_wi, g_wo, g_w1, g_w2)
        dmods[l][0] = sums1[R_DSHIFT:R_DSHIFT + 2]
        dmods[l][1] = sums1[R_DSCALE:R_DSCALE + 2]
        dmods[l][2] = sums2[R_DGATE:R_DGATE + 2]
        dmods[l][3] = sums2[R_DSHIFT:R_DSHIFT + 2]
        dmods[l][4] = sums2[R_DSCALE:R_DSCALE + 2]
        sg["g_norm1"][l] = sums1[R_DG]
        sg["g_norm2"][l] = sums2[R_DG]
        sg["g_out_conv"][l] = msum[R_DGOC]
        sg["g_out_attn"][l] = msum[R_DGOA]
        sg["conv_b"][l] = msum[R_DCB]
        sg["conv_w"][l] = msum[R_DCW:R_DCW + 3]
        sg["sink"][l] = dsink[0, :da // HEAD_DIM]
    dmods = jnp.stack([jnp.stack(row, axis=1) for row in dmods])
    sg = {k: jnp.stack(v) for k, v in sg.items()}
    sg["g_final"] = fin[R_FG]
    return fin[R_FLOSS, 0], dx, grads, dmods, sg, sync


N_DEV = 8
N_CHIP = 4


def _place():
    mx, my, mc = lax.axis_index("x"), lax.axis_index("y"), lax.axis_index("c")
    others = [(1 - mx, my), (mx, 1 - my), (1 - mx, 1 - my)]
    return mx, my, mc, others


def _remote(src, dst, send_sems, recv_sems, k, dev):
    return pltpu.make_async_remote_copy(src_ref=src, dst_ref=dst, send_sem=send_sems.at[k], recv_sem=recv_sems.at[k],
                                        device_id=dev, device_id_type=MESH)


def _allgather8(x, name):
    r, ccols = x.shape

    def body(x_ref, out_ref, send_sems, recv_sems, local_sem):
        mx, my, mc, _ = _place()
        me = 4 * mx + 2 * my + mc
        mine = pltpu.make_async_copy(x_ref, out_ref.at[me], local_sem)
        mine.start()
        sent = []
        for k in range(1, N_DEV):
            fx, fy, fc = (k >> 2) & 1, (k >> 1) & 1, k & 1
            px, py, pc = (1 - mx if fx else mx), (1 - my if fy else my), (1 - mc if fc else mc)
            cp = _remote(x_ref, out_ref.at[me], send_sems, recv_sems, k - 1, (px, py, pc))
            cp.start()
            sent.append((cp, 4 * px + 2 * py + pc, (px, py, pc)))
        for k, (cp, peer, dev) in enumerate(sent):
            _remote(x_ref, out_ref.at[peer], send_sems, recv_sems, k, dev).wait_recv()
        for cp, _, _ in sent:
            cp.wait_send()
        mine.wait()

    vm = pl.BlockSpec(memory_space=pltpu.VMEM)
    return pl.pallas_call(
        body, in_specs=[vm], out_specs=vm, out_shape=jax.ShapeDtypeStruct((N_DEV, r, ccols), x.dtype),
        scratch_shapes=[pltpu.SemaphoreType.DMA((N_DEV - 1,)), pltpu.SemaphoreType.DMA((N_DEV - 1,)),
                        pltpu.SemaphoreType.DMA], name=name)(x)


def _gather_side(halves):
    n = len(halves)

    def copies(ins, outs, sems):
        send_sems, recv_sems, local_sems = sems
        mx, my, mc, others = _place()
        chip = 2 * mx + my
        sib = (mx, my, 1 - mc)

        def src(w):
            return ins[w].at[pl.ds(mc, 1)]

        def slot(w, ch, core):
            return outs[w].at[ch, pl.ds(core, 1)]

        locs = [pltpu.make_async_copy(src(w), slot(w, chip, mc), local_sems.at[w]) for w in range(n)]
        first, landed, passed, from_sib = [], [], [], []
        for w in range(n):
            first.append(_remote(src(w), slot(w, chip, mc), send_sems, recv_sems, 7 * w, sib))
            from_sib.append(_remote(src(w), slot(w, chip, 1 - mc), send_sems, recv_sems, 7 * w, sib))
            for j, (ox, oy) in enumerate(others):
                och = 2 * ox + oy
                first.append(_remote(src(w), slot(w, chip, mc), send_sems, recv_sems, 7 * w + 1 + j, (ox, oy, mc)))
                landed.append(_remote(src(w), slot(w, och, mc), send_sems, recv_sems, 7 * w + 1 + j, (ox, oy, mc)))
                passed.append(_remote(slot(w, och, mc), slot(w, och, mc), send_sems, recv_sems, 7 * w + 4 + j, sib))
                from_sib.append(_remote(src(w), slot(w, och, 1 - mc), send_sems, recv_sems, 7 * w + 4 + j, sib))
        return locs, first, landed, passed, from_sib

    def start(ins, outs, sems):
        locs, first, _, _, _ = copies(ins, outs, sems)
        for cp in locs + first:
            cp.start()

    def middle(ins, outs, sems):
        _, _, landed, passed, _ = copies(ins, outs, sems)
        for cp, fw in zip(landed, passed):
            cp.wait_recv()
            fw.start()

    def finish(ins, outs, sems):
        locs, first, _, passed, from_sib = copies(ins, outs, sems)
        for cp in from_sib:
            cp.wait_recv()
        for cp in first + passed:
            cp.wait_send()
        for cp in locs:
            cp.wait()

    return _Side(halves, [jax.ShapeDtypeStruct((N_CHIP,) + h.shape, h.dtype) for h in halves],
                 [pltpu.SemaphoreType.DMA((7 * n,)), pltpu.SemaphoreType.DMA((7 * n,)), pltpu.SemaphoreType.DMA((n,))],
                 start, finish, middle=middle)


def _pair_side(gs):
    n = len(gs)

    def copies(ins, outs, sems):
        mx, my, mc, _ = _place()
        return [_remote(ins[w].at[:, pl.ds(1 - mc, 1)], outs[w], sems[0], sems[1], w, (mx, my, 1 - mc)) for w in range(n)]

    def start(ins, outs, sems):
        for cp in copies(ins, outs, sems):
            cp.start()

    def finish(ins, outs, sems):
        for cp in copies(ins, outs, sems):
            cp.wait()

    return _Side(gs, [jax.ShapeDtypeStruct((g.shape[0], 1) + g.shape[2:], g.dtype) for g in gs],
                 [pltpu.SemaphoreType.DMA((n,)), pltpu.SemaphoreType.DMA((n,))], start, finish)


def _pair_add(g, got, core, name):
    s, _, rh, ccols = g.shape
    tr = _pick(rh, (256, 128))

    def body(core_ref, g_ref, r_ref, o_ref):
        o_ref[...] = (g_ref[...].astype(F32) + r_ref[...].astype(F32)).astype(o_ref.dtype)

    spec = pltpu.PrefetchScalarGridSpec(
        num_scalar_prefetch=1, grid=(s, rh // tr),
        in_specs=[pl.BlockSpec((None, None, tr, ccols), lambda a, i, cr: (a, cr[0], i, 0)),
                  pl.BlockSpec((None, None, tr, ccols), lambda a, i, cr: (a, 0, i, 0))],
        out_specs=pl.BlockSpec((None, tr, ccols), lambda a, i, cr: (a, i, 0)))
    return pl.pallas_call(body, grid_spec=spec, out_shape=jax.ShapeDtypeStruct((s, rh, ccols), MXU_DTYPE),
                          compiler_params=_params(), name=name)(core, g, got)


def _chips_side(ps):
    n = len(ps)

    def copies(ins, outs, sems):
        send_sems, recv_sems, local_sems = sems
        mx, my, mc, others = _place()
        chip = 2 * mx + my
        locs = [pltpu.make_async_copy(ins[w].at[chip], outs[w].at[chip], local_sems.at[w]) for w in range(n)]
        sends, lands = [], []
        for w in range(n):
            for j, (ox, oy) in enumerate(others):
                och = 2 * ox + oy
                sends.append(_remote(ins[w].at[och], outs[w].at[chip], send_sems, recv_sems, 3 * w + j, (ox, oy, mc)))
                lands.append(_remote(ins[w].at[och], outs[w].at[och], send_sems, recv_sems, 3 * w + j, (ox, oy, mc)))
        return locs, sends, lands

    def start(ins, outs, sems):
        locs, sends, _ = copies(ins, outs, sems)
        for cp in locs + sends:
            cp.start()

    def finish(ins, outs, sems):
        locs, sends, lands = copies(ins, outs, sems)
        for cp in lands:
            cp.wait_recv()
        for cp in sends:
            cp.wait_send()
        for cp in locs:
            cp.wait()

    return _Side(ps, [jax.ShapeDtypeStruct(p.shape, p.dtype) for p in ps],
                 [pltpu.SemaphoreType.DMA((3 * n,)), pltpu.SemaphoreType.DMA((3 * n,)), pltpu.SemaphoreType.DMA((n,))],
                 start, finish)


def _chip_sum(rb, core, name):
    s, rh, ccols = rb.shape
    tr = _pick(rh, (256, 128))

    def body(core_ref, r_ref, o_ref):
        tot = r_ref[0].astype(F32)
        for k in range(1, s):
            tot = tot + r_ref[k].astype(F32)
        o_ref[...] = tot

    spec = pltpu.PrefetchScalarGridSpec(
        num_scalar_prefetch=1, grid=(rh // tr,),
        in_specs=[pl.BlockSpec((s, tr, ccols), lambda i, cr: (0, i, 0))],
        out_specs=pl.BlockSpec((None, tr, ccols), lambda i, cr: (cr[0], i, 0)))
    return pl.pallas_call(body, grid_spec=spec, out_shape=jax.ShapeDtypeStruct((2, rh, ccols), F32),
                          compiler_params=_params(), name=name)(core, rb)


def _share_side(fulls):
    n = len(fulls)

    def copies(ins, outs, sems):
        mx, my, mc, _ = _place()
        sib = (mx, my, 1 - mc)
        sends = [_remote(ins[w].at[mc], outs[w].at[mc], sems[0], sems[1], w, sib) for w in range(n)]
        lands = [_remote(ins[w].at[mc], outs[w].at[1 - mc], sems[0], sems[1], w, sib) for w in range(n)]
        return sends, lands

    def start(ins, outs, sems):
        for cp in copies(ins, outs, sems)[0]:
            cp.start()

    def finish(ins, outs, sems):
        sends, lands = copies(ins, outs, sems)
        for cp in lands:
            cp.wait_recv()
        for cp in sends:
            cp.wait_send()

    return _Side(fulls, [jax.ShapeDtypeStruct(f.shape, f.dtype) for f in fulls],
                 [pltpu.SemaphoreType.DMA((n,)), pltpu.SemaphoreType.DMA((n,))], start, finish,
                 aliases=[(w, w) for w in range(n)])


def _cast(w, l, name):
    _, r, ccols = w.shape
    tr = _pick(r, (256, 128))

    def body(w_ref, o_ref):
        o_ref[...] = w_ref[...].astype(o_ref.dtype)

    return pl.pallas_call(body, grid=(r // tr,), in_specs=[pl.BlockSpec((None, tr, ccols), lambda i: (l, i, 0))],
                          out_specs=pl.BlockSpec((tr, ccols), lambda i: (i, 0)),
                          out_shape=jax.ShapeDtypeStruct((r, ccols), MXU_DTYPE), compiler_params=_params(), name=name)(w)


def _adam_math(g, w, m, v):
    m = ADAM_B1 * m + (1.0 - ADAM_B1) * g
    v = ADAM_B2 * v + (1.0 - ADAM_B2) * jnp.square(g)
    m_hat = m / (1.0 - ADAM_B1 ** ADAM_STEP)
    v_hat = v / (1.0 - ADAM_B2 ** ADAM_STEP)
    return -ADAM_LR * (m_hat / (jnp.sqrt(v_hat) + ADAM_EPS) + ADAM_WD * w), m, v


def _adamw_layer(l, g, w, m, v, bufs, name, side=None):
    depth, r, ccols = w.shape
    tr = _pick(r, (128,))

    def body(g_ref, w_ref, m_ref, v_ref, b0, b1, b2, b3, go_ref, d_ref, mo_ref, vo_ref):
        gv = g_ref[...]
        d, m2, v2 = _adam_math(gv, w_ref[...], m_ref[...], v_ref[...])
        go_ref[...] = gv
        d_ref[...] = d
        mo_ref[...] = m2
        vo_ref[...] = v2

    lay = pl.BlockSpec((None, tr, ccols), lambda i: (l, i, 0))
    return _call(
        body, grid=(r // tr,), in_specs=[pl.BlockSpec((tr, ccols), lambda i: (i, 0)), lay, lay, lay] + [ANY] * 4,
        out_specs=[lay] * 4, out_shape=[jax.ShapeDtypeStruct((depth, r, ccols), F32)] * 4, scratch=[],
        args=(g, w, m, v, *bufs), name=name, side=side, aliases={4: 0, 5: 1, 6: 2, 7: 3})


def _adamw_small(g, g2, w, m, v, name):
    two = g2 is not None

    def body(*refs):
        if two:
            g_ref, g2_ref, w_ref, m_ref, v_ref, go_ref, d_ref, mo_ref, vo_ref = refs
            gv = g_ref[...] + g2_ref[...]
        else:
            g_ref, w_ref, m_ref, v_ref, go_ref, d_ref, mo_ref, vo_ref = refs
            gv = g_ref[...]
        d, m2, v2 = _adam_math(gv, w_ref[...], m_ref[...], v_ref[...])
        go_ref[...] = gv
        d_ref[...] = d
        mo_ref[...] = m2
        vo_ref[...] = v2

    args = [g] + ([g2] if two else []) + [w, m, v]
    vm = pl.BlockSpec(memory_space=pltpu.VMEM)
    return pl.pallas_call(body, in_specs=[vm] * len(args), out_specs=[vm] * 4,
                          out_shape=[jax.ShapeDtypeStruct(w.shape, F32)] * 4, name=name)(*args)


def _sum8(g, name):
    def body(g_ref, o_ref):
        tot = g_ref[0]
        for k in range(1, N_DEV):
            tot = tot + g_ref[k]
        o_ref[...] = tot

    vm = pl.BlockSpec(memory_space=pltpu.VMEM)
    return pl.pallas_call(body, in_specs=[vm], out_specs=vm, out_shape=jax.ShapeDtypeStruct(g.shape[1:], F32),
                          compiler_params=_params(), name=name)(g)


def _pack(arrs, width):
    flat = jnp.concatenate([a.reshape(-1).astype(F32) for a in arrs])
    rows = -(-flat.size // (8 * width)) * 8
    return jnp.pad(flat, (0, rows * width - flat.size)).reshape(rows, width)


def _unpack(flat, shapes):
    out, off = [], 0
    for shp in shapes:
        size = 1
        for v in shp:
            size *= v
        out.append(flat[..., off:off + size].reshape(flat.shape[:-1] + tuple(shp)))
        off += size
    return out


class _Comm:
    def __init__(self, core, params):
        self.core, self.params = core, params
        depth = params[0][1].shape[0]
        self.halves = [[_cast(w, l, f"cast_{nm}_{l}").reshape(2, w.shape[1] // 2, w.shape[2]) for nm, w, _, _ in params]
                       for l in range(depth)]
        self.stacked = [[lax.empty(w.shape, F32) for _ in range(4)] for _, w, _, _ in params]

    def weight(self, k, gathered):
        rows, cols = 2 * gathered.shape[2], gathered.shape[3]
        return gathered.reshape(N_CHIP, rows, cols) if k % 2 == 0 else gathered.reshape(1, N_CHIP * rows, cols)


class _GradSync:
    def __init__(self, comm, l, grads):
        self.comm, self.l, self.keys = comm, l, tuple(sorted(grads))
        self.gs = {k: g.reshape(N_CHIP, 2, g.shape[0] * g.shape[1] // (2 * N_CHIP), g.shape[2]) for k, g in grads.items()}
        self.ps, self.rb, self.full = {}, {}, {}

    def pair_side(self):
        return _pair_side([self.gs[k] for k in self.keys])

    def add(self, got):
        for k, r in zip(self.keys, got):
            self.ps[k] = _pair_add(self.gs[k], r, self.comm.core, f"rs_add_{self.l}_{k}")

    def chips_side(self, which):
        return _chips_side([self.ps[k] for k in which])

    def land(self, which, got):
        self.rb.update(zip(which, got))

    def sum(self):
        for k in self.keys:
            self.full[k] = _chip_sum(self.rb[k], self.comm.core, f"rs_sum_{self.l}_{k}")

    def share_side(self):
        return _share_side([self.full[k] for k in self.keys])

    def adam(self, got):
        for k, full in zip(self.keys, got):
            nm, w, m, v = self.comm.params[k]
            gsum = full.reshape(2 * full.shape[1], full.shape[2])
            self.comm.stacked[k], _ = _adamw_layer(self.l, gsum, w, m, v, self.comm.stacked[k], f"adamw_{nm}_{self.l}")


COND_ROWS = 16


def _ada_fwd(cond, w_ada, b_cols, name):
    depth, d, ns = w_ada.shape
    tn = _pick(ns, (512, 384, 256, 128))

    def body(c_ref, w_ref, b_ref, o_ref):
        o_ref[...] = _dot(_silu(c_ref[...]), w_ref[...]) + b_ref[...]

    return pl.pallas_call(
        body, grid=(depth, ns // tn),
        in_specs=[pl.BlockSpec((COND_ROWS, d), lambda l, j: (0, 0)), pl.BlockSpec((None, d, tn), lambda l, j: (l, 0, j)),
                  pl.BlockSpec((None, 1, tn), lambda l, j: (l, 0, j))],
        out_specs=pl.BlockSpec((None, COND_ROWS, tn), lambda l, j: (l, 0, j)),
        out_shape=jax.ShapeDtypeStruct((depth, COND_ROWS, ns), F32), compiler_params=_params(), name=name)(cond, w_ada, b_cols)


def _ada_bwd(cond, dmod, w_ada, name, side=None):
    depth, d, ns = w_ada.shape
    tn = _pick(ns, (512, 384, 256, 128))

    def body(c_ref, dm_ref, w_ref, gw_ref, dc_ref):
        @pl.when((pl.program_id(0) == 0) & (pl.program_id(1) == 0))
        def _():
            dc_ref[...] = jnp.zeros_like(dc_ref)

        dm = dm_ref[...]
        gw_ref[...] = _dot_tn(_silu(c_ref[...]), dm)
        dc_ref[...] += _dot_nt(dm, w_ref[...])

    return _call(
        body, grid=(depth, ns // tn),
        in_specs=[pl.BlockSpec((COND_ROWS, d), lambda l, j: (0, 0)),
                  pl.BlockSpec((None, COND_ROWS, tn), lambda l, j: (l, 0, j)),
                  pl.BlockSpec((None, d, tn), lambda l, j: (l, 0, j))],
        out_specs=[pl.BlockSpec((None, d, tn), lambda l, j: (l, 0, j)), pl.BlockSpec((COND_ROWS, d), lambda l, j: (0, 0))],
        out_shape=[jax.ShapeDtypeStruct((depth, d, ns), F32), jax.ShapeDtypeStruct((COND_ROWS, d), F32)],
        scratch=[], args=(cond, dmod, w_ada), name=name, side=side)


def _cctx_grad(parts, c_ctx, name):
    def body(p_ref, c_ref, o_ref):
        tot = p_ref[0, 0:1, :]
        for k in range(1, N_CHIP):
            tot = tot + p_ref[2 * k, 0:1, :]
        z = c_ref[...]
        sg = 1.0 / (1.0 + jnp.exp(-z))
        o_ref[...] = tot * (sg + z * sg * (1.0 - sg))

    vm = pl.BlockSpec(memory_space=pltpu.VMEM)
    return pl.pallas_call(body, in_specs=[vm, vm], out_specs=vm, out_shape=jax.ShapeDtypeStruct(c_ctx.shape, F32),
                          name=name)(parts, c_ctx)


def kernel(x, c, ctx, c_ctx, w_ada, b_ada, g_norm1, g_norm2, w_in, conv_w, conv_b, sink, g_out_conv, g_out_attn, w_out, w_mlp1, w_mlp2, g_final, loss_target, m_c_ctx, m_w_ada, m_b_ada, m_g_norm1, m_g_norm2, m_w_in, m_conv_w, m_conv_b, m_sink, m_g_out_conv, m_g_out_attn, m_w_out, m_w_mlp1, m_w_mlp2, m_g_final, v_c_ctx, v_w_ada, v_b_ada, v_g_norm1, v_g_norm2, v_w_in, v_conv_w, v_conv_b, v_sink, v_g_out_conv, v_g_out_attn, v_w_out, v_w_mlp1, v_w_mlp2, v_g_final):
    mx, my, mc = lax.axis_index("x"), lax.axis_index("y"), lax.axis_index("c")
    chip, rank = 2 * mx + my, 4 * mx + 2 * my + mc
    core = jnp.reshape(mc, (1,)).astype(I32)
    depth, d = g_norm1.shape
    s_len, ctx_rows = x.shape[1], ctx.shape[1]
    cw_cols = conv_w.shape[2]
    c_conv = cw_cols * N_CHIP
    n_heads = sink.shape[1]
    ns_ada = w_ada.shape[2]

    got = _allgather8(_pack([c, conv_w], d), "gather_cond")
    flat = got.reshape(N_DEV, -1)
    conv_w_full = jnp.transpose(flat[::2, d:d + conv_w.size].reshape(N_CHIP, depth, 3, cw_cols), (1, 2, 0, 3))
    conv_w_full = conv_w_full.reshape(depth, 3, c_conv)
    cond = jnp.zeros((COND_ROWS, d), F32).at[:N_DEV].set(flat[:, :d]).at[N_DEV].set(c_ctx)

    b_cols = lax.dynamic_slice_in_dim(b_ada, chip * ns_ada, ns_ada, axis=1)[:, None, :]
    mod_cols = _ada_fwd(cond, w_ada, b_cols, "ada_fwd")
    got = _allgather8(mod_cols.reshape(depth * COND_ROWS, ns_ada), "gather_mod")
    mod_all = jnp.transpose(got[::2].reshape(N_CHIP, depth, COND_ROWS, ns_ada), (1, 2, 0, 3))
    mod_all = mod_all.reshape(depth, COND_ROWS, N_CHIP * ns_ada)
    mod_me = lax.dynamic_index_in_dim(mod_all, rank, axis=1, keepdims=False)
    mods = jnp.stack([mod_all[:, N_DEV], mod_me], axis=1).reshape(depth, 2, N_MOD, 1, d)

    comm = _Comm(core, (("w_in", w_in, m_w_in, v_w_in), ("w_out", w_out, m_w_out, v_w_out),
                        ("w_mlp1", w_mlp1, m_w_mlp1, v_w_mlp1), ("w_mlp2", w_mlp2, m_w_mlp2, v_w_mlp2)))
    small = dict(g_norm1=g_norm1[:, None], g_norm2=g_norm2[:, None], conv_w=conv_w_full, conv_b=conv_b[:, None], sink=sink,
                 g_out_conv=g_out_conv[:, None], g_out_attn=g_out_attn[:, None], g_final=g_final[None])
    x0 = jnp.concatenate([ctx[0], x[0]], axis=0)
    loss_part, dx0, _, dmods, sg, last = _local_step(x0, loss_target[0], mods, None, small, ctx_rows=ctx_rows, comm=comm)
    loss = lax.psum(loss_part, ("x", "y", "c"))
    grad_x = dx0[ctx_rows:][None]

    names = ("g_norm1", "g_norm2", "conv_w", "conv_b", "sink", "g_out_conv", "g_out_attn", "g_final")
    shapes = [(depth, 2, N_MOD * d)] + [sg[k].shape for k in names]
    got = _allgather8(_pack([dmods] + [sg[k] for k in names], d), "gather_small")
    tot = _unpack(_sum8(got, "sum_small").reshape(-1), shapes)
    dmod_tot, small_tot = tot[0], dict(zip(names, tot[1:]))
    dmod_lat = _unpack(got.reshape(N_DEV, -1), shapes[:1])[0][:, :, 1]
    dm_rows = jnp.zeros((depth, COND_ROWS, N_MOD * d), F32)
    dm_rows = dm_rows.at[:, :N_DEV].set(jnp.transpose(dmod_lat, (1, 0, 2))).at[:, N_DEV].set(dmod_tot[:, 0])
    dm_cols = lax.dynamic_slice_in_dim(dm_rows, chip * ns_ada, ns_ada, axis=2)
    (g_w_ada, dcond), got = _ada_bwd(cond, dm_cols, w_ada, "ada_bwd", side=last.chips_side((1,)))
    last.land((1,), got)
    last.sum()
    last.adam(_run_side(last.share_side(), "rs_share_last"))
    got = _allgather8(dcond[N_DEV:N_DEV + 8], "gather_dcond")
    g_c_ctx = _cctx_grad(got, c_ctx[None], "c_ctx_grad")

    res = {}
    ada_bufs = [lax.empty((1,) + (depth * d, ns_ada), F32) for _ in range(4)]
    ada_res, _ = _adamw_layer(
        0, g_w_ada.reshape(depth * d, ns_ada), w_ada.reshape(1, depth * d, ns_ada),
        m_w_ada.reshape(1, depth * d, ns_ada), v_w_ada.reshape(1, depth * d, ns_ada), ada_bufs, "adamw_w_ada")
    res["w_ada"] = [r.reshape(w_ada.shape) for r in ada_res]
    res.update(zip(("w_in", "w_out", "w_mlp1", "w_mlp2"), comm.stacked))
    res["c_ctx"] = [r[0] for r in _adamw_small(g_c_ctx, None, c_ctx[None], m_c_ctx[None], v_c_ctx[None], "adamw_c_ctx")]
    res["b_ada"] = _adamw_small(dmod_tot[:, 0], dmod_tot[:, 1], b_ada, m_b_ada, v_b_ada, "adamw_b_ada")
    cw_grad = lax.dynamic_slice_in_dim(small_tot["conv_w"], chip * cw_cols, cw_cols, axis=2)
    res["conv_w"] = [r.reshape(conv_w.shape) for r in _adamw_small(
        cw_grad.reshape(depth * 3, cw_cols), None, conv_w.reshape(depth * 3, cw_cols),
        m_conv_w.reshape(depth * 3, cw_cols), v_conv_w.reshape(depth * 3, cw_cols), "adamw_conv_w")]
    for nm, w, m, v in (("g_norm1", g_norm1, m_g_norm1, v_g_norm1), ("g_norm2", g_norm2, m_g_norm2, v_g_norm2),
                        ("conv_b", conv_b, m_conv_b, v_conv_b), ("sink", sink, m_sink, v_sink),
                        ("g_out_conv", g_out_conv, m_g_out_conv, v_g_out_conv),
                        ("g_out_attn", g_out_attn, m_g_out_attn, v_g_out_attn)):
        res[nm] = _adamw_small(small_tot[nm], None, w, m, v, f"adamw_{nm}")
    res["g_final"] = [r[0] for r in _adamw_small(small_tot["g_final"][None], None, g_final[None], m_g_final[None],
                                                 v_g_final[None], "adamw_g_final")]
    order = ("c_ctx", "w_ada", "b_ada", "g_norm1", "g_norm2", "w_in", "conv_w", "conv_b", "sink", "g_out_conv",
             "g_out_attn", "w_out", "w_mlp1", "w_mlp2", "g_final")
    return (loss, grad_x, *[res[n][0] for n in order], *[res[n][1] for n in order], *[res[n][2] for n in order],
            *[res[n][3] for n in order])
```

```python
import functools

import jax
import jax.numpy as jnp
from jax import lax
from jax.experimental import pallas as pl
from jax.experimental.pallas import tpu as pltpu

F32 = jnp.float32
I32 = jnp.int32
MXU_DTYPE = jnp.bfloat16
EPS = 1e-6
HEAD_DIM = 64
N_KV = 4
WINDOW = 128
QBLK = 128
LANES = 128
GRID_W = 64
ROPE_THETA = 10000.0
NEG_INF = -1e30
N_MOD = 6
HALO = 8
ADAM_LR, ADAM_B1, ADAM_B2, ADAM_EPS, ADAM_WD, ADAM_STEP = 0.001, 0.9, 0.999, 1e-08, 0.01, 10
V7X_VMEM_BYTES = 64 * 1024 * 1024
VMEM_LIMIT = V7X_VMEM_BYTES * 3 // 4
ROW_TILES = (1408, 768, 512, 640, 256, 128)
MESH = pl.DeviceIdType.MESH
ANY = pl.BlockSpec(memory_space=pl.ANY)


def _params():
    return pltpu.CompilerParams(vmem_limit_bytes=VMEM_LIMIT)


def _pick(n, cands):
    for c in cands:
        if n % c == 0:
            return c
    raise ValueError(f"no tile of {cands} divides {n}")


def _mx(v):
    return v.astype(MXU_DTYPE)


def _dot(a, b):
    return jnp.dot(_mx(a), _mx(b), preferred_element_type=F32)


def _dot_nt(a, b):
    return lax.dot_general(_mx(a), _mx(b), (((1,), (1,)), ((), ())), preferred_element_type=F32)


def _dot_tn(a, b):
    return lax.dot_general(_mx(a), _mx(b), (((0,), (0,)), ((), ())), preferred_element_type=F32)


def _silu(v):
    return v / (1.0 + jnp.exp(-v))


class _Side:
    def __init__(self, inputs, out_shapes, scratch, start, finish, aliases=(), middle=None):
        self.inputs, self.out_shapes, self.scratch = list(inputs), list(out_shapes), list(scratch)
        self.start, self.finish, self.aliases = start, finish, tuple(aliases)
        self.middle = middle


def _call(body, *, grid, in_specs, out_specs, out_shape, scratch, args, name, side=None, aliases=None):
    aliases = dict(aliases or {})
    if side is None:
        res = pl.pallas_call(body, grid=grid, in_specs=in_specs, out_specs=out_specs, out_shape=out_shape,
                             scratch_shapes=scratch, input_output_aliases=aliases, compiler_params=_params(),
                             name=name)(*args)
        return list(res), []
    ni, no, ns = len(in_specs), len(out_specs), len(scratch)
    si, so = len(side.inputs), len(side.out_shapes)

    def full(*refs):
        ins, sins = refs[:ni], refs[ni:ni + si]
        outs, souts = refs[ni + si:ni + si + no], refs[ni + si + no:ni + si + no + so]
        scr, sems = refs[ni + si + no + so:ni + si + no + so + ns], refs[ni + si + no + so + ns:]
        ids = [pl.program_id(k) for k in range(len(grid))]
        first, last = ids[0] == 0, ids[0] == grid[0] - 1
        for k in range(1, len(grid)):
            first, last = first & (ids[k] == 0), last & (ids[k] == grid[k] - 1)

        @pl.when(first)
        def _():
            side.start(sins, souts, sems)

        body(*ins, *outs, *scr)

        if side.middle is not None:
            lin, total = ids[0], grid[0]
            for k in range(1, len(grid)):
                lin, total = lin * grid[k] + ids[k], total * grid[k]

            @pl.when(lin == (3 * total) // 4)
            def _():
                side.middle(sins, souts, sems)

        @pl.when(last)
        def _():
            side.finish(sins, souts, sems)

    res = pl.pallas_call(
        full, grid=grid, in_specs=list(in_specs) + [ANY] * si, out_specs=list(out_specs) + [ANY] * so,
        out_shape=list(out_shape) + side.out_shapes, scratch_shapes=list(scratch) + side.scratch,
        input_output_aliases={**aliases, **{ni + a: no + b for a, b in side.aliases}}, compiler_params=_params(),
        name=name)(*args, *side.inputs)
    return list(res[:no]), list(res[no:])


def _run_side(side, name):
    si, so = len(side.inputs), len(side.out_shapes)

    def body(*refs):
        side.start(refs[:si], refs[si:si + so], refs[si + so:])
        if side.middle is not None:
            side.middle(refs[:si], refs[si:si + so], refs[si + so:])
        side.finish(refs[:si], refs[si:si + so], refs[si + so:])

    res = pl.pallas_call(body, in_specs=[ANY] * si, out_specs=[ANY] * so, out_shape=side.out_shapes,
                         scratch_shapes=side.scratch, input_output_aliases=dict(side.aliases), name=name)(*side.inputs)
    return list(res)


def _mm_nn(a, b3, *, out_dtypes, epilogue, name, side=None):
    m, k = a.shape
    s, _, ns = b3.shape
    tm = _pick(m, ROW_TILES)
    tn = _pick(ns, (1024, 1152, 640, 512, 256, 128))
    tk = _pick(k, (2048, 1024, 512))
    nbs, nk = ns // tn, k // tk
    n_out = len(out_dtypes)

    def body(a_ref, b_ref, *rest):
        outs = rest[:n_out]

        def write(v):
            for o, r in zip(outs, epilogue(v)):
                o[...] = r.astype(o.dtype)

        if nk == 1:
            write(jnp.dot(a_ref[...], b_ref[...], preferred_element_type=F32))
            return
        acc = rest[n_out]
        kk = pl.program_id(2)

        @pl.when(kk == 0)
        def _():
            acc[...] = jnp.zeros_like(acc)

        acc[...] += jnp.dot(a_ref[...], b_ref[...], preferred_element_type=F32)

        @pl.when(kk == nk - 1)
        def _():
            write(acc[...])

    return _call(
        body, grid=(m // tm, s * nbs, nk),
        in_specs=[pl.BlockSpec((tm, tk), lambda i, j, kk: (i, kk)),
                  pl.BlockSpec((None, tk, tn), lambda i, j, kk: (j // nbs, kk, j % nbs))],
        out_specs=[pl.BlockSpec((tm, tn), lambda i, j, kk: (i, j))] * n_out,
        out_shape=[jax.ShapeDtypeStruct((m, s * ns), dt) for dt in out_dtypes],
        scratch=[pltpu.VMEM((tm, tn), F32)] if nk > 1 else [], args=(a, b3), name=name, side=side)


def _mm_nt(a, b3, *, out_dtype, name, extra=None, epilogue=None, side=None):
    m = a.shape[0]
    s, ko, ns = b3.shape
    tm = _pick(m, ROW_TILES)
    tko = _pick(ko, (1024, 512))
    tn = _pick(ns, (2048, 1152, 1024, 640, 512, 256, 128))
    nbs = ns // tn
    nk = s * nbs
    n_in = 3 if extra is not None else 2

    def body(a_ref, b_ref, *rest):
        x_ref = rest[0] if extra is not None else None
        o_ref = rest[n_in - 2]

        def write(v):
            if epilogue is not None:
                v = epilogue(v, x_ref[...])
            o_ref[...] = v.astype(o_ref.dtype)

        if nk == 1:
            write(lax.dot_general(a_ref[...], b_ref[...], (((1,), (1,)), ((), ())), preferred_element_type=F32))
            return
        acc = rest[n_in - 1]
        kk = pl.program_id(2)

        @pl.when(kk == 0)
        def _():
            acc[...] = jnp.zeros_like(acc)

        acc[...] += lax.dot_general(a_ref[...], b_ref[...], (((1,), (1,)), ((), ())), preferred_element_type=F32)

        @pl.when(kk == nk - 1)
        def _():
            write(acc[...])

    in_specs = [pl.BlockSpec((tm, tn), lambda i, j, kk: (i, kk)),
                pl.BlockSpec((None, tko, tn), lambda i, j, kk: (kk // nbs, j, kk % nbs))]
    args = [a, b3]
    if extra is not None:
        in_specs.append(pl.BlockSpec((tm, tko), lambda i, j, kk: (i, j)))
        args.append(extra)
    return _call(
        body, grid=(m // tm, ko // tko, nk), in_specs=in_specs,
        out_specs=[pl.BlockSpec((tm, tko), lambda i, j, kk: (i, j))],
        out_shape=[jax.ShapeDtypeStruct((m, ko), out_dtype)],
        scratch=[pltpu.VMEM((tm, tko), F32)] if nk > 1 else [], args=args, name=name, side=side)


def _mm_tn(a, b, *, shards, name, side=None):
    t, k = a.shape
    ns = b.shape[1] // shards
    tt = _pick(t, (2 * ROW_TILES[0],) + ROW_TILES)
    tk = _pick(k, (1024, 512))
    tn = _pick(ns, (1024, 1152, 640, 512, 256, 128))
    nbs, nt = ns // tn, t // tt

    def body(a_ref, b_ref, o_ref, acc):
        tt_i = pl.program_id(2)

        @pl.when(tt_i == 0)
        def _():
            acc[...] = jnp.zeros_like(acc)

        acc[...] += lax.dot_general(a_ref[...], b_ref[...], (((0,), (0,)), ((), ())), preferred_element_type=F32)

        @pl.when(tt_i == nt - 1)
        def _():
            o_ref[...] = acc[...].astype(o_ref.dtype)

    return _call(
        body, grid=(k // tk, shards * nbs, nt),
        in_specs=[pl.BlockSpec((tt, tk), lambda i, j, q: (q, i)),
                  pl.BlockSpec((tt, tn), lambda i, j, q: (q, j))],
        out_specs=[pl.BlockSpec((None, tk, tn), lambda i, j, q: (j // nbs, i, j % nbs))],
        out_shape=[jax.ShapeDtypeStruct((shards, k, ns), MXU_DTYPE)], scratch=[pltpu.VMEM((tk, tn), F32)], args=(a, b),
        name=name, side=side)


def _row_tile(t, ctx_rows):
    return 256 if t % 256 == 0 and ctx_rows % 256 == 0 else 128


def _mod_spec(d, ncb, idx):
    return pl.BlockSpec((None, None, 1, d), lambda i: (jnp.where(i >= ncb, 1, 0), idx, 0, 0))


def _norm_fwd(x, z, gate_mods, mods, g, *, gate_idx, shift_idx, scale_idx, ctx_rows, name):
    t, d = x.shape
    tr = _row_tile(t, ctx_rows)
    ncb = ctx_rows // tr
    row = pl.BlockSpec((tr, d), lambda i: (i, 0))
    vec = pl.BlockSpec((1, d), lambda i: (0, 0))
    resid = z is not None

    def body(*refs):
        if resid:
            x_ref, z_ref, gt_ref, g_ref, sh_ref, sc_ref, xo_ref, h_ref = refs
            xn = x_ref[...] + gt_ref[...] * z_ref[...].astype(F32)
            xo_ref[...] = xn
        else:
            x_ref, g_ref, sh_ref, sc_ref, h_ref = refs
            xn = x_ref[...]
        r = lax.rsqrt(jnp.mean(xn * xn, axis=-1, keepdims=True) + EPS)
        h_ref[...] = ((xn * r * g_ref[...]) * (1.0 + sc_ref[...]) + sh_ref[...]).astype(h_ref.dtype)

    mspecs = [vec, _mod_spec(d, ncb, shift_idx), _mod_spec(d, ncb, scale_idx)]
    if resid:
        in_specs = [row, row, _mod_spec(d, ncb, gate_idx)] + mspecs
        args = (x, z, gate_mods, g, mods, mods)
        out_specs = [row, row]
        out_shape = [jax.ShapeDtypeStruct((t, d), F32), jax.ShapeDtypeStruct((t, d), MXU_DTYPE)]
    else:
        in_specs = [row] + mspecs
        args = (x, g, mods, mods)
        out_specs = row
        out_shape = jax.ShapeDtypeStruct((t, d), MXU_DTYPE)
    return pl.pallas_call(body, grid=(t // tr,), in_specs=in_specs, out_specs=out_specs, out_shape=out_shape,
                          compiler_params=_params(), name=name)(*args)


R_DSHIFT, R_DSCALE, R_DG, R_DGATE = 0, 2, 4, 5


def _norm_bwd(dx, dh, xin, mods, g, *, scale_idx, ctx_rows, name, prev=None, side=None):
    t, d = dx.shape
    tr = _row_tile(t, ctx_rows)
    ncb = ctx_rows // tr
    row = pl.BlockSpec((tr, d), lambda i: (i, 0))
    vec = pl.BlockSpec((1, d), lambda i: (0, 0))
    acc_spec = pl.BlockSpec((8, d), lambda i: (0, 0))
    has_prev = prev is not None

    def body(*refs):
        if has_prev:
            dx_ref, dh_ref, x_ref, g_ref, sc_ref, z_ref, gt_ref, dxo_ref, dz_ref, acc = refs
        else:
            dx_ref, dh_ref, x_ref, g_ref, sc_ref, dxo_ref, acc = refs
        i = pl.program_id(0)

        @pl.when(i == 0)
        def _():
            acc[...] = jnp.zeros_like(acc)

        lat = jnp.where(i >= ncb, 1.0, 0.0)
        x = x_ref[...]
        r = lax.rsqrt(jnp.mean(x * x, axis=-1, keepdims=True) + EPS)
        xhat = x * r
        gv = g_ref[...]
        dhv = dh_ref[...].astype(F32)
        dn = dhv * (1.0 + sc_ref[...])
        dxhat = dn * gv
        dxin = dx_ref[...] + r * (dxhat - xhat * jnp.mean(dxhat * xhat, axis=-1, keepdims=True))
        dxo_ref[...] = dxin
        dshift = jnp.sum(dhv, axis=0, keepdims=True)
        dscale = jnp.sum(dhv * (xhat * gv), axis=0, keepdims=True)
        acc[R_DSHIFT:R_DSHIFT + 1, :] += dshift * (1.0 - lat)
        acc[R_DSHIFT + 1:R_DSHIFT + 2, :] += dshift * lat
        acc[R_DSCALE:R_DSCALE + 1, :] += dscale * (1.0 - lat)
        acc[R_DSCALE + 1:R_DSCALE + 2, :] += dscale * lat
        acc[R_DG:R_DG + 1, :] += jnp.sum(dn * xhat, axis=0, keepdims=True)
        if has_prev:
            dz_ref[...] = (dxin * gt_ref[...]).astype(dz_ref.dtype)
            dgate = jnp.sum(dxin * z_ref[...].astype(F32), axis=0, keepdims=True)
            acc[R_DGATE:R_DGATE + 1, :] += dgate * (1.0 - lat)
            acc[R_DGATE + 1:R_DGATE + 2, :] += dgate * lat

    in_specs = [row, row, row, vec, _mod_spec(d, ncb, scale_idx)]
    args = [dx, dh, xin, g, mods]
    out_specs = [row]
    out_shape = [jax.ShapeDtypeStruct((t, d), F32)]
    if has_prev:
        z, gate_mods, gate_idx = prev
        in_specs += [row, _mod_spec(d, ncb, gate_idx)]
        args += [z, gate_mods]
        out_specs.append(row)
        out_shape.append(jax.ShapeDtypeStruct((t, d), MXU_DTYPE))
    out_specs.append(acc_spec)
    out_shape.append(jax.ShapeDtypeStruct((8, d), F32))
    outs, got = _call(body, grid=(t // tr,), in_specs=in_specs, out_specs=out_specs, out_shape=out_shape, scratch=[],
                      args=args, name=name, side=side)
    return outs if side is None else (outs, got)


R_FGATE, R_FG, R_FLOSS = 0, 2, 3


def _final(x1, o, gate_mods, g_final, tgt, *, ctx_rows, name):
    t, d = x1.shape
    tr = _row_tile(t, ctx_rows)
    ncb, nb = ctx_rows // tr, t // tr
    row = pl.BlockSpec((tr, d), lambda i: (i, 0))
    vec = pl.BlockSpec((1, d), lambda i: (0, 0))

    def body(x_ref, o_ref, gt_ref, g_ref, t_ref, dx_ref, do_ref, acc, lsum):
        i = pl.program_id(0)

        @pl.when(i == 0)
        def _():
            acc[...] = jnp.zeros_like(acc)
            lsum[...] = jnp.zeros_like(lsum)

        lat = jnp.where(i >= ncb, 1.0, 0.0)
        gt = gt_ref[...]
        ov = o_ref[...].astype(F32)
        x = x_ref[...] + gt * ov
        r = lax.rsqrt(jnp.mean(x * x, axis=-1, keepdims=True) + EPS)
        xhat = x * r
        gv = g_ref[...]
        err = (xhat * gv - t_ref[...]) * lat
        dy = err / d
        dxhat = dy * gv
        dxv = r * (dxhat - xhat * jnp.mean(dxhat * xhat, axis=-1, keepdims=True))
        dx_ref[...] = dxv
        do_ref[...] = (dxv * gt).astype(do_ref.dtype)
        dgate = jnp.sum(dxv * ov, axis=0, keepdims=True)
        acc[R_FGATE:R_FGATE + 1, :] += dgate * (1.0 - lat)
        acc[R_FGATE + 1:R_FGATE + 2, :] += dgate * lat
        acc[R_FG:R_FG + 1, :] += jnp.sum(dy * xhat, axis=0, keepdims=True)
        lsum[...] += jnp.sum(err * err, axis=0, keepdims=True)

        @pl.when(i == nb - 1)
        def _():
            total = (0.5 / d) * jnp.sum(lsum[...], axis=-1, keepdims=True)
            acc[R_FLOSS:R_FLOSS + 1, :] = jnp.broadcast_to(total, (1, d))

    return pl.pallas_call(
        body, grid=(nb,),
        in_specs=[row, row, _mod_spec(d, ncb, 5), vec, pl.BlockSpec((tr, d), lambda i: (jnp.maximum(i - ncb, 0), 0))],
        out_specs=[row, row, pl.BlockSpec((8, d), lambda i: (0, 0))],
        out_shape=[jax.ShapeDtypeStruct((t, d), F32), jax.ShapeDtypeStruct((t, d), MXU_DTYPE),
                   jax.ShapeDtypeStruct((8, d), F32)],
        scratch_shapes=[pltpu.VMEM((1, d), F32)], compiler_params=_params(), name=name)(x1, o, gate_mods, g_final, tgt)


def _rope_tables(s, ctx_rows):
    rows = s // GRID_W
    row_pos = jnp.repeat(jnp.arange(rows, dtype=F32), GRID_W)
    col_pos = jnp.tile(jnp.arange(GRID_W, dtype=F32), rows)
    quarter = HEAD_DIM // 4
    inv = ROPE_THETA ** (-jnp.arange(0, 2 * quarter, 2, dtype=F32) / (2 * quarter))
    ang_r, ang_c = row_pos[:, None] * inv[None, :], col_pos[:, None] * inv[None, :]
    cr, sr, cc, sc = jnp.cos(ang_r), jnp.sin(ang_r), jnp.cos(ang_c), jnp.sin(ang_c)
    zero = jnp.zeros_like(sr)
    cos = jnp.concatenate([cr, cr, cc, cc], axis=1)
    sa = jnp.concatenate([zero, sr, zero, sc], axis=1)
    sb = jnp.concatenate([-sr, zero, -sc, zero], axis=1)

    def full(tab, fill):
        tab = jnp.tile(tab, (1, LANES // HEAD_DIM))
        return jnp.concatenate([jnp.full((ctx_rows, LANES), fill, F32), tab], axis=0)

    return full(cos, 1.0), full(sa, 0.0), full(sb, 0.0)


def _rope_apply(x, cos, sa, sb, transpose):
    n = x.shape[1] // LANES
    cos, sa, sb = (jnp.tile(v, (1, n)) for v in (cos, sa, sb))
    quarter = HEAD_DIM // 4
    width = x.shape[1]
    if transpose:
        return x * cos + pltpu.roll(x * sa, width - quarter, 1) + pltpu.roll(x * sb, quarter, 1)
    return x * cos + pltpu.roll(x, quarter, 1) * sa + pltpu.roll(x, width - quarter, 1) * sb


def _twice(x):
    lane = lax.broadcasted_iota(I32, (x.shape[0], LANES), 1)
    out = []
    for j in range(N_KV):
        blk = x[:, (j // 2) * LANES:(j // 2 + 1) * LANES]
        own = jnp.where((lane < HEAD_DIM) if j % 2 == 0 else (lane >= HEAD_DIM), blk, 0.0)
        out.append(own + pltpu.roll(own, HEAD_DIM, 1))
    return jnp.concatenate(out, axis=1)


def _rope_fwd(p, tabs, *, da, kw, kv_start, name):
    t = p.shape[0]
    tr = _pick(t, (768, 256, 128))

    def body(q_ref, k_ref, v_ref, c_ref, a_ref, b_ref, qo_ref, ko_ref, vo_ref):
        cos, sa, sb = c_ref[...], a_ref[...], b_ref[...]
        qo_ref[...] = (_rope_apply(q_ref[...], cos, sa, sb, False) * (HEAD_DIM ** -0.5)).astype(qo_ref.dtype)
        ko_ref[...] = _twice(_rope_apply(k_ref[...], cos, sa, sb, False)).astype(ko_ref.dtype)
        vo_ref[...] = _twice(v_ref[...]).astype(vo_ref.dtype)

    tab = pl.BlockSpec((tr, LANES), lambda i: (i, 0))
    two = pl.BlockSpec((tr, N_KV * LANES), lambda i: (i, 0))
    return pl.pallas_call(
        body, grid=(t // tr,),
        in_specs=[pl.BlockSpec((tr, da), lambda i: (i, (kv_start - da) // da)),
                  pl.BlockSpec((tr, kw), lambda i: (i, kv_start // kw)),
                  pl.BlockSpec((tr, kw), lambda i: (i, kv_start // kw + 1)), tab, tab, tab],
        out_specs=[pl.BlockSpec((tr, da), lambda i: (i, 0)), two, two],
        out_shape=[jax.ShapeDtypeStruct((t, da), MXU_DTYPE), jax.ShapeDtypeStruct((t, N_KV * LANES), MXU_DTYPE),
                   jax.ShapeDtypeStruct((t, N_KV * LANES), MXU_DTYPE)],
        compiler_params=_params(), name=name)(p, p, p, *tabs)


def _rope_bwd(dq, dk, dv, tabs, dp, *, name):
    t, da = dq.shape
    kw = dk.shape[1]
    width = da + 2 * kw
    tr = _pick(t, (768, 256, 128))
    col = (dp.shape[1] - width) // width
    assert col * width == dp.shape[1] - width

    def body(q_ref, k_ref, v_ref, c_ref, a_ref, b_ref, dp_ref, o_ref):
        cos, sa, sb = c_ref[...], a_ref[...], b_ref[...]
        o_ref[:, :da] = _rope_apply(q_ref[...], cos, sa, sb, True).astype(o_ref.dtype)
        o_ref[:, da:da + kw] = _rope_apply(k_ref[...], cos, sa, sb, True).astype(o_ref.dtype)
        o_ref[:, da + kw:] = v_ref[...].astype(o_ref.dtype)

    tab = pl.BlockSpec((tr, LANES), lambda i: (i, 0))
    return pl.pallas_call(
        body, grid=(t // tr,),
        in_specs=[pl.BlockSpec((tr, da), lambda i: (i, 0)), pl.BlockSpec((tr, kw), lambda i: (i, 0)),
                  pl.BlockSpec((tr, kw), lambda i: (i, 0)), tab, tab, tab, ANY],
        out_specs=pl.BlockSpec((tr, width), lambda i: (i, col)),
        out_shape=jax.ShapeDtypeStruct(dp.shape, dp.dtype), input_output_aliases={6: 0},
        compiler_params=_params(), name=name)(dq, dk, dv, *tabs, dp)


def _attn_specs(t, ctx_rows):
    nblk = t // QBLK

    def clip(i):
        return jnp.clip(i, 0, nblk - 1)

    return [pl.BlockSpec((QBLK, 2 * LANES), lambda b, i: (clip(i - 1), b)),
            pl.BlockSpec((QBLK, 2 * LANES), lambda b, i: (i, b)),
            pl.BlockSpec((QBLK, 2 * LANES), lambda b, i: (clip(i + 1), b)),
            pl.BlockSpec((ctx_rows, 2 * LANES), lambda b, i: (0, b))]


def _band_bias(rows):
    rr = lax.broadcasted_iota(I32, (rows, 3 * QBLK), 0) & (QBLK - 1)
    cc = lax.broadcasted_iota(I32, (rows, 3 * QBLK), 1)
    return jnp.where(jnp.abs(cc - QBLK - rr) <= WINDOW, 0.0, NEG_INF).astype(F32)


def _local_bias(band_ref, i, t, ctx_rows):
    cc = lax.broadcasted_iota(I32, (1, 3 * QBLK), 1)
    keyrow = (i - 1) * QBLK + cc
    first_key = jnp.where(i * QBLK >= ctx_rows, ctx_rows, t)
    return band_ref[...] + jnp.where((keyrow >= first_key) & (keyrow < t), 0.0, NEG_INF)


def _stack_heads(ref, c0, nblocks):
    lane = lax.broadcasted_iota(I32, (QBLK, LANES), 1)
    rows = []
    for qb in range(nblocks):
        blk = ref[:, c0 + qb * LANES:c0 + (qb + 1) * LANES].astype(F32)
        rows += [jnp.where(lane < HEAD_DIM, blk, 0.0), jnp.where(lane < HEAD_DIM, 0.0, blk)]
    return jnp.concatenate(rows, axis=0)


def _unstack_heads(x, nblocks):
    lane = lax.broadcasted_iota(I32, (QBLK, LANES), 1)
    return [jnp.where(lane < HEAD_DIM, x[2 * qb * QBLK:(2 * qb + 1) * QBLK], x[(2 * qb + 1) * QBLK:(2 * qb + 2) * QBLK])
            for qb in range(nblocks)]


def _sink_column(sink_ref, head0, group):
    row = lax.broadcasted_iota(I32, (group * QBLK, 1), 0)
    col = jnp.zeros((group * QBLK, 1), F32)
    for g in range(group):
        col = jnp.where((row >= g * QBLK) & (row < (g + 1) * QBLK), sink_ref[head0 + g], col)
    return col


def _exps(qs, k_loc, k_ctx, bias, snk):
    s_loc = _dot_nt(qs, k_loc) + bias
    s_ctx = _dot_nt(qs, k_ctx)
    m = jnp.maximum(jnp.maximum(jnp.max(s_loc, axis=-1, keepdims=True), jnp.max(s_ctx, axis=-1, keepdims=True)), snk)
    e_loc, e_ctx, e_snk = jnp.exp(s_loc - m), jnp.exp(s_ctx - m), jnp.exp(snk - m)
    inv = 1.0 / (jnp.sum(e_loc, axis=-1, keepdims=True) + jnp.sum(e_ctx, axis=-1, keepdims=True) + e_snk)
    return e_loc, e_ctx, e_snk, inv


def _attn_fwd(qs, k2, v2, sink, *, ctx_rows, name, side=None):
    t, da = qs.shape
    group = da // HEAD_DIM // N_KV
    nqb = group // 2
    kvspecs = _attn_specs(t, ctx_rows)

    def body(sink_ref, band_ref, q_ref, kp, kc, kn, kx, vp, vc, vn, vx, o_ref):
        b, i = pl.program_id(0), pl.program_id(1)
        bias = _local_bias(band_ref, i, t, ctx_rows)
        for half in range(2):
            lanes = slice(half * LANES, (half + 1) * LANES)
            qst = _mx(_stack_heads(q_ref, half * nqb * LANES, nqb))
            k_loc = jnp.concatenate([kp[:, lanes], kc[:, lanes], kn[:, lanes]], axis=0)
            v_loc = jnp.concatenate([vp[:, lanes], vc[:, lanes], vn[:, lanes]], axis=0)
            snk = _sink_column(sink_ref, (2 * b + half) * group, group)
            e_loc, e_ctx, _, inv = _exps(qst, k_loc, kx[:, lanes], bias, snk)
            out = (_dot(e_loc, v_loc) + _dot(e_ctx, vx[:, lanes])) * inv
            for qb, blk in enumerate(_unstack_heads(out, nqb)):
                c0 = (half * nqb + qb) * LANES
                o_ref[:, c0:c0 + LANES] = blk

    qspec = pl.BlockSpec((QBLK, 2 * group * HEAD_DIM), lambda b, i: (i, b))
    band = pl.BlockSpec((group * QBLK, 3 * QBLK), lambda b, i: (0, 0))
    (ao,), got = _call(
        body, grid=(N_KV // 2, t // QBLK),
        in_specs=[pl.BlockSpec(memory_space=pltpu.SMEM), band, qspec] + kvspecs + kvspecs,
        out_specs=[qspec], out_shape=[jax.ShapeDtypeStruct((t, da), F32)], scratch=[],
        args=(sink, _band_bias(group * QBLK), qs, k2, k2, k2, k2, v2, v2, v2, v2), name=name, side=side)
    return ao, got


def _attn_bwd(qs, k2, v2, dao, sink, *, ctx_rows, name, side=None):
    t, da = qs.shape
    nblk = t // QBLK
    group = da // HEAD_DIM // N_KV
    nqb = group // 2
    kw = N_KV * HEAD_DIM
    nloc = 3 * QBLK
    kvspecs = _attn_specs(t, ctx_rows)

    def body(sink_ref, band_ref, q_ref, do_ref, kp, kc, kn, kx, vp, vc, vn, vx,
             dq_ref, dkl_ref, dvl_ref, dkc_ref, dvc_ref, ds_ref):
        b, i = pl.program_id(0), pl.program_id(1)

        @pl.when((b == 0) & (i == 0))
        def _():
            ds_ref[...] = jnp.zeros_like(ds_ref)

        @pl.when(i == 0)
        def _():
            dkc_ref[...] = jnp.zeros_like(dkc_ref)
            dvc_ref[...] = jnp.zeros_like(dvc_ref)

        bias = _local_bias(band_ref, i, t, ctx_rows)
        srow = lax.broadcasted_iota(I32, ds_ref.shape, 0)
        slane = lax.broadcasted_iota(I32, ds_ref.shape, 1)

        def both(x):
            return x + pltpu.roll(x, HEAD_DIM, 1)

        folded = []
        for half in range(2):
            lanes = slice(half * LANES, (half + 1) * LANES)
            head0 = (2 * b + half) * group
            qst = _mx(_stack_heads(q_ref, half * nqb * LANES, nqb))
            dost = _mx(_stack_heads(do_ref, half * nqb * LANES, nqb))
            k_loc = jnp.concatenate([kp[:, lanes], kc[:, lanes], kn[:, lanes]], axis=0)
            v_loc = jnp.concatenate([vp[:, lanes], vc[:, lanes], vn[:, lanes]], axis=0)
            k_ctx, v_ctx = kx[:, lanes], vx[:, lanes]
            e_loc, e_ctx, e_snk, inv = _exps(qst, k_loc, k_ctx, bias, _sink_column(sink_ref, head0, group))
            p_loc, p_ctx = e_loc * inv, e_ctx * inv
            dp_loc, dp_ctx = _dot_nt(dost, v_loc), _dot_nt(dost, v_ctx)
            dsum = jnp.sum(p_loc * dp_loc, axis=-1, keepdims=True) + jnp.sum(p_ctx * dp_ctx, axis=-1, keepdims=True)
            ds_loc, ds_ctx = _mx(p_loc * (dp_loc - dsum)), _mx(p_ctx * (dp_ctx - dsum))
            dq = (_dot(ds_loc, k_loc) + _dot(ds_ctx, k_ctx)) * (HEAD_DIM ** -0.5)
            for qb, blk in enumerate(_unstack_heads(dq, nqb)):
                c0 = (half * nqb + qb) * LANES
                dq_ref[:, c0:c0 + LANES] = blk
            p_loc, p_ctx = _mx(p_loc), _mx(p_ctx)
            folded.append((both(_dot_tn(ds_loc, qst)), both(_dot_tn(p_loc, dost)),
                           both(_dot_tn(ds_ctx, qst)), both(_dot_tn(p_ctx, dost))))
            dsnk = e_snk * inv * dsum
            for g in range(group):
                part = -jnp.sum(dsnk[g * QBLK:(g + 1) * QBLK])
                ds_ref[...] += jnp.where((srow == 0) & (slane == head0 + g), part, 0.0)
        lane_l = lax.broadcasted_iota(I32, (nloc, LANES), 1)
        lane_c = lax.broadcasted_iota(I32, (ctx_rows, LANES), 1)
        dkl_ref[...] = jnp.where(lane_l < HEAD_DIM, folded[0][0], folded[1][0])
        dvl_ref[...] = jnp.where(lane_l < HEAD_DIM, folded[0][1], folded[1][1])
        dkc_ref[...] += jnp.where(lane_c < HEAD_DIM, folded[0][2], folded[1][2])
        dvc_ref[...] += jnp.where(lane_c < HEAD_DIM, folded[0][3], folded[1][3])

    qspec = pl.BlockSpec((QBLK, 2 * group * HEAD_DIM), lambda b, i: (i, b))
    loc = pl.BlockSpec((None, nloc, LANES), lambda b, i: (i, 0, b))
    cspec = pl.BlockSpec((ctx_rows, LANES), lambda b, i: (0, b))
    band = pl.BlockSpec((group * QBLK, nloc), lambda b, i: (0, 0))
    return _call(
        body, grid=(N_KV // 2, nblk),
        in_specs=[pl.BlockSpec(memory_space=pltpu.SMEM), band, qspec, qspec] + kvspecs + kvspecs,
        out_specs=[qspec, loc, loc, cspec, cspec, pl.BlockSpec((8, LANES), lambda b, i: (0, 0))],
        out_shape=[jax.ShapeDtypeStruct((t, da), F32), jax.ShapeDtypeStruct((nblk, nloc, kw), F32),
                   jax.ShapeDtypeStruct((nblk, nloc, kw), F32), jax.ShapeDtypeStruct((ctx_rows, kw), F32),
                   jax.ShapeDtypeStruct((ctx_rows, kw), F32), jax.ShapeDtypeStruct((8, LANES), F32)],
        scratch=[], args=(sink, _band_bias(group * QBLK), qs, dao, k2, k2, k2, k2, v2, v2, v2, v2), name=name, side=side)


def _kv_reduce(dkl, dvl, dkc, dvc, *, ctx_rows, name):
    nblk, _, kw = dkl.shape
    t = nblk * QBLK
    ncb = ctx_rows // QBLK

    def clip(i):
        return jnp.clip(i, 0, nblk - 1)

    def body(ka, kb, kc, kx, va, vb, vc, vx, dk_ref, dv_ref):
        m = pl.program_id(0)
        lat = m >= ncb
        wa = jnp.where(lat & (m + 1 <= nblk - 1), 1.0, 0.0)
        wc = jnp.where(lat & (m - 1 >= ncb), 1.0, 0.0)
        wl = jnp.where(lat, 1.0, 0.0)
        dk_ref[...] = wl * (kb[...] + wa * ka[...] + wc * kc[...]) + (1.0 - wl) * kx[...]
        dv_ref[...] = wl * (vb[...] + wa * va[...] + wc * vc[...]) + (1.0 - wl) * vx[...]

    slots = [pl.BlockSpec((None, QBLK, kw), lambda m: (clip(m + 1), 0, 0)),
             pl.BlockSpec((None, QBLK, kw), lambda m: (m, 1, 0)),
             pl.BlockSpec((None, QBLK, kw), lambda m: (clip(m - 1), 2, 0))]
    cspec = pl.BlockSpec((QBLK, kw), lambda m: (jnp.clip(m, 0, ncb - 1), 0))
    out = pl.BlockSpec((QBLK, kw), lambda m: (m, 0))
    return pl.pallas_call(
        body, grid=(nblk,), in_specs=slots + [cspec] + slots + [cspec], out_specs=[out, out],
        out_shape=[jax.ShapeDtypeStruct((t, kw), F32)] * 2, compiler_params=_params(), name=name)(
            dkl, dkl, dkl, dkc, dvl, dvl, dvl, dvc)


MERGE_ROWS = 128


def _halo_specs(t, c, col):
    hb = MERGE_ROWS // HALO
    return [pl.BlockSpec((HALO, c), lambda i: (jnp.maximum(i * hb - 1, 0), col)),
            pl.BlockSpec((MERGE_ROWS, c), lambda i: (i, col)),
            pl.BlockSpec((HALO, c), lambda i: (jnp.minimum((i + 1) * hb, t // HALO - 1), col))]


def _ext(refs):
    return jnp.concatenate([r[...] for r in refs], axis=0)


def _conv_ext(cg, hh, w_ref, b_ref, i, t, ctx_rows):
    n = cg.shape[0]
    u = cg * hh
    row = i * MERGE_ROWS - HALO + lax.broadcasted_iota(I32, u.shape, 0)
    first = (row == 0) | (row == ctx_rows)
    last = (row == ctx_rows - 1) | (row == t - 1)
    u_dn = jnp.where(first, 0.0, pltpu.roll(u, 1, 0))
    u_up = jnp.where(last, 0.0, pltpu.roll(u, n - 1, 0))
    cv = w_ref[0:1, :] * u_dn + w_ref[1:2, :] * u + w_ref[2:3, :] * u_up + b_ref[...]
    return u, u_dn, u_up, cv, first, last


def _merge_fwd(p, ao, conv_w, conv_b, g_oc, g_oa, *, ctx_rows, name):
    t = p.shape[0]
    c = ao.shape[1]
    main = slice(HALO, HALO + MERGE_ROWS)

    def body(bg_ref, cgp, cgm, cgn, hhp, hhm, hhn, ao_ref, w_ref, b_ref, gc_ref, ga_ref, o_ref):
        i = pl.program_id(0)
        _, _, _, cv, _, _ = _conv_ext(_ext((cgp, cgm, cgn)), _ext((hhp, hhm, hhn)), w_ref, b_ref, i, t, ctx_rows)
        co = bg_ref[...] * cv[main]
        rc = lax.rsqrt(jnp.mean(co * co, axis=-1, keepdims=True) + EPS)
        o_ref[:, :c] = (co * rc * gc_ref[...]).astype(o_ref.dtype)
        av = ao_ref[...]
        ra = lax.rsqrt(jnp.mean(av * av, axis=-1, keepdims=True) + EPS)
        o_ref[:, c:] = (av * ra * ga_ref[...]).astype(o_ref.dtype)

    vec = pl.BlockSpec((1, c), lambda i: (0, 0))
    return pl.pallas_call(
        body, grid=(t // MERGE_ROWS,),
        in_specs=[pl.BlockSpec((MERGE_ROWS, c), lambda i: (i, 0))] + _halo_specs(t, c, 1) + _halo_specs(t, c, 2)
        + [pl.BlockSpec((MERGE_ROWS, c), lambda i: (i, 0)), pl.BlockSpec((3, c), lambda i: (0, 0)), vec, vec, vec],
        out_specs=pl.BlockSpec((MERGE_ROWS, 2 * c), lambda i: (i, 0)),
        out_shape=jax.ShapeDtypeStruct((t, 2 * c), MXU_DTYPE), compiler_params=_params(), name=name)(
            p, p, p, p, p, p, p, ao, conv_w, conv_b, g_oc, g_oa)


R_DGOC, R_DGOA, R_DCB, R_DCW = 0, 1, 2, 3


def _merge_bwd(dmg, p, ao, conv_w, conv_b, g_oc, g_oa, *, width, ctx_rows, name):
    t = p.shape[0]
    c = ao.shape[1]
    main = slice(HALO, HALO + MERGE_ROWS)

    def body(dyp, dym, dyn, dya_ref, bgp, bgm, bgn, cgp, cgm, cgn, hhp, hhm, hhn, ao_ref, w_ref, b_ref, gc_ref, ga_ref,
             dp_ref, dao_ref, acc):
        i = pl.program_id(0)

        @pl.when(i == 0)
        def _():
            acc[...] = jnp.zeros_like(acc)

        bg, cg, hh = _ext((bgp, bgm, bgn)), _ext((cgp, cgm, cgn)), _ext((hhp, hhm, hhn))
        n = bg.shape[0]
        u, u_dn, u_up, cv, first, last = _conv_ext(cg, hh, w_ref, b_ref, i, t, ctx_rows)
        co = bg * cv
        rc = lax.rsqrt(jnp.mean(co * co, axis=-1, keepdims=True) + EPS)
        cohat = co * rc
        dyc = _ext((dyp, dym, dyn))
        t1 = dyc * gc_ref[...]
        dco = rc * (t1 - cohat * jnp.mean(t1 * cohat, axis=-1, keepdims=True))
        dcv = dco * bg
        dcv_next = jnp.where(last, 0.0, pltpu.roll(dcv, n - 1, 0))
        dcv_prev = jnp.where(first, 0.0, pltpu.roll(dcv, 1, 0))
        du = w_ref[1:2, :] * dcv + w_ref[0:1, :] * dcv_next + w_ref[2:3, :] * dcv_prev
        dp_ref[:, :c] = (dco * cv)[main].astype(dp_ref.dtype)
        dp_ref[:, c:2 * c] = (du * hh)[main].astype(dp_ref.dtype)
        dp_ref[:, 2 * c:] = (du * cg)[main].astype(dp_ref.dtype)
        dcv_m = dcv[main]
        acc[R_DGOC:R_DGOC + 1, :] += jnp.sum((dyc * cohat)[main], axis=0, keepdims=True)
        acc[R_DCB:R_DCB + 1, :] += jnp.sum(dcv_m, axis=0, keepdims=True)
        acc[R_DCW:R_DCW + 1, :] += jnp.sum(dcv_m * u_dn[main], axis=0, keepdims=True)
        acc[R_DCW + 1:R_DCW + 2, :] += jnp.sum(dcv_m * u[main], axis=0, keepdims=True)
        acc[R_DCW + 2:R_DCW + 3, :] += jnp.sum(dcv_m * u_up[main], axis=0, keepdims=True)
        av = ao_ref[...]
        ra = lax.rsqrt(jnp.mean(av * av, axis=-1, keepdims=True) + EPS)
        ahat = av * ra
        dya = dya_ref[...]
        t2 = dya * ga_ref[...]
        dao_ref[...] = ra * (t2 - ahat * jnp.mean(t2 * ahat, axis=-1, keepdims=True))
        acc[R_DGOA:R_DGOA + 1, :] += jnp.sum(dya * ahat, axis=0, keepdims=True)

    vec = pl.BlockSpec((1, c), lambda i: (0, 0))
    tile = pl.BlockSpec((MERGE_ROWS, c), lambda i: (i, 0))
    return pl.pallas_call(
        body, grid=(t // MERGE_ROWS,),
        in_specs=_halo_specs(t, c, 0) + [pl.BlockSpec((MERGE_ROWS, c), lambda i: (i, 1))]
        + _halo_specs(t, c, 0) + _halo_specs(t, c, 1) + _halo_specs(t, c, 2)
        + [tile, pl.BlockSpec((3, c), lambda i: (0, 0)), vec, vec, vec],
        out_specs=[pl.BlockSpec((MERGE_ROWS, 3 * c), lambda i: (i, 0)), tile, pl.BlockSpec((8, c), lambda i: (0, 0))],
        out_shape=[jax.ShapeDtypeStruct((t, width), MXU_DTYPE), jax.ShapeDtypeStruct((t, c), F32),
                   jax.ShapeDtypeStruct((8, c), F32)],
        compiler_params=_params(), name=name)(dmg, dmg, dmg, dmg, p, p, p, p, p, p, p, p, p, ao, conv_w, conv_b, g_oc, g_oa)


def _local_step(x0, tgt, mods, wts, small, *, ctx_rows, comm=None):
    t, d = x0.shape
    depth = mods.shape[0]
    c = d // 2
    da = d - c
    kw = N_KV * HEAD_DIM
    kv_start = 3 * c + da
    shards = N_CHIP
    tabs = _rope_tables(t - ctx_rows, ctx_rows)
    kwargs = dict(ctx_rows=ctx_rows)

    saved = []
    xs, z_prev = x0, None
    have = {}

    def riding(l, host):
        want = {"in": [(l, 1), (l + 1, 0)], "attn": [(l, 2)], "out": [], "mlp1": [(l, 3)], "mlp2": []}
        keys = [key for key in want[host] if key[0] < depth] if comm is not None else []
        return keys, (_gather_side([comm.halves[a][k] for a, k in keys]) if keys else None)

    def weight(l, k):
        return wts[l][k] if comm is None else comm.weight(k, have[(l, k)])

    if comm is not None:
        have[(0, 0)] = _run_side(_gather_side([comm.halves[0][0]]), "gather_weights_0")[0]
    for l in range(depth):
        if z_prev is None:
            x_in = xs
            h1 = _norm_fwd(xs, None, None, mods[l], small["g_norm1"][l], gate_idx=None, shift_idx=0, scale_idx=1,
                           name=f"norm1_fwd_{l}", **kwargs)
        else:
            x_in, h1 = _norm_fwd(xs, z_prev, mods[l - 1], mods[l], small["g_norm1"][l], gate_idx=5, shift_idx=0,
                                 scale_idx=1, name=f"norm1_fwd_{l}", **kwargs)
        keys, side = riding(l, "in")
        (p,), got = _mm_nn(h1, weight(l, 0), out_dtypes=(F32,), epilogue=lambda v: (v,), name=f"in_proj_{l}", side=side)
        have.update(zip(keys, got))
        qs, k2, v2 = _rope_fwd(p, tabs, da=da, kw=kw, kv_start=kv_start, name=f"rope_fwd_{l}")
        keys, side = riding(l, "attn")
        ao, got = _attn_fwd(qs, k2, v2, small["sink"][l], name=f"attn_fwd_{l}", side=side, **kwargs)
        have.update(zip(keys, got))
        mg = _merge_fwd(p, ao, small["conv_w"][l], small["conv_b"][l], small["g_out_conv"][l], small["g_out_attn"][l],
                        name=f"merge_fwd_{l}", **kwargs)
        keys, side = riding(l, "out")
        (z,), got = _mm_nn(mg, weight(l, 1), out_dtypes=(MXU_DTYPE,), epilogue=lambda v: (v,), name=f"out_proj_{l}", side=side)
        have.update(zip(keys, got))
        x_mid, h2 = _norm_fwd(x_in, z, mods[l], mods[l], small["g_norm2"][l], gate_idx=2, shift_idx=3, scale_idx=4,
                              name=f"norm2_fwd_{l}", **kwargs)
        keys, side = riding(l, "mlp1")
        (a_act, s_act), got = _mm_nn(h2, weight(l, 2), out_dtypes=(MXU_DTYPE, MXU_DTYPE),
                                     epilogue=lambda v: (v, jnp.square(jnp.maximum(v, 0.0))), name=f"mlp1_{l}", side=side)
        have.update(zip(keys, got))
        keys, side = riding(l, "mlp2")
        (o,), got = _mm_nn(s_act, weight(l, 3), out_dtypes=(MXU_DTYPE,), epilogue=lambda v: (v,), name=f"mlp2_{l}", side=side)
        have.update(zip(keys, got))
        saved.append(dict(x_in=x_in, h1=h1, p=p, qs=qs, k2=k2, v2=v2, ao=ao, mg=mg, z=z, x_mid=x_mid, h2=h2, a=a_act, s=s_act, o=o))
        xs, z_prev = x_mid, o

    dx, do, fin = _final(xs, z_prev, mods[depth - 1], small["g_final"], tgt, name="final", **kwargs)

    grads = [None] * depth
    dmods = [[None] * N_MOD for _ in range(depth)]
    sg = {k: [None] * depth for k in ("g_norm1", "g_norm2", "conv_w", "conv_b", "sink", "g_out_conv", "g_out_attn")}
    dmods[depth - 1][5] = fin[R_FGATE:R_FGATE + 2]
    sync = None
    for l in reversed(range(depth)):
        w_in, w_out, w1, w2 = (weight(l, k) for k in range(4))
        sv = saved[l]
        (da_act,), got = _mm_nt(do, w2, out_dtype=MXU_DTYPE, extra=sv["a"],
                                epilogue=lambda v, a: v * (2.0 * jnp.maximum(a.astype(F32), 0.0)), name=f"mlp2_dx_{l}",
                                side=sync and sync.pair_side())
        sync and sync.add(got)
        (g_w2,), got = _mm_tn(sv["s"], do, shards=1, name=f"mlp2_dw_{l}", side=sync and sync.chips_side((2,)))
        sync and sync.land((2,), got)
        (dh2,), got = _mm_nt(da_act, w1, out_dtype=MXU_DTYPE, name=f"mlp1_dx_{l}", side=sync and sync.chips_side((3,)))
        sync and sync.land((3,), got)
        (g_w1,), got = _mm_tn(sv["h2"], da_act, shards=shards, name=f"mlp1_dw_{l}", side=sync and sync.chips_side((0, 1)))
        sync and sync.land((0, 1), got)
        sync and sync.sum()
        own = _GradSync(comm, l, {2: g_w1, 3: g_w2}) if comm is not None and l == 0 else None
        dx, dz, sums2 = _norm_bwd(dx, dh2, sv["x_mid"], mods[l], small["g_norm2"][l], scale_idx=4,
                                  prev=(sv["z"], mods[l], 2), name=f"norm2_bwd_{l}", **kwargs)
        (dmg,), got = _mm_nt(dz, w_out, out_dtype=F32, name=f"out_proj_dx_{l}", side=own and own.pair_side())
        own and own.add(got)
        (g_wo,), _ = _mm_tn(sv["mg"], dz, shards=1, name=f"out_proj_dw_{l}")
        qkv_w = da + 2 * kw
        in_place = (3 * c) % qkv_w == 0
        dpc, dao, msum = _merge_bwd(dmg, sv["p"], sv["ao"], small["conv_w"][l], small["conv_b"][l],
                                    small["g_out_conv"][l], small["g_out_attn"][l],
                                    width=3 * c + qkv_w if in_place else 3 * c, name=f"merge_bwd_{l}", **kwargs)
        (dq, dkl, dvl, dkc, dvc, dsink), got = _attn_bwd(sv["qs"], sv["k2"], sv["v2"], dao, small["sink"][l],
                                                         name=f"attn_bwd_{l}", side=own and own.chips_side(own.keys), **kwargs)
        own and own.land(own.keys, got)
        own and own.sum()
        dk, dv = _kv_reduce(dkl, dvl, dkc, dvc, ctx_rows=ctx_rows, name=f"kv_reduce_{l}")
        if in_place:
            dp = _rope_bwd(dq, dk, dv, tabs, dpc, name=f"rope_bwd_{l}")
        else:
            dqkv = _rope_bwd(dq, dk, dv, tabs, lax.empty((t, qkv_w), MXU_DTYPE), name=f"rope_bwd_{l}")
            dp = jnp.concatenate([dpc, dqkv], axis=1)
        (dh1,), got = _mm_nt(dp, w_in, out_dtype=MXU_DTYPE, name=f"in_proj_dx_{l}", side=sync and sync.share_side())
        sync and sync.adam(got)
        (g_wi,), got = _mm_tn(sv["h1"], dp, shards=shards, name=f"in_proj_dw_{l}", side=own and own.share_side())
        own and own.adam(got)
        if comm is not None:
            sync = _GradSync(comm, l, {0: g_wi, 1: g_wo} if own else {0: g_wi, 1: g_wo, 2: g_w1, 3: g_w2})
        if l > 0:
            dx, do, sums1 = _norm_bwd(dx, dh1, sv["x_in"], mods[l], small["g_norm1"][l], scale_idx=1,
                                      prev=(saved[l - 1]["o"], mods[l - 1], 5), name=f"norm1_bwd_{l}", **kwargs)
            dmods[l - 1][5] = sums1[R_DGATE:R_DGATE + 2]
        elif sync is not None:
            sync.add(_run_side(sync.pair_side(), "rs_pair_last"))
            (dx, sums1), got = _norm_bwd(dx, dh1, sv["x_in"], mods[l], small["g_norm1"][l], scale_idx=1,
                                         name=f"norm1_bwd_{l}", side=sync.chips_side((0,)), **kwargs)
            sync.land((0,), got)
        else:
            dx, sums1 = _norm_bwd(dx, dh1, sv["x_in"], mods[l], small["g_norm1"][l], scale_idx=1,
                                  name=f"norm1_bwd_{l}", **kwargs)
        grads[l] = (g_wi, g_wo, g_w1, g_w2)
        dmods[l][0] = sums1[R_DSHIFT:R_DSHIFT + 2]
        dmods[l][1] = sums1[R_DSCALE:R_DSCALE + 2]
        dmods[l][2] = sums2[R_DGATE:R_DGATE + 2]
        dmods[l][3] = sums2[R_DSHIFT:R_DSHIFT + 2]
        dmods[l][4] = sums2[R_DSCALE:R_DSCALE + 2]
        sg["g_norm1"][l] = sums1[R_DG]
        sg["g_norm2"][l] = sums2[R_DG]
        sg["g_out_conv"][l] = msum[R_DGOC]
        sg["g_out_attn"][l] = msum[R_DGOA]
        sg["conv_b"][l] = msum[R_DCB]
        sg["conv_w"][l] = msum[R_DCW:R_DCW + 3]
        sg["sink"][l] = dsink[0, :da // HEAD_DIM]
    dmods = jnp.stack([jnp.stack(row, axis=1) for row in dmods])
    sg = {k: jnp.stack(v) for k, v in sg.items()}
    sg["g_final"] = fin[R_FG]
    return fin[R_FLOSS, 0], dx, grads, dmods, sg, sync


N_DEV = 8
N_CHIP = 4


def _place():
    mx, my, mc = lax.axis_index("x"), lax.axis_index("y"), lax.axis_index("c")
    others = [(1 - mx, my), (mx, 1 - my), (1 - mx, 1 - my)]
    return mx, my, mc, others


def _remote(src, dst, send_sems, recv_sems, k, dev):
    return pltpu.make_async_remote_copy(src_ref=src, dst_ref=dst, send_sem=send_sems.at[k], recv_sem=recv_sems.at[k],
                                        device_id=dev, device_id_type=MESH)


def _allgather8(x, name):
    r, ccols = x.shape

    def body(x_ref, out_ref, send_sems, recv_sems, local_sem):
        mx, my, mc, _ = _place()
        me = 4 * mx + 2 * my + mc
        mine = pltpu.make_async_copy(x_ref, out_ref.at[me], local_sem)
        mine.start()
        sent = []
        for k in range(1, N_DEV):
            fx, fy, fc = (k >> 2) & 1, (k >> 1) & 1, k & 1
            px, py, pc = (1 - mx if fx else mx), (1 - my if fy else my), (1 - mc if fc else mc)
            cp = _remote(x_ref, out_ref.at[me], send_sems, recv_sems, k - 1, (px, py, pc))
            cp.start()
            sent.append((cp, 4 * px + 2 * py + pc, (px, py, pc)))
        for k, (cp, peer, dev) in enumerate(sent):
            _remote(x_ref, out_ref.at[peer], send_sems, recv_sems, k, dev).wait_recv()
        for cp, _, _ in sent:
            cp.wait_send()
        mine.wait()

    vm = pl.BlockSpec(memory_space=pltpu.VMEM)
    return pl.pallas_call(
        body, in_specs=[vm], out_specs=vm, out_shape=jax.ShapeDtypeStruct((N_DEV, r, ccols), x.dtype),
        scratch_shapes=[pltpu.SemaphoreType.DMA((N_DEV - 1,)), pltpu.SemaphoreType.DMA((N_DEV - 1,)),
                        pltpu.SemaphoreType.DMA], name=name)(x)


def _allgather_chips(x, name):
    r, ccols = x.shape

    def body(x_ref, out_ref, send_sems, recv_sems, local_sem):
        mx, my, mc, others = _place()
        chip = 2 * mx + my
        mine = pltpu.make_async_copy(x_ref, out_ref.at[chip], local_sem)
        mine.start()
        sends = [_remote(x_ref, out_ref.at[chip], send_sems, recv_sems, j, (ox, oy, mc)) for j, (ox, oy) in enumerate(others)]
        for cp in sends:
            cp.start()
        for j, (ox, oy) in enumerate(others):
            _remote(x_ref, out_ref.at[2 * ox + oy], send_sems, recv_sems, j, (ox, oy, mc)).wait_recv()
        for cp in sends:
            cp.wait_send()
        mine.wait()

    vm = pl.BlockSpec(memory_space=pltpu.VMEM)
    return pl.pallas_call(
        body, in_specs=[vm], out_specs=vm, out_shape=jax.ShapeDtypeStruct((N_CHIP, r, ccols), x.dtype),
        scratch_shapes=[pltpu.SemaphoreType.DMA((N_CHIP - 1,)), pltpu.SemaphoreType.DMA((N_CHIP - 1,)),
                        pltpu.SemaphoreType.DMA], name=name)(x)


def _gather_side(halves):
    n = len(halves)

    def copies(ins, outs, sems):
        send_sems, recv_sems, local_sems = sems
        mx, my, mc, others = _place()
        chip = 2 * mx + my
        sib = (mx, my, 1 - mc)

        def src(w):
            return ins[w].at[pl.ds(mc, 1)]

        def slot(w, ch, core):
            return outs[w].at[ch, pl.ds(core, 1)]

        locs = [pltpu.make_async_copy(src(w), slot(w, chip, mc), local_sems.at[w]) for w in range(n)]
        first, landed, passed, from_sib = [], [], [], []
        for w in range(n):
            first.append(_remote(src(w), slot(w, chip, mc), send_sems, recv_sems, 7 * w, sib))
            from_sib.append(_remote(src(w), slot(w, chip, 1 - mc), send_sems, recv_sems, 7 * w, sib))
            for j, (ox, oy) in enumerate(others):
                och = 2 * ox + oy
                first.append(_remote(src(w), slot(w, chip, mc), send_sems, recv_sems, 7 * w + 1 + j, (ox, oy, mc)))
                landed.append(_remote(src(w), slot(w, och, mc), send_sems, recv_sems, 7 * w + 1 + j, (ox, oy, mc)))
                passed.append(_remote(slot(w, och, mc), slot(w, och, mc), send_sems, recv_sems, 7 * w + 4 + j, sib))
                from_sib.append(_remote(src(w), slot(w, och, 1 - mc), send_sems, recv_sems, 7 * w + 4 + j, sib))
        return locs, first, landed, passed, from_sib

    def start(ins, outs, sems):
        locs, first, _, _, _ = copies(ins, outs, sems)
        for cp in locs + first:
            cp.start()

    def middle(ins, outs, sems):
        _, _, landed, passed, _ = copies(ins, outs, sems)
        for cp, fw in zip(landed, passed):
            cp.wait_recv()
            fw.start()

    def finish(ins, outs, sems):
        locs, first, _, passed, from_sib = copies(ins, outs, sems)
        for cp in from_sib:
            cp.wait_recv()
        for cp in first + passed:
            cp.wait_send()
        for cp in locs:
            cp.wait()

    return _Side(halves, [jax.ShapeDtypeStruct((N_CHIP,) + h.shape, h.dtype) for h in halves],
                 [pltpu.SemaphoreType.DMA((7 * n,)), pltpu.SemaphoreType.DMA((7 * n,)), pltpu.SemaphoreType.DMA((n,))],
                 start, finish, middle=middle)


def _pair_side(gs):
    n = len(gs)

    def copies(ins, outs, sems):
        mx, my, mc, _ = _place()
        return [_remote(ins[w].at[:, pl.ds(1 - mc, 1)], outs[w], sems[0], sems[1], w, (mx, my, 1 - mc)) for w in range(n)]

    def start(ins, outs, sems):
        for cp in copies(ins, outs, sems):
            cp.start()

    def finish(ins, outs, sems):
        for cp in copies(ins, outs, sems):
            cp.wait()

    return _Side(gs, [jax.ShapeDtypeStruct((g.shape[0], 1) + g.shape[2:], g.dtype) for g in gs],
                 [pltpu.SemaphoreType.DMA((n,)), pltpu.SemaphoreType.DMA((n,))], start, finish)


def _pair_add(g, got, core, name):
    s, _, rh, ccols = g.shape
    tr = _pick(rh, (256, 128))

    def body(core_ref, g_ref, r_ref, o_ref):
        o_ref[...] = (g_ref[...].astype(F32) + r_ref[...].astype(F32)).astype(o_ref.dtype)

    spec = pltpu.PrefetchScalarGridSpec(
        num_scalar_prefetch=1, grid=(s, rh // tr),
        in_specs=[pl.BlockSpec((None, None, tr, ccols), lambda a, i, cr: (a, cr[0], i, 0)),
                  pl.BlockSpec((None, None, tr, ccols), lambda a, i, cr: (a, 0, i, 0))],
        out_specs=pl.BlockSpec((None, tr, ccols), lambda a, i, cr: (a, i, 0)))
    return pl.pallas_call(body, grid_spec=spec, out_shape=jax.ShapeDtypeStruct((s, rh, ccols), MXU_DTYPE),
                          compiler_params=_params(), name=name)(core, g, got)


def _chips_side(ps):
    n = len(ps)

    def copies(ins, outs, sems):
        send_sems, recv_sems, local_sems = sems
        mx, my, mc, others = _place()
        chip = 2 * mx + my
        locs = [pltpu.make_async_copy(ins[w].at[chip], outs[w].at[chip], local_sems.at[w]) for w in range(n)]
        sends, lands = [], []
        for w in range(n):
            for j, (ox, oy) in enumerate(others):
                och = 2 * ox + oy
                sends.append(_remote(ins[w].at[och], outs[w].at[chip], send_sems, recv_sems, 3 * w + j, (ox, oy, mc)))
                lands.append(_remote(ins[w].at[och], outs[w].at[och], send_sems, recv_sems, 3 * w + j, (ox, oy, mc)))
        return locs, sends, lands

    def start(ins, outs, sems):
        locs, sends, _ = copies(ins, outs, sems)
        for cp in locs + sends:
            cp.start()

    def finish(ins, outs, sems):
        locs, sends, lands = copies(ins, outs, sems)
        for cp in lands:
            cp.wait_recv()
        for cp in sends:
            cp.wait_send()
        for cp in locs:
            cp.wait()

    return _Side(ps, [jax.ShapeDtypeStruct(p.shape, p.dtype) for p in ps],
                 [pltpu.SemaphoreType.DMA((3 * n,)), pltpu.SemaphoreType.DMA((3 * n,)), pltpu.SemaphoreType.DMA((n,))],
                 start, finish)


def _chip_sum(rb, core, name):
    s, rh, ccols = rb.shape
    tr = _pick(rh, (256, 128))

    def body(core_ref, r_ref, o_ref):
        tot = r_ref[0].astype(F32)
        for k in range(1, s):
            tot = tot + r_ref[k].astype(F32)
        o_ref[...] = tot

    spec = pltpu.PrefetchScalarGridSpec(
        num_scalar_prefetch=1, grid=(rh // tr,),
        in_specs=[pl.BlockSpec((s, tr, ccols), lambda i, cr: (0, i, 0))],
        out_specs=pl.BlockSpec((None, tr, ccols), lambda i, cr: (cr[0], i, 0)))
    return pl.pallas_call(body, grid_spec=spec, out_shape=jax.ShapeDtypeStruct((2, rh, ccols), F32),
                          compiler_params=_params(), name=name)(core, rb)


def _share_side(fulls):
    n = len(fulls)

    def copies(ins, outs, sems):
        mx, my, mc, _ = _place()
        sib = (mx, my, 1 - mc)
        sends = [_remote(ins[w].at[mc], outs[w].at[mc], sems[0], sems[1], w, sib) for w in range(n)]
        lands = [_remote(ins[w].at[mc], outs[w].at[1 - mc], sems[0], sems[1], w, sib) for w in range(n)]
        return sends, lands

    def start(ins, outs, sems):
        for cp in copies(ins, outs, sems)[0]:
            cp.start()

    def finish(ins, outs, sems):
        sends, lands = copies(ins, outs, sems)
        for cp in lands:
            cp.wait_recv()
        for cp in sends:
            cp.wait_send()

    return _Side(fulls, [jax.ShapeDtypeStruct(f.shape, f.dtype) for f in fulls],
                 [pltpu.SemaphoreType.DMA((n,)), pltpu.SemaphoreType.DMA((n,))], start, finish,
                 aliases=[(w, w) for w in range(n)])


def _cast(w, l, name):
    _, r, ccols = w.shape
    tr = _pick(r, (256, 128))

    def body(w_ref, o_ref):
        o_ref[...] = w_ref[...].astype(o_ref.dtype)

    return pl.pallas_call(body, grid=(r // tr,), in_specs=[pl.BlockSpec((None, tr, ccols), lambda i: (l, i, 0))],
                          out_specs=pl.BlockSpec((tr, ccols), lambda i: (i, 0)),
                          out_shape=jax.ShapeDtypeStruct((r, ccols), MXU_DTYPE), compiler_params=_params(), name=name)(w)


def _adam_math(g, w, m, v):
    m = ADAM_B1 * m + (1.0 - ADAM_B1) * g
    v = ADAM_B2 * v + (1.0 - ADAM_B2) * jnp.square(g)
    m_hat = m / (1.0 - ADAM_B1 ** ADAM_STEP)
    v_hat = v / (1.0 - ADAM_B2 ** ADAM_STEP)
    return -ADAM_LR * (m_hat / (jnp.sqrt(v_hat) + ADAM_EPS) + ADAM_WD * w), m, v


def _adamw_layer(l, g, w, m, v, bufs, name, side=None):
    depth, r, ccols = w.shape
    tr = _pick(r, (128,))

    def body(g_ref, w_ref, m_ref, v_ref, b0, b1, b2, b3, go_ref, d_ref, mo_ref, vo_ref):
        gv = g_ref[...]
        d, m2, v2 = _adam_math(gv, w_ref[...], m_ref[...], v_ref[...])
        go_ref[...] = gv
        d_ref[...] = d
        mo_ref[...] = m2
        vo_ref[...] = v2

    lay = pl.BlockSpec((None, tr, ccols), lambda i: (l, i, 0))
    return _call(
        body, grid=(r // tr,), in_specs=[pl.BlockSpec((tr, ccols), lambda i: (i, 0)), lay, lay, lay] + [ANY] * 4,
        out_specs=[lay] * 4, out_shape=[jax.ShapeDtypeStruct((depth, r, ccols), F32)] * 4, scratch=[],
        args=(g, w, m, v, *bufs), name=name, side=side, aliases={4: 0, 5: 1, 6: 2, 7: 3})


def _adamw_small(g, g2, w, m, v, name):
    two = g2 is not None

    def body(*refs):
        if two:
            g_ref, g2_ref, w_ref, m_ref, v_ref, go_ref, d_ref, mo_ref, vo_ref = refs
            gv = g_ref[...] + g2_ref[...]
        else:
            g_ref, w_ref, m_ref, v_ref, go_ref, d_ref, mo_ref, vo_ref = refs
            gv = g_ref[...]
        d, m2, v2 = _adam_math(gv, w_ref[...], m_ref[...], v_ref[...])
        go_ref[...] = gv
        d_ref[...] = d
        mo_ref[...] = m2
        vo_ref[...] = v2

    args = [g] + ([g2] if two else []) + [w, m, v]
    vm = pl.BlockSpec(memory_space=pltpu.VMEM)
    return pl.pallas_call(body, in_specs=[vm] * len(args), out_specs=[vm] * 4,
                          out_shape=[jax.ShapeDtypeStruct(w.shape, F32)] * 4, name=name)(*args)


def _sum8(g, name):
    def body(g_ref, o_ref):
        tot = g_ref[0]
        for k in range(1, N_DEV):
            tot = tot + g_ref[k]
        o_ref[...] = tot

    vm = pl.BlockSpec(memory_space=pltpu.VMEM)
    return pl.pallas_call(body, in_specs=[vm], out_specs=vm, out_shape=jax.ShapeDtypeStruct(g.shape[1:], F32),
                          compiler_params=_params(), name=name)(g)


def _pack(arrs, width):
    flat = jnp.concatenate([a.reshape(-1).astype(F32) for a in arrs])
    rows = -(-flat.size // (8 * width)) * 8
    return jnp.pad(flat, (0, rows * width - flat.size)).reshape(rows, width)


def _unpack(flat, shapes):
    out, off = [], 0
    for shp in shapes:
        size = 1
        for v in shp:
            size *= v
        out.append(flat[..., off:off + size].reshape(flat.shape[:-1] + tuple(shp)))
        off += size
    return out


class _Comm:
    def __init__(self, core, params):
        self.core, self.params = core, params
        depth = params[0][1].shape[0]
        self.halves = [[_cast(w, l, f"cast_{nm}_{l}").reshape(2, w.shape[1] // 2, w.shape[2]) for nm, w, _, _ in params]
                       for l in range(depth)]
        self.stacked = [[lax.empty(w.shape, F32) for _ in range(4)] for _, w, _, _ in params]

    def weight(self, k, gathered):
        rows, cols = 2 * gathered.shape[2], gathered.shape[3]
        return gathered.reshape(N_CHIP, rows, cols) if k % 2 == 0 else gathered.reshape(1, N_CHIP * rows, cols)


class _GradSync:
    def __init__(self, comm, l, grads):
        self.comm, self.l, self.keys = comm, l, tuple(sorted(grads))
        self.gs = {k: g.reshape(N_CHIP, 2, g.shape[0] * g.shape[1] // (2 * N_CHIP), g.shape[2]) for k, g in grads.items()}
        self.ps, self.rb, self.full = {}, {}, {}

    def pair_side(self):
        return _pair_side([self.gs[k] for k in self.keys])

    def add(self, got):
        for k, r in zip(self.keys, got):
            self.ps[k] = _pair_add(self.gs[k], r, self.comm.core, f"rs_add_{self.l}_{k}")

    def chips_side(self, which):
        return _chips_side([self.ps[k] for k in which])

    def land(self, which, got):
        self.rb.update(zip(which, got))

    def sum(self):
        for k in self.keys:
            self.full[k] = _chip_sum(self.rb[k], self.comm.core, f"rs_sum_{self.l}_{k}")

    def share_side(self):
        return _share_side([self.full[k] for k in self.keys])

    def adam(self, got):
        for k, full in zip(self.keys, got):
            nm, w, m, v = self.comm.params[k]
            gsum = full.reshape(2 * full.shape[1], full.shape[2])
            self.comm.stacked[k], _ = _adamw_layer(self.l, gsum, w, m, v, self.comm.stacked[k], f"adamw_{nm}_{self.l}")


COND_ROWS = 16


def _ada_fwd(cond, w_ada, b_cols, name):
    depth, d, ns = w_ada.shape
    tn = _pick(ns, (512, 384, 256, 128))

    def body(c_ref, w_ref, b_ref, o_ref):
        o_ref[...] = _dot(_silu(c_ref[...]), w_ref[...]) + b_ref[...]

    return pl.pallas_call(
        body, grid=(depth, ns // tn),
        in_specs=[pl.BlockSpec((COND_ROWS, d), lambda l, j: (0, 0)), pl.BlockSpec((None, d, tn), lambda l, j: (l, 0, j)),
                  pl.BlockSpec((None, 1, tn), lambda l, j: (l, 0, j))],
        out_specs=pl.BlockSpec((None, COND_ROWS, tn), lambda l, j: (l, 0, j)),
        out_shape=jax.ShapeDtypeStruct((depth, COND_ROWS, ns), F32), compiler_params=_params(), name=name)(cond, w_ada, b_cols)


def _ada_bwd(cond, dmod, w_ada, name, side=None):
    depth, d, ns = w_ada.shape
    tn = _pick(ns, (512, 384, 256, 128))

    def body(c_ref, dm_ref, w_ref, gw_ref, dc_ref):
        @pl.when((pl.program_id(0) == 0) & (pl.program_id(1) == 0))
        def _():
            dc_ref[...] = jnp.zeros_like(dc_ref)

        dm = dm_ref[...]
        gw_ref[...] = _dot_tn(_silu(c_ref[...]), dm)
        dc_ref[...] += _dot_nt(dm, w_ref[...])

    return _call(
        body, grid=(depth, ns // tn),
        in_specs=[pl.BlockSpec((COND_ROWS, d), lambda l, j: (0, 0)),
                  pl.BlockSpec((None, COND_ROWS, tn), lambda l, j: (l, 0, j)),
                  pl.BlockSpec((None, d, tn), lambda l, j: (l, 0, j))],
        out_specs=[pl.BlockSpec((None, d, tn), lambda l, j: (l, 0, j)), pl.BlockSpec((COND_ROWS, d), lambda l, j: (0, 0))],
        out_shape=[jax.ShapeDtypeStruct((depth, d, ns), F32), jax.ShapeDtypeStruct((COND_ROWS, d), F32)],
        scratch=[], args=(cond, dmod, w_ada), name=name, side=side)


def _cctx_grad(parts, c_ctx, name):
    def body(p_ref, c_ref, o_ref):
        tot = p_ref[0, 0:1, :]
        for k in range(1, N_CHIP):
            tot = tot + p_ref[k, 0:1, :]
        z = c_ref[...]
        sg = 1.0 / (1.0 + jnp.exp(-z))
        o_ref[...] = tot * (sg + z * sg * (1.0 - sg))

    vm = pl.BlockSpec(memory_space=pltpu.VMEM)
    return pl.pallas_call(body, in_specs=[vm, vm], out_specs=vm, out_shape=jax.ShapeDtypeStruct(c_ctx.shape, F32),
                          name=name)(parts, c_ctx)


def kernel(x, c, ctx, c_ctx, w_ada, b_ada, g_norm1, g_norm2, w_in, conv_w, conv_b, sink, g_out_conv, g_out_attn, w_out, w_mlp1, w_mlp2, g_final, loss_target, m_c_ctx, m_w_ada, m_b_ada, m_g_norm1, m_g_norm2, m_w_in, m_conv_w, m_conv_b, m_sink, m_g_out_conv, m_g_out_attn, m_w_out, m_w_mlp1, m_w_mlp2, m_g_final, v_c_ctx, v_w_ada, v_b_ada, v_g_norm1, v_g_norm2, v_w_in, v_conv_w, v_conv_b, v_sink, v_g_out_conv, v_g_out_attn, v_w_out, v_w_mlp1, v_w_mlp2, v_g_final):
    mx, my, mc = lax.axis_index("x"), lax.axis_index("y"), lax.axis_index("c")
    chip, rank = 2 * mx + my, 4 * mx + 2 * my + mc
    core = jnp.reshape(mc, (1,)).astype(I32)
    depth, d = g_norm1.shape
    s_len, ctx_rows = x.shape[1], ctx.shape[1]
    cw_cols = conv_w.shape[2]
    c_conv = cw_cols * N_CHIP
    n_heads = sink.shape[1]
    ns_ada = w_ada.shape[2]

    got = _allgather8(_pack([c, conv_w], d), "gather_cond")
    flat = got.reshape(N_DEV, -1)
    conv_w_full = jnp.transpose(flat[::2, d:d + conv_w.size].reshape(N_CHIP, depth, 3, cw_cols), (1, 2, 0, 3))
    conv_w_full = conv_w_full.reshape(depth, 3, c_conv)
    cond = jnp.zeros((COND_ROWS, d), F32).at[:N_DEV].set(flat[:, :d]).at[N_DEV].set(c_ctx)

    b_cols = lax.dynamic_slice_in_dim(b_ada, chip * ns_ada, ns_ada, axis=1)[:, None, :]
    mod_cols = _ada_fwd(cond, w_ada, b_cols, "ada_fwd")
    got = _allgather_chips(mod_cols.reshape(depth * COND_ROWS, ns_ada), "gather_mod")
    mod_all = jnp.transpose(got.reshape(N_CHIP, depth, COND_ROWS, ns_ada), (1, 2, 0, 3))
    mod_all = mod_all.reshape(depth, COND_ROWS, N_CHIP * ns_ada)
    mod_me = lax.dynamic_index_in_dim(mod_all, rank, axis=1, keepdims=False)
    mods = jnp.stack([mod_all[:, N_DEV], mod_me], axis=1).reshape(depth, 2, N_MOD, 1, d)

    comm = _Comm(core, (("w_in", w_in, m_w_in, v_w_in), ("w_out", w_out, m_w_out, v_w_out),
                        ("w_mlp1", w_mlp1, m_w_mlp1, v_w_mlp1), ("w_mlp2", w_mlp2, m_w_mlp2, v_w_mlp2)))
    small = dict(g_norm1=g_norm1[:, None], g_norm2=g_norm2[:, None], conv_w=conv_w_full, conv_b=conv_b[:, None], sink=sink,
                 g_out_conv=g_out_conv[:, None], g_out_attn=g_out_attn[:, None], g_final=g_final[None])
    x0 = jnp.concatenate([ctx[0], x[0]], axis=0)
    loss_part, dx0, _, dmods, sg, last = _local_step(x0, loss_target[0], mods, None, small, ctx_rows=ctx_rows, comm=comm)
    loss = lax.psum(loss_part, ("x", "y", "c"))
    grad_x = dx0[ctx_rows:][None]

    names = ("g_norm1", "g_norm2", "conv_w", "conv_b", "sink", "g_out_conv", "g_out_attn", "g_final")
    shapes = [(depth, 2, N_MOD * d)] + [sg[k].shape for k in names]
    got = _allgather8(_pack([dmods] + [sg[k] for k in names], d), "gather_small")
    tot = _unpack(_sum8(got, "sum_small").reshape(-1), shapes)
    dmod_tot, small_tot = tot[0], dict(zip(names, tot[1:]))
    dmod_lat = _unpack(got.reshape(N_DEV, -1), shapes[:1])[0][:, :, 1]
    dm_rows = jnp.zeros((depth, COND_ROWS, N_MOD * d), F32)
    dm_rows = dm_rows.at[:, :N_DEV].set(jnp.transpose(dmod_lat, (1, 0, 2))).at[:, N_DEV].set(dmod_tot[:, 0])
    dm_cols = lax.dynamic_slice_in_dim(dm_rows, chip * ns_ada, ns_ada, axis=2)
    (g_w_ada, dcond), got = _ada_bwd(cond, dm_cols, w_ada, "ada_bwd", side=last.chips_side((1,)))
    last.land((1,), got)
    last.sum()
    last.adam(_run_side(last.share_side(), "rs_share_last"))
    got = _allgather_chips(dcond[N_DEV:N_DEV + 8], "gather_dcond")
    g_c_ctx = _cctx_grad(got, c_ctx[None], "c_ctx_grad")

    res = {}
    ada_bufs = [lax.empty((1,) + (depth * d, ns_ada), F32) for _ in range(4)]
    ada_res, _ = _adamw_layer(
        0, g_w_ada.reshape(depth * d, ns_ada), w_ada.reshape(1, depth * d, ns_ada),
        m_w_ada.reshape(1, depth * d, ns_ada), v_w_ada.reshape(1, depth * d, ns_ada), ada_bufs, "adamw_w_ada")
    res["w_ada"] = [r.reshape(w_ada.shape) for r in ada_res]
    res.update(zip(("w_in", "w_out", "w_mlp1", "w_mlp2"), comm.stacked))
    res["c_ctx"] = [r[0] for r in _adamw_small(g_c_ctx, None, c_ctx[None], m_c_ctx[None], v_c_ctx[None], "adamw_c_ctx")]
    res["b_ada"] = _adamw_small(dmod_tot[:, 0], dmod_tot[:, 1], b_ada, m_b_ada, v_b_ada, "adamw_b_ada")
    cw_grad = lax.dynamic_slice_in_dim(small_tot["conv_w"], chip * cw_cols, cw_cols, axis=2)
    res["conv_w"] = [r.reshape(conv_w.shape) for r in _adamw_small(
        cw_grad.reshape(depth * 3, cw_cols), None, conv_w.reshape(depth * 3, cw_cols),
        m_conv_w.reshape(depth * 3, cw_cols), v_conv_w.reshape(depth * 3, cw_cols), "adamw_conv_w")]
    for nm, w, m, v in (("g_norm1", g_norm1, m_g_norm1, v_g_norm1), ("g_norm2", g_norm2, m_g_norm2, v_g_norm2),
                        ("conv_b", conv_b, m_conv_b, v_conv_b), ("sink", sink, m_sink, v_sink),
                        ("g_out_conv", g_out_conv, m_g_out_conv, v_g_out_conv),
                        ("g_out_attn", g_out_attn, m_g_out_attn, v_g_out_attn)):
        res[nm] = _adamw_small(small_tot[nm], None, w, m, v, f"adamw_{nm}")
    res["g_final"] = [r[0] for r in _adamw_small(small_tot["g_final"][None], None, g_final[None], m_g_final[None],
                                                 v_g_final[None], "adamw_g_final")]
    order = ("c_ctx", "w_ada", "b_ada", "g_norm1", "g_norm2", "w_in", "conv_w", "conv_b", "sink", "g_out_conv",
             "g_out_attn", "w_out", "w_mlp1", "w_mlp2", "g_final")
    return (loss, grad_x, *[res[n][0] for n in order], *[res[n][1] for n in order], *[res[n][2] for n in order],
            *[res[n][3] for n in order])
```
